```python
import jax, jax.numpy as jnp
from jax import lax
import numpy as np

D_MODEL = 2048
BATCH = 16
SEQ = 256
DEPTH = 2
DEC_BATCH = 2
DEC_SEQ = 1024
PAST_LEN = 256

GRID_W = 64
ROPE_THETA = 10000.0
NORM_EPS = 1e-6
Q_BLOCK = 128
N_EVEN = (DEPTH + 1) // 2
N_ODD = DEPTH // 2
ML_HEADS = 8
ML_QK = 64
ML_V = 128
ML_CHUNK = 64
MLA_HEADS = 8
MLA_Q_RANK = 512
MLA_KV_RANK = 512
MLA_NOPE = 128
MLA_ROPE = 64
MLA_V = 128
GQA_HEADS = 16
GQA_KV_HEADS = 4
GQA_HEAD_DIM = 128
N_EXPERTS = 16
EXPERT_DIM = 1024
EC_FACTOR = 2
EVEN_COLS = (ML_HEADS * ML_QK, ML_HEADS * ML_QK, ML_HEADS * ML_V, ML_HEADS * ML_V,
             2 * ML_HEADS, 2 * ML_HEADS, MLA_Q_RANK, MLA_KV_RANK, MLA_ROPE)
EVEN_IN = sum(EVEN_COLS)
EVEN_OUT = ML_HEADS * ML_V + MLA_HEADS * MLA_V
ODD_COLS = (GQA_HEADS * GQA_HEAD_DIM, GQA_KV_HEADS * GQA_HEAD_DIM, GQA_KV_HEADS * GQA_HEAD_DIM)
ODD_IN = sum(ODD_COLS)

kernel_name = "hybrid_diffusion_mlstm_mla_gqa_ec_step"


def split_cols(x, sizes):
    out, off = [], 0
    for s in sizes:
        out.append(x[..., off:off + s])
        off += s
    return out


def rms_norm(x):
    xf = x.astype(jnp.float32)
    return (xf * lax.rsqrt(jnp.mean(xf * xf, axis=-1, keepdims=True) + NORM_EPS)).astype(x.dtype)


def modulate(x, shift, scale):
    return rms_norm(x) * (1 + scale) + shift


def axial_rope_angles(n_tokens, rot_dim):
    rows = n_tokens // GRID_W
    row = jnp.repeat(jnp.arange(rows), GRID_W).astype(jnp.float32)
    col = jnp.tile(jnp.arange(GRID_W), rows).astype(jnp.float32)
    quarter = rot_dim // 4
    inv = ROPE_THETA ** (-jnp.arange(quarter, dtype=jnp.float32) / quarter)
    return row[:, None] * inv, col[:, None] * inv


def apply_axial_rope(x, a_row, a_col):
    half = x.shape[-1] // 2

    def rot(xh, a):
        x1, x2 = jnp.split(xh.astype(jnp.float32), 2, axis=-1)
        cs, sn = jnp.cos(a), jnp.sin(a)
        return jnp.concatenate([x1 * cs - x2 * sn, x1 * sn + x2 * cs], axis=-1)

    return jnp.concatenate([rot(x[..., :half], a_row), rot(x[..., half:], a_col)], axis=-1).astype(x.dtype)


def multi_source_attention(sources, scale):
    n_q = sources[0][0].shape[3]
    sizes = [k.shape[2] for _, k, _ in sources]

    def block(i):
        start = i * Q_BLOCK
        scores = [jnp.einsum('bgrqd,bgkd->bgrqk', lax.dynamic_slice_in_dim(q, start, Q_BLOCK, axis=3), k).astype(jnp.float32) * scale
                  for q, k, _ in sources]
        p = jax.nn.softmax(jnp.concatenate(scores, axis=-1), axis=-1)
        out, off = None, 0
        for (_, _, v), sz in zip(sources, sizes):
            term = jnp.einsum('bgrqk,bgkd->bgrqd', p[..., off:off + sz].astype(v.dtype), v)
            out = term if out is None else out + term
            off += sz
        return out

    outs = lax.map(block, jnp.arange(n_q // Q_BLOCK))
    b, g, r = outs.shape[1:4]
    return jnp.moveaxis(outs, 0, 3).reshape(b, g, r, n_q, outs.shape[-1])


def mlstm_chunkwise(q, k, v, log_i, log_f, c0, n0, m0):
    bsz, nh, s_len, _ = q.shape
    dv = v.shape[-1]
    nc = s_len // ML_CHUNK

    def chunks(a):
        return jnp.moveaxis(a.reshape(bsz, nh, nc, ML_CHUNK, *a.shape[3:]), 2, 0)

    causal = jnp.tril(jnp.ones((ML_CHUNK, ML_CHUNK), dtype=bool))

    def step(carry, xs):
        c, n, m = carry
        qc, kc, vc, ic, fc = xs
        b = jnp.cumsum(fc, axis=-1)
        d = jnp.where(causal, b[..., :, None] - b[..., None, :] + ic[..., None, :], -jnp.inf)
        inter = b + m[..., None]
        m_t = jnp.maximum(inter, d.max(axis=-1))
        w_inter = jnp.exp(inter - m_t)
        sw = jnp.einsum('bhtd,bhsd->bhts', qc, kc) * jnp.exp(d - m_t[..., None])
        num = w_inter[..., None] * jnp.einsum('bhtd,bhdv->bhtv', qc, c) + jnp.einsum('bhts,bhsv->bhtv', sw, vc)
        den = w_inter * jnp.einsum('bhtd,bhd->bht', qc, n) + sw.sum(axis=-1)
        h = num / jnp.maximum(jnp.abs(den), jnp.exp(-m_t))[..., None]
        b_end = b[..., -1]
        g = b_end[..., None] - b + ic
        m_new = jnp.maximum(b_end + m, g.max(axis=-1))
        w_end = jnp.exp(g - m_new[..., None])
        decay = jnp.exp(b_end + m - m_new)
        c_new = decay[..., None, None] * c + jnp.einsum('bhs,bhsd,bhsv->bhdv', w_end, kc, vc)
        n_new = decay[..., None] * n + jnp.einsum('bhs,bhsd->bhd', w_end, kc)
        return (c_new, n_new, m_new), h

    (c, n, m), hs = lax.scan(step, (c0, n0, m0), tuple(chunks(a) for a in (q, k, v, log_i, log_f)))
    h = jnp.moveaxis(hs, 0, 2).reshape(bsz, nh, s_len, dv)
    return h, c, n, m


def even_mixer(h, w_in, b_igate, b_fgate, g_mlstm, g_cq, w_uq, g_ckv, w_ukv, w_out, ctx=None):
    f32 = jnp.float32
    bsz, s_len, _ = h.shape
    q, k, v, o, ig, fg, cq, ckv, kr = split_cols(h @ w_in, EVEN_COLS)

    def heads(a, n):
        return a.reshape(a.shape[0], a.shape[1], n, -1).transpose(0, 2, 1, 3)

    q_m = heads(q, ML_HEADS).astype(f32) * (ML_QK ** -0.5)
    k_m = heads(k, ML_HEADS).astype(f32)
    v_m = heads(v, ML_HEADS).astype(f32)
    log_i = (ig.reshape(bsz, s_len, 2, ML_HEADS) + b_igate).astype(f32).transpose(2, 0, 3, 1)
    log_f = jax.nn.log_sigmoid((fg.reshape(bsz, s_len, 2, ML_HEADS) + b_fgate).astype(f32)).transpose(2, 0, 3, 1)
    if ctx is None:
        c0 = jnp.zeros((bsz, 2, ML_HEADS, ML_QK, ML_V), f32)
        n0 = jnp.zeros((bsz, 2, ML_HEADS, ML_QK), f32)
        m0 = jnp.zeros((bsz, 2, ML_HEADS), f32)
    else:
        c0, n0, m0 = (a.astype(f32) for a in ctx[:3])
    rev = lambda a: jnp.flip(a, axis=2)
    h_f, cf, nf, mf = mlstm_chunkwise(q_m, k_m, v_m, log_i[0], log_f[0], c0[:, 0], n0[:, 0], m0[:, 0])
    h_b, cb, nb, mb = mlstm_chunkwise(rev(q_m), rev(k_m), rev(v_m), rev(log_i[1]), rev(log_f[1]),
                                      c0[:, 1], n0[:, 1], m0[:, 1])
    h_ml = rms_norm(h_f + rev(h_b)) * g_mlstm.reshape(ML_HEADS, 1, ML_V)
    y_ml = h_ml.transpose(0, 2, 1, 3).reshape(bsz, s_len, -1).astype(h.dtype) * jax.nn.sigmoid(o)

    cq = rms_norm(cq) * g_cq
    ckv = rms_norm(ckv) * g_ckv
    q_a = heads(cq @ w_uq, MLA_HEADS)
    q_nope, q_rope = q_a[..., :MLA_NOPE], q_a[..., MLA_NOPE:]

    def expand(c_lat, k_rope):
        kv = heads(c_lat @ w_ukv, MLA_HEADS)
        k_nope, v_a = kv[..., :MLA_NOPE], kv[..., MLA_NOPE:]
        k_r = jnp.broadcast_to(k_rope[:, None], k_nope.shape[:3] + (MLA_ROPE,)).astype(k_nope.dtype)
        return jnp.concatenate([k_nope, k_r], axis=-1), v_a

    q_plain = jnp.concatenate([q_nope, q_rope], axis=-1)[:, :, None]
    if ctx is None:
        k_c, v_c = expand(ckv, kr)
        sources = [(q_plain, k_c, v_c)]
    else:
        a_row, a_col = axial_rope_angles(s_len, MLA_ROPE)
        q_rot = jnp.concatenate([q_nope, apply_axial_rope(q_rope, a_row, a_col)], axis=-1)[:, :, None]
        k_l, v_l = expand(ckv, apply_axial_rope(kr, a_row, a_col))
        k_c, v_c = expand(ctx[3], ctx[4])
        sources = [(q_rot, k_l, v_l), (q_plain, k_c, v_c)]
    o_a = multi_source_attention(sources, (MLA_NOPE + MLA_ROPE) ** -0.5)[:, :, 0]
    y_a = o_a.transpose(0, 2, 1, 3).reshape(bsz, s_len, -1)

    out = jnp.concatenate([y_ml, y_a], axis=-1) @ w_out
    new_ctx = (jnp.stack([cf, cb], axis=1), jnp.stack([nf, nb], axis=1), jnp.stack([mf, mb], axis=1), ckv, kr)
    return out, new_ctx


def odd_mixer(h, w_in, g_q, g_k, w_out, ctx=None):
    bsz, s_len, _ = h.shape
    q, k, v = split_cols(h @ w_in, ODD_COLS)
    heads = lambda a, n: a.reshape(bsz, s_len, n, -1).transpose(0, 2, 1, 3)
    q = rms_norm(heads(q, GQA_HEADS)) * g_q
    k = rms_norm(heads(k, GQA_KV_HEADS)) * g_k
    v = heads(v, GQA_KV_HEADS)
    group = lambda a: a.reshape(bsz, GQA_KV_HEADS, GQA_HEADS // GQA_KV_HEADS, s_len, GQA_HEAD_DIM)
    if ctx is None:
        sources = [(group(q), k, v)]
    else:
        a_row, a_col = axial_rope_angles(s_len, GQA_HEAD_DIM)
        sources = [(group(apply_axial_rope(q, a_row, a_col)), apply_axial_rope(k, a_row, a_col), v),
                   (group(q), ctx[0], ctx[1])]
    o = multi_source_attention(sources, GQA_HEAD_DIM ** -0.5).reshape(bsz, GQA_HEADS, s_len, GQA_HEAD_DIM)
    out = o.transpose(0, 2, 1, 3).reshape(bsz, s_len, -1) @ w_out
    return out, (k, v)


def expert_choice_ffn(h, w_router, w_gate, w_up, w_down):
    bsz, s_len, dm = h.shape
    cap = EC_FACTOR * s_len // N_EXPERTS
    aff = jax.nn.softmax(jnp.einsum('bsd,de->bse', h, w_router).astype(jnp.float32), axis=-1)
    gate, idx = lax.top_k(jnp.swapaxes(aff, 1, 2), cap)
    xe = jax.vmap(lambda hb, ib: hb[ib])(h, idx)
    a = jnp.einsum('becd,edf->becf', xe, w_gate)
    u = jnp.einsum('becd,edf->becf', xe, w_up)
    y = jnp.einsum('becf,efd->becd', jax.nn.silu(a) * u, w_down) * gate[..., None].astype(h.dtype)
    return jax.vmap(lambda ib, yb: jnp.zeros((s_len, dm), yb.dtype).at[ib.reshape(-1)].add(yb.reshape(-1, dm)))(idx, y)


def setup_inputs(seed: int = 0) -> dict:
    key = jax.random.key(seed)
    ks = iter(jax.random.split(key, 40))
    nrm = lambda shape, scale: jax.random.normal(next(ks), shape, jnp.float32) * scale
    gain = lambda shape: 1.0 + nrm(shape, 0.02)
    d = D_MODEL
    return {
        "x_prompt": nrm((BATCH, SEQ, d), 1.0),
        "x_sample": nrm((DEC_BATCH, DEC_SEQ, d), 1.0),
        "state_mlstm_c": nrm((DEC_BATCH, N_EVEN, 2, ML_HEADS, ML_QK, ML_V), 0.1),
        "state_mlstm_n": nrm((DEC_BATCH, N_EVEN, 2, ML_HEADS, ML_QK), 0.3),
        "state_mlstm_m": nrm((DEC_BATCH, N_EVEN, 2, ML_HEADS), 1.0),
        "cache_mla_ckv": nrm((DEC_BATCH, N_EVEN, PAST_LEN, MLA_KV_RANK), 1.0),
        "cache_mla_krope": nrm((DEC_BATCH, N_EVEN, PAST_LEN, MLA_ROPE), 1.0),
        "cache_gqa_k": nrm((DEC_BATCH, N_ODD, GQA_KV_HEADS, PAST_LEN, GQA_HEAD_DIM), 1.0),
        "cache_gqa_v": nrm((DEC_BATCH, N_ODD, GQA_KV_HEADS, PAST_LEN, GQA_HEAD_DIM), 1.0),
        "c": nrm((DEC_BATCH, d), 1.0),
        "c_ctx": nrm((d,), 1.0),
        "w_mod": nrm((DEPTH, d, 6 * d), 0.5 * d ** -0.5),
        "b_mod": nrm((DEPTH, 6 * d), 0.02),
        "w_in_even": nrm((N_EVEN, d, EVEN_IN), d ** -0.5),
        "b_igate": nrm((N_EVEN, 2, ML_HEADS), 0.1),
        "b_fgate": jnp.linspace(3.0, 6.0, ML_HEADS, dtype=jnp.float32) + nrm((N_EVEN, 2, ML_HEADS), 0.1),
        "g_mlstm": gain((N_EVEN, ML_HEADS * ML_V)),
        "g_cq": gain((N_EVEN, MLA_Q_RANK)),
        "w_uq": nrm((N_EVEN, MLA_Q_RANK, MLA_HEADS * (MLA_NOPE + MLA_ROPE)), MLA_Q_RANK ** -0.5),
        "g_ckv": gain((N_EVEN, MLA_KV_RANK)),
        "w_ukv": nrm((N_EVEN, MLA_KV_RANK, MLA_HEADS * (MLA_NOPE + MLA_V)), MLA_KV_RANK ** -0.5),
        "w_out_even": nrm((N_EVEN, EVEN_OUT, d), EVEN_OUT ** -0.5),
        "w_in_odd": nrm((N_ODD, d, ODD_IN), d ** -0.5),
        "g_qnorm": gain((N_ODD, GQA_HEAD_DIM)),
        "g_knorm": gain((N_ODD, GQA_HEAD_DIM)),
        "w_out_odd": nrm((N_ODD, GQA_HEADS * GQA_HEAD_DIM, d), (GQA_HEADS * GQA_HEAD_DIM) ** -0.5),
        "w_router": nrm((DEPTH, d, N_EXPERTS), d ** -0.5),
        "w_expert_gate": nrm((DEPTH, N_EXPERTS, d, EXPERT_DIM), d ** -0.5),
        "w_expert_up": nrm((DEPTH, N_EXPERTS, d, EXPERT_DIM), d ** -0.5),
        "w_expert_down": nrm((DEPTH, N_EXPERTS, EXPERT_DIM, d), EXPERT_DIM ** -0.5),
        "g_final": gain((d,)),
    }


def reference(x_prompt, x_sample, state_mlstm_c, state_mlstm_n, state_mlstm_m, cache_mla_ckv, cache_mla_krope,
              cache_gqa_k, cache_gqa_v, c, c_ctx, w_mod, b_mod, w_in_even, b_igate, b_fgate, g_mlstm, g_cq, w_uq,
              g_ckv, w_ukv, w_out_even, w_in_odd, g_qnorm, g_knorm, w_out_odd, w_router, w_expert_gate,
              w_expert_up, w_expert_down, g_final):
    cond_ctx = jax.nn.silu(c_ctx)
    cond_lat = jax.nn.silu(c)[:, None, :]
    xp, xs = x_prompt, x_sample
    new_even, new_odd = [], []
    for layer in range(DEPTH):
        j = layer // 2
        mp = jnp.split(cond_ctx @ w_mod[layer] + b_mod[layer], 6, axis=-1)
        ms = jnp.split(cond_lat @ w_mod[layer] + b_mod[layer], 6, axis=-1)
        hp = modulate(xp, mp[0], mp[1])
        hs = modulate(xs, ms[0], ms[1])
        if layer % 2 == 0:
            args = (w_in_even[j], b_igate[j], b_fgate[j], g_mlstm[j], g_cq[j], w_uq[j], g_ckv[j], w_ukv[j], w_out_even[j])
            op, ctx_p = even_mixer(hp, *args)
            os_, _ = even_mixer(hs, *args, ctx=(state_mlstm_c[:, j], state_mlstm_n[:, j], state_mlstm_m[:, j],
                                                cache_mla_ckv[:, j], cache_mla_krope[:, j]))
            new_even.append(ctx_p)
        else:
            args = (w_in_odd[j], g_qnorm[j], g_knorm[j], w_out_odd[j])
            op, ctx_p = odd_mixer(hp, *args)
            os_, _ = odd_mixer(hs, *args, ctx=(cache_gqa_k[:, j], cache_gqa_v[:, j]))
            new_odd.append(ctx_p)
        xp = xp + mp[2] * op
        xs = xs + ms[2] * os_
        moe = (w_router[layer], w_expert_gate[layer], w_expert_up[layer], w_expert_down[layer])
        xp = xp + mp[5] * expert_choice_ffn(modulate(xp, mp[3], mp[4]), *moe)
        xs = xs + ms[5] * expert_choice_ffn(modulate(xs, ms[3], ms[4]), *moe)
    y_prompt = rms_norm(xp) * g_final
    y_sample = rms_norm(xs) * g_final
    new_c = jnp.stack([e[0] for e in new_even], axis=1)
    new_n = jnp.stack([e[1] for e in new_even], axis=1)
    new_m = jnp.stack([e[2] for e in new_even], axis=1)
    new_ckv = jnp.stack([e[3] for e in new_even], axis=1)
    new_krope = jnp.stack([e[4] for e in new_even], axis=1)
    new_k = jnp.stack([e[0] for e in new_odd], axis=1)
    new_v = jnp.stack([e[1] for e in new_odd], axis=1)
    return (y_prompt, y_sample, new_c, new_n, new_m, new_ckv, new_krope, new_k, new_v)
```

```python
import functools

import jax
import jax.numpy as jnp
from jax import lax
from jax.experimental import pallas as pl
from jax.experimental.pallas import tpu as pltpu

BF = jnp.bfloat16
F32 = jnp.float32

D_MODEL = 2048
GRID_W = 64
ROPE_THETA = 10000.0
NORM_EPS = 1e-6
ML_HEADS = 8
ML_QK = 64
ML_V = 128
ML_CHUNK = 64
MLA_HEADS = 8
MLA_Q_RANK = 512
MLA_KV_RANK = 512
MLA_NOPE = 128
MLA_ROPE = 64
MLA_V = 128
GQA_HEADS = 16
GQA_KV_HEADS = 4
GQA_HEAD_DIM = 128
N_EXPERTS = 16
EXPERT_DIM = 1024
EC_FACTOR = 2

QKVO_COLS = 2 * ML_HEADS * ML_QK + 2 * ML_HEADS * ML_V
SIDE_COLS = MLA_Q_RANK + MLA_KV_RANK + 128
GATE_LANE0 = MLA_ROPE

VMEM_LIMIT_BYTES = 56 * 1024 * 1024


def _params(n_axes):
    return pltpu.CompilerParams(dimension_semantics=("arbitrary",) * n_axes,
                                vmem_limit_bytes=VMEM_LIMIT_BYTES)


def _bdot(a, b):
    return jnp.dot(a.astype(BF), b.astype(BF), preferred_element_type=F32)


def _bdot_nt(a, b):
    return lax.dot_general(a.astype(BF), b.astype(BF), (((1,), (1,)), ((), ())),
                           preferred_element_type=F32)


def _bdot_tn(a, b):
    return lax.dot_general(a.astype(BF), b.astype(BF), (((0,), (0,)), ((), ())),
                           preferred_element_type=F32)


def _sigmoid(x):
    return 1.0 / (1.0 + jnp.exp(-x))


def _log_sigmoid(x):
    return jnp.minimum(x, 0.0) - jnp.log1p(jnp.exp(-jnp.abs(x)))


def _rms(x):
    return x * lax.rsqrt(jnp.mean(x * x, axis=-1, keepdims=True) + NORM_EPS)


def _rope(x, cos_t, sin_t, quarter):
    width = x.shape[-1]
    axis = x.ndim - 1
    lane = lax.broadcasted_iota(jnp.int32, x.shape, axis)
    partner = jnp.where((lane & quarter) == 0,
                        pltpu.roll(x, width - quarter, axis=axis),
                        pltpu.roll(x, quarter, axis=axis))
    return x * cos_t + partner * sin_t


def _mod_kernel(c_ref, w_ref, b_ref, o_ref):
    c = c_ref[...]
    o_ref[...] = _bdot(c * _sigmoid(c), w_ref[...]) + b_ref[...]


def _mod_vectors(c8, w_mod, b_mod):
    n_layers, k, n = w_mod.shape
    tn = 1024
    return pl.pallas_call(
        _mod_kernel,
        grid=(n_layers, n // tn),
        in_specs=[pl.BlockSpec((8, k), lambda l, j: (0, 0)),
                  pl.BlockSpec((None, k, tn), lambda l, j: (l, 0, j)),
                  pl.BlockSpec((None, 1, tn), lambda l, j: (l, 0, j))],
        out_specs=pl.BlockSpec((None, 8, tn), lambda l, j: (l, 0, j)),
        out_shape=jax.ShapeDtypeStruct((n_layers, 8, n), F32),
        compiler_params=_params(2),
        name="mod_vectors",
    )(c8, w_mod, b_mod.reshape(n_layers, 1, n))


def _norm_mod_kernel(x_ref, sh_ref, sc_ref, h_ref):
    h_ref[...] = (_rms(x_ref[...]) * (1.0 + sc_ref[...]) + sh_ref[...]).astype(BF)


def _norm_router_kernel(x_ref, sh_ref, sc_ref, wr_ref, h_ref, aff_ref):
    h = (_rms(x_ref[...]) * (1.0 + sc_ref[...]) + sh_ref[...]).astype(BF)
    h_ref[...] = h
    logits = _bdot(h, wr_ref[...])
    e = jnp.exp(logits - jnp.max(logits, axis=-1, keepdims=True))
    aff_ref[...] = e / jnp.sum(e, axis=-1, keepdims=True)


def _batch_of_tile(n_vectors, tm, seq):
    if n_vectors == 1:
        return lambda i: 0
    assert seq % tm == 0
    return lambda i: (i * tm) // seq


def _norm_mod(x, shift, scale, seq, w_router=None, layer=0):
    t, d = x.shape
    tm = 256
    which = _batch_of_tile(shift.shape[0], tm, seq)
    vec = pl.BlockSpec((None, 1, d), lambda i: (which(i), 0, 0))
    x_spec = pl.BlockSpec((tm, d), lambda i: (i, 0))
    if w_router is None:
        return pl.pallas_call(
            _norm_mod_kernel, grid=(t // tm,),
            in_specs=[x_spec, vec, vec], out_specs=x_spec,
            out_shape=jax.ShapeDtypeStruct((t, d), BF),
            compiler_params=_params(1), name="norm_mod",
        )(x, shift, scale)
    ne = w_router.shape[-1]
    return pl.pallas_call(
        _norm_router_kernel, grid=(t // tm,),
        in_specs=[x_spec, vec, vec, pl.BlockSpec((None, d, ne), lambda i: (layer, 0, 0))],
        out_specs=[x_spec, pl.BlockSpec((tm, ne), lambda i: (i, 0))],
        out_shape=[jax.ShapeDtypeStruct((t, d), BF), jax.ShapeDtypeStruct((t, ne), F32)],
        compiler_params=_params(1), name="norm_router",
    )(x, shift, scale, w_router)


def _mm_kernel(*refs, k_sizes, residual):
    n_a = len(k_sizes)
    a_refs, w_ref = refs[:n_a], refs[n_a]
    o_ref, wbf_ref = refs[-2], refs[-1]

    @pl.when(pl.program_id(1) == 0)
    def _():
        wbf_ref[...] = w_ref[...].astype(BF)

    acc, off = None, 0
    for a_ref, ks in zip(a_refs, k_sizes):
        term = jnp.dot(a_ref[...].astype(BF), wbf_ref[off:off + ks, :], preferred_element_type=F32)
        acc = term if acc is None else acc + term
        off += ks
    if residual:
        x_ref, g_ref = refs[n_a + 1], refs[n_a + 2]
        acc = x_ref[...] + g_ref[...] * acc
    o_ref[...] = acc


def _mm(a_list, w3, layer, n_cols, tn, col_blk0=0, residual=None, name="mm"):
    m = a_list[0].shape[0]
    k_sizes = tuple(a.shape[1] for a in a_list)
    k = sum(k_sizes)
    assert w3.shape[1] == k and n_cols % tn == 0
    tm = min(512, m)
    in_specs = [pl.BlockSpec((tm, ks), lambda j, i: (i, 0)) for ks in k_sizes]
    in_specs.append(pl.BlockSpec((None, k, tn), lambda j, i: (layer, 0, j + col_blk0)))
    args = list(a_list) + [w3]
    if residual is not None:
        x, gate, seq = residual
        which = _batch_of_tile(gate.shape[0], tm, seq)
        in_specs.append(pl.BlockSpec((tm, tn), lambda j, i: (i, j)))
        in_specs.append(pl.BlockSpec((None, 1, tn), lambda j, i: (which(i), 0, j)))
        args += [x, gate]
    return pl.pallas_call(
        functools.partial(_mm_kernel, k_sizes=k_sizes, residual=residual is not None),
        grid=(n_cols // tn, m // tm),
        in_specs=in_specs,
        out_specs=pl.BlockSpec((tm, tn), lambda j, i: (i, j)),
        out_shape=jax.ShapeDtypeStruct((m, n_cols), F32),
        scratch_shapes=[pltpu.VMEM((k, tn), BF)],
        compiler_params=_params(2), name=name,
    )(*args)


def _norm_mm_kernel(x_ref, g_ref, w_ref, *out_refs, with_normed):
    wbf_ref = out_refs[-1]

    @pl.when(pl.program_id(0) == 0)
    def _():
        wbf_ref[...] = w_ref[...].astype(BF)

    xn = _rms(x_ref[...]) * g_ref[...]
    out_refs[0][...] = jnp.dot(xn.astype(BF), wbf_ref[...], preferred_element_type=F32)
    if with_normed:
        out_refs[1][...] = xn


def _norm_mm(x, col_blk, gain, w, with_normed, name):
    t = x.shape[0]
    k, n = w.shape
    tm = 512
    out_specs = [pl.BlockSpec((tm, n), lambda i: (i, 0))]
    out_shape = [jax.ShapeDtypeStruct((t, n), F32)]
    if with_normed:
        out_specs.append(pl.BlockSpec((tm, k), lambda i: (i, 0)))
        out_shape.append(jax.ShapeDtypeStruct((t, k), F32))
    outs = pl.pallas_call(
        functools.partial(_norm_mm_kernel, with_normed=with_normed),
        grid=(t // tm,),
        in_specs=[pl.BlockSpec((tm, k), lambda i: (i, col_blk)),
                  pl.BlockSpec((1, k), lambda i: (0, 0)),
                  pl.BlockSpec((k, n), lambda i: (0, 0))],
        out_specs=out_specs, out_shape=out_shape,
        scratch_shapes=[pltpu.VMEM((k, n), BF)],
        compiler_params=_params(1), name=name,
    )(x, gain.reshape(1, k), w)
    return outs if with_normed else outs[0]


def _mlstm_kernel(qf_ref, kf_ref, vf_ref, qb_ref, kb_ref, vb_ref, gcf_ref, gcb_ref, grf_ref, grb_ref,
                  bc_ref, br_ref, c0_ref, n0_ref, m0_ref, hf_ref, hb_ref, c_ref, n_ref, m_ref):
    @pl.when(pl.program_id(1) == 0)
    def _():
        c_ref[...] = c0_ref[...]
        n_ref[...] = n0_ref[...]
        m_ref[...] = m0_ref[...]

    nh, lc = ML_HEADS, ML_CHUNK
    row = lax.broadcasted_iota(jnp.int32, (lc, lc), 0)
    col = lax.broadcasted_iota(jnp.int32, (lc, lc), 1)
    lower = col <= row
    upper = col >= row
    directions = ((qf_ref, kf_ref, vf_ref, gcf_ref, grf_ref, hf_ref, lower, upper),
                  (qb_ref, kb_ref, vb_ref, gcb_ref, grb_ref, hb_ref, upper, lower))
    for d, (q_ref, k_ref, v_ref, gc_ref, gr_ref, h_ref, allowed, allowed_t) in enumerate(directions):
        g_col = gc_ref[:, GATE_LANE0:GATE_LANE0 + 4 * nh] + bc_ref[...]
        g_row = gr_ref[...] + br_ref[...]
        i_col = g_col[:, nh * d:nh * (d + 1)]
        f_col = _log_sigmoid(g_col[:, 2 * nh + nh * d:2 * nh + nh * (d + 1)])
        i_row = g_row[nh * d:nh * (d + 1), :]
        f_row = _log_sigmoid(g_row[2 * nh + nh * d:2 * nh + nh * (d + 1), :])
        for h in range(nh):
            s = d * nh + h
            q = q_ref[:, h * ML_QK:(h + 1) * ML_QK] * (ML_QK ** -0.5)
            k = k_ref[:, h * ML_QK:(h + 1) * ML_QK]
            v = v_ref[:, h * ML_V:(h + 1) * ML_V]
            ic_c, lf_c = i_col[:, h:h + 1], f_col[:, h:h + 1]
            ic_r, lf_r = i_row[h:h + 1, :], f_row[h:h + 1, :]
            c_prev = c_ref[d, h]
            n_prev = n_ref[s:s + 1, :]
            m_prev = m_ref[s:s + 1, :]
            b_c = jnp.sum(jnp.where(allowed, lf_r, 0.0), axis=1, keepdims=True)
            b_r = jnp.sum(jnp.where(allowed_t, lf_c, 0.0), axis=0, keepdims=True)
            dmat = jnp.where(allowed, b_c - b_r + ic_r, -jnp.inf)
            inter = b_c + m_prev
            m_t = jnp.maximum(inter, jnp.max(dmat, axis=1, keepdims=True))
            w_inter = jnp.exp(inter - m_t)
            sw = _bdot_nt(q, k) * jnp.exp(dmat - m_t)
            num = w_inter * _bdot(q, c_prev) + _bdot(sw, v)
            den = w_inter * jnp.sum(q * n_prev, axis=1, keepdims=True) + jnp.sum(sw, axis=1, keepdims=True)
            h_ref[:, h * ML_V:(h + 1) * ML_V] = num / jnp.maximum(jnp.abs(den), jnp.exp(-m_t))
            b_end = jnp.sum(lf_r, axis=1, keepdims=True)
            g_r = b_end - b_r + ic_r
            g_c = b_end - b_c + ic_c
            m_new = jnp.maximum(b_end + m_prev, jnp.max(g_r, axis=1, keepdims=True))
            kw = k * jnp.exp(g_c - m_new)
            decay = jnp.exp(b_end + m_prev - m_new)
            c_ref[d, h] = decay * c_prev + _bdot_tn(kw, v)
            n_ref[s:s + 1, :] = decay * n_prev + jnp.sum(kw, axis=0, keepdims=True)
            m_ref[s:s + 1, :] = m_new


def _mlstm(qkvo, side, g_row, b_col, b_row, c0, n0, m0, bsz, seq):
    t = qkvo.shape[0]
    nc = seq // ML_CHUNK
    lc = ML_CHUNK
    nq = ML_HEADS * ML_QK
    nv = ML_HEADS * ML_V
    side_blk = (SIDE_COLS - 128) // 128
    fwd = lambda b, c: b * nc + c
    bwd = lambda b, c: b * nc + (nc - 1 - c)

    def specs(pos):
        return [pl.BlockSpec((lc, nq), lambda b, c: (pos(b, c), 0)),
                pl.BlockSpec((lc, nq), lambda b, c: (pos(b, c), 1)),
                pl.BlockSpec((lc, nv), lambda b, c: (pos(b, c), 1))]

    state = lambda *shape: pl.BlockSpec((None,) + shape, lambda b, c: (b,) + (0,) * len(shape))
    in_specs = specs(fwd) + specs(bwd) + [
        pl.BlockSpec((lc, 128), lambda b, c: (fwd(b, c), side_blk)),
        pl.BlockSpec((lc, 128), lambda b, c: (bwd(b, c), side_blk)),
        pl.BlockSpec((None, 4 * ML_HEADS, lc), lambda b, c: (fwd(b, c), 0, 0)),
        pl.BlockSpec((None, 4 * ML_HEADS, lc), lambda b, c: (bwd(b, c), 0, 0)),
        pl.BlockSpec((1, 4 * ML_HEADS), lambda b, c: (0, 0)),
        pl.BlockSpec((4 * ML_HEADS, 1), lambda b, c: (0, 0)),
        state(2, ML_HEADS, ML_QK, ML_V), state(2 * ML_HEADS, ML_QK), state(2 * ML_HEADS, 1)]
    out_specs = [pl.BlockSpec((lc, nv), lambda b, c: (fwd(b, c), 0)),
                 pl.BlockSpec((lc, nv), lambda b, c: (bwd(b, c), 0)),
                 state(2, ML_HEADS, ML_QK, ML_V), state(2 * ML_HEADS, ML_QK), state(2 * ML_HEADS, 1)]
    out_shape = [jax.ShapeDtypeStruct((t, nv), F32), jax.ShapeDtypeStruct((t, nv), F32),
                 jax.ShapeDtypeStruct((bsz, 2, ML_HEADS, ML_QK, ML_V), F32),
                 jax.ShapeDtypeStruct((bsz, 2 * ML_HEADS, ML_QK), F32),
                 jax.ShapeDtypeStruct((bsz, 2 * ML_HEADS, 1), F32)]
    return pl.pallas_call(
        _mlstm_kernel, grid=(bsz, nc), in_specs=in_specs, out_specs=out_specs, out_shape=out_shape,
        compiler_params=_params(2), name="mlstm",
    )(qkvo, qkvo, qkvo, qkvo, qkvo, qkvo, side, side, g_row, g_row, b_col, b_row, c0, n0, m0)


def _mlstm_post_kernel(hf_ref, hb_ref, o_ref, g_ref, y_ref):
    for h in range(ML_HEADS):
        sl = slice(h * ML_V, (h + 1) * ML_V)
        hn = _rms(hf_ref[:, sl] + hb_ref[:, sl]) * g_ref[:, sl]
        y_ref[:, sl] = (hn * _sigmoid(o_ref[:, sl])).astype(BF)


def _mlstm_post(hf, hb, qkvo, gain):
    t, nv = hf.shape
    tm = 512
    blk = pl.BlockSpec((tm, nv), lambda i: (i, 0))
    return pl.pallas_call(
        _mlstm_post_kernel, grid=(t // tm,),
        in_specs=[blk, blk, pl.BlockSpec((tm, nv), lambda i: (i, 2)), pl.BlockSpec((1, nv), lambda i: (0, 0))],
        out_specs=blk, out_shape=jax.ShapeDtypeStruct((t, nv), BF),
        compiler_params=_params(1), name="mlstm_post",
    )(hf, hb, qkvo, gain.reshape(1, nv))


def _softmax_pair(s1, s2):
    m = jnp.max(s1, axis=-1, keepdims=True)
    if s2 is not None:
        m = jnp.maximum(m, jnp.max(s2, axis=-1, keepdims=True))
    e1 = jnp.exp(s1 - m)
    den = jnp.sum(e1, axis=-1, keepdims=True)
    if s2 is None:
        return e1 / den, None
    e2 = jnp.exp(s2 - m)
    den = den + jnp.sum(e2, axis=-1, keepdims=True)
    return e1 / den, e2 / den


def _mla_kernel(*refs, with_ctx):
    if with_ctx:
        qa_ref, kv_ref, side_ref, cq_ref, sq_ref, ck_ref, sk_ref, kvc_ref, krc_ref, o_ref = refs
    else:
        qa_ref, kv_ref, side_ref, o_ref = refs
    scale = (MLA_NOPE + MLA_ROPE) ** -0.5
    nope_cols = MLA_HEADS * MLA_NOPE
    q_rope = qa_ref[:, nope_cols:]
    k_rope = side_ref[...]
    if with_ctx:
        q_rope_rot = _rope(q_rope, cq_ref[...], sq_ref[...], MLA_ROPE // 4)
        k_rope_rot = _rope(k_rope, ck_ref[...], sk_ref[...], MLA_ROPE // 4)[:, :MLA_ROPE]
        k_rope_ctx = krc_ref[...]
    else:
        q_rope_rot = q_rope
        k_rope_rot = k_rope[:, :MLA_ROPE]
    for h in range(MLA_HEADS):
        q_n = qa_ref[:, h * MLA_NOPE:(h + 1) * MLA_NOPE]
        rs = slice(h * MLA_ROPE, (h + 1) * MLA_ROPE)
        kv0 = h * (MLA_NOPE + MLA_V)
        k_n = kv_ref[:, kv0:kv0 + MLA_NOPE]
        v = kv_ref[:, kv0 + MLA_NOPE:kv0 + MLA_NOPE + MLA_V]
        s1 = (_bdot_nt(q_n, k_n) + _bdot_nt(q_rope_rot[:, rs], k_rope_rot)) * scale
        if with_ctx:
            s2 = (_bdot_nt(q_n, kvc_ref[:, kv0:kv0 + MLA_NOPE]) + _bdot_nt(q_rope[:, rs], k_rope_ctx)) * scale
        else:
            s2 = None
        p1, p2 = _softmax_pair(s1, s2)
        out = _bdot(p1, v)
        if with_ctx:
            out = out + _bdot(p2, kvc_ref[:, kv0 + MLA_NOPE:kv0 + MLA_NOPE + MLA_V])
        o_ref[:, h * MLA_V:(h + 1) * MLA_V] = out.astype(BF)


def _mla_attention(qa, kv, side, bsz, seq, ctx=None):
    t = qa.shape[0]
    tq = 256
    nq = seq // tq
    side_blk = (SIDE_COLS - 128) // 128
    nkv = MLA_HEADS * (MLA_NOPE + MLA_V)
    in_specs = [pl.BlockSpec((tq, qa.shape[1]), lambda b, i: (b * nq + i, 0)),
                pl.BlockSpec((seq, nkv), lambda b, i: (b, 0)),
                pl.BlockSpec((seq, 128), lambda b, i: (b, side_blk))]
    args = [qa, kv, side]
    if ctx is not None:
        cos_q, sin_q, cos_k, sin_k, kvc, krc = ctx
        past = krc.shape[1]
        in_specs += [pl.BlockSpec((tq, cos_q.shape[1]), lambda b, i: (i, 0)),
                     pl.BlockSpec((tq, cos_q.shape[1]), lambda b, i: (i, 0)),
                     pl.BlockSpec((seq, 128), lambda b, i: (0, 0)),
                     pl.BlockSpec((seq, 128), lambda b, i: (0, 0)),
                     pl.BlockSpec((past, nkv), lambda b, i: (b, 0)),
                     pl.BlockSpec((None, past, MLA_ROPE), lambda b, i: (b, 0, 0))]
        args += [cos_q, sin_q, cos_k, sin_k, kvc, krc]
    nout = MLA_HEADS * MLA_V
    return pl.pallas_call(
        functools.partial(_mla_kernel, with_ctx=ctx is not None),
        grid=(bsz, nq), in_specs=in_specs,
        out_specs=pl.BlockSpec((tq, nout), lambda b, i: (b * nq + i, 0)),
        out_shape=jax.ShapeDtypeStruct((t, nout), BF),
        compiler_params=_params(2), name="mla_attention",
    )(*args)


def _gqa_kernel(*refs, with_ctx):
    if with_ctx:
        q_ref, k_ref, v_ref, gq_ref, gk_ref, cq_ref, sq_ref, ck_ref, sk_ref, kc_ref, vc_ref, o_ref = refs
    else:
        q_ref, k_ref, v_ref, gq_ref, gk_ref, o_ref, kn_ref, vo_ref = refs
    hd = GQA_HEAD_DIM
    scale = hd ** -0.5
    k_n = _rms(k_ref[...]) * gk_ref[...]
    v = v_ref[...]
    if with_ctx:
        k_src = _rope(k_n, ck_ref[...], sk_ref[...], hd // 4)
    else:
        k_src = k_n

        @pl.when(pl.program_id(2) == 0)
        def _():
            kn_ref[...] = k_n
            vo_ref[...] = v

    for r in range(GQA_HEADS // GQA_KV_HEADS):
        sl = slice(r * hd, (r + 1) * hd)
        q_n = _rms(q_ref[:, sl]) * gq_ref[...]
        if with_ctx:
            s1 = _bdot_nt(_rope(q_n, cq_ref[...], sq_ref[...], hd // 4), k_src) * scale
            s2 = _bdot_nt(q_n, kc_ref[...]) * scale
        else:
            s1, s2 = _bdot_nt(q_n, k_src) * scale, None
        p1, p2 = _softmax_pair(s1, s2)
        out = _bdot(p1, v)
        if with_ctx:
            out = out + _bdot(p2, vc_ref[...])
        o_ref[:, sl] = out.astype(BF)


def _gqa_attention(qkv, g_q, g_k, bsz, seq, ctx=None):
    t = qkv.shape[0]
    hd = GQA_HEAD_DIM
    rep = GQA_HEADS // GQA_KV_HEADS
    tq = 256
    nq = seq // tq
    in_specs = [pl.BlockSpec((tq, rep * hd), lambda b, g, i: (b * nq + i, g)),
                pl.BlockSpec((seq, hd), lambda b, g, i: (b, GQA_HEADS + g)),
                pl.BlockSpec((seq, hd), lambda b, g, i: (b, GQA_HEADS + GQA_KV_HEADS + g)),
                pl.BlockSpec((1, hd), lambda b, g, i: (0, 0)),
                pl.BlockSpec((1, hd), lambda b, g, i: (0, 0))]
    args = [qkv, qkv, qkv, g_q.reshape(1, hd), g_k.reshape(1, hd)]
    o_spec = pl.BlockSpec((tq, rep * hd), lambda b, g, i: (b * nq + i, g))
    o_shape = jax.ShapeDtypeStruct((t, GQA_HEADS * hd), BF)
    if ctx is not None:
        cos_t, sin_t, kc, vc, j = ctx
        past = kc.shape[3]
        cache = pl.BlockSpec((None, None, None, past, hd), lambda b, g, i: (b, j, g, 0, 0))
        in_specs += [pl.BlockSpec((tq, hd), lambda b, g, i: (i, 0)),
                     pl.BlockSpec((tq, hd), lambda b, g, i: (i, 0)),
                     pl.BlockSpec((seq, hd), lambda b, g, i: (0, 0)),
                     pl.BlockSpec((seq, hd), lambda b, g, i: (0, 0)),
                     cache, cache]
        args += [cos_t, sin_t, cos_t, sin_t, kc, vc]
        out_specs, out_shape = o_spec, o_shape
    else:
        head_major = pl.BlockSpec((None, None, seq, hd), lambda b, g, i: (b, g, 0, 0))
        out_specs = [o_spec, head_major, head_major]
        kv_shape = jax.ShapeDtypeStruct((bsz, GQA_KV_HEADS, seq, hd), F32)
        out_shape = [o_shape, kv_shape, kv_shape]
    return pl.pallas_call(
        functools.partial(_gqa_kernel, with_ctx=ctx is not None),
        grid=(bsz, GQA_KV_HEADS, nq), in_specs=in_specs, out_specs=out_specs, out_shape=out_shape,
        compiler_params=_params(3), name="gqa_attention",
    )(*args)


def _dispatch_kernel(ar_ref, ac_ref, h_ref, xe_ref, gate_ref, rank_ref, *, cap):
    a_row = ar_ref[...]
    a_col = ac_ref[...]
    s = a_row.shape[1]
    i0 = lax.broadcasted_iota(jnp.int32, (s, s), 0)
    i1 = lax.broadcasted_iota(jnp.int32, (s, s), 1)
    before_row = (a_col > a_row) | ((a_col == a_row) & (i0 < i1))
    rank_row = jnp.sum(jnp.where(before_row, 1.0, 0.0), axis=0, keepdims=True)
    before_col = (a_row > a_col) | ((a_row == a_col) & (i1 < i0))
    rank_ref[...] = jnp.sum(jnp.where(before_col, 1.0, 0.0), axis=1, keepdims=True)
    slot = lax.broadcasted_iota(jnp.int32, (cap, s), 0).astype(F32)
    pick = rank_row == slot
    xe_ref[...] = jnp.dot(jnp.where(pick, 1.0, 0.0).astype(BF), h_ref[...],
                          preferred_element_type=F32).astype(BF)
    gate_ref[...] = jnp.sum(jnp.where(pick, a_row, 0.0), axis=1, keepdims=True)


def _dispatch(aff_row, aff_col, h, bsz, seq):
    t, d = h.shape
    ne = aff_row.shape[0]
    cap = EC_FACTOR * seq // ne
    return pl.pallas_call(
        functools.partial(_dispatch_kernel, cap=cap),
        grid=(bsz, ne),
        in_specs=[pl.BlockSpec((None, 1, seq), lambda b, e: (e, 0, b)),
                  pl.BlockSpec((None, seq, 1), lambda b, e: (e, b, 0)),
                  pl.BlockSpec((seq, d), lambda b, e: (b, 0))],
        out_specs=[pl.BlockSpec((None, cap, d), lambda b, e: (e, b, 0)),
                   pl.BlockSpec((None, cap, 1), lambda b, e: (e, b, 0)),
                   pl.BlockSpec((None, seq, 1), lambda b, e: (e, b, 0))],
        out_shape=[jax.ShapeDtypeStruct((ne, bsz * cap, d), BF),
                   jax.ShapeDtypeStruct((ne, bsz * cap, 1), F32),
                   jax.ShapeDtypeStruct((ne, t, 1), F32)],
        compiler_params=_params(2), name="ec_dispatch",
    )(aff_row, aff_col, h)


def _experts_kernel(xp_ref, xs_ref, gp_ref, gs_ref, wg_ref, wu_ref, wd_ref, yp_ref, ys_ref):
    f = pl.program_id(1)
    wg = wg_ref[...].astype(BF)
    wu = wu_ref[...].astype(BF)
    wd = wd_ref[...].astype(BF)
    for x_ref, g_ref, y_ref in ((xp_ref, gp_ref, yp_ref), (xs_ref, gs_ref, ys_ref)):
        x = x_ref[...]
        a = jnp.dot(x, wg, preferred_element_type=F32)
        u = jnp.dot(x, wu, preferred_element_type=F32)
        part = jnp.dot((a * _sigmoid(a) * u).astype(BF), wd, preferred_element_type=F32)

        @pl.when(f == 0)
        def _():
            y_ref[...] = part

        @pl.when(f > 0)
        def _():
            y_ref[...] += part

        @pl.when(f == pl.num_programs(1) - 1)
        def _():
            y_ref[...] = y_ref[...] * g_ref[...]


def _experts(xe_p, xe_s, gate_p, gate_s, w_gate, w_up, w_down, layer):
    ne, mp, d = xe_p.shape
    ms = xe_s.shape[1]
    fdim = w_gate.shape[-1]
    tf = 256
    rows = lambda m, last: pl.BlockSpec((None, m, last), lambda e, f: (e, 0, 0))
    return pl.pallas_call(
        _experts_kernel, grid=(ne, fdim // tf),
        in_specs=[rows(mp, d), rows(ms, d), rows(mp, 1), rows(ms, 1),
                  pl.BlockSpec((None, None, d, tf), lambda e, f: (layer, e, 0, f)),
                  pl.BlockSpec((None, None, d, tf), lambda e, f: (layer, e, 0, f)),
                  pl.BlockSpec((None, None, tf, d), lambda e, f: (layer, e, f, 0))],
        out_specs=[rows(mp, d), rows(ms, d)],
        out_shape=[jax.ShapeDtypeStruct((ne, mp, d), F32), jax.ShapeDtypeStruct((ne, ms, d), F32)],
        compiler_params=_params(2), name="ec_experts",
    )(xe_p, xe_s, gate_p, gate_s, w_gate, w_up, w_down)


def _combine_kernel(y_ref, rank_ref, x_ref, g_ref, o_ref, *, cap):
    s = x_ref.shape[0]
    slot = lax.broadcasted_iota(jnp.int32, (s, cap), 1).astype(F32)
    acc = jnp.zeros(x_ref.shape, F32)
    for e in range(y_ref.shape[0]):
        place = jnp.where(rank_ref[e] == slot, 1.0, 0.0).astype(BF)
        y = y_ref[e]
        y_hi = y.astype(BF)
        y_lo = (y - y_hi.astype(F32)).astype(BF)
        acc = acc + jnp.dot(place, y_hi, preferred_element_type=F32)
        acc = acc + jnp.dot(place, y_lo, preferred_element_type=F32)
    o_ref[...] = x_ref[...] + g_ref[...] * acc


def _combine(y, rank, x, gate, bsz, seq):
    t, d = x.shape
    ne = y.shape[0]
    cap = EC_FACTOR * seq // ne
    td = 512
    return pl.pallas_call(
        functools.partial(_combine_kernel, cap=cap),
        grid=(bsz, d // td),
        in_specs=[pl.BlockSpec((ne, cap, td), lambda b, j: (0, b, j)),
                  pl.BlockSpec((ne, seq, 1), lambda b, j: (0, b, 0)),
                  pl.BlockSpec((seq, td), lambda b, j: (b, j)),
                  pl.BlockSpec((None, 1, td), lambda b, j: (b if gate.shape[0] > 1 else 0, 0, j))],
        out_specs=pl.BlockSpec((seq, td), lambda b, j: (b, j)),
        out_shape=jax.ShapeDtypeStruct((t, d), F32),
        compiler_params=_params(2), name="ec_combine",
    )(y, rank, x, gate)


def _final_norm_kernel(x_ref, g_ref, o_ref):
    o_ref[...] = _rms(x_ref[...]) * g_ref[...]


def _final_norm(x, gain):
    t, d = x.shape
    tm = 256
    blk = pl.BlockSpec((tm, d), lambda i: (i, 0))
    return pl.pallas_call(
        _final_norm_kernel, grid=(t // tm,),
        in_specs=[blk, pl.BlockSpec((1, d), lambda i: (0, 0))],
        out_specs=blk, out_shape=jax.ShapeDtypeStruct((t, d), F32),
        compiler_params=_params(1), name="final_norm",
    )(x, gain.reshape(1, d))


def _rope_tables(n_tokens, rot_dim):
    rows = n_tokens // GRID_W
    row = jnp.repeat(jnp.arange(rows), GRID_W).astype(F32)
    col = jnp.tile(jnp.arange(GRID_W), rows).astype(F32)
    quarter = rot_dim // 4
    inv = ROPE_THETA ** (-jnp.arange(quarter, dtype=F32) / quarter)
    a_row, a_col = row[:, None] * inv, col[:, None] * inv
    cos_t = jnp.concatenate([jnp.cos(a_row), jnp.cos(a_row), jnp.cos(a_col), jnp.cos(a_col)], axis=-1)
    sin_t = jnp.concatenate([-jnp.sin(a_row), jnp.sin(a_row), -jnp.sin(a_col), jnp.sin(a_col)], axis=-1)
    return cos_t, sin_t


def kernel(x_prompt, x_sample, state_mlstm_c, state_mlstm_n, state_mlstm_m, cache_mla_ckv, cache_mla_krope,
           cache_gqa_k, cache_gqa_v, c, c_ctx, w_mod, b_mod, w_in_even, b_igate, b_fgate, g_mlstm, g_cq, w_uq,
           g_ckv, w_ukv, w_out_even, w_in_odd, g_qnorm, g_knorm, w_out_odd, w_router, w_expert_gate,
           w_expert_up, w_expert_down, g_final):
    d = D_MODEL
    bp, sp, _ = x_prompt.shape
    bs, ss, _ = x_sample.shape
    depth = w_mod.shape[0]
    nh = ML_HEADS
    streams = {"p": (bp, sp), "s": (bs, ss)}
    x = {"p": x_prompt.reshape(bp * sp, d), "s": x_sample.reshape(bs * ss, d)}

    c8 = jnp.concatenate([c_ctx[None], c, jnp.zeros((8 - 1 - bs, d), F32)], axis=0)
    mod_all = _mod_vectors(c8, w_mod, b_mod).reshape(depth, 8, 6, 1, d)

    def mod(layer, key, idx):
        rows = mod_all[layer, 0:1, idx] if key == "p" else mod_all[layer, 1:1 + bs, idx]
        return rows

    new_even, new_odd = [], []
    for layer in range(depth):
        j = layer // 2
        if layer % 2 == 0:
            w_in = w_in_even[j]
            w_side = jnp.concatenate(
                [w_in[:, QKVO_COLS + 4 * nh:],
                 w_in[:, QKVO_COLS:QKVO_COLS + 4 * nh],
                 jnp.zeros((d, 128 - MLA_ROPE - 4 * nh), F32)], axis=1)[None]
            w_q = w_uq[j].reshape(MLA_Q_RANK, MLA_HEADS, MLA_NOPE + MLA_ROPE)
            w_q = jnp.concatenate([w_q[:, :, :MLA_NOPE].reshape(MLA_Q_RANK, -1),
                                   w_q[:, :, MLA_NOPE:].reshape(MLA_Q_RANK, -1)], axis=1)
            bias_col = jnp.concatenate([b_igate[j].reshape(1, -1), b_fgate[j].reshape(1, -1)], axis=1)
            bias_row = bias_col.reshape(-1, 1)
            cos64, sin64 = _rope_tables(ss, MLA_ROPE)
            cos_q, sin_q = jnp.tile(cos64, (1, MLA_HEADS)), jnp.tile(sin64, (1, MLA_HEADS))
            pad = jnp.zeros((ss, 128 - MLA_ROPE), F32)
            cos_k, sin_k = jnp.concatenate([cos64, pad], axis=1), jnp.concatenate([sin64, pad], axis=1)
            kvc = _mm([cache_mla_ckv[:, j].reshape(-1, MLA_KV_RANK)], w_ukv, j, w_ukv.shape[-1], 512,
                      name="mla_ctx_expand")
            for key, (bsz, seq) in streams.items():
                h = _norm_mod(x[key], mod(layer, key, 0), mod(layer, key, 1), seq)
                qkvo = _mm([h], w_in_even, j, QKVO_COLS, 512, name="even_in_main")
                side = _mm([h], w_side, 0, SIDE_COLS, SIDE_COLS // 3, name="even_in_side")
                gates = side[:, SIDE_COLS - 128 + GATE_LANE0:SIDE_COLS - 128 + GATE_LANE0 + 4 * nh]
                g_row = gates.reshape(-1, ML_CHUNK, 4 * nh).transpose(0, 2, 1)
                if key == "p":
                    c0 = jnp.zeros((bsz, 2, nh, ML_QK, ML_V), F32)
                    n0 = jnp.zeros((bsz, 2 * nh, ML_QK), F32)
                    m0 = jnp.zeros((bsz, 2 * nh, 1), F32)
                else:
                    c0 = state_mlstm_c[:, j]
                    n0 = state_mlstm_n[:, j].reshape(bsz, 2 * nh, ML_QK)
                    m0 = state_mlstm_m[:, j].reshape(bsz, 2 * nh, 1)
                hf, hb, c_fin, n_fin, m_fin = _mlstm(qkvo, side, g_row, bias_col, bias_row, c0, n0, m0, bsz, seq)
                y_ml = _mlstm_post(hf, hb, qkvo, g_mlstm[j])
                qa = _norm_mm(side, 0, g_cq[j], w_q, False, "mla_q_up")
                kv, ckv_n = _norm_mm(side, 1, g_ckv[j], w_ukv[j], True, "mla_kv_up")
                if key == "p":
                    y_a = _mla_attention(qa, kv, side, bsz, seq)
                    k_rope = side[:, SIDE_COLS - 128:SIDE_COLS - 128 + MLA_ROPE]
                    new_even.append((c_fin, n_fin.reshape(bsz, 2, nh, ML_QK), m_fin.reshape(bsz, 2, nh),
                                     ckv_n.reshape(bsz, seq, -1), k_rope.reshape(bsz, seq, -1)))
                else:
                    y_a = _mla_attention(qa, kv, side, bsz, seq,
                                         ctx=(cos_q, sin_q, cos_k, sin_k, kvc, cache_mla_krope[:, j]))
                x[key] = _mm([y_ml, y_a], w_out_even, j, d, 512,
                             residual=(x[key], mod(layer, key, 2), seq), name="even_out")
        else:
            cos_t, sin_t = _rope_tables(ss, GQA_HEAD_DIM)
            for key, (bsz, seq) in streams.items():
                h = _norm_mod(x[key], mod(layer, key, 0), mod(layer, key, 1), seq)
                qkv = _mm([h], w_in_odd, j, w_in_odd.shape[-1], 512, name="odd_in")
                if key == "p":
                    o, k_n, v = _gqa_attention(qkv, g_qnorm[j], g_knorm[j], bsz, seq)
                    new_odd.append((k_n, v))
                else:
                    o = _gqa_attention(qkv, g_qnorm[j], g_knorm[j], bsz, seq,
                                       ctx=(cos_t, sin_t, cache_gqa_k, cache_gqa_v, j))
                x[key] = _mm([o], w_out_odd, j, d, 512,
                             residual=(x[key], mod(layer, key, 2), seq), name="odd_out")
        routed = {}
        for key, (bsz, seq) in streams.items():
            h, aff = _norm_mod(x[key], mod(layer, key, 3), mod(layer, key, 4), seq, w_router=w_router, layer=layer)
            aff_t = aff.T
            routed[key] = _dispatch(aff_t[:, None, :], aff_t[:, :, None], h, bsz, seq)
        y_p, y_s = _experts(routed["p"][0], routed["s"][0], routed["p"][1], routed["s"][1],
                            w_expert_gate, w_expert_up, w_expert_down, layer)
        for key, y in (("p", y_p), ("s", y_s)):
            bsz, seq = streams[key]
            x[key] = _combine(y, routed[key][2], x[key], mod(layer, key, 5), bsz, seq)

    y_prompt = _final_norm(x["p"], g_final).reshape(bp, sp, d)
    y_sample = _final_norm(x["s"], g_final).reshape(bs, ss, d)
    new_c = jnp.stack([e[0] for e in new_even], axis=1)
    new_n = jnp.stack([e[1] for e in new_even], axis=1)
    new_m = jnp.stack([e[2] for e in new_even], axis=1)
    new_ckv = jnp.stack([e[3] for e in new_even], axis=1)
    new_krope = jnp.stack([e[4] for e in new_even], axis=1)
    new_k = jnp.stack([e[0] for e in new_odd], axis=1)
    new_v = jnp.stack([e[1] for e in new_odd], axis=1)
    return (y_prompt, y_sample, new_c, new_n, new_m, new_ckv, new_krope, new_k, new_v)
```

```python
import functools

import jax
import jax.numpy as jnp
from jax import lax
from jax.experimental import pallas as pl
from jax.experimental.pallas import tpu as pltpu

BF = jnp.bfloat16
F32 = jnp.float32

D_MODEL = 2048
GRID_W = 64
ROPE_THETA = 10000.0
NORM_EPS = 1e-6
ML_HEADS = 8
ML_QK = 64
ML_V = 128
ML_CHUNK = 64
MLA_HEADS = 8
MLA_Q_RANK = 512
MLA_KV_RANK = 512
MLA_NOPE = 128
MLA_ROPE = 64
MLA_V = 128
GQA_HEADS = 16
GQA_KV_HEADS = 4
GQA_HEAD_DIM = 128
N_EXPERTS = 16
EXPERT_DIM = 1024
EC_FACTOR = 2

QKVO_COLS = 2 * ML_HEADS * ML_QK + 2 * ML_HEADS * ML_V
SIDE_COLS = MLA_Q_RANK + MLA_KV_RANK + 128
GATE_LANE0 = MLA_ROPE

VMEM_LIMIT_BYTES = 56 * 1024 * 1024


def _params(n_axes):
    return pltpu.CompilerParams(dimension_semantics=("arbitrary",) * n_axes,
                                vmem_limit_bytes=VMEM_LIMIT_BYTES)


def _bdot(a, b):
    return jnp.dot(a.astype(BF), b.astype(BF), preferred_element_type=F32)


def _bdot_nt(a, b):
    return lax.dot_general(a.astype(BF), b.astype(BF), (((1,), (1,)), ((), ())),
                           preferred_element_type=F32)


def _bdot_tn(a, b):
    return lax.dot_general(a.astype(BF), b.astype(BF), (((0,), (0,)), ((), ())),
                           preferred_element_type=F32)


def _sigmoid(x):
    return 1.0 / (1.0 + jnp.exp(-x))


def _log_sigmoid(x):
    return jnp.minimum(x, 0.0) - jnp.log1p(jnp.exp(-jnp.abs(x)))


def _rms(x):
    return x * lax.rsqrt(jnp.mean(x * x, axis=-1, keepdims=True) + NORM_EPS)


def _rope(x, cos_t, sin_t, quarter):
    width = x.shape[-1]
    axis = x.ndim - 1
    lane = lax.broadcasted_iota(jnp.int32, x.shape, axis)
    partner = jnp.where((lane & quarter) == 0,
                        pltpu.roll(x, width - quarter, axis=axis),
                        pltpu.roll(x, quarter, axis=axis))
    return x * cos_t + partner * sin_t


def _mod_kernel(c_ref, w_ref, b_ref, o_ref):
    c = c_ref[...]
    o_ref[...] = _bdot(c * _sigmoid(c), w_ref[...]) + b_ref[...]


def _mod_vectors(c8, w_mod, b_mod):
    n_layers, k, n = w_mod.shape
    tn = 1024
    return pl.pallas_call(
        _mod_kernel,
        grid=(n_layers, n // tn),
        in_specs=[pl.BlockSpec((8, k), lambda l, j: (0, 0)),
                  pl.BlockSpec((None, k, tn), lambda l, j: (l, 0, j)),
                  pl.BlockSpec((None, 1, tn), lambda l, j: (l, 0, j))],
        out_specs=pl.BlockSpec((None, 8, tn), lambda l, j: (l, 0, j)),
        out_shape=jax.ShapeDtypeStruct((n_layers, 8, n), F32),
        compiler_params=_params(2),
        name="mod_vectors",
    )(c8, w_mod, b_mod.reshape(n_layers, 1, n))


def _norm_mod_kernel(x_ref, sh_ref, sc_ref, h_ref):
    h_ref[...] = (_rms(x_ref[...]) * (1.0 + sc_ref[...]) + sh_ref[...]).astype(BF)


def _norm_router_kernel(x_ref, sh_ref, sc_ref, wr_ref, h_ref, aff_ref):
    h = (_rms(x_ref[...]) * (1.0 + sc_ref[...]) + sh_ref[...]).astype(BF)
    h_ref[...] = h
    logits = _bdot(h, wr_ref[...])
    e = jnp.exp(logits - jnp.max(logits, axis=-1, keepdims=True))
    aff_ref[...] = e / jnp.sum(e, axis=-1, keepdims=True)


def _batch_of_tile(n_vectors, tm, seq):
    if n_vectors == 1:
        return lambda i: 0
    assert seq % tm == 0
    return lambda i: (i * tm) // seq


def _norm_mod(x, shift, scale, seq, w_router=None, layer=0):
    t, d = x.shape
    tm = 256
    which = _batch_of_tile(shift.shape[0], tm, seq)
    vec = pl.BlockSpec((None, 1, d), lambda i: (which(i), 0, 0))
    x_spec = pl.BlockSpec((tm, d), lambda i: (i, 0))
    if w_router is None:
        return pl.pallas_call(
            _norm_mod_kernel, grid=(t // tm,),
            in_specs=[x_spec, vec, vec], out_specs=x_spec,
            out_shape=jax.ShapeDtypeStruct((t, d), BF),
            compiler_params=_params(1), name="norm_mod",
        )(x, shift, scale)
    ne = w_router.shape[-1]
    return pl.pallas_call(
        _norm_router_kernel, grid=(t // tm,),
        in_specs=[x_spec, vec, vec, pl.BlockSpec((None, d, ne), lambda i: (layer, 0, 0))],
        out_specs=[x_spec, pl.BlockSpec((tm, ne), lambda i: (i, 0))],
        out_shape=[jax.ShapeDtypeStruct((t, d), BF), jax.ShapeDtypeStruct((t, ne), F32)],
        compiler_params=_params(1), name="norm_router",
    )(x, shift, scale, w_router)


def _mm_kernel(*refs, k_sizes, residual):
    n_a = len(k_sizes)
    a_refs, w_ref = refs[:n_a], refs[n_a]
    o_ref, wbf_ref = refs[-2], refs[-1]

    @pl.when(pl.program_id(1) == 0)
    def _():
        wbf_ref[...] = w_ref[...].astype(BF)

    acc, off = None, 0
    for a_ref, ks in zip(a_refs, k_sizes):
        term = jnp.dot(a_ref[...].astype(BF), wbf_ref[off:off + ks, :], preferred_element_type=F32)
        acc = term if acc is None else acc + term
        off += ks
    if residual:
        x_ref, g_ref = refs[n_a + 1], refs[n_a + 2]
        acc = x_ref[...] + g_ref[...] * acc
    o_ref[...] = acc


def _mm(a_list, w3, layer, n_cols, tn, col_blk0=0, residual=None, name="mm"):
    m = a_list[0].shape[0]
    k_sizes = tuple(a.shape[1] for a in a_list)
    k = sum(k_sizes)
    assert w3.shape[1] == k and n_cols % tn == 0
    tm = min(512, m)
    in_specs = [pl.BlockSpec((tm, ks), lambda j, i: (i, 0)) for ks in k_sizes]
    in_specs.append(pl.BlockSpec((None, k, tn), lambda j, i: (layer, 0, j + col_blk0)))
    args = list(a_list) + [w3]
    if residual is not None:
        x, gate, seq = residual
        which = _batch_of_tile(gate.shape[0], tm, seq)
        in_specs.append(pl.BlockSpec((tm, tn), lambda j, i: (i, j)))
        in_specs.append(pl.BlockSpec((None, 1, tn), lambda j, i: (which(i), 0, j)))
        args += [x, gate]
    return pl.pallas_call(
        functools.partial(_mm_kernel, k_sizes=k_sizes, residual=residual is not None),
        grid=(n_cols // tn, m // tm),
        in_specs=in_specs,
        out_specs=pl.BlockSpec((tm, tn), lambda j, i: (i, j)),
        out_shape=jax.ShapeDtypeStruct((m, n_cols), F32),
        scratch_shapes=[pltpu.VMEM((k, tn), BF)],
        compiler_params=_params(2), name=name,
    )(*args)


def _norm_mm_kernel(x_ref, g_ref, w_ref, *out_refs, with_normed):
    wbf_ref = out_refs[-1]

    @pl.when(pl.program_id(0) == 0)
    def _():
        wbf_ref[...] = w_ref[...].astype(BF)

    xn = _rms(x_ref[...]) * g_ref[...]
    out_refs[0][...] = jnp.dot(xn.astype(BF), wbf_ref[...], preferred_element_type=F32)
    if with_normed:
        out_refs[1][...] = xn


def _norm_mm(x, col_blk, gain, w, with_normed, name):
    t = x.shape[0]
    k, n = w.shape
    tm = 512
    out_specs = [pl.BlockSpec((tm, n), lambda i: (i, 0))]
    out_shape = [jax.ShapeDtypeStruct((t, n), F32)]
    if with_normed:
        out_specs.append(pl.BlockSpec((tm, k), lambda i: (i, 0)))
        out_shape.append(jax.ShapeDtypeStruct((t, k), F32))
    outs = pl.pallas_call(
        functools.partial(_norm_mm_kernel, with_normed=with_normed),
        grid=(t // tm,),
        in_specs=[pl.BlockSpec((tm, k), lambda i: (i, col_blk)),
                  pl.BlockSpec((1, k), lambda i: (0, 0)),
                  pl.BlockSpec((k, n), lambda i: (0, 0))],
        out_specs=out_specs, out_shape=out_shape,
        scratch_shapes=[pltpu.VMEM((k, n), BF)],
        compiler_params=_params(1), name=name,
    )(x, gain.reshape(1, k), w)
    return outs if with_normed else outs[0]


def _mlstm_kernel(qf_ref, kf_ref, vf_ref, qb_ref, kb_ref, vb_ref, gcf_ref, gcb_ref, grf_ref, grb_ref,
                  bc_ref, br_ref, c0_ref, n0_ref, m0_ref, hf_ref, hb_ref, c_ref, n_ref, m_ref):
    @pl.when(pl.program_id(1) == 0)
    def _():
        c_ref[...] = c0_ref[...]
        n_ref[...] = n0_ref[...]
        m_ref[...] = m0_ref[...]

    nh, lc = ML_HEADS, ML_CHUNK
    row = lax.broadcasted_iota(jnp.int32, (lc, lc), 0)
    col = lax.broadcasted_iota(jnp.int32, (lc, lc), 1)
    lower = col <= row
    upper = col >= row
    directions = ((qf_ref, kf_ref, vf_ref, gcf_ref, grf_ref, hf_ref, lower, upper),
                  (qb_ref, kb_ref, vb_ref, gcb_ref, grb_ref, hb_ref, upper, lower))
    for d, (q_ref, k_ref, v_ref, gc_ref, gr_ref, h_ref, allowed, allowed_t) in enumerate(directions):
        g_col = gc_ref[:, GATE_LANE0:GATE_LANE0 + 4 * nh] + bc_ref[...]
        g_row = gr_ref[...] + br_ref[...]
        i_col = g_col[:, nh * d:nh * (d + 1)]
        f_col = _log_sigmoid(g_col[:, 2 * nh + nh * d:2 * nh + nh * (d + 1)])
        i_row = g_row[nh * d:nh * (d + 1), :]
        f_row = _log_sigmoid(g_row[2 * nh + nh * d:2 * nh + nh * (d + 1), :])
        for h in range(nh):
            s = d * nh + h
            q = q_ref[:, h * ML_QK:(h + 1) * ML_QK] * (ML_QK ** -0.5)
            k = k_ref[:, h * ML_QK:(h + 1) * ML_QK]
            v = v_ref[:, h * ML_V:(h + 1) * ML_V]
            ic_c, lf_c = i_col[:, h:h + 1], f_col[:, h:h + 1]
            ic_r, lf_r = i_row[h:h + 1, :], f_row[h:h + 1, :]
            c_prev = c_ref[d, h]
            n_prev = n_ref[s:s + 1, :]
            m_prev = m_ref[s:s + 1, :]
            b_c = jnp.sum(jnp.where(allowed, lf_r, 0.0), axis=1, keepdims=True)
            b_r = jnp.sum(jnp.where(allowed_t, lf_c, 0.0), axis=0, keepdims=True)
            dmat = jnp.where(allowed, b_c - b_r + ic_r, -jnp.inf)
            inter = b_c + m_prev
            m_t = jnp.maximum(inter, jnp.max(dmat, axis=1, keepdims=True))
            w_inter = jnp.exp(inter - m_t)
            sw = _bdot_nt(q, k) * jnp.exp(dmat - m_t)
            num = w_inter * _bdot(q, c_prev) + _bdot(sw, v)
            den = w_inter * jnp.sum(q * n_prev, axis=1, keepdims=True) + jnp.sum(sw, axis=1, keepdims=True)
            h_ref[:, h * ML_V:(h + 1) * ML_V] = num / jnp.maximum(jnp.abs(den), jnp.exp(-m_t))
            b_end = jnp.sum(lf_r, axis=1, keepdims=True)
            g_r = b_end - b_r + ic_r
            g_c = b_end - b_c + ic_c
            m_new = jnp.maximum(b_end + m_prev, jnp.max(g_r, axis=1, keepdims=True))
            kw = k * jnp.exp(g_c - m_new)
            decay = jnp.exp(b_end + m_prev - m_new)
            c_ref[d, h] = decay * c_prev + _bdot_tn(kw, v)
            n_ref[s:s + 1, :] = decay * n_prev + jnp.sum(kw, axis=0, keepdims=True)
            m_ref[s:s + 1, :] = m_new


def _mlstm(qkvo, side, g_row, b_col, b_row, c0, n0, m0, bsz, seq):
    t = qkvo.shape[0]
    nc = seq // ML_CHUNK
    lc = ML_CHUNK
    nq = ML_HEADS * ML_QK
    nv = ML_HEADS * ML_V
    side_blk = (SIDE_COLS - 128) // 128
    fwd = lambda b, c: b * nc + c
    bwd = lambda b, c: b * nc + (nc - 1 - c)

    def specs(pos):
        return [pl.BlockSpec((lc, nq), lambda b, c: (pos(b, c), 0)),
                pl.BlockSpec((lc, nq), lambda b, c: (pos(b, c), 1)),
                pl.BlockSpec((lc, nv), lambda b, c: (pos(b, c), 1))]

    state = lambda *shape: pl.BlockSpec((None,) + shape, lambda b, c: (b,) + (0,) * len(shape))
    in_specs = specs(fwd) + specs(bwd) + [
        pl.BlockSpec((lc, 128), lambda b, c: (fwd(b, c), side_blk)),
        pl.BlockSpec((lc, 128), lambda b, c: (bwd(b, c), side_blk)),
        pl.BlockSpec((None, 4 * ML_HEADS, lc), lambda b, c: (fwd(b, c), 0, 0)),
        pl.BlockSpec((None, 4 * ML_HEADS, lc), lambda b, c: (bwd(b, c), 0, 0)),
        pl.BlockSpec((1, 4 * ML_HEADS), lambda b, c: (0, 0)),
        pl.BlockSpec((4 * ML_HEADS, 1), lambda b, c: (0, 0)),
        state(2, ML_HEADS, ML_QK, ML_V), state(2 * ML_HEADS, ML_QK), state(2 * ML_HEADS, 1)]
    out_specs = [pl.BlockSpec((lc, nv), lambda b, c: (fwd(b, c), 0)),
                 pl.BlockSpec((lc, nv), lambda b, c: (bwd(b, c), 0)),
                 state(2, ML_HEADS, ML_QK, ML_V), state(2 * ML_HEADS, ML_QK), state(2 * ML_HEADS, 1)]
    out_shape = [jax.ShapeDtypeStruct((t, nv), F32), jax.ShapeDtypeStruct((t, nv), F32),
                 jax.ShapeDtypeStruct((bsz, 2, ML_HEADS, ML_QK, ML_V), F32),
                 jax.ShapeDtypeStruct((bsz, 2 * ML_HEADS, ML_QK), F32),
                 jax.ShapeDtypeStruct((bsz, 2 * ML_HEADS, 1), F32)]
    return pl.pallas_call(
        _mlstm_kernel, grid=(bsz, nc), in_specs=in_specs, out_specs=out_specs, out_shape=out_shape,
        compiler_params=_params(2), name="mlstm",
    )(qkvo, qkvo, qkvo, qkvo, qkvo, qkvo, side, side, g_row, g_row, b_col, b_row, c0, n0, m0)


def _mlstm_post_kernel(hf_ref, hb_ref, o_ref, g_ref, y_ref):
    for h in range(ML_HEADS):
        sl = slice(h * ML_V, (h + 1) * ML_V)
        hn = _rms(hf_ref[:, sl] + hb_ref[:, sl]) * g_ref[:, sl]
        y_ref[:, sl] = (hn * _sigmoid(o_ref[:, sl])).astype(BF)


def _mlstm_post(hf, hb, qkvo, gain):
    t, nv = hf.shape
    tm = 512
    blk = pl.BlockSpec((tm, nv), lambda i: (i, 0))
    return pl.pallas_call(
        _mlstm_post_kernel, grid=(t // tm,),
        in_specs=[blk, blk, pl.BlockSpec((tm, nv), lambda i: (i, 2)), pl.BlockSpec((1, nv), lambda i: (0, 0))],
        out_specs=blk, out_shape=jax.ShapeDtypeStruct((t, nv), BF),
        compiler_params=_params(1), name="mlstm_post",
    )(hf, hb, qkvo, gain.reshape(1, nv))


def _softmax_pair(s1, s2):
    m = jnp.max(s1, axis=-1, keepdims=True)
    if s2 is not None:
        m = jnp.maximum(m, jnp.max(s2, axis=-1, keepdims=True))
    e1 = jnp.exp(s1 - m)
    den = jnp.sum(e1, axis=-1, keepdims=True)
    if s2 is None:
        return e1 / den, None
    e2 = jnp.exp(s2 - m)
    den = den + jnp.sum(e2, axis=-1, keepdims=True)
    return e1 / den, e2 / den


def _mla_kernel(*refs, with_ctx):
    if with_ctx:
        qa_ref, kv_ref, side_ref, cq_ref, sq_ref, ck_ref, sk_ref, kvc_ref, krc_ref, o_ref = refs
    else:
        qa_ref, kv_ref, side_ref, o_ref = refs
    scale = (MLA_NOPE + MLA_ROPE) ** -0.5
    nope_cols = MLA_HEADS * MLA_NOPE
    q_rope = qa_ref[:, nope_cols:]
    k_rope = side_ref[...]
    if with_ctx:
        q_rope_rot = _rope(q_rope, cq_ref[...], sq_ref[...], MLA_ROPE // 4)
        k_rope_rot = _rope(k_rope, ck_ref[...], sk_ref[...], MLA_ROPE // 4)[:, :MLA_ROPE]
        k_rope_ctx = krc_ref[...]
    else:
        q_rope_rot = q_rope
        k_rope_rot = k_rope[:, :MLA_ROPE]
    for h in range(MLA_HEADS):
        q_n = qa_ref[:, h * MLA_NOPE:(h + 1) * MLA_NOPE]
        rs = slice(h * MLA_ROPE, (h + 1) * MLA_ROPE)
        kv0 = h * (MLA_NOPE + MLA_V)
        k_n = kv_ref[:, kv0:kv0 + MLA_NOPE]
        v = kv_ref[:, kv0 + MLA_NOPE:kv0 + MLA_NOPE + MLA_V]
        s1 = (_bdot_nt(q_n, k_n) + _bdot_nt(q_rope_rot[:, rs], k_rope_rot)) * scale
        if with_ctx:
            s2 = (_bdot_nt(q_n, kvc_ref[:, kv0:kv0 + MLA_NOPE]) + _bdot_nt(q_rope[:, rs], k_rope_ctx)) * scale
        else:
            s2 = None
        p1, p2 = _softmax_pair(s1, s2)
        out = _bdot(p1, v)
        if with_ctx:
            out = out + _bdot(p2, kvc_ref[:, kv0 + MLA_NOPE:kv0 + MLA_NOPE + MLA_V])
        o_ref[:, h * MLA_V:(h + 1) * MLA_V] = out.astype(BF)


def _mla_attention(qa, kv, side, bsz, seq, ctx=None):
    t = qa.shape[0]
    tq = 256
    nq = seq // tq
    side_blk = (SIDE_COLS - 128) // 128
    nkv = MLA_HEADS * (MLA_NOPE + MLA_V)
    in_specs = [pl.BlockSpec((tq, qa.shape[1]), lambda b, i: (b * nq + i, 0)),
                pl.BlockSpec((seq, nkv), lambda b, i: (b, 0)),
                pl.BlockSpec((seq, 128), lambda b, i: (b, side_blk))]
    args = [qa, kv, side]
    if ctx is not None:
        cos_q, sin_q, cos_k, sin_k, kvc, krc = ctx
        past = krc.shape[1]
        in_specs += [pl.BlockSpec((tq, cos_q.shape[1]), lambda b, i: (i, 0)),
                     pl.BlockSpec((tq, cos_q.shape[1]), lambda b, i: (i, 0)),
                     pl.BlockSpec((seq, 128), lambda b, i: (0, 0)),
                     pl.BlockSpec((seq, 128), lambda b, i: (0, 0)),
                     pl.BlockSpec((past, nkv), lambda b, i: (b, 0)),
                     pl.BlockSpec((None, past, MLA_ROPE), lambda b, i: (b, 0, 0))]
        args += [cos_q, sin_q, cos_k, sin_k, kvc, krc]
    nout = MLA_HEADS * MLA_V
    return pl.pallas_call(
        functools.partial(_mla_kernel, with_ctx=ctx is not None),
        grid=(bsz, nq), in_specs=in_specs,
        out_specs=pl.BlockSpec((tq, nout), lambda b, i: (b * nq + i, 0)),
        out_shape=jax.ShapeDtypeStruct((t, nout), BF),
        compiler_params=_params(2), name="mla_attention",
    )(*args)


def _gqa_kernel(*refs, with_ctx):
    if with_ctx:
        q_ref, k_ref, v_ref, gq_ref, gk_ref, cq_ref, sq_ref, ck_ref, sk_ref, kc_ref, vc_ref, o_ref = refs
    else:
        q_ref, k_ref, v_ref, gq_ref, gk_ref, o_ref, kn_ref, vo_ref = refs
    hd = GQA_HEAD_DIM
    scale = hd ** -0.5
    k_n = _rms(k_ref[...]) * gk_ref[...]
    v = v_ref[...]
    if with_ctx:
        k_src = _rope(k_n, ck_ref[...], sk_ref[...], hd // 4)
    else:
        k_src = k_n

        @pl.when(pl.program_id(2) == 0)
        def _():
            kn_ref[...] = k_n
            vo_ref[...] = v

    for r in range(GQA_HEADS // GQA_KV_HEADS):
        sl = slice(r * hd, (r + 1) * hd)
        q_n = _rms(q_ref[:, sl]) * gq_ref[...]
        if with_ctx:
            s1 = _bdot_nt(_rope(q_n, cq_ref[...], sq_ref[...], hd // 4), k_src) * scale
            s2 = _bdot_nt(q_n, kc_ref[...]) * scale
        else:
            s1, s2 = _bdot_nt(q_n, k_src) * scale, None
        p1, p2 = _softmax_pair(s1, s2)
        out = _bdot(p1, v)
        if with_ctx:
            out = out + _bdot(p2, vc_ref[...])
        o_ref[:, sl] = out.astype(BF)


def _gqa_attention(qkv, g_q, g_k, bsz, seq, ctx=None):
    t = qkv.shape[0]
    hd = GQA_HEAD_DIM
    rep = GQA_HEADS // GQA_KV_HEADS
    tq = 256
    nq = seq // tq
    in_specs = [pl.BlockSpec((tq, rep * hd), lambda b, g, i: (b * nq + i, g)),
                pl.BlockSpec((seq, hd), lambda b, g, i: (b, GQA_HEADS + g)),
                pl.BlockSpec((seq, hd), lambda b, g, i: (b, GQA_HEADS + GQA_KV_HEADS + g)),
                pl.BlockSpec((1, hd), lambda b, g, i: (0, 0)),
                pl.BlockSpec((1, hd), lambda b, g, i: (0, 0))]
    args = [qkv, qkv, qkv, g_q.reshape(1, hd), g_k.reshape(1, hd)]
    o_spec = pl.BlockSpec((tq, rep * hd), lambda b, g, i: (b * nq + i, g))
    o_shape = jax.ShapeDtypeStruct((t, GQA_HEADS * hd), BF)
    if ctx is not None:
        cos_t, sin_t, kc, vc, j = ctx
        past = kc.shape[3]
        cache = pl.BlockSpec((None, None, None, past, hd), lambda b, g, i: (b, j, g, 0, 0))
        in_specs += [pl.BlockSpec((tq, hd), lambda b, g, i: (i, 0)),
                     pl.BlockSpec((tq, hd), lambda b, g, i: (i, 0)),
                     pl.BlockSpec((seq, hd), lambda b, g, i: (0, 0)),
                     pl.BlockSpec((seq, hd), lambda b, g, i: (0, 0)),
                     cache, cache]
        args += [cos_t, sin_t, cos_t, sin_t, kc, vc]
        out_specs, out_shape = o_spec, o_shape
    else:
        head_major = pl.BlockSpec((None, None, seq, hd), lambda b, g, i: (b, g, 0, 0))
        out_specs = [o_spec, head_major, head_major]
        kv_shape = jax.ShapeDtypeStruct((bsz, GQA_KV_HEADS, seq, hd), F32)
        out_shape = [o_shape, kv_shape, kv_shape]
    return pl.pallas_call(
        functools.partial(_gqa_kernel, with_ctx=ctx is not None),
        grid=(bsz, GQA_KV_HEADS, nq), in_specs=in_specs, out_specs=out_specs, out_shape=out_shape,
        compiler_params=_params(3), name="gqa_attention",
    )(*args)


RANK_BLOCK = 256
GATHER_ROWS = 512


def _dispatch_kernel(ar_ref, ac_ref, h_ref, xe_ref, gate_ref, rankc_ref, rankr_ref, *, cap):
    ne, s = ar_ref.shape
    blk = min(RANK_BLOCK, s)
    rankr_ref[...] = jnp.zeros_like(rankr_ref)
    for j in range(s // blk):
        rows = pl.ds(j * blk, blk)
        i0 = lax.broadcasted_iota(jnp.int32, (blk, s), 0) + j * blk
        i1 = lax.broadcasted_iota(jnp.int32, (blk, s), 1)
        sub_first = jnp.where(i0 < i1, 1.0, 0.0)
        lane_first = jnp.where(i1 < i0, 1.0, 0.0)
        for e in range(ne):
            a_row = ar_ref[e:e + 1, :]
            a_col = ac_ref[rows, e:e + 1]
            ahead = jnp.where(a_col > a_row, 1.0, jnp.where(a_col >= a_row, sub_first, 0.0))
            rankr_ref[e:e + 1, :] += jnp.sum(ahead, axis=0, keepdims=True)
            ahead = jnp.where(a_row > a_col, 1.0, jnp.where(a_row >= a_col, lane_first, 0.0))
            rankc_ref[rows, e:e + 1] = jnp.sum(ahead, axis=1, keepdims=True)
    group = GATHER_ROWS // cap
    slot = lax.broadcasted_iota(jnp.int32, (group, cap, s), 1).astype(F32)
    for g in range(ne // group):
        es = slice(g * group, (g + 1) * group)
        pick = rankr_ref[es, :][:, None, :] == slot
        onehot = jnp.where(pick, 1.0, 0.0).reshape(group * cap, s).astype(BF)
        rows_f32 = jnp.dot(onehot, h_ref[...], preferred_element_type=F32)
        xe_ref[es] = rows_f32.reshape(group, cap, -1).astype(BF)
        gate_ref[es] = jnp.sum(jnp.where(pick, ar_ref[es, :][:, None, :], 0.0), axis=2, keepdims=True)


def _dispatch(aff_row, aff_col, h, bsz, seq):
    t, d = h.shape
    ne = aff_row.shape[0]
    cap = EC_FACTOR * seq // ne
    return pl.pallas_call(
        functools.partial(_dispatch_kernel, cap=cap),
        grid=(bsz,),
        in_specs=[pl.BlockSpec((ne, seq), lambda b: (0, b)),
                  pl.BlockSpec((seq, ne), lambda b: (b, 0)),
                  pl.BlockSpec((seq, d), lambda b: (b, 0))],
        out_specs=[pl.BlockSpec((ne, cap, d), lambda b: (0, b, 0)),
                   pl.BlockSpec((ne, cap, 1), lambda b: (0, b, 0)),
                   pl.BlockSpec((seq, ne), lambda b: (b, 0))],
        out_shape=[jax.ShapeDtypeStruct((ne, bsz * cap, d), BF),
                   jax.ShapeDtypeStruct((ne, bsz * cap, 1), F32),
                   jax.ShapeDtypeStruct((t, ne), F32)],
        scratch_shapes=[pltpu.VMEM((ne, seq), F32)],
        compiler_params=_params(1), name="ec_dispatch",
    )(aff_row, aff_col, h)


def _experts_kernel(xp_ref, xs_ref, gp_ref, gs_ref, wg_ref, wu_ref, wd_ref, yp_ref, ys_ref):
    f = pl.program_id(1)
    wg = wg_ref[...].astype(BF)
    wu = wu_ref[...].astype(BF)
    wd = wd_ref[...].astype(BF)
    for x_ref, g_ref, y_ref in ((xp_ref, gp_ref, yp_ref), (xs_ref, gs_ref, ys_ref)):
        x = x_ref[...]
        a = jnp.dot(x, wg, preferred_element_type=F32)
        u = jnp.dot(x, wu, preferred_element_type=F32)
        part = jnp.dot((a * _sigmoid(a) * u).astype(BF), wd, preferred_element_type=F32)

        @pl.when(f == 0)
        def _():
            y_ref[...] = part

        @pl.when(f > 0)
        def _():
            y_ref[...] += part

        @pl.when(f == pl.num_programs(1) - 1)
        def _():
            y_ref[...] = y_ref[...] * g_ref[...]


def _experts(xe_p, xe_s, gate_p, gate_s, w_gate, w_up, w_down, layer):
    ne, mp, d = xe_p.shape
    ms = xe_s.shape[1]
    fdim = w_gate.shape[-1]
    tf = 256
    rows = lambda m, last: pl.BlockSpec((None, m, last), lambda e, f: (e, 0, 0))
    return pl.pallas_call(
        _experts_kernel, grid=(ne, fdim // tf),
        in_specs=[rows(mp, d), rows(ms, d), rows(mp, 1), rows(ms, 1),
                  pl.BlockSpec((None, None, d, tf), lambda e, f: (layer, e, 0, f)),
                  pl.BlockSpec((None, None, d, tf), lambda e, f: (layer, e, 0, f)),
                  pl.BlockSpec((None, None, tf, d), lambda e, f: (layer, e, f, 0))],
        out_specs=[rows(mp, d), rows(ms, d)],
        out_shape=[jax.ShapeDtypeStruct((ne, mp, d), F32), jax.ShapeDtypeStruct((ne, ms, d), F32)],
        compiler_params=_params(2), name="ec_experts",
    )(xe_p, xe_s, gate_p, gate_s, w_gate, w_up, w_down)


RANK_RADIX = 32


def _combine_kernel(y_ref, rank_ref, x_ref, g_ref, o_ref, place_ref, *, cap):
    ne = y_ref.shape[0]
    n_slots = ne * cap

    @pl.when(pl.program_id(1) == 0)
    def _():
        rank = rank_ref[...]
        hi = jnp.floor(rank * (1.0 / RANK_RADIX))
        lo = rank - RANK_RADIX * hi
        lane = lax.broadcasted_iota(jnp.int32, (ne, n_slots), 1)
        expert = lax.broadcasted_iota(jnp.int32, (ne, n_slots), 0)
        own = (lane >= expert * cap) & (lane < (expert + 1) * cap)
        spread_hi = jnp.where(own, float(RANK_RADIX), 0.0).astype(BF)
        spread_lo = jnp.where(own, 1.0, 0.0).astype(BF)
        spread = (jnp.dot(hi.astype(BF), spread_hi, preferred_element_type=F32)
                  + jnp.dot(lo.astype(BF), spread_lo, preferred_element_type=F32))
        slot = (lax.broadcasted_iota(jnp.int32, (1, n_slots), 1) & (cap - 1)).astype(F32)
        place_ref[...] = jnp.where(spread == slot, 1.0, 0.0).astype(BF)

    y = y_ref[...].reshape(n_slots, y_ref.shape[2])
    y_hi = y.astype(BF)
    y_lo = (y - y_hi.astype(F32)).astype(BF)
    place = place_ref[...]
    acc = jnp.dot(place, y_hi, preferred_element_type=F32) + jnp.dot(place, y_lo, preferred_element_type=F32)
    o_ref[...] = x_ref[...] + g_ref[...] * acc


def _combine(y, rank, x, gate, bsz, seq):
    t, d = x.shape
    ne = y.shape[0]
    cap = EC_FACTOR * seq // ne
    assert seq <= RANK_RADIX * RANK_RADIX and cap & (cap - 1) == 0
    td = 512
    return pl.pallas_call(
        functools.partial(_combine_kernel, cap=cap),
        grid=(bsz, d // td),
        in_specs=[pl.BlockSpec((ne, cap, td), lambda b, j: (0, b, j)),
                  pl.BlockSpec((seq, ne), lambda b, j: (b, 0)),
                  pl.BlockSpec((seq, td), lambda b, j: (b, j)),
                  pl.BlockSpec((None, 1, td), lambda b, j: (b if gate.shape[0] > 1 else 0, 0, j))],
        out_specs=pl.BlockSpec((seq, td), lambda b, j: (b, j)),
        out_shape=jax.ShapeDtypeStruct((t, d), F32),
        scratch_shapes=[pltpu.VMEM((seq, ne * cap), BF)],
        compiler_params=_params(2), name="ec_combine",
    )(y, rank, x, gate)


def _final_norm_kernel(x_ref, g_ref, o_ref):
    o_ref[...] = _rms(x_ref[...]) * g_ref[...]


def _final_norm(x, gain):
    t, d = x.shape
    tm = 256
    blk = pl.BlockSpec((tm, d), lambda i: (i, 0))
    return pl.pallas_call(
        _final_norm_kernel, grid=(t // tm,),
        in_specs=[blk, pl.BlockSpec((1, d), lambda i: (0, 0))],
        out_specs=blk, out_shape=jax.ShapeDtypeStruct((t, d), F32),
        compiler_params=_params(1), name="final_norm",
    )(x, gain.reshape(1, d))


def _rope_tables(n_tokens, rot_dim):
    rows = n_tokens // GRID_W
    row = jnp.repeat(jnp.arange(rows), GRID_W).astype(F32)
    col = jnp.tile(jnp.arange(GRID_W), rows).astype(F32)
    quarter = rot_dim // 4
    inv = ROPE_THETA ** (-jnp.arange(quarter, dtype=F32) / quarter)
    a_row, a_col = row[:, None] * inv, col[:, None] * inv
    cos_t = jnp.concatenate([jnp.cos(a_row), jnp.cos(a_row), jnp.cos(a_col), jnp.cos(a_col)], axis=-1)
    sin_t = jnp.concatenate([-jnp.sin(a_row), jnp.sin(a_row), -jnp.sin(a_col), jnp.sin(a_col)], axis=-1)
    return cos_t, sin_t


def kernel(x_prompt, x_sample, state_mlstm_c, state_mlstm_n, state_mlstm_m, cache_mla_ckv, cache_mla_krope,
           cache_gqa_k, cache_gqa_v, c, c_ctx, w_mod, b_mod, w_in_even, b_igate, b_fgate, g_mlstm, g_cq, w_uq,
           g_ckv, w_ukv, w_out_even, w_in_odd, g_qnorm, g_knorm, w_out_odd, w_router, w_expert_gate,
           w_expert_up, w_expert_down, g_final):
    d = D_MODEL
    bp, sp, _ = x_prompt.shape
    bs, ss, _ = x_sample.shape
    depth = w_mod.shape[0]
    nh = ML_HEADS
    streams = {"p": (bp, sp), "s": (bs, ss)}
    x = {"p": x_prompt.reshape(bp * sp, d), "s": x_sample.reshape(bs * ss, d)}

    c8 = jnp.concatenate([c_ctx[None], c, jnp.zeros((8 - 1 - bs, d), F32)], axis=0)
    mod_all = _mod_vectors(c8, w_mod, b_mod).reshape(depth, 8, 6, 1, d)

    def mod(layer, key, idx):
        rows = mod_all[layer, 0:1, idx] if key == "p" else mod_all[layer, 1:1 + bs, idx]
        return rows

    new_even, new_odd = [], []
    for layer in range(depth):
        j = layer // 2
        if layer % 2 == 0:
            w_in = w_in_even[j]
            w_side = jnp.concatenate(
                [w_in[:, QKVO_COLS + 4 * nh:],
                 w_in[:, QKVO_COLS:QKVO_COLS + 4 * nh],
                 jnp.zeros((d, 128 - MLA_ROPE - 4 * nh), F32)], axis=1)[None]
            w_q = w_uq[j].reshape(MLA_Q_RANK, MLA_HEADS, MLA_NOPE + MLA_ROPE)
            w_q = jnp.concatenate([w_q[:, :, :MLA_NOPE].reshape(MLA_Q_RANK, -1),
                                   w_q[:, :, MLA_NOPE:].reshape(MLA_Q_RANK, -1)], axis=1)
            bias_col = jnp.concatenate([b_igate[j].reshape(1, -1), b_fgate[j].reshape(1, -1)], axis=1)
            bias_row = bias_col.reshape(-1, 1)
            cos64, sin64 = _rope_tables(ss, MLA_ROPE)
            cos_q, sin_q = jnp.tile(cos64, (1, MLA_HEADS)), jnp.tile(sin64, (1, MLA_HEADS))
            pad = jnp.zeros((ss, 128 - MLA_ROPE), F32)
            cos_k, sin_k = jnp.concatenate([cos64, pad], axis=1), jnp.concatenate([sin64, pad], axis=1)
            kvc = _mm([cache_mla_ckv[:, j].reshape(-1, MLA_KV_RANK)], w_ukv, j, w_ukv.shape[-1], 512,
                      name="mla_ctx_expand")
            for key, (bsz, seq) in streams.items():
                h = _norm_mod(x[key], mod(layer, key, 0), mod(layer, key, 1), seq)
                qkvo = _mm([h], w_in_even, j, QKVO_COLS, 512, name="even_in_main")
                side = _mm([h], w_side, 0, SIDE_COLS, SIDE_COLS // 3, name="even_in_side")
                gates = side[:, SIDE_COLS - 128 + GATE_LANE0:SIDE_COLS - 128 + GATE_LANE0 + 4 * nh]
                g_row = gates.reshape(-1, ML_CHUNK, 4 * nh).transpose(0, 2, 1)
                if key == "p":
                    c0 = jnp.zeros((bsz, 2, nh, ML_QK, ML_V), F32)
                    n0 = jnp.zeros((bsz, 2 * nh, ML_QK), F32)
                    m0 = jnp.zeros((bsz, 2 * nh, 1), F32)
                else:
                    c0 = state_mlstm_c[:, j]
                    n0 = state_mlstm_n[:, j].reshape(bsz, 2 * nh, ML_QK)
                    m0 = state_mlstm_m[:, j].reshape(bsz, 2 * nh, 1)
                hf, hb, c_fin, n_fin, m_fin = _mlstm(qkvo, side, g_row, bias_col, bias_row, c0, n0, m0, bsz, seq)
                y_ml = _mlstm_post(hf, hb, qkvo, g_mlstm[j])
                qa = _norm_mm(side, 0, g_cq[j], w_q, False, "mla_q_up")
                kv, ckv_n = _norm_mm(side, 1, g_ckv[j], w_ukv[j], True, "mla_kv_up")
                if key == "p":
                    y_a = _mla_attention(qa, kv, side, bsz, seq)
                    k_rope = side[:, SIDE_COLS - 128:SIDE_COLS - 128 + MLA_ROPE]
                    new_even.append((c_fin, n_fin.reshape(bsz, 2, nh, ML_QK), m_fin.reshape(bsz, 2, nh),
                                     ckv_n.reshape(bsz, seq, -1), k_rope.reshape(bsz, seq, -1)))
                else:
                    y_a = _mla_attention(qa, kv, side, bsz, seq,
                                         ctx=(cos_q, sin_q, cos_k, sin_k, kvc, cache_mla_krope[:, j]))
                x[key] = _mm([y_ml, y_a], w_out_even, j, d, 512,
                             residual=(x[key], mod(layer, key, 2), seq), name="even_out")
        else:
            cos_t, sin_t = _rope_tables(ss, GQA_HEAD_DIM)
            for key, (bsz, seq) in streams.items():
                h = _norm_mod(x[key], mod(layer, key, 0), mod(layer, key, 1), seq)
                qkv = _mm([h], w_in_odd, j, w_in_odd.shape[-1], 512, name="odd_in")
                if key == "p":
                    o, k_n, v = _gqa_attention(qkv, g_qnorm[j], g_knorm[j], bsz, seq)
                    new_odd.append((k_n, v))
                else:
                    o = _gqa_attention(qkv, g_qnorm[j], g_knorm[j], bsz, seq,
                                       ctx=(cos_t, sin_t, cache_gqa_k, cache_gqa_v, j))
                x[key] = _mm([o], w_out_odd, j, d, 512,
                             residual=(x[key], mod(layer, key, 2), seq), name="odd_out")
        routed = {}
        for key, (bsz, seq) in streams.items():
            h, aff = _norm_mod(x[key], mod(layer, key, 3), mod(layer, key, 4), seq, w_router=w_router, layer=layer)
            routed[key] = _dispatch(aff.T, aff, h, bsz, seq)
        y_p, y_s = _experts(routed["p"][0], routed["s"][0], routed["p"][1], routed["s"][1],
                            w_expert_gate, w_expert_up, w_expert_down, layer)
        for key, y in (("p", y_p), ("s", y_s)):
            bsz, seq = streams[key]
            x[key] = _combine(y, routed[key][2], x[key], mod(layer, key, 5), bsz, seq)

    y_prompt = _final_norm(x["p"], g_final).reshape(bp, sp, d)
    y_sample = _final_norm(x["s"], g_final).reshape(bs, ss, d)
    new_c = jnp.stack([e[0] for e in new_even], axis=1)
    new_n = jnp.stack([e[1] for e in new_even], axis=1)
    new_m = jnp.stack([e[2] for e in new_even], axis=1)
    new_ckv = jnp.stack([e[3] for e in new_even], axis=1)
    new_krope = jnp.stack([e[4] for e in new_even], axis=1)
    new_k = jnp.stack([e[0] for e in new_odd], axis=1)
    new_v = jnp.stack([e[1] for e in new_odd], axis=1)
    return (y_prompt, y_sample, new_c, new_n, new_m, new_ckv, new_krope, new_k, new_v)
```

```python
import functools

import jax
import jax.numpy as jnp
from jax import lax
from jax.experimental import pallas as pl
from jax.experimental.pallas import tpu as pltpu

BF = jnp.bfloat16
F32 = jnp.float32

D_MODEL = 2048
GRID_W = 64
ROPE_THETA = 10000.0
NORM_EPS = 1e-6
ML_HEADS = 8
ML_QK = 64
ML_V = 128
ML_CHUNK = 64
MLA_HEADS = 8
MLA_Q_RANK = 512
MLA_KV_RANK = 512
MLA_NOPE = 128
MLA_ROPE = 64
MLA_V = 128
GQA_HEADS = 16
GQA_KV_HEADS = 4
GQA_HEAD_DIM = 128
N_EXPERTS = 16
EXPERT_DIM = 1024
EC_FACTOR = 2

QKVO_COLS = 2 * ML_HEADS * ML_QK + 2 * ML_HEADS * ML_V
SIDE_COLS = MLA_Q_RANK + MLA_KV_RANK + 128
GATE_LANE0 = MLA_ROPE

VMEM_LIMIT_BYTES = 56 * 1024 * 1024


def _params(n_axes):
    return pltpu.CompilerParams(dimension_semantics=("arbitrary",) * n_axes,
                                vmem_limit_bytes=VMEM_LIMIT_BYTES)


def _bdot(a, b):
    return jnp.dot(a.astype(BF), b.astype(BF), preferred_element_type=F32)


def _bdot_nt(a, b):
    return lax.dot_general(a.astype(BF), b.astype(BF), (((1,), (1,)), ((), ())),
                           preferred_element_type=F32)


def _bdot_tn(a, b):
    return lax.dot_general(a.astype(BF), b.astype(BF), (((0,), (0,)), ((), ())),
                           preferred_element_type=F32)


def _sigmoid(x):
    return 1.0 / (1.0 + jnp.exp(-x))


def _log_sigmoid(x):
    return jnp.minimum(x, 0.0) - jnp.log1p(jnp.exp(-jnp.abs(x)))


def _rms(x):
    return x * lax.rsqrt(jnp.mean(x * x, axis=-1, keepdims=True) + NORM_EPS)


def _rope(x, cos_t, sin_t, quarter):
    width = x.shape[-1]
    axis = x.ndim - 1
    lane = lax.broadcasted_iota(jnp.int32, x.shape, axis)
    partner = jnp.where((lane & quarter) == 0,
                        pltpu.roll(x, width - quarter, axis=axis),
                        pltpu.roll(x, quarter, axis=axis))
    return x * cos_t + partner * sin_t


def _mod_kernel(c_ref, w_ref, b_ref, o_ref):
    c = c_ref[...]
    o_ref[...] = _bdot(c * _sigmoid(c), w_ref[...]) + b_ref[...]


def _mod_vectors(c8, w_mod, b_mod):
    n_layers, k, n = w_mod.shape
    tn = 1024
    return pl.pallas_call(
        _mod_kernel,
        grid=(n_layers, n // tn),
        in_specs=[pl.BlockSpec((8, k), lambda l, j: (0, 0)),
                  pl.BlockSpec((None, k, tn), lambda l, j: (l, 0, j)),
                  pl.BlockSpec((None, 1, tn), lambda l, j: (l, 0, j))],
        out_specs=pl.BlockSpec((None, 8, tn), lambda l, j: (l, 0, j)),
        out_shape=jax.ShapeDtypeStruct((n_layers, 8, n), F32),
        compiler_params=_params(2),
        name="mod_vectors",
    )(c8, w_mod, b_mod.reshape(n_layers, 1, n))


def _norm_mod_kernel(x_ref, sh_ref, sc_ref, h_ref):
    h_ref[...] = (_rms(x_ref[...]) * (1.0 + sc_ref[...]) + sh_ref[...]).astype(BF)


def _norm_router_kernel(x_ref, sh_ref, sc_ref, wr_ref, h_ref, aff_ref):
    h = (_rms(x_ref[...]) * (1.0 + sc_ref[...]) + sh_ref[...]).astype(BF)
    h_ref[...] = h
    logits = _bdot(h, wr_ref[...])
    e = jnp.exp(logits - jnp.max(logits, axis=-1, keepdims=True))
    aff_ref[...] = e / jnp.sum(e, axis=-1, keepdims=True)


def _batch_of_tile(n_vectors, tm, seq):
    if n_vectors == 1:
        return lambda i: 0
    assert seq % tm == 0
    return lambda i: (i * tm) // seq


def _norm_mod(x, shift, scale, seq, w_router=None, layer=0):
    t, d = x.shape
    tm = 256
    which = _batch_of_tile(shift.shape[0], tm, seq)
    vec = pl.BlockSpec((None, 1, d), lambda i: (which(i), 0, 0))
    x_spec = pl.BlockSpec((tm, d), lambda i: (i, 0))
    if w_router is None:
        return pl.pallas_call(
            _norm_mod_kernel, grid=(t // tm,),
            in_specs=[x_spec, vec, vec], out_specs=x_spec,
            out_shape=jax.ShapeDtypeStruct((t, d), BF),
            compiler_params=_params(1), name="norm_mod",
        )(x, shift, scale)
    ne = w_router.shape[-1]
    return pl.pallas_call(
        _norm_router_kernel, grid=(t // tm,),
        in_specs=[x_spec, vec, vec, pl.BlockSpec((None, d, ne), lambda i: (layer, 0, 0))],
        out_specs=[x_spec, pl.BlockSpec((tm, ne), lambda i: (i, 0))],
        out_shape=[jax.ShapeDtypeStruct((t, d), BF), jax.ShapeDtypeStruct((t, ne), F32)],
        compiler_params=_params(1), name="norm_router",
    )(x, shift, scale, w_router)


def _mm_kernel(*refs, k_sizes, residual):
    n_a = len(k_sizes)
    a_refs, w_ref = refs[:n_a], refs[n_a]
    o_ref, wbf_ref = refs[-2], refs[-1]

    @pl.when(pl.program_id(1) == 0)
    def _():
        wbf_ref[...] = w_ref[...].astype(BF)

    acc, off = None, 0
    for a_ref, ks in zip(a_refs, k_sizes):
        term = jnp.dot(a_ref[...].astype(BF), wbf_ref[off:off + ks, :], preferred_element_type=F32)
        acc = term if acc is None else acc + term
        off += ks
    if residual:
        x_ref, g_ref = refs[n_a + 1], refs[n_a + 2]
        acc = x_ref[...] + g_ref[...] * acc
    o_ref[...] = acc


def _mm(a_list, w3, layer, n_cols, tn, col_blk0=0, residual=None, name="mm"):
    m = a_list[0].shape[0]
    k_sizes = tuple(a.shape[1] for a in a_list)
    k = sum(k_sizes)
    assert w3.shape[1] == k and n_cols % tn == 0
    tm = min(1024, m)
    in_specs = [pl.BlockSpec((tm, ks), lambda j, i: (i, 0)) for ks in k_sizes]
    in_specs.append(pl.BlockSpec((None, k, tn), lambda j, i: (layer, 0, j + col_blk0)))
    args = list(a_list) + [w3]
    if residual is not None:
        x, gate, seq = residual
        which = _batch_of_tile(gate.shape[0], tm, seq)
        in_specs.append(pl.BlockSpec((tm, tn), lambda j, i: (i, j)))
        in_specs.append(pl.BlockSpec((None, 1, tn), lambda j, i: (which(i), 0, j)))
        args += [x, gate]
    return pl.pallas_call(
        functools.partial(_mm_kernel, k_sizes=k_sizes, residual=residual is not None),
        grid=(n_cols // tn, m // tm),
        in_specs=in_specs,
        out_specs=pl.BlockSpec((tm, tn), lambda j, i: (i, j)),
        out_shape=jax.ShapeDtypeStruct((m, n_cols), F32),
        scratch_shapes=[pltpu.VMEM((k, tn), BF)],
        compiler_params=_params(2), name=name,
    )(*args)


def _norm_mm_kernel(x_ref, g_ref, w_ref, *out_refs, with_normed):
    wbf_ref = out_refs[-1]

    @pl.when(pl.program_id(0) == 0)
    def _():
        wbf_ref[...] = w_ref[...].astype(BF)

    xn = _rms(x_ref[...]) * g_ref[...]
    out_refs[0][...] = jnp.dot(xn.astype(BF), wbf_ref[...], preferred_element_type=F32)
    if with_normed:
        out_refs[1][...] = xn


def _norm_mm(x, col_blk, gain, w, with_normed, name):
    t = x.shape[0]
    k, n = w.shape
    tm = 512
    out_specs = [pl.BlockSpec((tm, n), lambda i: (i, 0))]
    out_shape = [jax.ShapeDtypeStruct((t, n), F32)]
    if with_normed:
        out_specs.append(pl.BlockSpec((tm, k), lambda i: (i, 0)))
        out_shape.append(jax.ShapeDtypeStruct((t, k), F32))
    outs = pl.pallas_call(
        functools.partial(_norm_mm_kernel, with_normed=with_normed),
        grid=(t // tm,),
        in_specs=[pl.BlockSpec((tm, k), lambda i: (i, col_blk)),
                  pl.BlockSpec((1, k), lambda i: (0, 0)),
                  pl.BlockSpec((k, n), lambda i: (0, 0))],
        out_specs=out_specs, out_shape=out_shape,
        scratch_shapes=[pltpu.VMEM((k, n), BF)],
        compiler_params=_params(1), name=name,
    )(x, gain.reshape(1, k), w)
    return outs if with_normed else outs[0]


def _split3(x):
    hi = x.astype(BF)
    rest = x - hi.astype(F32)
    mid = rest.astype(BF)
    return hi, mid, (rest - mid.astype(F32)).astype(BF)


def _scan_max(x, reverse):
    n = x.shape[0]
    row = lax.broadcasted_iota(jnp.int32, x.shape, 0)
    k = 1
    while k < n:
        if reverse:
            shifted = jnp.where(row < n - k, pltpu.roll(x, n - k, axis=0), -jnp.inf)
        else:
            shifted = jnp.where(row >= k, pltpu.roll(x, k, axis=0), -jnp.inf)
        x = jnp.maximum(x, shifted)
        k *= 2
    return x


def _mlstm_kernel(qf_ref, kf_ref, vf_ref, qb_ref, kb_ref, vb_ref, gcf_ref, gcb_ref, grf_ref, grb_ref,
                  bc_ref, br_ref, c0_ref, n0_ref, m0_ref, hf_ref, hb_ref, c_ref, n_ref, m_ref):
    @pl.when(pl.program_id(1) == 0)
    def _():
        c_ref[...] = c0_ref[...]
        n_ref[...] = n0_ref[...]
        m_ref[...] = m0_ref[...]

    nh, lc = ML_HEADS, ML_CHUNK
    row = lax.broadcasted_iota(jnp.int32, (lc, lc), 0)
    col = lax.broadcasted_iota(jnp.int32, (lc, lc), 1)
    lower = col <= row
    upper = col >= row
    ones_v = jnp.ones((lc, ML_V), BF)
    directions = ((qf_ref, kf_ref, vf_ref, gcf_ref, grf_ref, hf_ref, lower, upper),
                  (qb_ref, kb_ref, vb_ref, gcb_ref, grb_ref, hb_ref, upper, lower))
    gate_terms = []
    for d, (_, _, _, gc_ref, gr_ref, _, allowed, allowed_t) in enumerate(directions):
        g_col = gc_ref[:, GATE_LANE0:GATE_LANE0 + 4 * nh] + bc_ref[...]
        g_row = gr_ref[...] + br_ref[...]
        i_col = g_col[:, nh * d:nh * (d + 1)]
        f_col = _log_sigmoid(g_col[:, 2 * nh + nh * d:2 * nh + nh * (d + 1)])
        i_row = g_row[nh * d:nh * (d + 1), :]
        f_row = _log_sigmoid(g_row[2 * nh + nh * d:2 * nh + nh * (d + 1), :])
        tri = jnp.where(allowed, 1.0, 0.0).astype(BF)
        tri_t = jnp.where(allowed_t, 1.0, 0.0).astype(BF)
        b_col = sum(jnp.dot(tri, part, preferred_element_type=F32) for part in _split3(f_col))
        b_row = sum(jnp.dot(part, tri_t, preferred_element_type=F32) for part in _split3(f_row))
        b_end = jnp.sum(f_col, axis=0, keepdims=True)
        m_prev = m_ref[:, nh * d:nh * (d + 1)]
        m_t = b_col + jnp.maximum(m_prev, _scan_max(i_col - b_col, reverse=d == 1))
        g_col_end = b_end - b_col + i_col
        m_new = jnp.maximum(b_end + m_prev, jnp.max(g_col_end, axis=0, keepdims=True))
        gate_terms.append(dict(
            u=b_col - m_t, r_row=i_row - b_row, w_inter=jnp.exp(b_col + m_prev - m_t), floor=jnp.exp(-m_t),
            k_scale=jnp.exp(g_col_end - m_new), decay=jnp.exp(b_end + m_prev - m_new), m_new=m_new))
    work = []
    for d, (q_ref, k_ref, v_ref, _, _, h_ref, allowed, _) in enumerate(directions):
        for h in range(nh):
            w = dict(d=d, h=h, h_ref=h_ref, allowed=allowed, g=gate_terms[d])
            w["q"] = (q_ref[:, h * ML_QK:(h + 1) * ML_QK] * (ML_QK ** -0.5)).astype(BF)
            w["k"] = k_ref[:, h * ML_QK:(h + 1) * ML_QK]
            w["v1"] = jnp.concatenate([v_ref[:, h * ML_V:(h + 1) * ML_V].astype(BF), ones_v], axis=1)
            work.append(w)
    for w in work:
        w["qk"] = _bdot_nt(w["q"], w["k"])
    for w in work:
        d, h = w["d"], w["h"]
        w["c_prev"], w["n_prev"] = c_ref[d, h], n_ref[d, h]
        state = jnp.concatenate([w["c_prev"], w["n_prev"]], axis=1).astype(BF)
        w["q_state"] = jnp.dot(w["q"], state, preferred_element_type=F32)
    for w in work:
        h, g = w["h"], w["g"]
        kw = w["k"] * g["k_scale"][:, h:h + 1]
        w["kv"] = _bdot_tn(kw, w["v1"])
    for w in work:
        h, g = w["h"], w["g"]
        dmat = jnp.where(w["allowed"], g["u"][:, h:h + 1] + g["r_row"][h:h + 1, :], -jnp.inf)
        sw = w["qk"] * jnp.exp(dmat)
        sw_hi = sw.astype(BF)
        sw_lo = (sw - sw_hi.astype(F32)).astype(BF)
        w["pv"] = jnp.dot(sw_hi, w["v1"], preferred_element_type=F32)
        w["den_lo"] = jnp.dot(sw_lo, ones_v, preferred_element_type=F32)
    for w in work:
        d, h, g = w["d"], w["h"], w["g"]
        w_inter = g["w_inter"][:, h:h + 1]
        num = w_inter * w["q_state"][:, :ML_V] + w["pv"][:, :ML_V]
        den = w_inter * w["q_state"][:, ML_V:] + (w["pv"][:, ML_V:] + w["den_lo"])
        w["h_ref"][:, h * ML_V:(h + 1) * ML_V] = num / jnp.maximum(jnp.abs(den), g["floor"][:, h:h + 1])
        decay = g["decay"][:, h:h + 1]
        c_ref[d, h] = decay * w["c_prev"] + w["kv"][:, :ML_V]
        n_ref[d, h] = decay * w["n_prev"] + w["kv"][:, ML_V:]
    m_ref[:, :nh] = gate_terms[0]["m_new"]
    m_ref[:, nh:] = gate_terms[1]["m_new"]


def _mlstm(qkvo, side, g_row, b_col, b_row, c0, n0, m0, bsz, seq):
    t = qkvo.shape[0]
    nc = seq // ML_CHUNK
    lc = ML_CHUNK
    nq = ML_HEADS * ML_QK
    nv = ML_HEADS * ML_V
    side_blk = (SIDE_COLS - 128) // 128
    fwd = lambda b, c: b * nc + c
    bwd = lambda b, c: b * nc + (nc - 1 - c)

    def specs(pos):
        return [pl.BlockSpec((lc, nq), lambda b, c: (pos(b, c), 0)),
                pl.BlockSpec((lc, nq), lambda b, c: (pos(b, c), 1)),
                pl.BlockSpec((lc, nv), lambda b, c: (pos(b, c), 1))]

    state = lambda *shape: pl.BlockSpec((None,) + shape, lambda b, c: (b,) + (0,) * len(shape))
    in_specs = specs(fwd) + specs(bwd) + [
        pl.BlockSpec((lc, 128), lambda b, c: (fwd(b, c), side_blk)),
        pl.BlockSpec((lc, 128), lambda b, c: (bwd(b, c), side_blk)),
        pl.BlockSpec((None, 4 * ML_HEADS, lc), lambda b, c: (fwd(b, c), 0, 0)),
        pl.BlockSpec((None, 4 * ML_HEADS, lc), lambda b, c: (bwd(b, c), 0, 0)),
        pl.BlockSpec((1, 4 * ML_HEADS), lambda b, c: (0, 0)),
        pl.BlockSpec((4 * ML_HEADS, 1), lambda b, c: (0, 0)),
        state(2, ML_HEADS, ML_QK, ML_V), state(2, ML_HEADS, ML_QK, ML_V), state(1, 2 * ML_HEADS)]
    out_specs = [pl.BlockSpec((lc, nv), lambda b, c: (fwd(b, c), 0)),
                 pl.BlockSpec((lc, nv), lambda b, c: (bwd(b, c), 0)),
                 state(2, ML_HEADS, ML_QK, ML_V), state(2, ML_HEADS, ML_QK, ML_V), state(1, 2 * ML_HEADS)]
    out_shape = [jax.ShapeDtypeStruct((t, nv), F32), jax.ShapeDtypeStruct((t, nv), F32),
                 jax.ShapeDtypeStruct((bsz, 2, ML_HEADS, ML_QK, ML_V), F32),
                 jax.ShapeDtypeStruct((bsz, 2, ML_HEADS, ML_QK, ML_V), F32),
                 jax.ShapeDtypeStruct((bsz, 1, 2 * ML_HEADS), F32)]
    return pl.pallas_call(
        _mlstm_kernel, grid=(bsz, nc), in_specs=in_specs, out_specs=out_specs, out_shape=out_shape,
        compiler_params=_params(2), name="mlstm",
    )(qkvo, qkvo, qkvo, qkvo, qkvo, qkvo, side, side, g_row, g_row, b_col, b_row, c0, n0, m0)


def _mlstm_post_kernel(hf_ref, hb_ref, o_ref, g_ref, y_ref):
    for h in range(ML_HEADS):
        sl = slice(h * ML_V, (h + 1) * ML_V)
        hn = _rms(hf_ref[:, sl] + hb_ref[:, sl]) * g_ref[:, sl]
        y_ref[:, sl] = (hn * _sigmoid(o_ref[:, sl])).astype(BF)


def _mlstm_post(hf, hb, qkvo, gain):
    t, nv = hf.shape
    tm = 512
    blk = pl.BlockSpec((tm, nv), lambda i: (i, 0))
    return pl.pallas_call(
        _mlstm_post_kernel, grid=(t // tm,),
        in_specs=[blk, blk, pl.BlockSpec((tm, nv), lambda i: (i, 2)), pl.BlockSpec((1, nv), lambda i: (0, 0))],
        out_specs=blk, out_shape=jax.ShapeDtypeStruct((t, nv), BF),
        compiler_params=_params(1), name="mlstm_post",
    )(hf, hb, qkvo, gain.reshape(1, nv))


def _softmax_pair(s1, s2):
    m = jnp.max(s1, axis=-1, keepdims=True)
    if s2 is not None:
        m = jnp.maximum(m, jnp.max(s2, axis=-1, keepdims=True))
    e1 = jnp.exp(s1 - m)
    den = jnp.sum(e1, axis=-1, keepdims=True)
    if s2 is None:
        return e1 / den, None
    e2 = jnp.exp(s2 - m)
    den = den + jnp.sum(e2, axis=-1, keepdims=True)
    return e1 / den, e2 / den


def _mla_kernel(*refs, with_ctx):
    if with_ctx:
        qa_ref, kv_ref, side_ref, cq_ref, sq_ref, ck_ref, sk_ref, kvc_ref, krc_ref, o_ref = refs
    else:
        qa_ref, kv_ref, side_ref, o_ref = refs
    scale = (MLA_NOPE + MLA_ROPE) ** -0.5
    nope_cols = MLA_HEADS * MLA_NOPE
    q_rope = qa_ref[:, nope_cols:]
    k_rope = side_ref[...]
    if with_ctx:
        q_rope_rot = _rope(q_rope, cq_ref[...], sq_ref[...], MLA_ROPE // 4)
        k_rope_rot = _rope(k_rope, ck_ref[...], sk_ref[...], MLA_ROPE // 4)[:, :MLA_ROPE]
        k_rope_ctx = krc_ref[...]
    else:
        q_rope_rot = q_rope
        k_rope_rot = k_rope[:, :MLA_ROPE]
    for h in range(MLA_HEADS):
        q_n = qa_ref[:, h * MLA_NOPE:(h + 1) * MLA_NOPE]
        rs = slice(h * MLA_ROPE, (h + 1) * MLA_ROPE)
        kv0 = h * (MLA_NOPE + MLA_V)
        k_n = kv_ref[:, kv0:kv0 + MLA_NOPE]
        v = kv_ref[:, kv0 + MLA_NOPE:kv0 + MLA_NOPE + MLA_V]
        s1 = (_bdot_nt(q_n, k_n) + _bdot_nt(q_rope_rot[:, rs], k_rope_rot)) * scale
        if with_ctx:
            s2 = (_bdot_nt(q_n, kvc_ref[:, kv0:kv0 + MLA_NOPE]) + _bdot_nt(q_rope[:, rs], k_rope_ctx)) * scale
        else:
            s2 = None
        p1, p2 = _softmax_pair(s1, s2)
        out = _bdot(p1, v)
        if with_ctx:
            out = out + _bdot(p2, kvc_ref[:, kv0 + MLA_NOPE:kv0 + MLA_NOPE + MLA_V])
        o_ref[:, h * MLA_V:(h + 1) * MLA_V] = out.astype(BF)


def _mla_attention(qa, kv, side, bsz, seq, ctx=None):
    t = qa.shape[0]
    tq = 256
    nq = seq // tq
    side_blk = (SIDE_COLS - 128) // 128
    nkv = MLA_HEADS * (MLA_NOPE + MLA_V)
    in_specs = [pl.BlockSpec((tq, qa.shape[1]), lambda b, i: (b * nq + i, 0)),
                pl.BlockSpec((seq, nkv), lambda b, i: (b, 0)),
                pl.BlockSpec((seq, 128), lambda b, i: (b, side_blk))]
    args = [qa, kv, side]
    if ctx is not None:
        cos_q, sin_q, cos_k, sin_k, kvc, krc = ctx
        past = krc.shape[1]
        in_specs += [pl.BlockSpec((tq, cos_q.shape[1]), lambda b, i: (i, 0)),
                     pl.BlockSpec((tq, cos_q.shape[1]), lambda b, i: (i, 0)),
                     pl.BlockSpec((seq, 128), lambda b, i: (0, 0)),
                     pl.BlockSpec((seq, 128), lambda b, i: (0, 0)),
                     pl.BlockSpec((past, nkv), lambda b, i: (b, 0)),
                     pl.BlockSpec((None, past, MLA_ROPE), lambda b, i: (b, 0, 0))]
        args += [cos_q, sin_q, cos_k, sin_k, kvc, krc]
    nout = MLA_HEADS * MLA_V
    return pl.pallas_call(
        functools.partial(_mla_kernel, with_ctx=ctx is not None),
        grid=(bsz, nq), in_specs=in_specs,
        out_specs=pl.BlockSpec((tq, nout), lambda b, i: (b * nq + i, 0)),
        out_shape=jax.ShapeDtypeStruct((t, nout), BF),
        compiler_params=_params(2), name="mla_attention",
    )(*args)


def _gqa_kernel(*refs, with_ctx):
    if with_ctx:
        q_ref, k_ref, v_ref, gq_ref, gk_ref, cq_ref, sq_ref, ck_ref, sk_ref, kc_ref, vc_ref, o_ref = refs
    else:
        q_ref, k_ref, v_ref, gq_ref, gk_ref, o_ref, kn_ref, vo_ref = refs
    hd = GQA_HEAD_DIM
    scale = hd ** -0.5
    k_n = _rms(k_ref[...]) * gk_ref[...]
    v = v_ref[...]
    if with_ctx:
        k_src = _rope(k_n, ck_ref[...], sk_ref[...], hd // 4)
    else:
        k_src = k_n

        @pl.when(pl.program_id(2) == 0)
        def _():
            kn_ref[...] = k_n
            vo_ref[...] = v

    for r in range(GQA_HEADS // GQA_KV_HEADS):
        sl = slice(r * hd, (r + 1) * hd)
        q_n = _rms(q_ref[:, sl]) * gq_ref[...]
        if with_ctx:
            s1 = _bdot_nt(_rope(q_n, cq_ref[...], sq_ref[...], hd // 4), k_src) * scale
            s2 = _bdot_nt(q_n, kc_ref[...]) * scale
        else:
            s1, s2 = _bdot_nt(q_n, k_src) * scale, None
        p1, p2 = _softmax_pair(s1, s2)
        out = _bdot(p1, v)
        if with_ctx:
            out = out + _bdot(p2, vc_ref[...])
        o_ref[:, sl] = out.astype(BF)


def _gqa_attention(qkv, g_q, g_k, bsz, seq, ctx=None):
    t = qkv.shape[0]
    hd = GQA_HEAD_DIM
    rep = GQA_HEADS // GQA_KV_HEADS
    tq = 256
    nq = seq // tq
    in_specs = [pl.BlockSpec((tq, rep * hd), lambda b, g, i: (b * nq + i, g)),
                pl.BlockSpec((seq, hd), lambda b, g, i: (b, GQA_HEADS + g)),
                pl.BlockSpec((seq, hd), lambda b, g, i: (b, GQA_HEADS + GQA_KV_HEADS + g)),
                pl.BlockSpec((1, hd), lambda b, g, i: (0, 0)),
                pl.BlockSpec((1, hd), lambda b, g, i: (0, 0))]
    args = [qkv, qkv, qkv, g_q.reshape(1, hd), g_k.reshape(1, hd)]
    o_spec = pl.BlockSpec((tq, rep * hd), lambda b, g, i: (b * nq + i, g))
    o_shape = jax.ShapeDtypeStruct((t, GQA_HEADS * hd), BF)
    if ctx is not None:
        cos_t, sin_t, kc, vc, j = ctx
        past = kc.shape[3]
        cache = pl.BlockSpec((None, None, None, past, hd), lambda b, g, i: (b, j, g, 0, 0))
        in_specs += [pl.BlockSpec((tq, hd), lambda b, g, i: (i, 0)),
                     pl.BlockSpec((tq, hd), lambda b, g, i: (i, 0)),
                     pl.BlockSpec((seq, hd), lambda b, g, i: (0, 0)),
                     pl.BlockSpec((seq, hd), lambda b, g, i: (0, 0)),
                     cache, cache]
        args += [cos_t, sin_t, cos_t, sin_t, kc, vc]
        out_specs, out_shape = o_spec, o_shape
    else:
        head_major = pl.BlockSpec((None, None, seq, hd), lambda b, g, i: (b, g, 0, 0))
        out_specs = [o_spec, head_major, head_major]
        kv_shape = jax.ShapeDtypeStruct((bsz, GQA_KV_HEADS, seq, hd), F32)
        out_shape = [o_shape, kv_shape, kv_shape]
    return pl.pallas_call(
        functools.partial(_gqa_kernel, with_ctx=ctx is not None),
        grid=(bsz, GQA_KV_HEADS, nq), in_specs=in_specs, out_specs=out_specs, out_shape=out_shape,
        compiler_params=_params(3), name="gqa_attention",
    )(*args)


RANK_BLOCK = 256
GATHER_ROWS = 512


def _dispatch_kernel(ar_ref, ac_ref, h_ref, xe_ref, gate_ref, rankc_ref, rankr_ref, *, cap):
    n_groups, group, s = ar_ref.shape
    g = pl.program_id(1)

    @pl.when(g == 0)
    def _():
        blk = min(RANK_BLOCK, s)
        rankr_ref[...] = jnp.zeros_like(rankr_ref)
        for j in range(s // blk):
            rows = pl.ds(j * blk, blk)
            i0 = lax.broadcasted_iota(jnp.int32, (blk, s), 0) + j * blk
            i1 = lax.broadcasted_iota(jnp.int32, (blk, s), 1)
            sub_first = jnp.where(i0 < i1, 1.0, 0.0)
            for e in range(n_groups * group):
                eg, ei = e // group, e % group
                a_row = ar_ref[eg, ei:ei + 1, :]
                a_col = ac_ref[rows, e:e + 1]
                ahead = jnp.where(a_col > a_row, 1.0, jnp.where(a_col >= a_row, sub_first, 0.0))
                rankr_ref[eg, ei:ei + 1, :] += jnp.sum(ahead, axis=0, keepdims=True)
                rankc_ref[rows, e:e + 1] = (s - 1.0) - jnp.sum(ahead, axis=1, keepdims=True)

    slot = lax.broadcasted_iota(jnp.int32, (group, cap, s), 1).astype(F32)
    pick = rankr_ref[g][:, None, :] == slot
    onehot = jnp.where(pick, 1.0, 0.0).reshape(group * cap, s).astype(BF)
    rows_f32 = jnp.dot(onehot, h_ref[...], preferred_element_type=F32)
    xe_ref[...] = rows_f32.reshape(group, cap, -1).astype(BF)
    gate_ref[...] = jnp.sum(jnp.where(pick, ar_ref[g][:, None, :], 0.0), axis=2, keepdims=True)


def _dispatch(aff_row, aff_col, h, bsz, seq):
    t, d = h.shape
    ne = aff_row.shape[0]
    cap = EC_FACTOR * seq // ne
    group = GATHER_ROWS // cap
    n_groups = ne // group
    return pl.pallas_call(
        functools.partial(_dispatch_kernel, cap=cap),
        grid=(bsz, n_groups),
        in_specs=[pl.BlockSpec((n_groups, group, seq), lambda b, g: (0, 0, b)),
                  pl.BlockSpec((seq, ne), lambda b, g: (b, 0)),
                  pl.BlockSpec((seq, d), lambda b, g: (b, 0))],
        out_specs=[pl.BlockSpec((group, cap, d), lambda b, g: (g, b, 0)),
                   pl.BlockSpec((group, cap, 1), lambda b, g: (g, b, 0)),
                   pl.BlockSpec((seq, ne), lambda b, g: (b, 0))],
        out_shape=[jax.ShapeDtypeStruct((ne, bsz * cap, d), BF),
                   jax.ShapeDtypeStruct((ne, bsz * cap, 1), F32),
                   jax.ShapeDtypeStruct((t, ne), F32)],
        scratch_shapes=[pltpu.VMEM((n_groups, group, seq), F32)],
        compiler_params=_params(2), name="ec_dispatch",
    )(aff_row.reshape(n_groups, group, t), aff_col, h)


def _experts_kernel(xp_ref, xs_ref, gp_ref, gs_ref, wg_ref, wu_ref, wd_ref, yp_ref, ys_ref, accp_ref, accs_ref):
    f = pl.program_id(1)
    wg = wg_ref[...].astype(BF)
    wu = wu_ref[...].astype(BF)
    wd = wd_ref[...].astype(BF)
    for x_ref, g_ref, y_ref, acc_ref in ((xp_ref, gp_ref, yp_ref, accp_ref), (xs_ref, gs_ref, ys_ref, accs_ref)):
        x = x_ref[...]
        a = jnp.dot(x, wg, preferred_element_type=F32)
        u = jnp.dot(x, wu, preferred_element_type=F32)
        part = jnp.dot((a * _sigmoid(a) * u).astype(BF), wd, preferred_element_type=F32)

        @pl.when(f == 0)
        def _():
            acc_ref[...] = part

        @pl.when(f > 0)
        def _():
            acc_ref[...] += part

        @pl.when(f == pl.num_programs(1) - 1)
        def _():
            y_ref[...] = (acc_ref[...] * g_ref[...]).astype(BF)


def _experts(xe_p, xe_s, gate_p, gate_s, w_gate, w_up, w_down, layer):
    ne, mp, d = xe_p.shape
    ms = xe_s.shape[1]
    fdim = w_gate.shape[-1]
    tf = 256
    rows = lambda m, last: pl.BlockSpec((None, m, last), lambda e, f: (e, 0, 0))
    return pl.pallas_call(
        _experts_kernel, grid=(ne, fdim // tf),
        in_specs=[rows(mp, d), rows(ms, d), rows(mp, 1), rows(ms, 1),
                  pl.BlockSpec((None, None, d, tf), lambda e, f: (layer, e, 0, f)),
                  pl.BlockSpec((None, None, d, tf), lambda e, f: (layer, e, 0, f)),
                  pl.BlockSpec((None, None, tf, d), lambda e, f: (layer, e, f, 0))],
        out_specs=[rows(mp, d), rows(ms, d)],
        out_shape=[jax.ShapeDtypeStruct((ne, mp, d), BF), jax.ShapeDtypeStruct((ne, ms, d), BF)],
        scratch_shapes=[pltpu.VMEM((mp, d), F32), pltpu.VMEM((ms, d), F32)],
        compiler_params=_params(2), name="ec_experts",
    )(xe_p, xe_s, gate_p, gate_s, w_gate, w_up, w_down)


RANK_RADIX = 32


def _combine_kernel(y_ref, rank_ref, x_ref, g_ref, o_ref, place_ref, *, cap):
    ne = y_ref.shape[0]
    n_slots = ne * cap

    @pl.when(pl.program_id(1) == 0)
    def _():
        rank = rank_ref[...]
        hi = jnp.floor(rank * (1.0 / RANK_RADIX))
        lo = rank - RANK_RADIX * hi
        lane = lax.broadcasted_iota(jnp.int32, (ne, n_slots), 1)
        expert = lax.broadcasted_iota(jnp.int32, (ne, n_slots), 0)
        own = (lane >= expert * cap) & (lane < (expert + 1) * cap)
        spread_hi = jnp.where(own, float(RANK_RADIX), 0.0).astype(BF)
        spread_lo = jnp.where(own, 1.0, 0.0).astype(BF)
        spread = (jnp.dot(hi.astype(BF), spread_hi, preferred_element_type=F32)
                  + jnp.dot(lo.astype(BF), spread_lo, preferred_element_type=F32))
        slot = (lax.broadcasted_iota(jnp.int32, (1, n_slots), 1) & (cap - 1)).astype(F32)
        place_ref[...] = jnp.where(spread == slot, 1.0, 0.0).astype(BF)

    y = y_ref[...].reshape(n_slots, y_ref.shape[2])
    acc = jnp.dot(place_ref[...], y, preferred_element_type=F32)
    o_ref[...] = x_ref[...] + g_ref[...] * acc


def _combine(y, rank, x, gate, bsz, seq):
    t, d = x.shape
    ne = y.shape[0]
    cap = EC_FACTOR * seq // ne
    assert seq <= RANK_RADIX * RANK_RADIX and cap & (cap - 1) == 0
    td = 1024
    return pl.pallas_call(
        functools.partial(_combine_kernel, cap=cap),
        grid=(bsz, d // td),
        in_specs=[pl.BlockSpec((ne, cap, td), lambda b, j: (0, b, j)),
                  pl.BlockSpec((seq, ne), lambda b, j: (b, 0)),
                  pl.BlockSpec((seq, td), lambda b, j: (b, j)),
                  pl.BlockSpec((None, 1, td), lambda b, j: (b if gate.shape[0] > 1 else 0, 0, j))],
        out_specs=pl.BlockSpec((seq, td), lambda b, j: (b, j)),
        out_shape=jax.ShapeDtypeStruct((t, d), F32),
        scratch_shapes=[pltpu.VMEM((seq, ne * cap), BF)],
        compiler_params=_params(2), name="ec_combine",
    )(y, rank, x, gate)


def _final_norm_kernel(x_ref, g_ref, o_ref):
    o_ref[...] = _rms(x_ref[...]) * g_ref[...]


def _final_norm(x, gain):
    t, d = x.shape
    tm = 256
    blk = pl.BlockSpec((tm, d), lambda i: (i, 0))
    return pl.pallas_call(
        _final_norm_kernel, grid=(t // tm,),
        in_specs=[blk, pl.BlockSpec((1, d), lambda i: (0, 0))],
        out_specs=blk, out_shape=jax.ShapeDtypeStruct((t, d), F32),
        compiler_params=_params(1), name="final_norm",
    )(x, gain.reshape(1, d))


def _rope_tables(n_tokens, rot_dim):
    rows = n_tokens // GRID_W
    row = jnp.repeat(jnp.arange(rows), GRID_W).astype(F32)
    col = jnp.tile(jnp.arange(GRID_W), rows).astype(F32)
    quarter = rot_dim // 4
    inv = ROPE_THETA ** (-jnp.arange(quarter, dtype=F32) / quarter)
    a_row, a_col = row[:, None] * inv, col[:, None] * inv
    cos_t = jnp.concatenate([jnp.cos(a_row), jnp.cos(a_row), jnp.cos(a_col), jnp.cos(a_col)], axis=-1)
    sin_t = jnp.concatenate([-jnp.sin(a_row), jnp.sin(a_row), -jnp.sin(a_col), jnp.sin(a_col)], axis=-1)
    return cos_t, sin_t


def kernel(x_prompt, x_sample, state_mlstm_c, state_mlstm_n, state_mlstm_m, cache_mla_ckv, cache_mla_krope,
           cache_gqa_k, cache_gqa_v, c, c_ctx, w_mod, b_mod, w_in_even, b_igate, b_fgate, g_mlstm, g_cq, w_uq,
           g_ckv, w_ukv, w_out_even, w_in_odd, g_qnorm, g_knorm, w_out_odd, w_router, w_expert_gate,
           w_expert_up, w_expert_down, g_final):
    d = D_MODEL
    bp, sp, _ = x_prompt.shape
    bs, ss, _ = x_sample.shape
    depth = w_mod.shape[0]
    nh = ML_HEADS
    streams = {"p": (bp, sp), "s": (bs, ss)}
    x = {"p": x_prompt.reshape(bp * sp, d), "s": x_sample.reshape(bs * ss, d)}

    c8 = jnp.concatenate([c_ctx[None], c, jnp.zeros((8 - 1 - bs, d), F32)], axis=0)
    mod_all = _mod_vectors(c8, w_mod, b_mod).reshape(depth, 8, 6, 1, d)

    def mod(layer, key, idx):
        rows = mod_all[layer, 0:1, idx] if key == "p" else mod_all[layer, 1:1 + bs, idx]
        return rows

    new_even, new_odd = [], []
    for layer in range(depth):
        j = layer // 2
        if layer % 2 == 0:
            w_in = w_in_even[j]
            w_side = jnp.concatenate(
                [w_in[:, QKVO_COLS + 4 * nh:],
                 w_in[:, QKVO_COLS:QKVO_COLS + 4 * nh],
                 jnp.zeros((d, 128 - MLA_ROPE - 4 * nh), F32)], axis=1)[None]
            w_q = w_uq[j].reshape(MLA_Q_RANK, MLA_HEADS, MLA_NOPE + MLA_ROPE)
            w_q = jnp.concatenate([w_q[:, :, :MLA_NOPE].reshape(MLA_Q_RANK, -1),
                                   w_q[:, :, MLA_NOPE:].reshape(MLA_Q_RANK, -1)], axis=1)
            bias_col = jnp.concatenate([b_igate[j].reshape(1, -1), b_fgate[j].reshape(1, -1)], axis=1)
            bias_row = bias_col.reshape(-1, 1)
            cos64, sin64 = _rope_tables(ss, MLA_ROPE)
            cos_q, sin_q = jnp.tile(cos64, (1, MLA_HEADS)), jnp.tile(sin64, (1, MLA_HEADS))
            pad = jnp.zeros((ss, 128 - MLA_ROPE), F32)
            cos_k, sin_k = jnp.concatenate([cos64, pad], axis=1), jnp.concatenate([sin64, pad], axis=1)
            kvc = _mm([cache_mla_ckv[:, j].reshape(-1, MLA_KV_RANK)], w_ukv, j, w_ukv.shape[-1], 512,
                      name="mla_ctx_expand")
            for key, (bsz, seq) in streams.items():
                h = _norm_mod(x[key], mod(layer, key, 0), mod(layer, key, 1), seq)
                qkvo = _mm([h], w_in_even, j, QKVO_COLS, 512, name="even_in_main")
                side = _mm([h], w_side, 0, SIDE_COLS, SIDE_COLS // 3, name="even_in_side")
                gates = side[:, SIDE_COLS - 128 + GATE_LANE0:SIDE_COLS - 128 + GATE_LANE0 + 4 * nh]
                g_row = gates.reshape(-1, ML_CHUNK, 4 * nh).transpose(0, 2, 1)
                if key == "p":
                    c0 = jnp.zeros((bsz, 2, nh, ML_QK, ML_V), F32)
                    n0 = jnp.zeros((bsz, 2, nh, ML_QK, ML_V), F32)
                    m0 = jnp.zeros((bsz, 1, 2 * nh), F32)
                else:
                    c0 = state_mlstm_c[:, j]
                    n0 = jnp.broadcast_to(state_mlstm_n[:, j][..., None], c0.shape)
                    m0 = state_mlstm_m[:, j].reshape(bsz, 1, 2 * nh)
                hf, hb, c_fin, n_fin, m_fin = _mlstm(qkvo, side, g_row, bias_col, bias_row, c0, n0, m0, bsz, seq)
                y_ml = _mlstm_post(hf, hb, qkvo, g_mlstm[j])
                qa = _norm_mm(side, 0, g_cq[j], w_q, False, "mla_q_up")
                kv, ckv_n = _norm_mm(side, 1, g_ckv[j], w_ukv[j], True, "mla_kv_up")
                if key == "p":
                    y_a = _mla_attention(qa, kv, side, bsz, seq)
                    k_rope = side[:, SIDE_COLS - 128:SIDE_COLS - 128 + MLA_ROPE]
                    new_even.append((c_fin, n_fin[..., 0], m_fin.reshape(bsz, 2, nh),
                                     ckv_n.reshape(bsz, seq, -1), k_rope.reshape(bsz, seq, -1)))
                else:
                    y_a = _mla_attention(qa, kv, side, bsz, seq,
                                         ctx=(cos_q, sin_q, cos_k, sin_k, kvc, cache_mla_krope[:, j]))
                x[key] = _mm([y_ml, y_a], w_out_even, j, d, 512,
                             residual=(x[key], mod(layer, key, 2), seq), name="even_out")
        else:
            cos_t, sin_t = _rope_tables(ss, GQA_HEAD_DIM)
            for key, (bsz, seq) in streams.items():
                h = _norm_mod(x[key], mod(layer, key, 0), mod(layer, key, 1), seq)
                qkv = _mm([h], w_in_odd, j, w_in_odd.shape[-1], 512, name="odd_in")
                if key == "p":
                    o, k_n, v = _gqa_attention(qkv, g_qnorm[j], g_knorm[j], bsz, seq)
                    new_odd.append((k_n, v))
                else:
                    o = _gqa_attention(qkv, g_qnorm[j], g_knorm[j], bsz, seq,
                                       ctx=(cos_t, sin_t, cache_gqa_k, cache_gqa_v, j))
                x[key] = _mm([o], w_out_odd, j, d, 512,
                             residual=(x[key], mod(layer, key, 2), seq), name="odd_out")
        routed = {}
        for key, (bsz, seq) in streams.items():
            h, aff = _norm_mod(x[key], mod(layer, key, 3), mod(layer, key, 4), seq, w_router=w_router, layer=layer)
            routed[key] = _dispatch(aff.T, aff, h, bsz, seq)
        y_p, y_s = _experts(routed["p"][0], routed["s"][0], routed["p"][1], routed["s"][1],
                            w_expert_gate, w_expert_up, w_expert_down, layer)
        for key, y in (("p", y_p), ("s", y_s)):
            bsz, seq = streams[key]
            x[key] = _combine(y, routed[key][2], x[key], mod(layer, key, 5), bsz, seq)

    y_prompt = _final_norm(x["p"], g_final).reshape(bp, sp, d)
    y_sample = _final_norm(x["s"], g_final).reshape(bs, ss, d)
    new_c = jnp.stack([e[0] for e in new_even], axis=1)
    new_n = jnp.stack([e[1] for e in new_even], axis=1)
    new_m = jnp.stack([e[2] for e in new_even], axis=1)
    new_ckv = jnp.stack([e[3] for e in new_even], axis=1)
    new_krope = jnp.stack([e[4] for e in new_even], axis=1)
    new_k = jnp.stack([e[0] for e in new_odd], axis=1)
    new_v = jnp.stack([e[1] for e in new_odd], axis=1)
    return (y_prompt, y_sample, new_c, new_n, new_m, new_ckv, new_krope, new_k, new_v)
```

```python
import functools

import jax
import jax.numpy as jnp
from jax import lax
from jax.experimental import pallas as pl
from jax.experimental.pallas import tpu as pltpu

BF = jnp.bfloat16
F32 = jnp.float32

D_MODEL = 2048
GRID_W = 64
ROPE_THETA = 10000.0
NORM_EPS = 1e-6
ML_HEADS = 8
ML_QK = 64
ML_V = 128
ML_CHUNK = 64
MLA_HEADS = 8
MLA_Q_RANK = 512
MLA_KV_RANK = 512
MLA_NOPE = 128
MLA_ROPE = 64
MLA_V = 128
GQA_HEADS = 16
GQA_KV_HEADS = 4
GQA_HEAD_DIM = 128
N_EXPERTS = 16
EXPERT_DIM = 1024
EC_FACTOR = 2

QKVO_COLS = 2 * ML_HEADS * ML_QK + 2 * ML_HEADS * ML_V
SIDE_COLS = MLA_Q_RANK + MLA_KV_RANK + 128
GATE_LANE0 = MLA_ROPE

VMEM_LIMIT_BYTES = 56 * 1024 * 1024


def _params(n_axes):
    return pltpu.CompilerParams(dimension_semantics=("arbitrary",) * n_axes,
                                vmem_limit_bytes=VMEM_LIMIT_BYTES)


def _bdot(a, b):
    return jnp.dot(a.astype(BF), b.astype(BF), preferred_element_type=F32)


def _bdot_nt(a, b):
    return lax.dot_general(a.astype(BF), b.astype(BF), (((1,), (1,)), ((), ())),
                           preferred_element_type=F32)


def _bdot_tn(a, b):
    return lax.dot_general(a.astype(BF), b.astype(BF), (((0,), (0,)), ((), ())),
                           preferred_element_type=F32)


def _sigmoid(x):
    return 1.0 / (1.0 + jnp.exp(-x))


def _log_sigmoid(x):
    return jnp.minimum(x, 0.0) - jnp.log1p(jnp.exp(-jnp.abs(x)))


def _rms(x):
    return x * lax.rsqrt(jnp.mean(x * x, axis=-1, keepdims=True) + NORM_EPS)


def _rope(x, cos_t, sin_t, quarter):
    width = x.shape[-1]
    axis = x.ndim - 1
    lane = lax.broadcasted_iota(jnp.int32, x.shape, axis)
    partner = jnp.where((lane & quarter) == 0,
                        pltpu.roll(x, width - quarter, axis=axis),
                        pltpu.roll(x, quarter, axis=axis))
    return x * cos_t + partner * sin_t


def _mod_kernel(c_ref, w_ref, b_ref, o_ref):
    c = c_ref[...]
    o_ref[...] = _bdot(c * _sigmoid(c), w_ref[...]) + b_ref[...]


def _mod_vectors(c8, w_mod, b_mod):
    n_layers, k, n = w_mod.shape
    tn = 1024
    return pl.pallas_call(
        _mod_kernel,
        grid=(n_layers, n // tn),
        in_specs=[pl.BlockSpec((8, k), lambda l, j: (0, 0)),
                  pl.BlockSpec((None, k, tn), lambda l, j: (l, 0, j)),
                  pl.BlockSpec((None, 1, tn), lambda l, j: (l, 0, j))],
        out_specs=pl.BlockSpec((None, 8, tn), lambda l, j: (l, 0, j)),
        out_shape=jax.ShapeDtypeStruct((n_layers, 8, n), F32),
        compiler_params=_params(2),
        name="mod_vectors",
    )(c8, w_mod, b_mod.reshape(n_layers, 1, n))


def _norm_mod_kernel(x_ref, sh_ref, sc_ref, h_ref):
    h_ref[...] = (_rms(x_ref[...]) * (1.0 + sc_ref[...]) + sh_ref[...]).astype(BF)


def _norm_router_kernel(x_ref, sh_ref, sc_ref, wr_ref, h_ref, aff_ref):
    h = (_rms(x_ref[...]) * (1.0 + sc_ref[...]) + sh_ref[...]).astype(BF)
    h_ref[...] = h
    logits = _bdot(h, wr_ref[...])
    e = jnp.exp(logits - jnp.max(logits, axis=-1, keepdims=True))
    aff_ref[...] = e / jnp.sum(e, axis=-1, keepdims=True)


def _batch_of_tile(n_vectors, tm, seq):
    if n_vectors == 1:
        return lambda i: 0
    assert seq % tm == 0
    return lambda i: (i * tm) // seq


def _norm_mod(x, shift, scale, seq, w_router=None, layer=0):
    t, d = x.shape
    tm = 256
    which = _batch_of_tile(shift.shape[0], tm, seq)
    vec = pl.BlockSpec((None, 1, d), lambda i: (which(i), 0, 0))
    x_spec = pl.BlockSpec((tm, d), lambda i: (i, 0))
    if w_router is None:
        return pl.pallas_call(
            _norm_mod_kernel, grid=(t // tm,),
            in_specs=[x_spec, vec, vec], out_specs=x_spec,
            out_shape=jax.ShapeDtypeStruct((t, d), BF),
            compiler_params=_params(1), name="norm_mod",
        )(x, shift, scale)
    ne = w_router.shape[-1]
    return pl.pallas_call(
        _norm_router_kernel, grid=(t // tm,),
        in_specs=[x_spec, vec, vec, pl.BlockSpec((None, d, ne), lambda i: (layer, 0, 0))],
        out_specs=[x_spec, pl.BlockSpec((tm, ne), lambda i: (i, 0))],
        out_shape=[jax.ShapeDtypeStruct((t, d), BF), jax.ShapeDtypeStruct((t, ne), F32)],
        compiler_params=_params(1), name="norm_router",
    )(x, shift, scale, w_router)


def _mm_kernel(*refs, k_sizes, residual, w_rows):
    n_a = len(k_sizes)
    a_refs, w_ref = refs[:n_a], refs[n_a]
    o_ref, wbf_ref = refs[-2], refs[-1]

    @pl.when(pl.program_id(1) == 0)
    def _():
        wbf_ref[...] = w_ref[...].astype(BF)

    acc, off = None, 0
    for a_ref, ks in zip(a_refs, k_sizes):
        if w_rows:
            term = lax.dot_general(a_ref[...].astype(BF), wbf_ref[:, off:off + ks], (((1,), (1,)), ((), ())),
                                   preferred_element_type=F32)
        else:
            term = jnp.dot(a_ref[...].astype(BF), wbf_ref[off:off + ks, :], preferred_element_type=F32)
        acc = term if acc is None else acc + term
        off += ks
    if residual:
        x_ref, g_ref = refs[n_a + 1], refs[n_a + 2]
        acc = x_ref[...] + g_ref[...] * acc
    o_ref[...] = acc


def _mm(a_list, w3, layer, n_cols, tn, col_blk0=0, residual=None, w_rows=False, name="mm"):
    m = a_list[0].shape[0]
    k_sizes = tuple(a.shape[1] for a in a_list)
    k = sum(k_sizes)
    assert w3.shape[2 if w_rows else 1] == k and n_cols % tn == 0
    tm = min(1024, m)
    in_specs = [pl.BlockSpec((tm, ks), lambda j, i: (i, 0)) for ks in k_sizes]
    if w_rows:
        in_specs.append(pl.BlockSpec((None, tn, k), lambda j, i: (layer, j + col_blk0, 0)))
    else:
        in_specs.append(pl.BlockSpec((None, k, tn), lambda j, i: (layer, 0, j + col_blk0)))
    args = list(a_list) + [w3]
    if residual is not None:
        x, gate, seq = residual
        which = _batch_of_tile(gate.shape[0], tm, seq)
        in_specs.append(pl.BlockSpec((tm, tn), lambda j, i: (i, j)))
        in_specs.append(pl.BlockSpec((None, 1, tn), lambda j, i: (which(i), 0, j)))
        args += [x, gate]
    return pl.pallas_call(
        functools.partial(_mm_kernel, k_sizes=k_sizes, residual=residual is not None, w_rows=w_rows),
        grid=(n_cols // tn, m // tm),
        in_specs=in_specs,
        out_specs=pl.BlockSpec((tm, tn), lambda j, i: (i, j)),
        out_shape=jax.ShapeDtypeStruct((m, n_cols), F32),
        scratch_shapes=[pltpu.VMEM((tn, k) if w_rows else (k, tn), BF)],
        compiler_params=_params(2), name=name,
    )(*args)


def _norm_mm_kernel(x_ref, g_ref, w_ref, *out_refs, with_normed):
    wbf_ref = out_refs[-1]

    @pl.when(pl.program_id(0) == 0)
    def _():
        wbf_ref[...] = w_ref[...].astype(BF)

    xn = _rms(x_ref[...]) * g_ref[...]
    out_refs[0][...] = jnp.dot(xn.astype(BF), wbf_ref[...], preferred_element_type=F32)
    if with_normed:
        out_refs[1][...] = xn


def _norm_mm(x, col_blk, gain, w, with_normed, name):
    t = x.shape[0]
    k, n = w.shape
    tm = 512
    out_specs = [pl.BlockSpec((tm, n), lambda i: (i, 0))]
    out_shape = [jax.ShapeDtypeStruct((t, n), F32)]
    if with_normed:
        out_specs.append(pl.BlockSpec((tm, k), lambda i: (i, 0)))
        out_shape.append(jax.ShapeDtypeStruct((t, k), F32))
    outs = pl.pallas_call(
        functools.partial(_norm_mm_kernel, with_normed=with_normed),
        grid=(t // tm,),
        in_specs=[pl.BlockSpec((tm, k), lambda i: (i, col_blk)),
                  pl.BlockSpec((1, k), lambda i: (0, 0)),
                  pl.BlockSpec((k, n), lambda i: (0, 0))],
        out_specs=out_specs, out_shape=out_shape,
        scratch_shapes=[pltpu.VMEM((k, n), BF)],
        compiler_params=_params(1), name=name,
    )(x, gain.reshape(1, k), w)
    return outs if with_normed else outs[0]


def _split3(x):
    hi = x.astype(BF)
    rest = x - hi.astype(F32)
    mid = rest.astype(BF)
    return hi, mid, (rest - mid.astype(F32)).astype(BF)


def _scan_max(x, reverse):
    n = x.shape[0]
    row = lax.broadcasted_iota(jnp.int32, x.shape, 0)
    k = 1
    while k < n:
        if reverse:
            shifted = jnp.where(row < n - k, pltpu.roll(x, n - k, axis=0), -jnp.inf)
        else:
            shifted = jnp.where(row >= k, pltpu.roll(x, k, axis=0), -jnp.inf)
        x = jnp.maximum(x, shifted)
        k *= 2
    return x


def _mlstm_kernel(qf_ref, kf_ref, vf_ref, qb_ref, kb_ref, vb_ref, gcf_ref, gcb_ref, grf_ref, grb_ref,
                  bc_ref, br_ref, c0_ref, n0_ref, m0_ref, hf_ref, hb_ref, c_ref, n_ref, m_ref):
    @pl.when(pl.program_id(1) == 0)
    def _():
        c_ref[...] = c0_ref[...]
        n_ref[...] = n0_ref[...]
        m_ref[...] = m0_ref[...]

    nh, lc = ML_HEADS, ML_CHUNK
    row = lax.broadcasted_iota(jnp.int32, (lc, lc), 0)
    col = lax.broadcasted_iota(jnp.int32, (lc, lc), 1)
    lower = col <= row
    upper = col >= row
    ones_v = jnp.ones((lc, ML_V), BF)
    directions = ((qf_ref, kf_ref, vf_ref, gcf_ref, grf_ref, hf_ref, lower, upper),
                  (qb_ref, kb_ref, vb_ref, gcb_ref, grb_ref, hb_ref, upper, lower))
    gate_terms = []
    for d, (_, _, _, gc_ref, gr_ref, _, allowed, allowed_t) in enumerate(directions):
        g_col = gc_ref[:, GATE_LANE0:GATE_LANE0 + 4 * nh] + bc_ref[...]
        g_row = gr_ref[...] + br_ref[...]
        i_col = g_col[:, nh * d:nh * (d + 1)]
        f_col = _log_sigmoid(g_col[:, 2 * nh + nh * d:2 * nh + nh * (d + 1)])
        i_row = g_row[nh * d:nh * (d + 1), :]
        f_row = _log_sigmoid(g_row[2 * nh + nh * d:2 * nh + nh * (d + 1), :])
        tri = jnp.where(allowed, 1.0, 0.0).astype(BF)
        tri_t = jnp.where(allowed_t, 1.0, 0.0).astype(BF)
        b_col = sum(jnp.dot(tri, part, preferred_element_type=F32) for part in _split3(f_col))
        b_row = sum(jnp.dot(part, tri_t, preferred_element_type=F32) for part in _split3(f_row))
        b_end = jnp.sum(f_col, axis=0, keepdims=True)
        m_prev = m_ref[:, nh * d:nh * (d + 1)]
        m_t = b_col + jnp.maximum(m_prev, _scan_max(i_col - b_col, reverse=d == 1))
        g_col_end = b_end - b_col + i_col
        m_new = jnp.maximum(b_end + m_prev, jnp.max(g_col_end, axis=0, keepdims=True))
        gate_terms.append(dict(
            u=b_col - m_t, r_row=i_row - b_row, w_inter=jnp.exp(b_col + m_prev - m_t), floor=jnp.exp(-m_t),
            k_scale=jnp.exp(g_col_end - m_new), decay=jnp.exp(b_end + m_prev - m_new), m_new=m_new))
    work = []
    for d, (q_ref, k_ref, v_ref, _, _, h_ref, allowed, _) in enumerate(directions):
        for h in range(nh):
            w = dict(d=d, h=h, h_ref=h_ref, allowed=allowed, g=gate_terms[d])
            w["q"] = (q_ref[:, h * ML_QK:(h + 1) * ML_QK] * (ML_QK ** -0.5)).astype(BF)
            w["k"] = k_ref[:, h * ML_QK:(h + 1) * ML_QK]
            w["v1"] = jnp.concatenate([v_ref[:, h * ML_V:(h + 1) * ML_V].astype(BF), ones_v], axis=1)
            work.append(w)
    for w in work:
        w["qk"] = _bdot_nt(w["q"], w["k"])
    for w in work:
        d, h = w["d"], w["h"]
        w["c_prev"], w["n_prev"] = c_ref[d, h], n_ref[d, h]
        state = jnp.concatenate([w["c_prev"], w["n_prev"]], axis=1).astype(BF)
        w["q_state"] = jnp.dot(w["q"], state, preferred_element_type=F32)
    for w in work:
        h, g = w["h"], w["g"]
        kw = w["k"] * g["k_scale"][:, h:h + 1]
        w["kv"] = _bdot_tn(kw, w["v1"])
    for w in work:
        h, g = w["h"], w["g"]
        dmat = jnp.where(w["allowed"], g["u"][:, h:h + 1] + g["r_row"][h:h + 1, :], -jnp.inf)
        sw = w["qk"] * jnp.exp(dmat)
        sw_hi = sw.astype(BF)
        sw_lo = (sw - sw_hi.astype(F32)).astype(BF)
        w["pv"] = jnp.dot(sw_hi, w["v1"], preferred_element_type=F32)
        w["den_lo"] = jnp.dot(sw_lo, ones_v, preferred_element_type=F32)
    for w in work:
        d, h, g = w["d"], w["h"], w["g"]
        w_inter = g["w_inter"][:, h:h + 1]
        num = w_inter * w["q_state"][:, :ML_V] + w["pv"][:, :ML_V]
        den = w_inter * w["q_state"][:, ML_V:] + (w["pv"][:, ML_V:] + w["den_lo"])
        w["h_ref"][:, h * ML_V:(h + 1) * ML_V] = num / jnp.maximum(jnp.abs(den), g["floor"][:, h:h + 1])
        decay = g["decay"][:, h:h + 1]
        c_ref[d, h] = decay * w["c_prev"] + w["kv"][:, :ML_V]
        n_ref[d, h] = decay * w["n_prev"] + w["kv"][:, ML_V:]
    m_ref[:, :nh] = gate_terms[0]["m_new"]
    m_ref[:, nh:] = gate_terms[1]["m_new"]


def _mlstm(qkvo, side, g_row, b_col, b_row, c0, n0, m0, bsz, seq):
    t = qkvo.shape[0]
    nc = seq // ML_CHUNK
    lc = ML_CHUNK
    nq = ML_HEADS * ML_QK
    nv = ML_HEADS * ML_V
    side_blk = (SIDE_COLS - 128) // 128
    fwd = lambda b, c: b * nc + c
    bwd = lambda b, c: b * nc + (nc - 1 - c)

    def specs(pos):
        return [pl.BlockSpec((lc, nq), lambda b, c: (pos(b, c), 0)),
                pl.BlockSpec((lc, nq), lambda b, c: (pos(b, c), 1)),
                pl.BlockSpec((lc, nv), lambda b, c: (pos(b, c), 1))]

    state = lambda *shape: pl.BlockSpec((None,) + shape, lambda b, c: (b,) + (0,) * len(shape))
    in_specs = specs(fwd) + specs(bwd) + [
        pl.BlockSpec((lc, 128), lambda b, c: (fwd(b, c), side_blk)),
        pl.BlockSpec((lc, 128), lambda b, c: (bwd(b, c), side_blk)),
        pl.BlockSpec((None, 4 * ML_HEADS, lc), lambda b, c: (fwd(b, c), 0, 0)),
        pl.BlockSpec((None, 4 * ML_HEADS, lc), lambda b, c: (bwd(b, c), 0, 0)),
        pl.BlockSpec((1, 4 * ML_HEADS), lambda b, c: (0, 0)),
        pl.BlockSpec((4 * ML_HEADS, 1), lambda b, c: (0, 0)),
        state(2, ML_HEADS, ML_QK, ML_V), state(2, ML_HEADS, ML_QK, ML_V), state(1, 2 * ML_HEADS)]
    out_specs = [pl.BlockSpec((lc, nv), lambda b, c: (fwd(b, c), 0)),
                 pl.BlockSpec((lc, nv), lambda b, c: (bwd(b, c), 0)),
                 state(2, ML_HEADS, ML_QK, ML_V), state(2, ML_HEADS, ML_QK, ML_V), state(1, 2 * ML_HEADS)]
    out_shape = [jax.ShapeDtypeStruct((t, nv), F32), jax.ShapeDtypeStruct((t, nv), F32),
                 jax.ShapeDtypeStruct((bsz, 2, ML_HEADS, ML_QK, ML_V), F32),
                 jax.ShapeDtypeStruct((bsz, 2, ML_HEADS, ML_QK, ML_V), F32),
                 jax.ShapeDtypeStruct((bsz, 1, 2 * ML_HEADS), F32)]
    return pl.pallas_call(
        _mlstm_kernel, grid=(bsz, nc), in_specs=in_specs, out_specs=out_specs, out_shape=out_shape,
        compiler_params=_params(2), name="mlstm",
    )(qkvo, qkvo, qkvo, qkvo, qkvo, qkvo, side, side, g_row, g_row, b_col, b_row, c0, n0, m0)


def _mlstm_post_kernel(hf_ref, hb_ref, o_ref, g_ref, y_ref):
    for h in range(ML_HEADS):
        sl = slice(h * ML_V, (h + 1) * ML_V)
        hn = _rms(hf_ref[:, sl] + hb_ref[:, sl]) * g_ref[:, sl]
        y_ref[:, sl] = (hn * _sigmoid(o_ref[:, sl])).astype(BF)


def _mlstm_post(hf, hb, qkvo, gain):
    t, nv = hf.shape
    tm = 512
    blk = pl.BlockSpec((tm, nv), lambda i: (i, 0))
    return pl.pallas_call(
        _mlstm_post_kernel, grid=(t // tm,),
        in_specs=[blk, blk, pl.BlockSpec((tm, nv), lambda i: (i, 2)), pl.BlockSpec((1, nv), lambda i: (0, 0))],
        out_specs=blk, out_shape=jax.ShapeDtypeStruct((t, nv), BF),
        compiler_params=_params(1), name="mlstm_post",
    )(hf, hb, qkvo, gain.reshape(1, nv))


LOG2_E = 1.4426950408889634


def _softmax_terms(scores, scale):
    c = scale * LOG2_E
    scaled = [s * c for s in scores]
    m = functools.reduce(jnp.maximum, [jnp.max(s, axis=-1, keepdims=True) for s in scaled])
    e = [jnp.exp2(s - m) for s in scaled]
    den = functools.reduce(lambda a, b: a + b, [jnp.sum(x, axis=-1, keepdims=True) for x in e])
    return [x.astype(BF) for x in e], 1.0 / den


def _mla_kernel(*refs, with_ctx):
    if with_ctx:
        qa_ref, kv_ref, side_ref, cq_ref, sq_ref, ck_ref, sk_ref, kvc_ref, krc_ref, o_ref = refs
    else:
        qa_ref, kv_ref, side_ref, o_ref = refs
    scale = (MLA_NOPE + MLA_ROPE) ** -0.5
    nope_cols = MLA_HEADS * MLA_NOPE
    q_rope = qa_ref[:, nope_cols:]
    k_rope = side_ref[...]
    if with_ctx:
        q_rope_rot = _rope(q_rope, cq_ref[...], sq_ref[...], MLA_ROPE // 4).astype(BF)
        k_rope_rot = _rope(k_rope, ck_ref[...], sk_ref[...], MLA_ROPE // 4)[:, :MLA_ROPE].astype(BF)
        k_rope_ctx = krc_ref[...].astype(BF)
        q_rope = q_rope.astype(BF)
    else:
        q_rope_rot = q_rope.astype(BF)
        k_rope_rot = k_rope[:, :MLA_ROPE].astype(BF)
    heads = range(MLA_HEADS)
    rope_cols = [slice(h * MLA_ROPE, (h + 1) * MLA_ROPE) for h in heads]
    kv0 = [h * (MLA_NOPE + MLA_V) for h in heads]
    q_n = [qa_ref[:, h * MLA_NOPE:(h + 1) * MLA_NOPE].astype(BF) for h in heads]
    scores = [[_bdot_nt(q_n[h], kv_ref[:, kv0[h]:kv0[h] + MLA_NOPE])
               + _bdot_nt(q_rope_rot[:, rope_cols[h]], k_rope_rot)] for h in heads]
    if with_ctx:
        for h in heads:
            scores[h].append(_bdot_nt(q_n[h], kvc_ref[:, kv0[h]:kv0[h] + MLA_NOPE])
                             + _bdot_nt(q_rope[:, rope_cols[h]], k_rope_ctx))
    weights = [_softmax_terms(scores[h], scale) for h in heads]
    for h in heads:
        (e, inv_den), v0 = weights[h], kv0[h] + MLA_NOPE
        out = jnp.dot(e[0], kv_ref[:, v0:v0 + MLA_V].astype(BF), preferred_element_type=F32)
        if with_ctx:
            out = out + jnp.dot(e[1], kvc_ref[:, v0:v0 + MLA_V].astype(BF), preferred_element_type=F32)
        o_ref[:, h * MLA_V:(h + 1) * MLA_V] = (out * inv_den).astype(BF)


def _mla_attention(qa, kv, side, bsz, seq, ctx=None):
    t = qa.shape[0]
    tq = 256
    nq = seq // tq
    side_blk = (SIDE_COLS - 128) // 128
    nkv = MLA_HEADS * (MLA_NOPE + MLA_V)
    in_specs = [pl.BlockSpec((tq, qa.shape[1]), lambda b, i: (b * nq + i, 0)),
                pl.BlockSpec((seq, nkv), lambda b, i: (b, 0)),
                pl.BlockSpec((seq, 128), lambda b, i: (b, side_blk))]
    args = [qa, kv, side]
    if ctx is not None:
        cos_q, sin_q, cos_k, sin_k, kvc, krc = ctx
        past = krc.shape[1]
        in_specs += [pl.BlockSpec((tq, cos_q.shape[1]), lambda b, i: (i, 0)),
                     pl.BlockSpec((tq, cos_q.shape[1]), lambda b, i: (i, 0)),
                     pl.BlockSpec((seq, 128), lambda b, i: (0, 0)),
                     pl.BlockSpec((seq, 128), lambda b, i: (0, 0)),
                     pl.BlockSpec((past, nkv), lambda b, i: (b, 0)),
                     pl.BlockSpec((None, past, MLA_ROPE), lambda b, i: (b, 0, 0))]
        args += [cos_q, sin_q, cos_k, sin_k, kvc, krc]
    nout = MLA_HEADS * MLA_V
    return pl.pallas_call(
        functools.partial(_mla_kernel, with_ctx=ctx is not None),
        grid=(bsz, nq), in_specs=in_specs,
        out_specs=pl.BlockSpec((tq, nout), lambda b, i: (b * nq + i, 0)),
        out_shape=jax.ShapeDtypeStruct((t, nout), BF),
        compiler_params=_params(2), name="mla_attention",
    )(*args)


def _gqa_kernel(*refs, with_ctx):
    if with_ctx:
        q_ref, k_ref, v_ref, gq_ref, gk_ref, cq_ref, sq_ref, ck_ref, sk_ref, kc_ref, vc_ref, o_ref = refs
    else:
        q_ref, k_ref, v_ref, gq_ref, gk_ref, o_ref, kn_ref, vo_ref = refs
    hd = GQA_HEAD_DIM
    scale = hd ** -0.5
    k_n = _rms(k_ref[...]) * gk_ref[...]
    v = v_ref[...]
    if with_ctx:
        k_src = _rope(k_n, ck_ref[...], sk_ref[...], hd // 4).astype(BF)
        k_ctx, v_ctx = kc_ref[...].astype(BF), vc_ref[...].astype(BF)
    else:
        k_src = k_n.astype(BF)

        @pl.when(pl.program_id(2) == 0)
        def _():
            kn_ref[...] = k_n
            vo_ref[...] = v

    v = v.astype(BF)
    heads = range(GQA_HEADS // GQA_KV_HEADS)
    cols = [slice(r * hd, (r + 1) * hd) for r in heads]
    q_n = [_rms(q_ref[:, cols[r]]) * gq_ref[...] for r in heads]
    if with_ctx:
        scores = [[_bdot_nt(_rope(q_n[r], cq_ref[...], sq_ref[...], hd // 4), k_src), _bdot_nt(q_n[r], k_ctx)]
                  for r in heads]
    else:
        scores = [[_bdot_nt(q_n[r], k_src)] for r in heads]
    weights = [_softmax_terms(scores[r], scale) for r in heads]
    for r in heads:
        e, inv_den = weights[r]
        out = jnp.dot(e[0], v, preferred_element_type=F32)
        if with_ctx:
            out = out + jnp.dot(e[1], v_ctx, preferred_element_type=F32)
        o_ref[:, cols[r]] = (out * inv_den).astype(BF)


def _gqa_attention(qkv, g_q, g_k, bsz, seq, ctx=None):
    t = qkv.shape[0]
    hd = GQA_HEAD_DIM
    rep = GQA_HEADS // GQA_KV_HEADS
    tq = 256
    nq = seq // tq
    in_specs = [pl.BlockSpec((tq, rep * hd), lambda b, g, i: (b * nq + i, g)),
                pl.BlockSpec((seq, hd), lambda b, g, i: (b, GQA_HEADS + g)),
                pl.BlockSpec((seq, hd), lambda b, g, i: (b, GQA_HEADS + GQA_KV_HEADS + g)),
                pl.BlockSpec((1, hd), lambda b, g, i: (0, 0)),
                pl.BlockSpec((1, hd), lambda b, g, i: (0, 0))]
    args = [qkv, qkv, qkv, g_q.reshape(1, hd), g_k.reshape(1, hd)]
    o_spec = pl.BlockSpec((tq, rep * hd), lambda b, g, i: (b * nq + i, g))
    o_shape = jax.ShapeDtypeStruct((t, GQA_HEADS * hd), BF)
    if ctx is not None:
        cos_t, sin_t, kc, vc, j = ctx
        past = kc.shape[3]
        cache = pl.BlockSpec((None, None, None, past, hd), lambda b, g, i: (b, j, g, 0, 0))
        in_specs += [pl.BlockSpec((tq, hd), lambda b, g, i: (i, 0)),
                     pl.BlockSpec((tq, hd), lambda b, g, i: (i, 0)),
                     pl.BlockSpec((seq, hd), lambda b, g, i: (0, 0)),
                     pl.BlockSpec((seq, hd), lambda b, g, i: (0, 0)),
                     cache, cache]
        args += [cos_t, sin_t, cos_t, sin_t, kc, vc]
        out_specs, out_shape = o_spec, o_shape
    else:
        head_major = pl.BlockSpec((None, None, seq, hd), lambda b, g, i: (b, g, 0, 0))
        out_specs = [o_spec, head_major, head_major]
        kv_shape = jax.ShapeDtypeStruct((bsz, GQA_KV_HEADS, seq, hd), F32)
        out_shape = [o_shape, kv_shape, kv_shape]
    return pl.pallas_call(
        functools.partial(_gqa_kernel, with_ctx=ctx is not None),
        grid=(bsz, GQA_KV_HEADS, nq), in_specs=in_specs, out_specs=out_specs, out_shape=out_shape,
        compiler_params=_params(3), name="gqa_attention",
    )(*args)


RANK_BLOCK = 256
GATHER_ROWS = 512
EXPERT_ROWS = 256


def _dispatch_kernel(ar_ref, ac_ref, h_ref, xe_ref, gate_ref, rankc_ref, rankr_ref, *, cap):
    n_groups, group, s = ar_ref.shape
    g = pl.program_id(1)

    @pl.when(g == 0)
    def _():
        blk = min(RANK_BLOCK, s)
        rankr_ref[...] = jnp.zeros_like(rankr_ref)
        for j in range(s // blk):
            rows = pl.ds(j * blk, blk)
            i0 = lax.broadcasted_iota(jnp.int32, (blk, s), 0) + j * blk
            i1 = lax.broadcasted_iota(jnp.int32, (blk, s), 1)
            sub_first = jnp.where(i0 < i1, 1.0, 0.0)
            for e in range(n_groups * group):
                eg, ei = e // group, e % group
                a_row = ar_ref[eg, ei:ei + 1, :]
                a_col = ac_ref[rows, e:e + 1]
                ahead = jnp.where(a_col > a_row, 1.0, jnp.where(a_col >= a_row, sub_first, 0.0))
                rankr_ref[eg, ei:ei + 1, :] += jnp.sum(ahead, axis=0, keepdims=True)
                rankc_ref[rows, e:e + 1] = (s - 1.0) - jnp.sum(ahead, axis=1, keepdims=True)

    slot = lax.broadcasted_iota(jnp.int32, (group, cap, s), 1).astype(F32)
    pick = rankr_ref[g][:, None, :] == slot
    onehot = jnp.where(pick, 1.0, 0.0).reshape(group * cap, s).astype(BF)
    rows_f32 = jnp.dot(onehot, h_ref[...], preferred_element_type=F32)
    xe_ref[...] = rows_f32.reshape(group, cap, -1).astype(BF)
    gate_ref[...] = jnp.sum(jnp.where(pick, ar_ref[g][:, None, :], 0.0), axis=2, keepdims=True)


def _dispatch(aff_row, aff_col, h, bsz, seq):
    t, d = h.shape
    ne = aff_row.shape[0]
    cap = EC_FACTOR * seq // ne
    group = GATHER_ROWS // cap
    n_groups = ne // group
    return pl.pallas_call(
        functools.partial(_dispatch_kernel, cap=cap),
        grid=(bsz, n_groups),
        in_specs=[pl.BlockSpec((n_groups, group, seq), lambda b, g: (0, 0, b)),
                  pl.BlockSpec((seq, ne), lambda b, g: (b, 0)),
                  pl.BlockSpec((seq, d), lambda b, g: (b, 0))],
        out_specs=[pl.BlockSpec((group, cap, d), lambda b, g: (g, b, 0)),
                   pl.BlockSpec((group, cap, 1), lambda b, g: (g, b, 0)),
                   pl.BlockSpec((seq, ne), lambda b, g: (b, 0))],
        out_shape=[jax.ShapeDtypeStruct((ne, bsz * cap, d), BF),
                   jax.ShapeDtypeStruct((ne, bsz * cap, 1), F32),
                   jax.ShapeDtypeStruct((t, ne), F32)],
        scratch_shapes=[pltpu.VMEM((n_groups, group, seq), F32)],
        compiler_params=_params(2), name="ec_dispatch",
    )(aff_row.reshape(n_groups, group, t), aff_col, h)


def _experts_kernel(xp_ref, xs_ref, gp_ref, gs_ref, wg_ref, wu_ref, wd_ref, yp_ref, ys_ref, accp_ref, accs_ref):
    f = pl.program_id(1)
    tf = wg_ref.shape[1]
    w_gate_up = jnp.concatenate([wg_ref[...].astype(BF), wu_ref[...].astype(BF)], axis=1)
    wd = wd_ref[...].astype(BF)
    groups = ((xp_ref, gp_ref, yp_ref, accp_ref), (xs_ref, gs_ref, ys_ref, accs_ref))

    @pl.when(f == 0)
    def _():
        for _, _, _, acc_ref in groups:
            acc_ref[...] = jnp.zeros_like(acc_ref)

    for x_ref, _, _, acc_ref in groups:
        for r in range(0, x_ref.shape[0], EXPERT_ROWS):
            rows = slice(r, r + EXPERT_ROWS)
            au = jnp.dot(x_ref[rows, :], w_gate_up, preferred_element_type=F32)
            a, u = au[:, :tf], au[:, tf:]
            acc_ref[rows, :] += jnp.dot((a * _sigmoid(a) * u).astype(BF), wd, preferred_element_type=F32)

    @pl.when(f == pl.num_programs(1) - 1)
    def _():
        for _, g_ref, y_ref, acc_ref in groups:
            y_ref[...] = (acc_ref[...] * g_ref[...]).astype(BF)


def _experts(xe_p, xe_s, gate_p, gate_s, w_gate, w_up, w_down, layer):
    ne, mp, d = xe_p.shape
    ms = xe_s.shape[1]
    fdim = w_gate.shape[-1]
    tf = 256
    rows = lambda m, last: pl.BlockSpec((None, m, last), lambda e, f: (e, 0, 0))
    return pl.pallas_call(
        _experts_kernel, grid=(ne, fdim // tf),
        in_specs=[rows(mp, d), rows(ms, d), rows(mp, 1), rows(ms, 1),
                  pl.BlockSpec((None, None, d, tf), lambda e, f: (layer, e, 0, f)),
                  pl.BlockSpec((None, None, d, tf), lambda e, f: (layer, e, 0, f)),
                  pl.BlockSpec((None, None, tf, d), lambda e, f: (layer, e, f, 0))],
        out_specs=[rows(mp, d), rows(ms, d)],
        out_shape=[jax.ShapeDtypeStruct((ne, mp, d), BF), jax.ShapeDtypeStruct((ne, ms, d), BF)],
        scratch_shapes=[pltpu.VMEM((mp, d), F32), pltpu.VMEM((ms, d), F32)],
        compiler_params=_params(2), name="ec_experts",
    )(xe_p, xe_s, gate_p, gate_s, w_gate, w_up, w_down)


RANK_RADIX = 32


def _combine_kernel(y_ref, rank_ref, x_ref, g_ref, o_ref, place_ref, *, cap):
    ne = y_ref.shape[0]
    n_slots = ne * cap

    @pl.when(pl.program_id(1) == 0)
    def _():
        rank = rank_ref[...]
        hi = jnp.floor(rank * (1.0 / RANK_RADIX))
        lo = rank - RANK_RADIX * hi
        lane = lax.broadcasted_iota(jnp.int32, (ne, n_slots), 1)
        expert = lax.broadcasted_iota(jnp.int32, (ne, n_slots), 0)
        own = (lane >= expert * cap) & (lane < (expert + 1) * cap)
        spread_hi = jnp.where(own, float(RANK_RADIX), 0.0).astype(BF)
        spread_lo = jnp.where(own, 1.0, 0.0).astype(BF)
        spread = (jnp.dot(hi.astype(BF), spread_hi, preferred_element_type=F32)
                  + jnp.dot(lo.astype(BF), spread_lo, preferred_element_type=F32))
        slot = (lax.broadcasted_iota(jnp.int32, (1, n_slots), 1) & (cap - 1)).astype(F32)
        place_ref[...] = jnp.where(spread == slot, 1.0, 0.0).astype(BF)

    y = y_ref[...].reshape(n_slots, y_ref.shape[2])
    acc = jnp.dot(place_ref[...], y, preferred_element_type=F32)
    o_ref[...] = x_ref[...] + g_ref[...] * acc


def _combine(y, rank, x, gate, bsz, seq):
    t, d = x.shape
    ne = y.shape[0]
    cap = EC_FACTOR * seq // ne
    assert seq <= RANK_RADIX * RANK_RADIX and cap & (cap - 1) == 0
    td = 1024
    return pl.pallas_call(
        functools.partial(_combine_kernel, cap=cap),
        grid=(bsz, d // td),
        in_specs=[pl.BlockSpec((ne, cap, td), lambda b, j: (0, b, j)),
                  pl.BlockSpec((seq, ne), lambda b, j: (b, 0)),
                  pl.BlockSpec((seq, td), lambda b, j: (b, j)),
                  pl.BlockSpec((None, 1, td), lambda b, j: (b if gate.shape[0] > 1 else 0, 0, j))],
        out_specs=pl.BlockSpec((seq, td), lambda b, j: (b, j)),
        out_shape=jax.ShapeDtypeStruct((t, d), F32),
        scratch_shapes=[pltpu.VMEM((seq, ne * cap), BF)],
        compiler_params=_params(2), name="ec_combine",
    )(y, rank, x, gate)


def _final_norm_kernel(x_ref, g_ref, o_ref):
    o_ref[...] = _rms(x_ref[...]) * g_ref[...]


def _final_norm(x, gain):
    t, d = x.shape
    tm = 256
    blk = pl.BlockSpec((tm, d), lambda i: (i, 0))
    return pl.pallas_call(
        _final_norm_kernel, grid=(t // tm,),
        in_specs=[blk, pl.BlockSpec((1, d), lambda i: (0, 0))],
        out_specs=blk, out_shape=jax.ShapeDtypeStruct((t, d), F32),
        compiler_params=_params(1), name="final_norm",
    )(x, gain.reshape(1, d))


def _rope_tables(n_tokens, rot_dim):
    rows = n_tokens // GRID_W
    row = jnp.repeat(jnp.arange(rows), GRID_W).astype(F32)
    col = jnp.tile(jnp.arange(GRID_W), rows).astype(F32)
    quarter = rot_dim // 4
    inv = ROPE_THETA ** (-jnp.arange(quarter, dtype=F32) / quarter)
    a_row, a_col = row[:, None] * inv, col[:, None] * inv
    cos_t = jnp.concatenate([jnp.cos(a_row), jnp.cos(a_row), jnp.cos(a_col), jnp.cos(a_col)], axis=-1)
    sin_t = jnp.concatenate([-jnp.sin(a_row), jnp.sin(a_row), -jnp.sin(a_col), jnp.sin(a_col)], axis=-1)
    return cos_t, sin_t


def kernel(x_prompt, x_sample, state_mlstm_c, state_mlstm_n, state_mlstm_m, cache_mla_ckv, cache_mla_krope,
           cache_gqa_k, cache_gqa_v, c, c_ctx, w_mod, b_mod, w_in_even, b_igate, b_fgate, g_mlstm, g_cq, w_uq,
           g_ckv, w_ukv, w_out_even, w_in_odd, g_qnorm, g_knorm, w_out_odd, w_router, w_expert_gate,
           w_expert_up, w_expert_down, g_final):
    d = D_MODEL
    bp, sp, _ = x_prompt.shape
    bs, ss, _ = x_sample.shape
    depth = w_mod.shape[0]
    nh = ML_HEADS
    streams = {"p": (bp, sp), "s": (bs, ss)}
    x = {"p": x_prompt.reshape(bp * sp, d), "s": x_sample.reshape(bs * ss, d)}

    c8 = jnp.concatenate([c_ctx[None], c, jnp.zeros((8 - 1 - bs, d), F32)], axis=0)
    mod_all = _mod_vectors(c8, w_mod, b_mod).reshape(depth, 8, 6, 1, d)

    def mod(layer, key, idx):
        rows = mod_all[layer, 0:1, idx] if key == "p" else mod_all[layer, 1:1 + bs, idx]
        return rows

    new_even, new_odd = [], []
    for layer in range(depth):
        j = layer // 2
        if layer % 2 == 0:
            w_in_rows = jnp.swapaxes(w_in_even, 1, 2)
            w_side = jnp.concatenate(
                [w_in_rows[j, QKVO_COLS + 4 * nh:],
                 w_in_rows[j, QKVO_COLS:QKVO_COLS + 4 * nh],
                 jnp.zeros((128 - MLA_ROPE - 4 * nh, d), F32)], axis=0)[None]
            w_q = w_uq[j].reshape(MLA_Q_RANK, MLA_HEADS, MLA_NOPE + MLA_ROPE)
            w_q = jnp.concatenate([w_q[:, :, :MLA_NOPE].reshape(MLA_Q_RANK, -1),
                                   w_q[:, :, MLA_NOPE:].reshape(MLA_Q_RANK, -1)], axis=1)
            bias_col = jnp.concatenate([b_igate[j].reshape(1, -1), b_fgate[j].reshape(1, -1)], axis=1)
            bias_row = bias_col.reshape(-1, 1)
            cos64, sin64 = _rope_tables(ss, MLA_ROPE)
            cos_q, sin_q = jnp.tile(cos64, (1, MLA_HEADS)), jnp.tile(sin64, (1, MLA_HEADS))
            pad = jnp.zeros((ss, 128 - MLA_ROPE), F32)
            cos_k, sin_k = jnp.concatenate([cos64, pad], axis=1), jnp.concatenate([sin64, pad], axis=1)
            kvc = _mm([cache_mla_ckv[:, j].reshape(-1, MLA_KV_RANK)], w_ukv, j, w_ukv.shape[-1], 512,
                      name="mla_ctx_expand")
            for key, (bsz, seq) in streams.items():
                h = _norm_mod(x[key], mod(layer, key, 0), mod(layer, key, 1), seq)
                qkvo = _mm([h], w_in_rows, j, QKVO_COLS, 512, w_rows=True, name="even_in_main")
                side = _mm([h], w_side, 0, SIDE_COLS, SIDE_COLS // 3, w_rows=True, name="even_in_side")
                gates = side[:, SIDE_COLS - 128 + GATE_LANE0:SIDE_COLS - 128 + GATE_LANE0 + 4 * nh]
                g_row = gates.reshape(-1, ML_CHUNK, 4 * nh).transpose(0, 2, 1)
                if key == "p":
                    c0 = jnp.zeros((bsz, 2, nh, ML_QK, ML_V), F32)
                    n0 = jnp.zeros((bsz, 2, nh, ML_QK, ML_V), F32)
                    m0 = jnp.zeros((bsz, 1, 2 * nh), F32)
                else:
                    c0 = state_mlstm_c[:, j]
                    n0 = jnp.broadcast_to(state_mlstm_n[:, j][..., None], c0.shape)
                    m0 = state_mlstm_m[:, j].reshape(bsz, 1, 2 * nh)
                hf, hb, c_fin, n_fin, m_fin = _mlstm(qkvo, side, g_row, bias_col, bias_row, c0, n0, m0, bsz, seq)
                y_ml = _mlstm_post(hf, hb, qkvo, g_mlstm[j])
                qa = _norm_mm(side, 0, g_cq[j], w_q, False, "mla_q_up")
                kv, ckv_n = _norm_mm(side, 1, g_ckv[j], w_ukv[j], True, "mla_kv_up")
                if key == "p":
                    y_a = _mla_attention(qa, kv, side, bsz, seq)
                    k_rope = side[:, SIDE_COLS - 128:SIDE_COLS - 128 + MLA_ROPE]
                    new_even.append((c_fin, n_fin[..., 0], m_fin.reshape(bsz, 2, nh),
                                     ckv_n.reshape(bsz, seq, -1), k_rope.reshape(bsz, seq, -1)))
                else:
                    y_a = _mla_attention(qa, kv, side, bsz, seq,
                                         ctx=(cos_q, sin_q, cos_k, sin_k, kvc, cache_mla_krope[:, j]))
                x[key] = _mm([y_ml, y_a], w_out_even, j, d, 512,
                             residual=(x[key], mod(layer, key, 2), seq), name="even_out")
        else:
            cos_t, sin_t = _rope_tables(ss, GQA_HEAD_DIM)
            for key, (bsz, seq) in streams.items():
                h = _norm_mod(x[key], mod(layer, key, 0), mod(layer, key, 1), seq)
                qkv = _mm([h], w_in_odd, j, w_in_odd.shape[-1], 512, name="odd_in")
                if key == "p":
                    o, k_n, v = _gqa_attention(qkv, g_qnorm[j], g_knorm[j], bsz, seq)
                    new_odd.append((k_n, v))
                else:
                    o = _gqa_attention(qkv, g_qnorm[j], g_knorm[j], bsz, seq,
                                       ctx=(cos_t, sin_t, cache_gqa_k, cache_gqa_v, j))
                x[key] = _mm([o], w_out_odd, j, d, 512,
                             residual=(x[key], mod(layer, key, 2), seq), name="odd_out")
        routed = {}
        for key, (bsz, seq) in streams.items():
            h, aff = _norm_mod(x[key], mod(layer, key, 3), mod(layer, key, 4), seq, w_router=w_router, layer=layer)
            routed[key] = _dispatch(aff.T, aff, h, bsz, seq)
        y_p, y_s = _experts(routed["p"][0], routed["s"][0], routed["p"][1], routed["s"][1],
                            w_expert_gate, w_expert_up, w_expert_down, layer)
        for key, y in (("p", y_p), ("s", y_s)):
            bsz, seq = streams[key]
            x[key] = _combine(y, routed[key][2], x[key], mod(layer, key, 5), bsz, seq)

    y_prompt = _final_norm(x["p"], g_final).reshape(bp, sp, d)
    y_sample = _final_norm(x["s"], g_final).reshape(bs, ss, d)
    new_c = jnp.stack([e[0] for e in new_even], axis=1)
    new_n = jnp.stack([e[1] for e in new_even], axis=1)
    new_m = jnp.stack([e[2] for e in new_even], axis=1)
    new_ckv = jnp.stack([e[3] for e in new_even], axis=1)
    new_krope = jnp.stack([e[4] for e in new_even], axis=1)
    new_k = jnp.stack([e[0] for e in new_odd], axis=1)
    new_v = jnp.stack([e[1] for e in new_odd], axis=1)
    return (y_prompt, y_sample, new_c, new_n, new_m, new_ckv, new_krope, new_k, new_v)
```

```python
import functools

import jax
import jax.numpy as jnp
from jax import lax
from jax.experimental import pallas as pl
from jax.experimental.pallas import tpu as pltpu

BF = jnp.bfloat16
F32 = jnp.float32

D_MODEL = 2048
GRID_W = 64
ROPE_THETA = 10000.0
NORM_EPS = 1e-6
ML_HEADS = 8
ML_QK = 64
ML_V = 128
ML_CHUNK = 64
MLA_HEADS = 8
MLA_Q_RANK = 512
MLA_KV_RANK = 512
MLA_NOPE = 128
MLA_ROPE = 64
MLA_V = 128
GQA_HEADS = 16
GQA_KV_HEADS = 4
GQA_HEAD_DIM = 128
N_EXPERTS = 16
EXPERT_DIM = 1024
EC_FACTOR = 2

QKVO_COLS = 2 * ML_HEADS * ML_QK + 2 * ML_HEADS * ML_V
SIDE_COLS = MLA_Q_RANK + MLA_KV_RANK + 128
GATE_LANE0 = MLA_ROPE

VMEM_LIMIT_BYTES = 56 * 1024 * 1024


def _params(n_axes):
    return pltpu.CompilerParams(dimension_semantics=("arbitrary",) * n_axes,
                                vmem_limit_bytes=VMEM_LIMIT_BYTES)


def _bdot(a, b):
    return jnp.dot(a.astype(BF), b.astype(BF), preferred_element_type=F32)


def _bdot_nt(a, b):
    return lax.dot_general(a.astype(BF), b.astype(BF), (((1,), (1,)), ((), ())),
                           preferred_element_type=F32)


def _bdot_tn(a, b):
    return lax.dot_general(a.astype(BF), b.astype(BF), (((0,), (0,)), ((), ())),
                           preferred_element_type=F32)


def _sigmoid(x):
    return 1.0 / (1.0 + jnp.exp(-x))


def _log_sigmoid(x):
    return jnp.minimum(x, 0.0) - jnp.log1p(jnp.exp(-jnp.abs(x)))


def _rms(x):
    return x * lax.rsqrt(jnp.mean(x * x, axis=-1, keepdims=True) + NORM_EPS)


def _rope(x, cos_t, sin_t, quarter):
    width = x.shape[-1]
    axis = x.ndim - 1
    lane = lax.broadcasted_iota(jnp.int32, x.shape, axis)
    partner = jnp.where((lane & quarter) == 0,
                        pltpu.roll(x, width - quarter, axis=axis),
                        pltpu.roll(x, quarter, axis=axis))
    return x * cos_t + partner * sin_t


def _mod_kernel(c_ref, w_ref, b_ref, o_ref):
    c = c_ref[...]
    o_ref[...] = _bdot(c * _sigmoid(c), w_ref[...]) + b_ref[...]


def _mod_vectors(c8, w_mod, b_mod):
    n_layers, k, n = w_mod.shape
    tn = 1024
    return pl.pallas_call(
        _mod_kernel,
        grid=(n_layers, n // tn),
        in_specs=[pl.BlockSpec((8, k), lambda l, j: (0, 0)),
                  pl.BlockSpec((None, k, tn), lambda l, j: (l, 0, j)),
                  pl.BlockSpec((None, 1, tn), lambda l, j: (l, 0, j))],
        out_specs=pl.BlockSpec((None, 8, tn), lambda l, j: (l, 0, j)),
        out_shape=jax.ShapeDtypeStruct((n_layers, 8, n), F32),
        compiler_params=_params(2),
        name="mod_vectors",
    )(c8, w_mod, b_mod.reshape(n_layers, 1, n))


def _norm_mod_kernel(x_ref, sh_ref, sc_ref, h_ref):
    h_ref[...] = (_rms(x_ref[...]) * (1.0 + sc_ref[...]) + sh_ref[...]).astype(BF)


def _norm_router_kernel(x_ref, sh_ref, sc_ref, wr_ref, h_ref, aff_ref):
    h = (_rms(x_ref[...]) * (1.0 + sc_ref[...]) + sh_ref[...]).astype(BF)
    h_ref[...] = h
    logits = _bdot(h, wr_ref[...])
    e = jnp.exp(logits - jnp.max(logits, axis=-1, keepdims=True))
    aff_ref[...] = e / jnp.sum(e, axis=-1, keepdims=True)


def _batch_of_tile(n_vectors, tm, seq):
    if n_vectors == 1:
        return lambda i: 0
    assert seq % tm == 0
    return lambda i: (i * tm) // seq


def _norm_mod(x, shift, scale, seq, w_router=None, layer=0):
    t, d = x.shape
    tm = 256
    which = _batch_of_tile(shift.shape[0], tm, seq)
    vec = pl.BlockSpec((None, 1, d), lambda i: (which(i), 0, 0))
    x_spec = pl.BlockSpec((tm, d), lambda i: (i, 0))
    if w_router is None:
        return pl.pallas_call(
            _norm_mod_kernel, grid=(t // tm,),
            in_specs=[x_spec, vec, vec], out_specs=x_spec,
            out_shape=jax.ShapeDtypeStruct((t, d), BF),
            compiler_params=_params(1), name="norm_mod",
        )(x, shift, scale)
    ne = w_router.shape[-1]
    return pl.pallas_call(
        _norm_router_kernel, grid=(t // tm,),
        in_specs=[x_spec, vec, vec, pl.BlockSpec((None, d, ne), lambda i: (layer, 0, 0))],
        out_specs=[x_spec, pl.BlockSpec((tm, ne), lambda i: (i, 0))],
        out_shape=[jax.ShapeDtypeStruct((t, d), BF), jax.ShapeDtypeStruct((t, ne), F32)],
        compiler_params=_params(1), name="norm_router",
    )(x, shift, scale, w_router)


def _mm_kernel(*refs, k_sizes, residual, w_rows):
    n_a = len(k_sizes)
    a_refs, w_ref = refs[:n_a], refs[n_a]
    o_ref, wbf_ref = refs[-2], refs[-1]

    @pl.when(pl.program_id(1) == 0)
    def _():
        wbf_ref[...] = w_ref[...].astype(BF)

    acc, off = None, 0
    for a_ref, ks in zip(a_refs, k_sizes):
        if w_rows:
            term = lax.dot_general(a_ref[...].astype(BF), wbf_ref[:, off:off + ks], (((1,), (1,)), ((), ())),
                                   preferred_element_type=F32)
        else:
            term = jnp.dot(a_ref[...].astype(BF), wbf_ref[off:off + ks, :], preferred_element_type=F32)
        acc = term if acc is None else acc + term
        off += ks
    if residual:
        x_ref, g_ref = refs[n_a + 1], refs[n_a + 2]
        acc = x_ref[...] + g_ref[...] * acc
    o_ref[...] = acc


def _mm(a_list, w3, layer, n_cols, tn, col_blk0=0, residual=None, w_rows=False, name="mm"):
    m = a_list[0].shape[0]
    k_sizes = tuple(a.shape[1] for a in a_list)
    k = sum(k_sizes)
    assert w3.shape[2 if w_rows else 1] == k and n_cols % tn == 0
    tm = min(1024, m)
    in_specs = [pl.BlockSpec((tm, ks), lambda j, i: (i, 0)) for ks in k_sizes]
    if w_rows:
        in_specs.append(pl.BlockSpec((None, tn, k), lambda j, i: (layer, j + col_blk0, 0)))
    else:
        in_specs.append(pl.BlockSpec((None, k, tn), lambda j, i: (layer, 0, j + col_blk0)))
    args = list(a_list) + [w3]
    if residual is not None:
        x, gate, seq = residual
        which = _batch_of_tile(gate.shape[0], tm, seq)
        in_specs.append(pl.BlockSpec((tm, tn), lambda j, i: (i, j)))
        in_specs.append(pl.BlockSpec((None, 1, tn), lambda j, i: (which(i), 0, j)))
        args += [x, gate]
    return pl.pallas_call(
        functools.partial(_mm_kernel, k_sizes=k_sizes, residual=residual is not None, w_rows=w_rows),
        grid=(n_cols // tn, m // tm),
        in_specs=in_specs,
        out_specs=pl.BlockSpec((tm, tn), lambda j, i: (i, j)),
        out_shape=jax.ShapeDtypeStruct((m, n_cols), F32),
        scratch_shapes=[pltpu.VMEM((tn, k) if w_rows else (k, tn), BF)],
        compiler_params=_params(2), name=name,
    )(*args)


def _mm_resident_kernel(a_ref, w_ref, o_ref, *, w_rows, chunk):
    w = w_ref[...].astype(BF)
    for r in range(0, a_ref.shape[0], chunk):
        rows = slice(r, r + chunk)
        if w_rows:
            o_ref[rows, :] = lax.dot_general(a_ref[rows, :], w, (((1,), (1,)), ((), ())),
                                             preferred_element_type=F32)
        else:
            o_ref[rows, :] = jnp.dot(a_ref[rows, :], w, preferred_element_type=F32)


def _mm_resident(a, w3, layer, n_cols, tn, w_rows=False, name="mm_resident"):
    m, k = a.shape
    assert a.dtype == BF and w3.shape[2 if w_rows else 1] == k and n_cols % tn == 0
    if w_rows:
        w_spec = pl.BlockSpec((None, tn, k), lambda j: (layer, j, 0))
    else:
        w_spec = pl.BlockSpec((None, k, tn), lambda j: (layer, 0, j))
    return pl.pallas_call(
        functools.partial(_mm_resident_kernel, w_rows=w_rows, chunk=min(512, m)),
        grid=(n_cols // tn,),
        in_specs=[pl.BlockSpec((m, k), lambda j: (0, 0), pipeline_mode=pl.Buffered(1)), w_spec],
        out_specs=pl.BlockSpec((m, tn), lambda j: (0, j)),
        out_shape=jax.ShapeDtypeStruct((m, n_cols), F32),
        compiler_params=_params(1), name=name,
    )(a, w3)


def _norm_mm_kernel(x_ref, g_ref, w_ref, *out_refs, with_normed):
    wbf_ref = out_refs[-1]

    @pl.when(pl.program_id(0) == 0)
    def _():
        wbf_ref[...] = w_ref[...].astype(BF)

    xn = _rms(x_ref[...]) * g_ref[...]
    out_refs[0][...] = jnp.dot(xn.astype(BF), wbf_ref[...], preferred_element_type=F32)
    if with_normed:
        out_refs[1][...] = xn


def _norm_mm(x, col_blk, gain, w, with_normed, name):
    t = x.shape[0]
    k, n = w.shape
    tm = 512
    out_specs = [pl.BlockSpec((tm, n), lambda i: (i, 0))]
    out_shape = [jax.ShapeDtypeStruct((t, n), F32)]
    if with_normed:
        out_specs.append(pl.BlockSpec((tm, k), lambda i: (i, 0)))
        out_shape.append(jax.ShapeDtypeStruct((t, k), F32))
    outs = pl.pallas_call(
        functools.partial(_norm_mm_kernel, with_normed=with_normed),
        grid=(t // tm,),
        in_specs=[pl.BlockSpec((tm, k), lambda i: (i, col_blk)),
                  pl.BlockSpec((1, k), lambda i: (0, 0)),
                  pl.BlockSpec((k, n), lambda i: (0, 0))],
        out_specs=out_specs, out_shape=out_shape,
        scratch_shapes=[pltpu.VMEM((k, n), BF)],
        compiler_params=_params(1), name=name,
    )(x, gain.reshape(1, k), w)
    return outs if with_normed else outs[0]


def _split3(x):
    hi = x.astype(BF)
    rest = x - hi.astype(F32)
    mid = rest.astype(BF)
    return hi, mid, (rest - mid.astype(F32)).astype(BF)


def _scan_max(x, reverse):
    n = x.shape[0]
    row = lax.broadcasted_iota(jnp.int32, x.shape, 0)
    k = 1
    while k < n:
        if reverse:
            shifted = jnp.where(row < n - k, pltpu.roll(x, n - k, axis=0), -jnp.inf)
        else:
            shifted = jnp.where(row >= k, pltpu.roll(x, k, axis=0), -jnp.inf)
        x = jnp.maximum(x, shifted)
        k *= 2
    return x


def _mlstm_kernel(qf_ref, kf_ref, vf_ref, qb_ref, kb_ref, vb_ref, gcf_ref, gcb_ref, grf_ref, grb_ref,
                  bc_ref, br_ref, c0_ref, n0_ref, m0_ref, hf_ref, hb_ref, c_ref, n_ref, m_ref):
    @pl.when(pl.program_id(1) == 0)
    def _():
        c_ref[...] = c0_ref[...]
        n_ref[...] = n0_ref[...]
        m_ref[...] = m0_ref[...]

    nh, lc = ML_HEADS, ML_CHUNK
    row = lax.broadcasted_iota(jnp.int32, (lc, lc), 0)
    col = lax.broadcasted_iota(jnp.int32, (lc, lc), 1)
    lower = col <= row
    upper = col >= row
    ones_v = jnp.ones((lc, ML_V), BF)
    directions = ((qf_ref, kf_ref, vf_ref, gcf_ref, grf_ref, hf_ref, lower, upper),
                  (qb_ref, kb_ref, vb_ref, gcb_ref, grb_ref, hb_ref, upper, lower))
    gate_terms = []
    for d, (_, _, _, gc_ref, gr_ref, _, allowed, allowed_t) in enumerate(directions):
        g_col = gc_ref[:, GATE_LANE0:GATE_LANE0 + 4 * nh] + bc_ref[...]
        g_row = gr_ref[...] + br_ref[...]
        i_col = g_col[:, nh * d:nh * (d + 1)]
        f_col = _log_sigmoid(g_col[:, 2 * nh + nh * d:2 * nh + nh * (d + 1)])
        i_row = g_row[nh * d:nh * (d + 1), :]
        f_row = _log_sigmoid(g_row[2 * nh + nh * d:2 * nh + nh * (d + 1), :])
        tri = jnp.where(allowed, 1.0, 0.0).astype(BF)
        tri_t = jnp.where(allowed_t, 1.0, 0.0).astype(BF)
        b_col = sum(jnp.dot(tri, part, preferred_element_type=F32) for part in _split3(f_col))
        b_row = sum(jnp.dot(part, tri_t, preferred_element_type=F32) for part in _split3(f_row))
        b_end = jnp.sum(f_col, axis=0, keepdims=True)
        m_prev = m_ref[:, nh * d:nh * (d + 1)]
        m_t = b_col + jnp.maximum(m_prev, _scan_max(i_col - b_col, reverse=d == 1))
        g_col_end = b_end - b_col + i_col
        m_new = jnp.maximum(b_end + m_prev, jnp.max(g_col_end, axis=0, keepdims=True))
        gate_terms.append(dict(
            u=b_col - m_t, r_row=i_row - b_row, w_inter=jnp.exp(b_col + m_prev - m_t), floor=jnp.exp(-m_t),
            k_scale=jnp.exp(g_col_end - m_new), decay=jnp.exp(b_end + m_prev - m_new), m_new=m_new))
    work = []
    for d, (q_ref, k_ref, v_ref, _, _, h_ref, allowed, _) in enumerate(directions):
        for h in range(nh):
            w = dict(d=d, h=h, h_ref=h_ref, allowed=allowed, g=gate_terms[d])
            w["q"] = (q_ref[:, h * ML_QK:(h + 1) * ML_QK] * (ML_QK ** -0.5)).astype(BF)
            w["k"] = k_ref[:, h * ML_QK:(h + 1) * ML_QK]
            w["v1"] = jnp.concatenate([v_ref[:, h * ML_V:(h + 1) * ML_V].astype(BF), ones_v], axis=1)
            work.append(w)
    for w in work:
        w["qk"] = _bdot_nt(w["q"], w["k"])
    for w in work:
        d, h = w["d"], w["h"]
        w["c_prev"], w["n_prev"] = c_ref[d, h], n_ref[d, h]
        state = jnp.concatenate([w["c_prev"], w["n_prev"]], axis=1).astype(BF)
        w["q_state"] = jnp.dot(w["q"], state, preferred_element_type=F32)
    for w in work:
        h, g = w["h"], w["g"]
        kw = w["k"] * g["k_scale"][:, h:h + 1]
        w["kv"] = _bdot_tn(kw, w["v1"])
    for w in work:
        h, g = w["h"], w["g"]
        dmat = jnp.where(w["allowed"], g["u"][:, h:h + 1] + g["r_row"][h:h + 1, :], -jnp.inf)
        sw = w["qk"] * jnp.exp(dmat)
        sw_hi = sw.astype(BF)
        sw_lo = (sw - sw_hi.astype(F32)).astype(BF)
        w["pv"] = jnp.dot(sw_hi, w["v1"], preferred_element_type=F32)
        w["den_lo"] = jnp.dot(sw_lo, ones_v, preferred_element_type=F32)
    for w in work:
        d, h, g = w["d"], w["h"], w["g"]
        w_inter = g["w_inter"][:, h:h + 1]
        num = w_inter * w["q_state"][:, :ML_V] + w["pv"][:, :ML_V]
        den = w_inter * w["q_state"][:, ML_V:] + (w["pv"][:, ML_V:] + w["den_lo"])
        w["h_ref"][:, h * ML_V:(h + 1) * ML_V] = num / jnp.maximum(jnp.abs(den), g["floor"][:, h:h + 1])
        decay = g["decay"][:, h:h + 1]
        c_ref[d, h] = decay * w["c_prev"] + w["kv"][:, :ML_V]
        n_ref[d, h] = decay * w["n_prev"] + w["kv"][:, ML_V:]
    m_ref[:, :nh] = gate_terms[0]["m_new"]
    m_ref[:, nh:] = gate_terms[1]["m_new"]


def _mlstm(qkvo, side, g_row, b_col, b_row, c0, n0, m0, bsz, seq):
    t = qkvo.shape[0]
    nc = seq // ML_CHUNK
    lc = ML_CHUNK
    nq = ML_HEADS * ML_QK
    nv = ML_HEADS * ML_V
    side_blk = (SIDE_COLS - 128) // 128
    fwd = lambda b, c: b * nc + c
    bwd = lambda b, c: b * nc + (nc - 1 - c)

    def specs(pos):
        return [pl.BlockSpec((lc, nq), lambda b, c: (pos(b, c), 0)),
                pl.BlockSpec((lc, nq), lambda b, c: (pos(b, c), 1)),
                pl.BlockSpec((lc, nv), lambda b, c: (pos(b, c), 1))]

    state = lambda *shape: pl.BlockSpec((None,) + shape, lambda b, c: (b,) + (0,) * len(shape))
    in_specs = specs(fwd) + specs(bwd) + [
        pl.BlockSpec((lc, 128), lambda b, c: (fwd(b, c), side_blk)),
        pl.BlockSpec((lc, 128), lambda b, c: (bwd(b, c), side_blk)),
        pl.BlockSpec((None, 4 * ML_HEADS, lc), lambda b, c: (fwd(b, c), 0, 0)),
        pl.BlockSpec((None, 4 * ML_HEADS, lc), lambda b, c: (bwd(b, c), 0, 0)),
        pl.BlockSpec((1, 4 * ML_HEADS), lambda b, c: (0, 0)),
        pl.BlockSpec((4 * ML_HEADS, 1), lambda b, c: (0, 0)),
        state(2, ML_HEADS, ML_QK, ML_V), state(2, ML_HEADS, ML_QK, ML_V), state(1, 2 * ML_HEADS)]
    out_specs = [pl.BlockSpec((lc, nv), lambda b, c: (fwd(b, c), 0)),
                 pl.BlockSpec((lc, nv), lambda b, c: (bwd(b, c), 0)),
                 state(2, ML_HEADS, ML_QK, ML_V), state(2, ML_HEADS, ML_QK, ML_V), state(1, 2 * ML_HEADS)]
    out_shape = [jax.ShapeDtypeStruct((t, nv), F32), jax.ShapeDtypeStruct((t, nv), F32),
                 jax.ShapeDtypeStruct((bsz, 2, ML_HEADS, ML_QK, ML_V), F32),
                 jax.ShapeDtypeStruct((bsz, 2, ML_HEADS, ML_QK, ML_V), F32),
                 jax.ShapeDtypeStruct((bsz, 1, 2 * ML_HEADS), F32)]
    return pl.pallas_call(
        _mlstm_kernel, grid=(bsz, nc), in_specs=in_specs, out_specs=out_specs, out_shape=out_shape,
        compiler_params=_params(2), name="mlstm",
    )(qkvo, qkvo, qkvo, qkvo, qkvo, qkvo, side, side, g_row, g_row, b_col, b_row, c0, n0, m0)


def _mlstm_post_kernel(hf_ref, hb_ref, o_ref, g_ref, y_ref):
    for h in range(ML_HEADS):
        sl = slice(h * ML_V, (h + 1) * ML_V)
        hn = _rms(hf_ref[:, sl] + hb_ref[:, sl]) * g_ref[:, sl]
        y_ref[:, sl] = (hn * _sigmoid(o_ref[:, sl])).astype(BF)


def _mlstm_post(hf, hb, qkvo, gain):
    t, nv = hf.shape
    tm = 512
    blk = pl.BlockSpec((tm, nv), lambda i: (i, 0))
    return pl.pallas_call(
        _mlstm_post_kernel, grid=(t // tm,),
        in_specs=[blk, blk, pl.BlockSpec((tm, nv), lambda i: (i, 2)), pl.BlockSpec((1, nv), lambda i: (0, 0))],
        out_specs=blk, out_shape=jax.ShapeDtypeStruct((t, nv), BF),
        compiler_params=_params(1), name="mlstm_post",
    )(hf, hb, qkvo, gain.reshape(1, nv))


LOG2_E = 1.4426950408889634


def _softmax_terms(scores, scale):
    c = scale * LOG2_E
    scaled = [s * c for s in scores]
    m = functools.reduce(jnp.maximum, [jnp.max(s, axis=-1, keepdims=True) for s in scaled])
    e = [jnp.exp2(s - m) for s in scaled]
    den = functools.reduce(lambda a, b: a + b, [jnp.sum(x, axis=-1, keepdims=True) for x in e])
    return [x.astype(BF) for x in e], 1.0 / den


def _mla_kernel(*refs, with_ctx):
    if with_ctx:
        qa_ref, kv_ref, side_ref, cq_ref, sq_ref, ck_ref, sk_ref, kvc_ref, krc_ref, o_ref = refs
    else:
        qa_ref, kv_ref, side_ref, o_ref = refs
    scale = (MLA_NOPE + MLA_ROPE) ** -0.5
    nope_cols = MLA_HEADS * MLA_NOPE
    q_rope = qa_ref[:, nope_cols:]
    k_rope = side_ref[...]
    if with_ctx:
        q_rope_rot = _rope(q_rope, cq_ref[...], sq_ref[...], MLA_ROPE // 4).astype(BF)
        k_rope_rot = _rope(k_rope, ck_ref[...], sk_ref[...], MLA_ROPE // 4)[:, :MLA_ROPE].astype(BF)
        k_rope_ctx = krc_ref[...].astype(BF)
        q_rope = q_rope.astype(BF)
    else:
        q_rope_rot = q_rope.astype(BF)
        k_rope_rot = k_rope[:, :MLA_ROPE].astype(BF)
    heads = range(MLA_HEADS)
    rope_cols = [slice(h * MLA_ROPE, (h + 1) * MLA_ROPE) for h in heads]
    kv0 = [h * (MLA_NOPE + MLA_V) for h in heads]
    q_n = [qa_ref[:, h * MLA_NOPE:(h + 1) * MLA_NOPE].astype(BF) for h in heads]
    scores = [[_bdot_nt(q_n[h], kv_ref[:, kv0[h]:kv0[h] + MLA_NOPE])
               + _bdot_nt(q_rope_rot[:, rope_cols[h]], k_rope_rot)] for h in heads]
    if with_ctx:
        for h in heads:
            scores[h].append(_bdot_nt(q_n[h], kvc_ref[:, kv0[h]:kv0[h] + MLA_NOPE])
                             + _bdot_nt(q_rope[:, rope_cols[h]], k_rope_ctx))
    weights = [_softmax_terms(scores[h], scale) for h in heads]
    for h in heads:
        (e, inv_den), v0 = weights[h], kv0[h] + MLA_NOPE
        out = jnp.dot(e[0], kv_ref[:, v0:v0 + MLA_V].astype(BF), preferred_element_type=F32)
        if with_ctx:
            out = out + jnp.dot(e[1], kvc_ref[:, v0:v0 + MLA_V].astype(BF), preferred_element_type=F32)
        o_ref[:, h * MLA_V:(h + 1) * MLA_V] = (out * inv_den).astype(BF)


def _mla_attention(qa, kv, side, bsz, seq, ctx=None):
    t = qa.shape[0]
    tq = 256
    nq = seq // tq
    side_blk = (SIDE_COLS - 128) // 128
    nkv = MLA_HEADS * (MLA_NOPE + MLA_V)
    in_specs = [pl.BlockSpec((tq, qa.shape[1]), lambda b, i: (b * nq + i, 0)),
                pl.BlockSpec((seq, nkv), lambda b, i: (b, 0)),
                pl.BlockSpec((seq, 128), lambda b, i: (b, side_blk))]
    args = [qa, kv, side]
    if ctx is not None:
        cos_q, sin_q, cos_k, sin_k, kvc, krc = ctx
        past = krc.shape[1]
        in_specs += [pl.BlockSpec((tq, cos_q.shape[1]), lambda b, i: (i, 0)),
                     pl.BlockSpec((tq, cos_q.shape[1]), lambda b, i: (i, 0)),
                     pl.BlockSpec((seq, 128), lambda b, i: (0, 0)),
                     pl.BlockSpec((seq, 128), lambda b, i: (0, 0)),
                     pl.BlockSpec((past, nkv), lambda b, i: (b, 0)),
                     pl.BlockSpec((None, past, MLA_ROPE), lambda b, i: (b, 0, 0))]
        args += [cos_q, sin_q, cos_k, sin_k, kvc, krc]
    nout = MLA_HEADS * MLA_V
    return pl.pallas_call(
        functools.partial(_mla_kernel, with_ctx=ctx is not None),
        grid=(bsz, nq), in_specs=in_specs,
        out_specs=pl.BlockSpec((tq, nout), lambda b, i: (b * nq + i, 0)),
        out_shape=jax.ShapeDtypeStruct((t, nout), BF),
        compiler_params=_params(2), name="mla_attention",
    )(*args)


def _gqa_kernel(*refs, with_ctx):
    if with_ctx:
        q_ref, k_ref, v_ref, gq_ref, gk_ref, cq_ref, sq_ref, ck_ref, sk_ref, kc_ref, vc_ref, o_ref = refs
    else:
        q_ref, k_ref, v_ref, gq_ref, gk_ref, o_ref, kn_ref, vo_ref = refs
    hd = GQA_HEAD_DIM
    scale = hd ** -0.5
    k_n = _rms(k_ref[...]) * gk_ref[...]
    v = v_ref[...]
    if with_ctx:
        k_src = _rope(k_n, ck_ref[...], sk_ref[...], hd // 4).astype(BF)
        k_ctx, v_ctx = kc_ref[...].astype(BF), vc_ref[...].astype(BF)
    else:
        k_src = k_n.astype(BF)

        @pl.when(pl.program_id(2) == 0)
        def _():
            kn_ref[...] = k_n
            vo_ref[...] = v

    v = v.astype(BF)
    heads = range(GQA_HEADS // GQA_KV_HEADS)
    cols = [slice(r * hd, (r + 1) * hd) for r in heads]
    q_n = [_rms(q_ref[:, cols[r]]) * gq_ref[...] for r in heads]
    if with_ctx:
        scores = [[_bdot_nt(_rope(q_n[r], cq_ref[...], sq_ref[...], hd // 4), k_src), _bdot_nt(q_n[r], k_ctx)]
                  for r in heads]
    else:
        scores = [[_bdot_nt(q_n[r], k_src)] for r in heads]
    weights = [_softmax_terms(scores[r], scale) for r in heads]
    for r in heads:
        e, inv_den = weights[r]
        out = jnp.dot(e[0], v, preferred_element_type=F32)
        if with_ctx:
            out = out + jnp.dot(e[1], v_ctx, preferred_element_type=F32)
        o_ref[:, cols[r]] = (out * inv_den).astype(BF)


def _gqa_attention(qkv, g_q, g_k, bsz, seq, ctx=None):
    t = qkv.shape[0]
    hd = GQA_HEAD_DIM
    rep = GQA_HEADS // GQA_KV_HEADS
    tq = 256
    nq = seq // tq
    in_specs = [pl.BlockSpec((tq, rep * hd), lambda b, g, i: (b * nq + i, g)),
                pl.BlockSpec((seq, hd), lambda b, g, i: (b, GQA_HEADS + g)),
                pl.BlockSpec((seq, hd), lambda b, g, i: (b, GQA_HEADS + GQA_KV_HEADS + g)),
                pl.BlockSpec((1, hd), lambda b, g, i: (0, 0)),
                pl.BlockSpec((1, hd), lambda b, g, i: (0, 0))]
    args = [qkv, qkv, qkv, g_q.reshape(1, hd), g_k.reshape(1, hd)]
    o_spec = pl.BlockSpec((tq, rep * hd), lambda b, g, i: (b * nq + i, g))
    o_shape = jax.ShapeDtypeStruct((t, GQA_HEADS * hd), BF)
    if ctx is not None:
        cos_t, sin_t, kc, vc, j = ctx
        past = kc.shape[3]
        cache = pl.BlockSpec((None, None, None, past, hd), lambda b, g, i: (b, j, g, 0, 0))
        in_specs += [pl.BlockSpec((tq, hd), lambda b, g, i: (i, 0)),
                     pl.BlockSpec((tq, hd), lambda b, g, i: (i, 0)),
                     pl.BlockSpec((seq, hd), lambda b, g, i: (0, 0)),
                     pl.BlockSpec((seq, hd), lambda b, g, i: (0, 0)),
                     cache, cache]
        args += [cos_t, sin_t, cos_t, sin_t, kc, vc]
        out_specs, out_shape = o_spec, o_shape
    else:
        head_major = pl.BlockSpec((None, None, seq, hd), lambda b, g, i: (b, g, 0, 0))
        out_specs = [o_spec, head_major, head_major]
        kv_shape = jax.ShapeDtypeStruct((bsz, GQA_KV_HEADS, seq, hd), F32)
        out_shape = [o_shape, kv_shape, kv_shape]
    return pl.pallas_call(
        functools.partial(_gqa_kernel, with_ctx=ctx is not None),
        grid=(bsz, GQA_KV_HEADS, nq), in_specs=in_specs, out_specs=out_specs, out_shape=out_shape,
        compiler_params=_params(3), name="gqa_attention",
    )(*args)


RANK_BLOCK = 256
GATHER_ROWS = 512
EXPERT_ROWS = 256


def _dispatch_kernel(ar_ref, ac_ref, h_ref, xe_ref, gate_ref, rankc_ref, rankr_ref, *, cap):
    n_groups, group, s = ar_ref.shape
    g = pl.program_id(1)

    @pl.when(g == 0)
    def _():
        blk = min(RANK_BLOCK, s)
        rankr_ref[...] = jnp.zeros_like(rankr_ref)
        for j in range(s // blk):
            rows = pl.ds(j * blk, blk)
            i0 = lax.broadcasted_iota(jnp.int32, (blk, s), 0) + j * blk
            i1 = lax.broadcasted_iota(jnp.int32, (blk, s), 1)
            sub_first = jnp.where(i0 < i1, 1.0, 0.0)
            for e in range(n_groups * group):
                eg, ei = e // group, e % group
                a_row = ar_ref[eg, ei:ei + 1, :]
                a_col = ac_ref[rows, e:e + 1]
                ahead = jnp.where(a_col > a_row, 1.0, jnp.where(a_col >= a_row, sub_first, 0.0))
                rankr_ref[eg, ei:ei + 1, :] += jnp.sum(ahead, axis=0, keepdims=True)
                rankc_ref[rows, e:e + 1] = (s - 1.0) - jnp.sum(ahead, axis=1, keepdims=True)

    slot = lax.broadcasted_iota(jnp.int32, (group, cap, s), 1).astype(F32)
    pick = rankr_ref[g][:, None, :] == slot
    onehot = jnp.where(pick, 1.0, 0.0).reshape(group * cap, s).astype(BF)
    rows_f32 = jnp.dot(onehot, h_ref[...], preferred_element_type=F32)
    xe_ref[...] = rows_f32.reshape(group, cap, -1).astype(BF)
    gate_ref[...] = jnp.sum(jnp.where(pick, ar_ref[g][:, None, :], 0.0), axis=2, keepdims=True)


def _dispatch(aff_row, aff_col, h, bsz, seq):
    t, d = h.shape
    ne = aff_row.shape[0]
    cap = EC_FACTOR * seq // ne
    group = GATHER_ROWS // cap
    n_groups = ne // group
    return pl.pallas_call(
        functools.partial(_dispatch_kernel, cap=cap),
        grid=(bsz, n_groups),
        in_specs=[pl.BlockSpec((n_groups, group, seq), lambda b, g: (0, 0, b)),
                  pl.BlockSpec((seq, ne), lambda b, g: (b, 0)),
                  pl.BlockSpec((seq, d), lambda b, g: (b, 0))],
        out_specs=[pl.BlockSpec((group, cap, d), lambda b, g: (g, b, 0)),
                   pl.BlockSpec((group, cap, 1), lambda b, g: (g, b, 0)),
                   pl.BlockSpec((seq, ne), lambda b, g: (b, 0))],
        out_shape=[jax.ShapeDtypeStruct((ne, bsz * cap, d), BF),
                   jax.ShapeDtypeStruct((ne, bsz * cap, 1), F32),
                   jax.ShapeDtypeStruct((t, ne), F32)],
        scratch_shapes=[pltpu.VMEM((n_groups, group, seq), F32)],
        compiler_params=_params(2), name="ec_dispatch",
    )(aff_row.reshape(n_groups, group, t), aff_col, h)


def _experts_kernel(xp_ref, xs_ref, gp_ref, gs_ref, wg_ref, wu_ref, wd_ref, yp_ref, ys_ref, accp_ref, accs_ref):
    f = pl.program_id(1)
    tf = wg_ref.shape[1]
    w_gate_up = jnp.concatenate([wg_ref[...].astype(BF), wu_ref[...].astype(BF)], axis=1)
    wd = wd_ref[...].astype(BF)
    groups = ((xp_ref, gp_ref, yp_ref, accp_ref), (xs_ref, gs_ref, ys_ref, accs_ref))

    @pl.when(f == 0)
    def _():
        for _, _, _, acc_ref in groups:
            acc_ref[...] = jnp.zeros_like(acc_ref)

    for x_ref, _, _, acc_ref in groups:
        for r in range(0, x_ref.shape[0], EXPERT_ROWS):
            rows = slice(r, r + EXPERT_ROWS)
            au = jnp.dot(x_ref[rows, :], w_gate_up, preferred_element_type=F32)
            a, u = au[:, :tf], au[:, tf:]
            acc_ref[rows, :] += jnp.dot((a * _sigmoid(a) * u).astype(BF), wd, preferred_element_type=F32)

    @pl.when(f == pl.num_programs(1) - 1)
    def _():
        for _, g_ref, y_ref, acc_ref in groups:
            y_ref[...] = (acc_ref[...] * g_ref[...]).astype(BF)


def _experts(xe_p, xe_s, gate_p, gate_s, w_gate, w_up, w_down, layer):
    ne, mp, d = xe_p.shape
    ms = xe_s.shape[1]
    fdim = w_gate.shape[-1]
    tf = 512
    rows = lambda m, last: pl.BlockSpec((None, m, last), lambda e, f: (e, 0, 0))
    return pl.pallas_call(
        _experts_kernel, grid=(ne, fdim // tf),
        in_specs=[rows(mp, d), rows(ms, d), rows(mp, 1), rows(ms, 1),
                  pl.BlockSpec((None, None, d, tf), lambda e, f: (layer, e, 0, f)),
                  pl.BlockSpec((None, None, d, tf), lambda e, f: (layer, e, 0, f)),
                  pl.BlockSpec((None, None, tf, d), lambda e, f: (layer, e, f, 0))],
        out_specs=[rows(mp, d), rows(ms, d)],
        out_shape=[jax.ShapeDtypeStruct((ne, mp, d), BF), jax.ShapeDtypeStruct((ne, ms, d), BF)],
        scratch_shapes=[pltpu.VMEM((mp, d), F32), pltpu.VMEM((ms, d), F32)],
        compiler_params=_params(2), name="ec_experts",
    )(xe_p, xe_s, gate_p, gate_s, w_gate, w_up, w_down)


RANK_RADIX = 32


def _combine_kernel(y_ref, rank_ref, x_ref, g_ref, o_ref, place_ref, *, cap):
    ne = y_ref.shape[0]
    n_slots = ne * cap

    @pl.when(pl.program_id(1) == 0)
    def _():
        rank = rank_ref[...]
        hi = jnp.floor(rank * (1.0 / RANK_RADIX))
        lo = rank - RANK_RADIX * hi
        lane = lax.broadcasted_iota(jnp.int32, (ne, n_slots), 1)
        expert = lax.broadcasted_iota(jnp.int32, (ne, n_slots), 0)
        own = (lane >= expert * cap) & (lane < (expert + 1) * cap)
        spread_hi = jnp.where(own, float(RANK_RADIX), 0.0).astype(BF)
        spread_lo = jnp.where(own, 1.0, 0.0).astype(BF)
        spread = (jnp.dot(hi.astype(BF), spread_hi, preferred_element_type=F32)
                  + jnp.dot(lo.astype(BF), spread_lo, preferred_element_type=F32))
        slot = (lax.broadcasted_iota(jnp.int32, (1, n_slots), 1) & (cap - 1)).astype(F32)
        place_ref[...] = jnp.where(spread == slot, 1.0, 0.0).astype(BF)

    y = y_ref[...].reshape(n_slots, y_ref.shape[2])
    acc = jnp.dot(place_ref[...], y, preferred_element_type=F32)
    o_ref[...] = x_ref[...] + g_ref[...] * acc


def _combine(y, rank, x, gate, bsz, seq):
    t, d = x.shape
    ne = y.shape[0]
    cap = EC_FACTOR * seq // ne
    assert seq <= RANK_RADIX * RANK_RADIX and cap & (cap - 1) == 0
    td = 1024
    return pl.pallas_call(
        functools.partial(_combine_kernel, cap=cap),
        grid=(bsz, d // td),
        in_specs=[pl.BlockSpec((ne, cap, td), lambda b, j: (0, b, j)),
                  pl.BlockSpec((seq, ne), lambda b, j: (b, 0)),
                  pl.BlockSpec((seq, td), lambda b, j: (b, j)),
                  pl.BlockSpec((None, 1, td), lambda b, j: (b if gate.shape[0] > 1 else 0, 0, j))],
        out_specs=pl.BlockSpec((seq, td), lambda b, j: (b, j)),
        out_shape=jax.ShapeDtypeStruct((t, d), F32),
        scratch_shapes=[pltpu.VMEM((seq, ne * cap), BF)],
        compiler_params=_params(2), name="ec_combine",
    )(y, rank, x, gate)


def _final_norm_kernel(x_ref, g_ref, o_ref):
    o_ref[...] = _rms(x_ref[...]) * g_ref[...]


def _final_norm(x, gain):
    t, d = x.shape
    tm = 256
    blk = pl.BlockSpec((tm, d), lambda i: (i, 0))
    return pl.pallas_call(
        _final_norm_kernel, grid=(t // tm,),
        in_specs=[blk, pl.BlockSpec((1, d), lambda i: (0, 0))],
        out_specs=blk, out_shape=jax.ShapeDtypeStruct((t, d), F32),
        compiler_params=_params(1), name="final_norm",
    )(x, gain.reshape(1, d))


def _rope_tables(n_tokens, rot_dim):
    rows = n_tokens // GRID_W
    row = jnp.repeat(jnp.arange(rows), GRID_W).astype(F32)
    col = jnp.tile(jnp.arange(GRID_W), rows).astype(F32)
    quarter = rot_dim // 4
    inv = ROPE_THETA ** (-jnp.arange(quarter, dtype=F32) / quarter)
    a_row, a_col = row[:, None] * inv, col[:, None] * inv
    cos_t = jnp.concatenate([jnp.cos(a_row), jnp.cos(a_row), jnp.cos(a_col), jnp.cos(a_col)], axis=-1)
    sin_t = jnp.concatenate([-jnp.sin(a_row), jnp.sin(a_row), -jnp.sin(a_col), jnp.sin(a_col)], axis=-1)
    return cos_t, sin_t


def kernel(x_prompt, x_sample, state_mlstm_c, state_mlstm_n, state_mlstm_m, cache_mla_ckv, cache_mla_krope,
           cache_gqa_k, cache_gqa_v, c, c_ctx, w_mod, b_mod, w_in_even, b_igate, b_fgate, g_mlstm, g_cq, w_uq,
           g_ckv, w_ukv, w_out_even, w_in_odd, g_qnorm, g_knorm, w_out_odd, w_router, w_expert_gate,
           w_expert_up, w_expert_down, g_final):
    d = D_MODEL
    bp, sp, _ = x_prompt.shape
    bs, ss, _ = x_sample.shape
    depth = w_mod.shape[0]
    nh = ML_HEADS
    streams = {"p": (bp, sp), "s": (bs, ss)}
    x = {"p": x_prompt.reshape(bp * sp, d), "s": x_sample.reshape(bs * ss, d)}

    c8 = jnp.concatenate([c_ctx[None], c, jnp.zeros((8 - 1 - bs, d), F32)], axis=0)
    mod_all = _mod_vectors(c8, w_mod, b_mod).reshape(depth, 8, 6, 1, d)

    def mod(layer, key, idx):
        rows = mod_all[layer, 0:1, idx] if key == "p" else mod_all[layer, 1:1 + bs, idx]
        return rows

    new_even, new_odd = [], []
    for layer in range(depth):
        j = layer // 2
        if layer % 2 == 0:
            w_in_rows = jnp.swapaxes(w_in_even, 1, 2)
            w_side = jnp.concatenate(
                [w_in_rows[j, QKVO_COLS + 4 * nh:],
                 w_in_rows[j, QKVO_COLS:QKVO_COLS + 4 * nh],
                 jnp.zeros((128 - MLA_ROPE - 4 * nh, d), F32)], axis=0)[None]
            w_q = w_uq[j].reshape(MLA_Q_RANK, MLA_HEADS, MLA_NOPE + MLA_ROPE)
            w_q = jnp.concatenate([w_q[:, :, :MLA_NOPE].reshape(MLA_Q_RANK, -1),
                                   w_q[:, :, MLA_NOPE:].reshape(MLA_Q_RANK, -1)], axis=1)
            bias_col = jnp.concatenate([b_igate[j].reshape(1, -1), b_fgate[j].reshape(1, -1)], axis=1)
            bias_row = bias_col.reshape(-1, 1)
            cos64, sin64 = _rope_tables(ss, MLA_ROPE)
            cos_q, sin_q = jnp.tile(cos64, (1, MLA_HEADS)), jnp.tile(sin64, (1, MLA_HEADS))
            pad = jnp.zeros((ss, 128 - MLA_ROPE), F32)
            cos_k, sin_k = jnp.concatenate([cos64, pad], axis=1), jnp.concatenate([sin64, pad], axis=1)
            kvc = _mm([cache_mla_ckv[:, j].reshape(-1, MLA_KV_RANK)], w_ukv, j, w_ukv.shape[-1], 512,
                      name="mla_ctx_expand")
            for key, (bsz, seq) in streams.items():
                h = _norm_mod(x[key], mod(layer, key, 0), mod(layer, key, 1), seq)
                qkvo = _mm_resident(h, w_in_rows, j, QKVO_COLS, 512, w_rows=True, name="even_in_main")
                side = _mm_resident(h, w_side, 0, SIDE_COLS, SIDE_COLS // 3, w_rows=True, name="even_in_side")
                gates = side[:, SIDE_COLS - 128 + GATE_LANE0:SIDE_COLS - 128 + GATE_LANE0 + 4 * nh]
                g_row = gates.reshape(-1, ML_CHUNK, 4 * nh).transpose(0, 2, 1)
                if key == "p":
                    c0 = jnp.zeros((bsz, 2, nh, ML_QK, ML_V), F32)
                    n0 = jnp.zeros((bsz, 2, nh, ML_QK, ML_V), F32)
                    m0 = jnp.zeros((bsz, 1, 2 * nh), F32)
                else:
                    c0 = state_mlstm_c[:, j]
                    n0 = jnp.broadcast_to(state_mlstm_n[:, j][..., None], c0.shape)
                    m0 = state_mlstm_m[:, j].reshape(bsz, 1, 2 * nh)
                hf, hb, c_fin, n_fin, m_fin = _mlstm(qkvo, side, g_row, bias_col, bias_row, c0, n0, m0, bsz, seq)
                y_ml = _mlstm_post(hf, hb, qkvo, g_mlstm[j])
                qa = _norm_mm(side, 0, g_cq[j], w_q, False, "mla_q_up")
                kv, ckv_n = _norm_mm(side, 1, g_ckv[j], w_ukv[j], True, "mla_kv_up")
                if key == "p":
                    y_a = _mla_attention(qa, kv, side, bsz, seq)
                    k_rope = side[:, SIDE_COLS - 128:SIDE_COLS - 128 + MLA_ROPE]
                    new_even.append((c_fin, n_fin[..., 0], m_fin.reshape(bsz, 2, nh),
                                     ckv_n.reshape(bsz, seq, -1), k_rope.reshape(bsz, seq, -1)))
                else:
                    y_a = _mla_attention(qa, kv, side, bsz, seq,
                                         ctx=(cos_q, sin_q, cos_k, sin_k, kvc, cache_mla_krope[:, j]))
                x[key] = _mm([y_ml, y_a], w_out_even, j, d, 1024,
                             residual=(x[key], mod(layer, key, 2), seq), name="even_out")
        else:
            cos_t, sin_t = _rope_tables(ss, GQA_HEAD_DIM)
            for key, (bsz, seq) in streams.items():
                h = _norm_mod(x[key], mod(layer, key, 0), mod(layer, key, 1), seq)
                qkv = _mm_resident(h, w_in_odd, j, w_in_odd.shape[-1], 512, name="odd_in")
                if key == "p":
                    o, k_n, v = _gqa_attention(qkv, g_qnorm[j], g_knorm[j], bsz, seq)
                    new_odd.append((k_n, v))
                else:
                    o = _gqa_attention(qkv, g_qnorm[j], g_knorm[j], bsz, seq,
                                       ctx=(cos_t, sin_t, cache_gqa_k, cache_gqa_v, j))
                x[key] = _mm([o], w_out_odd, j, d, 1024,
                             residual=(x[key], mod(layer, key, 2), seq), name="odd_out")
        routed = {}
        for key, (bsz, seq) in streams.items():
            h, aff = _norm_mod(x[key], mod(layer, key, 3), mod(layer, key, 4), seq, w_router=w_router, layer=layer)
            routed[key] = _dispatch(aff.T, aff, h, bsz, seq)
        y_p, y_s = _experts(routed["p"][0], routed["s"][0], routed["p"][1], routed["s"][1],
                            w_expert_gate, w_expert_up, w_expert_down, layer)
        for key, y in (("p", y_p), ("s", y_s)):
            bsz, seq = streams[key]
            x[key] = _combine(y, routed[key][2], x[key], mod(layer, key, 5), bsz, seq)

    y_prompt = _final_norm(x["p"], g_final).reshape(bp, sp, d)
    y_sample = _final_norm(x["s"], g_final).reshape(bs, ss, d)
    new_c = jnp.stack([e[0] for e in new_even], axis=1)
    new_n = jnp.stack([e[1] for e in new_even], axis=1)
    new_m = jnp.stack([e[2] for e in new_even], axis=1)
    new_ckv = jnp.stack([e[3] for e in new_even], axis=1)
    new_krope = jnp.stack([e[4] for e in new_even], axis=1)
    new_k = jnp.stack([e[0] for e in new_odd], axis=1)
    new_v = jnp.stack([e[1] for e in new_odd], axis=1)
    return (y_prompt, y_sample, new_c, new_n, new_m, new_ckv, new_krope, new_k, new_v)
```

```python
import functools

import jax
import jax.numpy as jnp
from jax import lax
from jax.experimental import pallas as pl
from jax.experimental.pallas import tpu as pltpu

BF = jnp.bfloat16
F32 = jnp.float32

D_MODEL = 2048
GRID_W = 64
ROPE_THETA = 10000.0
NORM_EPS = 1e-6
ML_HEADS = 8
ML_QK = 64
ML_V = 128
ML_CHUNK = 64
MLA_HEADS = 8
MLA_Q_RANK = 512
MLA_KV_RANK = 512
MLA_NOPE = 128
MLA_ROPE = 64
MLA_V = 128
GQA_HEADS = 16
GQA_KV_HEADS = 4
GQA_HEAD_DIM = 128
N_EXPERTS = 16
EXPERT_DIM = 1024
EC_FACTOR = 2

QKVO_COLS = 2 * ML_HEADS * ML_QK + 2 * ML_HEADS * ML_V
SIDE_COLS = MLA_Q_RANK + MLA_KV_RANK + 128
GATE_LANE0 = MLA_ROPE

VMEM_LIMIT_BYTES = 56 * 1024 * 1024
OUT_PROJ_ROWS = 512


def _params(n_axes):
    return pltpu.CompilerParams(dimension_semantics=("arbitrary",) * n_axes,
                                vmem_limit_bytes=VMEM_LIMIT_BYTES)


def _bdot(a, b):
    return jnp.dot(a.astype(BF), b.astype(BF), preferred_element_type=F32)


def _bdot_nt(a, b):
    return lax.dot_general(a.astype(BF), b.astype(BF), (((1,), (1,)), ((), ())),
                           preferred_element_type=F32)


def _bdot_tn(a, b):
    return lax.dot_general(a.astype(BF), b.astype(BF), (((0,), (0,)), ((), ())),
                           preferred_element_type=F32)


def _sigmoid(x):
    return 1.0 / (1.0 + jnp.exp(-x))


def _log_sigmoid(x):
    return jnp.minimum(x, 0.0) - jnp.log1p(jnp.exp(-jnp.abs(x)))


def _rms(x):
    return x * lax.rsqrt(jnp.mean(x * x, axis=-1, keepdims=True) + NORM_EPS)


def _rope(x, cos_t, sin_t, quarter):
    width = x.shape[-1]
    axis = x.ndim - 1
    lane = lax.broadcasted_iota(jnp.int32, x.shape, axis)
    partner = jnp.where((lane & quarter) == 0,
                        pltpu.roll(x, width - quarter, axis=axis),
                        pltpu.roll(x, quarter, axis=axis))
    return x * cos_t + partner * sin_t


def _mod_kernel(c_ref, w_ref, b_ref, o_ref):
    c = c_ref[...]
    o_ref[...] = _bdot(c * _sigmoid(c), w_ref[...]) + b_ref[...]


def _mod_vectors(c8, w_mod, b_mod):
    n_layers, k, n = w_mod.shape
    tn = 1024
    return pl.pallas_call(
        _mod_kernel,
        grid=(n_layers, n // tn),
        in_specs=[pl.BlockSpec((8, k), lambda l, j: (0, 0)),
                  pl.BlockSpec((None, k, tn), lambda l, j: (l, 0, j)),
                  pl.BlockSpec((None, 1, tn), lambda l, j: (l, 0, j))],
        out_specs=pl.BlockSpec((None, 8, tn), lambda l, j: (l, 0, j)),
        out_shape=jax.ShapeDtypeStruct((n_layers, 8, n), F32),
        compiler_params=_params(2),
        name="mod_vectors",
    )(c8, w_mod, b_mod.reshape(n_layers, 1, n))


def _norm_mod_kernel(x_ref, sh_ref, sc_ref, h_ref):
    h_ref[...] = (_rms(x_ref[...]) * (1.0 + sc_ref[...]) + sh_ref[...]).astype(BF)


def _batch_of_tile(n_vectors, tm, seq):
    if n_vectors == 1:
        return lambda i: 0
    assert seq % tm == 0
    return lambda i: (i * tm) // seq


def _norm_mod(x, shift, scale, seq):
    t, d = x.shape
    tm = OUT_PROJ_ROWS
    which = _batch_of_tile(shift.shape[0], tm, seq)
    vec = pl.BlockSpec((None, 1, d), lambda i: (which(i), 0, 0))
    x_spec = pl.BlockSpec((tm, d), lambda i: (i, 0))
    return pl.pallas_call(
        _norm_mod_kernel, grid=(t // tm,),
        in_specs=[x_spec, vec, vec], out_specs=x_spec,
        out_shape=jax.ShapeDtypeStruct((t, d), BF),
        compiler_params=_params(1), name="norm_mod",
    )(x, shift, scale)


def _to_bf16_kernel(w_ref, o_ref):
    o_ref[...] = w_ref[...].astype(BF)


def _to_bf16(w3, layer):
    _, k, n = w3.shape
    tk = 512
    return pl.pallas_call(
        _to_bf16_kernel, grid=(k // tk,),
        in_specs=[pl.BlockSpec((None, tk, n), lambda i: (layer, i, 0))],
        out_specs=pl.BlockSpec((tk, n), lambda i: (i, 0)),
        out_shape=jax.ShapeDtypeStruct((k, n), BF),
        compiler_params=_params(1), name="weight_to_bf16",
    )(w3)


def _out_proj_router_kernel(*refs, k_sizes):
    n_a = len(k_sizes)
    a_refs = refs[:n_a]
    w_ref, x_ref, g_ref, sh_ref, sc_ref, wr_ref, xo_ref, h_ref, aff_ref = refs[n_a:]
    w_router = wr_ref[...].astype(BF)
    chunk = OUT_PROJ_ROWS // 2
    for r in range(0, x_ref.shape[0], chunk):
        rows = slice(r, r + chunk)
        acc, off = None, 0
        for a_ref, ks in zip(a_refs, k_sizes):
            term = jnp.dot(a_ref[rows, :], w_ref[off:off + ks, :], preferred_element_type=F32)
            acc = term if acc is None else acc + term
            off += ks
        x_new = x_ref[rows, :] + g_ref[...] * acc
        xo_ref[rows, :] = x_new
        h = (_rms(x_new) * (1.0 + sc_ref[...]) + sh_ref[...]).astype(BF)
        h_ref[rows, :] = h
        logits = jnp.dot(h, w_router, preferred_element_type=F32)
        e = jnp.exp(logits - jnp.max(logits, axis=-1, keepdims=True))
        aff_ref[rows, :] = e / jnp.sum(e, axis=-1, keepdims=True)


def _out_proj_router(a_list, w_bf, x, gate, shift, scale, w_router, layer, seq):
    m, d = x.shape
    k_sizes = tuple(a.shape[1] for a in a_list)
    assert w_bf.shape == (sum(k_sizes), d) and all(a.dtype == BF for a in a_list)
    tm = OUT_PROJ_ROWS
    ne = w_router.shape[-1]
    which = _batch_of_tile(gate.shape[0], tm, seq)
    vec = pl.BlockSpec((None, 1, d), lambda i: (which(i), 0, 0))
    tile = pl.BlockSpec((tm, d), lambda i: (i, 0))
    in_specs = [pl.BlockSpec((tm, ks), lambda i: (i, 0)) for ks in k_sizes]
    in_specs += [pl.BlockSpec(w_bf.shape, lambda i: (0, 0), pipeline_mode=pl.Buffered(1)),
                 tile, vec, vec, vec, pl.BlockSpec((None, d, ne), lambda i: (layer, 0, 0))]
    return pl.pallas_call(
        functools.partial(_out_proj_router_kernel, k_sizes=k_sizes),
        grid=(m // tm,), in_specs=in_specs,
        out_specs=[tile, tile, pl.BlockSpec((tm, ne), lambda i: (i, 0))],
        out_shape=[jax.ShapeDtypeStruct((m, d), F32), jax.ShapeDtypeStruct((m, d), BF),
                   jax.ShapeDtypeStruct((m, ne), F32)],
        compiler_params=_params(1), name="out_proj_router",
    )(*a_list, w_bf, x, gate, shift, scale, w_router)


def _mm_resident_kernel(a_ref, w_ref, o_ref, *, w_rows, chunk):
    w = w_ref[...].astype(BF)
    for r in range(0, a_ref.shape[0], chunk):
        rows = slice(r, r + chunk)
        a = a_ref[rows, :].astype(BF)
        if w_rows:
            o_ref[rows, :] = lax.dot_general(a, w, (((1,), (1,)), ((), ())), preferred_element_type=F32)
        else:
            o_ref[rows, :] = jnp.dot(a, w, preferred_element_type=F32)


def _mm_resident(a, w3, layer, n_cols, tn, w_rows=False, name="mm_resident"):
    m, k = a.shape
    assert w3.shape[2 if w_rows else 1] == k and n_cols % tn == 0
    if w_rows:
        w_spec = pl.BlockSpec((None, tn, k), lambda j: (layer, j, 0))
    else:
        w_spec = pl.BlockSpec((None, k, tn), lambda j: (layer, 0, j))
    return pl.pallas_call(
        functools.partial(_mm_resident_kernel, w_rows=w_rows, chunk=min(512, m)),
        grid=(n_cols // tn,),
        in_specs=[pl.BlockSpec((m, k), lambda j: (0, 0), pipeline_mode=pl.Buffered(1)), w_spec],
        out_specs=pl.BlockSpec((m, tn), lambda j: (0, j)),
        out_shape=jax.ShapeDtypeStruct((m, n_cols), F32),
        compiler_params=_params(1), name=name,
    )(a, w3)


def _norm_mm_kernel(x_ref, g_ref, w_ref, *out_refs, with_normed):
    wbf_ref = out_refs[-1]

    @pl.when(pl.program_id(0) == 0)
    def _():
        wbf_ref[...] = w_ref[...].astype(BF)

    xn = _rms(x_ref[...]) * g_ref[...]
    out_refs[0][...] = jnp.dot(xn.astype(BF), wbf_ref[...], preferred_element_type=F32)
    if with_normed:
        out_refs[1][...] = xn


def _norm_mm(x, col_blk, gain, w, with_normed, name):
    t = x.shape[0]
    k, n = w.shape
    tm = 512
    out_specs = [pl.BlockSpec((tm, n), lambda i: (i, 0))]
    out_shape = [jax.ShapeDtypeStruct((t, n), F32)]
    if with_normed:
        out_specs.append(pl.BlockSpec((tm, k), lambda i: (i, 0)))
        out_shape.append(jax.ShapeDtypeStruct((t, k), F32))
    outs = pl.pallas_call(
        functools.partial(_norm_mm_kernel, with_normed=with_normed),
        grid=(t // tm,),
        in_specs=[pl.BlockSpec((tm, k), lambda i: (i, col_blk)),
                  pl.BlockSpec((1, k), lambda i: (0, 0)),
                  pl.BlockSpec((k, n), lambda i: (0, 0))],
        out_specs=out_specs, out_shape=out_shape,
        scratch_shapes=[pltpu.VMEM((k, n), BF)],
        compiler_params=_params(1), name=name,
    )(x, gain.reshape(1, k), w)
    return outs if with_normed else outs[0]


def _split3(x):
    hi = x.astype(BF)
    rest = x - hi.astype(F32)
    mid = rest.astype(BF)
    return hi, mid, (rest - mid.astype(F32)).astype(BF)


def _scan_max(x, reverse):
    n = x.shape[0]
    row = lax.broadcasted_iota(jnp.int32, x.shape, 0)
    k = 1
    while k < n:
        if reverse:
            shifted = jnp.where(row < n - k, pltpu.roll(x, n - k, axis=0), -jnp.inf)
        else:
            shifted = jnp.where(row >= k, pltpu.roll(x, k, axis=0), -jnp.inf)
        x = jnp.maximum(x, shifted)
        k *= 2
    return x


def _mlstm_kernel(qf_ref, kf_ref, vf_ref, qb_ref, kb_ref, vb_ref, gcf_ref, gcb_ref, grf_ref, grb_ref,
                  bc_ref, br_ref, c0_ref, n0_ref, m0_ref, hf_ref, hb_ref, c_ref, n_ref, m_ref):
    @pl.when(pl.program_id(1) == 0)
    def _():
        c_ref[...] = c0_ref[...]
        n_ref[...] = n0_ref[...]
        m_ref[...] = m0_ref[...]

    nh, lc = ML_HEADS, ML_CHUNK
    row = lax.broadcasted_iota(jnp.int32, (lc, lc), 0)
    col = lax.broadcasted_iota(jnp.int32, (lc, lc), 1)
    lower = col <= row
    upper = col >= row
    ones_v = jnp.ones((lc, ML_V), BF)
    directions = ((qf_ref, kf_ref, vf_ref, gcf_ref, grf_ref, hf_ref, lower, upper),
                  (qb_ref, kb_ref, vb_ref, gcb_ref, grb_ref, hb_ref, upper, lower))
    gate_terms = []
    for d, (_, _, _, gc_ref, gr_ref, _, allowed, allowed_t) in enumerate(directions):
        g_col = gc_ref[:, GATE_LANE0:GATE_LANE0 + 4 * nh] + bc_ref[...]
        g_row = gr_ref[...] + br_ref[...]
        i_col = g_col[:, nh * d:nh * (d + 1)]
        f_col = _log_sigmoid(g_col[:, 2 * nh + nh * d:2 * nh + nh * (d + 1)])
        i_row = g_row[nh * d:nh * (d + 1), :]
        f_row = _log_sigmoid(g_row[2 * nh + nh * d:2 * nh + nh * (d + 1), :])
        tri = jnp.where(allowed, 1.0, 0.0).astype(BF)
        tri_t = jnp.where(allowed_t, 1.0, 0.0).astype(BF)
        b_col = sum(jnp.dot(tri, part, preferred_element_type=F32) for part in _split3(f_col))
        b_row = sum(jnp.dot(part, tri_t, preferred_element_type=F32) for part in _split3(f_row))
        b_end = jnp.sum(f_col, axis=0, keepdims=True)
        m_prev = m_ref[:, nh * d:nh * (d + 1)]
        m_t = b_col + jnp.maximum(m_prev, _scan_max(i_col - b_col, reverse=d == 1))
        g_col_end = b_end - b_col + i_col
        m_new = jnp.maximum(b_end + m_prev, jnp.max(g_col_end, axis=0, keepdims=True))
        gate_terms.append(dict(
            u=b_col - m_t, r_row=i_row - b_row, w_inter=jnp.exp(b_col + m_prev - m_t), floor=jnp.exp(-m_t),
            k_scale=jnp.exp(g_col_end - m_new), decay=jnp.exp(b_end + m_prev - m_new), m_new=m_new))
    work = []
    for d, (q_ref, k_ref, v_ref, _, _, h_ref, allowed, _) in enumerate(directions):
        for h in range(nh):
            w = dict(d=d, h=h, h_ref=h_ref, allowed=allowed, g=gate_terms[d])
            w["q"] = (q_ref[:, h * ML_QK:(h + 1) * ML_QK] * (ML_QK ** -0.5)).astype(BF)
            w["k"] = k_ref[:, h * ML_QK:(h + 1) * ML_QK]
            w["v1"] = jnp.concatenate([v_ref[:, h * ML_V:(h + 1) * ML_V].astype(BF), ones_v], axis=1)
            work.append(w)
    for w in work:
        w["qk"] = _bdot_nt(w["q"], w["k"])
    for w in work:
        d, h = w["d"], w["h"]
        w["c_prev"], w["n_prev"] = c_ref[d, h], n_ref[d, h]
        state = jnp.concatenate([w["c_prev"], w["n_prev"]], axis=1).astype(BF)
        w["q_state"] = jnp.dot(w["q"], state, preferred_element_type=F32)
    for w in work:
        h, g = w["h"], w["g"]
        kw = w["k"] * g["k_scale"][:, h:h + 1]
        w["kv"] = _bdot_tn(kw, w["v1"])
    for w in work:
        h, g = w["h"], w["g"]
        dmat = jnp.where(w["allowed"], g["u"][:, h:h + 1] + g["r_row"][h:h + 1, :], -jnp.inf)
        sw = w["qk"] * jnp.exp(dmat)
        sw_hi = sw.astype(BF)
        sw_lo = (sw - sw_hi.astype(F32)).astype(BF)
        w["pv"] = jnp.dot(sw_hi, w["v1"], preferred_element_type=F32)
        w["den_lo"] = jnp.dot(sw_lo, ones_v, preferred_element_type=F32)
    for w in work:
        d, h, g = w["d"], w["h"], w["g"]
        w_inter = g["w_inter"][:, h:h + 1]
        num = w_inter * w["q_state"][:, :ML_V] + w["pv"][:, :ML_V]
        den = w_inter * w["q_state"][:, ML_V:] + (w["pv"][:, ML_V:] + w["den_lo"])
        w["h_ref"][:, h * ML_V:(h + 1) * ML_V] = num / jnp.maximum(jnp.abs(den), g["floor"][:, h:h + 1])
        decay = g["decay"][:, h:h + 1]
        c_ref[d, h] = decay * w["c_prev"] + w["kv"][:, :ML_V]
        n_ref[d, h] = decay * w["n_prev"] + w["kv"][:, ML_V:]
    m_ref[:, :nh] = gate_terms[0]["m_new"]
    m_ref[:, nh:] = gate_terms[1]["m_new"]


def _mlstm(qkvo, side, g_row, b_col, b_row, c0, n0, m0, bsz, seq):
    t = qkvo.shape[0]
    nc = seq // ML_CHUNK
    lc = ML_CHUNK
    nq = ML_HEADS * ML_QK
    nv = ML_HEADS * ML_V
    side_blk = (SIDE_COLS - 128) // 128
    fwd = lambda b, c: b * nc + c
    bwd = lambda b, c: b * nc + (nc - 1 - c)

    def specs(pos):
        return [pl.BlockSpec((lc, nq), lambda b, c: (pos(b, c), 0)),
                pl.BlockSpec((lc, nq), lambda b, c: (pos(b, c), 1)),
                pl.BlockSpec((lc, nv), lambda b, c: (pos(b, c), 1))]

    state = lambda *shape: pl.BlockSpec((None,) + shape, lambda b, c: (b,) + (0,) * len(shape))
    in_specs = specs(fwd) + specs(bwd) + [
        pl.BlockSpec((lc, 128), lambda b, c: (fwd(b, c), side_blk)),
        pl.BlockSpec((lc, 128), lambda b, c: (bwd(b, c), side_blk)),
        pl.BlockSpec((None, 4 * ML_HEADS, lc), lambda b, c: (fwd(b, c), 0, 0)),
        pl.BlockSpec((None, 4 * ML_HEADS, lc), lambda b, c: (bwd(b, c), 0, 0)),
        pl.BlockSpec((1, 4 * ML_HEADS), lambda b, c: (0, 0)),
        pl.BlockSpec((4 * ML_HEADS, 1), lambda b, c: (0, 0)),
        state(2, ML_HEADS, ML_QK, ML_V), state(2, ML_HEADS, ML_QK, ML_V), state(1, 2 * ML_HEADS)]
    out_specs = [pl.BlockSpec((lc, nv), lambda b, c: (fwd(b, c), 0)),
                 pl.BlockSpec((lc, nv), lambda b, c: (bwd(b, c), 0)),
                 state(2, ML_HEADS, ML_QK, ML_V), state(2, ML_HEADS, ML_QK, ML_V), state(1, 2 * ML_HEADS)]
    out_shape = [jax.ShapeDtypeStruct((t, nv), F32), jax.ShapeDtypeStruct((t, nv), F32),
                 jax.ShapeDtypeStruct((bsz, 2, ML_HEADS, ML_QK, ML_V), F32),
                 jax.ShapeDtypeStruct((bsz, 2, ML_HEADS, ML_QK, ML_V), F32),
                 jax.ShapeDtypeStruct((bsz, 1, 2 * ML_HEADS), F32)]
    return pl.pallas_call(
        _mlstm_kernel, grid=(bsz, nc), in_specs=in_specs, out_specs=out_specs, out_shape=out_shape,
        compiler_params=_params(2), name="mlstm",
    )(qkvo, qkvo, qkvo, qkvo, qkvo, qkvo, side, side, g_row, g_row, b_col, b_row, c0, n0, m0)


def _mlstm_post_kernel(hf_ref, hb_ref, o_ref, g_ref, y_ref):
    for h in range(ML_HEADS):
        sl = slice(h * ML_V, (h + 1) * ML_V)
        hn = _rms(hf_ref[:, sl] + hb_ref[:, sl]) * g_ref[:, sl]
        y_ref[:, sl] = (hn * _sigmoid(o_ref[:, sl])).astype(BF)


def _mlstm_post(hf, hb, qkvo, gain):
    t, nv = hf.shape
    tm = 512
    blk = pl.BlockSpec((tm, nv), lambda i: (i, 0))
    return pl.pallas_call(
        _mlstm_post_kernel, grid=(t // tm,),
        in_specs=[blk, blk, pl.BlockSpec((tm, nv), lambda i: (i, 2)), pl.BlockSpec((1, nv), lambda i: (0, 0))],
        out_specs=blk, out_shape=jax.ShapeDtypeStruct((t, nv), BF),
        compiler_params=_params(1), name="mlstm_post",
    )(hf, hb, qkvo, gain.reshape(1, nv))


LOG2_E = 1.4426950408889634


def _softmax_terms(scores, scale):
    c = scale * LOG2_E
    scaled = [s * c for s in scores]
    m = functools.reduce(jnp.maximum, [jnp.max(s, axis=-1, keepdims=True) for s in scaled])
    e = [jnp.exp2(s - m) for s in scaled]
    den = functools.reduce(lambda a, b: a + b, [jnp.sum(x, axis=-1, keepdims=True) for x in e])
    return [x.astype(BF) for x in e], 1.0 / den


def _mla_kernel(*refs, with_ctx):
    if with_ctx:
        qa_ref, kv_ref, side_ref, cq_ref, sq_ref, ck_ref, sk_ref, kvc_ref, krc_ref, o_ref = refs
    else:
        qa_ref, kv_ref, side_ref, o_ref = refs
    scale = (MLA_NOPE + MLA_ROPE) ** -0.5
    nope_cols = MLA_HEADS * MLA_NOPE
    q_rope = qa_ref[:, nope_cols:]
    k_rope = side_ref[...]
    if with_ctx:
        q_rope_rot = _rope(q_rope, cq_ref[...], sq_ref[...], MLA_ROPE // 4).astype(BF)
        k_rope_rot = _rope(k_rope, ck_ref[...], sk_ref[...], MLA_ROPE // 4)[:, :MLA_ROPE].astype(BF)
        k_rope_ctx = krc_ref[...].astype(BF)
        q_rope = q_rope.astype(BF)
    else:
        q_rope_rot = q_rope.astype(BF)
        k_rope_rot = k_rope[:, :MLA_ROPE].astype(BF)
    heads = range(MLA_HEADS)
    rope_cols = [slice(h * MLA_ROPE, (h + 1) * MLA_ROPE) for h in heads]
    kv0 = [h * (MLA_NOPE + MLA_V) for h in heads]
    q_n = [qa_ref[:, h * MLA_NOPE:(h + 1) * MLA_NOPE].astype(BF) for h in heads]
    scores = [[_bdot_nt(q_n[h], kv_ref[:, kv0[h]:kv0[h] + MLA_NOPE])
               + _bdot_nt(q_rope_rot[:, rope_cols[h]], k_rope_rot)] for h in heads]
    if with_ctx:
        for h in heads:
            scores[h].append(_bdot_nt(q_n[h], kvc_ref[:, kv0[h]:kv0[h] + MLA_NOPE])
                             + _bdot_nt(q_rope[:, rope_cols[h]], k_rope_ctx))
    weights = [_softmax_terms(scores[h], scale) for h in heads]
    for h in heads:
        (e, inv_den), v0 = weights[h], kv0[h] + MLA_NOPE
        out = jnp.dot(e[0], kv_ref[:, v0:v0 + MLA_V].astype(BF), preferred_element_type=F32)
        if with_ctx:
            out = out + jnp.dot(e[1], kvc_ref[:, v0:v0 + MLA_V].astype(BF), preferred_element_type=F32)
        o_ref[:, h * MLA_V:(h + 1) * MLA_V] = (out * inv_den).astype(BF)


def _mla_attention(qa, kv, side, bsz, seq, ctx=None):
    t = qa.shape[0]
    tq = 256
    nq = seq // tq
    side_blk = (SIDE_COLS - 128) // 128
    nkv = MLA_HEADS * (MLA_NOPE + MLA_V)
    in_specs = [pl.BlockSpec((tq, qa.shape[1]), lambda b, i: (b * nq + i, 0)),
                pl.BlockSpec((seq, nkv), lambda b, i: (b, 0)),
                pl.BlockSpec((seq, 128), lambda b, i: (b, side_blk))]
    args = [qa, kv, side]
    if ctx is not None:
        cos_q, sin_q, cos_k, sin_k, kvc, krc = ctx
        past = krc.shape[1]
        in_specs += [pl.BlockSpec((tq, cos_q.shape[1]), lambda b, i: (i, 0)),
                     pl.BlockSpec((tq, cos_q.shape[1]), lambda b, i: (i, 0)),
                     pl.BlockSpec((seq, 128), lambda b, i: (0, 0)),
                     pl.BlockSpec((seq, 128), lambda b, i: (0, 0)),
                     pl.BlockSpec((past, nkv), lambda b, i: (b, 0)),
                     pl.BlockSpec((None, past, MLA_ROPE), lambda b, i: (b, 0, 0))]
        args += [cos_q, sin_q, cos_k, sin_k, kvc, krc]
    nout = MLA_HEADS * MLA_V
    return pl.pallas_call(
        functools.partial(_mla_kernel, with_ctx=ctx is not None),
        grid=(bsz, nq), in_specs=in_specs,
        out_specs=pl.BlockSpec((tq, nout), lambda b, i: (b * nq + i, 0)),
        out_shape=jax.ShapeDtypeStruct((t, nout), BF),
        compiler_params=_params(2), name="mla_attention",
    )(*args)


def _gqa_kernel(*refs, with_ctx):
    if with_ctx:
        q_ref, k_ref, v_ref, gq_ref, gk_ref, cq_ref, sq_ref, ck_ref, sk_ref, kc_ref, vc_ref, o_ref, ksrc_ref = refs
    else:
        q_ref, k_ref, v_ref, gq_ref, gk_ref, o_ref, kn_ref, vo_ref, ksrc_ref = refs
    hd = GQA_HEAD_DIM
    scale = hd ** -0.5

    @pl.when(pl.program_id(2) == 0)
    def _():
        k_n = _rms(k_ref[...]) * gk_ref[...]
        if with_ctx:
            ksrc_ref[...] = _rope(k_n, ck_ref[...], sk_ref[...], hd // 4).astype(BF)
        else:
            ksrc_ref[...] = k_n.astype(BF)
            kn_ref[...] = k_n
            vo_ref[...] = v_ref[...]

    k_src = ksrc_ref[...]
    v = v_ref[...].astype(BF)
    if with_ctx:
        k_ctx, v_ctx = kc_ref[...].astype(BF), vc_ref[...].astype(BF)
    heads = range(GQA_HEADS // GQA_KV_HEADS)
    cols = [slice(r * hd, (r + 1) * hd) for r in heads]
    q_n = [_rms(q_ref[:, cols[r]]) * gq_ref[...] for r in heads]
    if with_ctx:
        scores = [[_bdot_nt(_rope(q_n[r], cq_ref[...], sq_ref[...], hd // 4), k_src), _bdot_nt(q_n[r], k_ctx)]
                  for r in heads]
    else:
        scores = [[_bdot_nt(q_n[r], k_src)] for r in heads]
    weights = [_softmax_terms(scores[r], scale) for r in heads]
    for r in heads:
        e, inv_den = weights[r]
        out = jnp.dot(e[0], v, preferred_element_type=F32)
        if with_ctx:
            out = out + jnp.dot(e[1], v_ctx, preferred_element_type=F32)
        o_ref[:, cols[r]] = (out * inv_den).astype(BF)


def _gqa_attention(qkv, g_q, g_k, bsz, seq, ctx=None):
    t = qkv.shape[0]
    hd = GQA_HEAD_DIM
    rep = GQA_HEADS // GQA_KV_HEADS
    tq = 256
    nq = seq // tq
    in_specs = [pl.BlockSpec((tq, rep * hd), lambda b, g, i: (b * nq + i, g)),
                pl.BlockSpec((seq, hd), lambda b, g, i: (b, GQA_HEADS + g)),
                pl.BlockSpec((seq, hd), lambda b, g, i: (b, GQA_HEADS + GQA_KV_HEADS + g)),
                pl.BlockSpec((1, hd), lambda b, g, i: (0, 0)),
                pl.BlockSpec((1, hd), lambda b, g, i: (0, 0))]
    args = [qkv, qkv, qkv, g_q.reshape(1, hd), g_k.reshape(1, hd)]
    o_spec = pl.BlockSpec((tq, rep * hd), lambda b, g, i: (b * nq + i, g))
    o_shape = jax.ShapeDtypeStruct((t, GQA_HEADS * hd), BF)
    if ctx is not None:
        cos_t, sin_t, kc, vc, j = ctx
        past = kc.shape[3]
        cache = pl.BlockSpec((None, None, None, past, hd), lambda b, g, i: (b, j, g, 0, 0))
        in_specs += [pl.BlockSpec((tq, hd), lambda b, g, i: (i, 0)),
                     pl.BlockSpec((tq, hd), lambda b, g, i: (i, 0)),
                     pl.BlockSpec((seq, hd), lambda b, g, i: (0, 0)),
                     pl.BlockSpec((seq, hd), lambda b, g, i: (0, 0)),
                     cache, cache]
        args += [cos_t, sin_t, cos_t, sin_t, kc, vc]
        out_specs, out_shape = o_spec, o_shape
    else:
        head_major = pl.BlockSpec((None, None, seq, hd), lambda b, g, i: (b, g, 0, 0))
        out_specs = [o_spec, head_major, head_major]
        kv_shape = jax.ShapeDtypeStruct((bsz, GQA_KV_HEADS, seq, hd), F32)
        out_shape = [o_shape, kv_shape, kv_shape]
    return pl.pallas_call(
        functools.partial(_gqa_kernel, with_ctx=ctx is not None),
        grid=(bsz, GQA_KV_HEADS, nq), in_specs=in_specs, out_specs=out_specs, out_shape=out_shape,
        scratch_shapes=[pltpu.VMEM((seq, hd), BF)],
        compiler_params=_params(3), name="gqa_attention",
    )(*args)


RANK_BLOCK = 256
GATHER_ROWS = 512
EXPERT_ROWS = 256


def _dispatch_kernel(ar_ref, ac_ref, h_ref, xe_ref, gate_ref, rankc_ref, rankr_ref, *, cap):
    n_groups, group, s = ar_ref.shape
    g = pl.program_id(1)

    @pl.when(g == 0)
    def _():
        blk = min(RANK_BLOCK, s)
        rankr_ref[...] = jnp.zeros_like(rankr_ref)
        for j in range(s // blk):
            rows = pl.ds(j * blk, blk)
            i0 = lax.broadcasted_iota(jnp.int32, (blk, s), 0) + j * blk
            i1 = lax.broadcasted_iota(jnp.int32, (blk, s), 1)
            sub_first = jnp.where(i0 < i1, 1.0, 0.0)
            for e in range(n_groups * group):
                eg, ei = e // group, e % group
                a_row = ar_ref[eg, ei:ei + 1, :]
                a_col = ac_ref[rows, e:e + 1]
                ahead = jnp.where(a_col > a_row, 1.0, jnp.where(a_col >= a_row, sub_first, 0.0))
                rankr_ref[eg, ei:ei + 1, :] += jnp.sum(ahead, axis=0, keepdims=True)
                rankc_ref[rows, e:e + 1] = (s - 1.0) - jnp.sum(ahead, axis=1, keepdims=True)

    slot = lax.broadcasted_iota(jnp.int32, (group, cap, s), 1).astype(F32)
    pick = rankr_ref[g][:, None, :] == slot
    onehot = jnp.where(pick, 1.0, 0.0).reshape(group * cap, s).astype(BF)
    rows_f32 = jnp.dot(onehot, h_ref[...], preferred_element_type=F32)
    xe_ref[...] = rows_f32.reshape(group, cap, -1).astype(BF)
    gate_ref[...] = jnp.sum(jnp.where(pick, ar_ref[g][:, None, :], 0.0), axis=2, keepdims=True)


def _dispatch(aff_row, aff_col, h, bsz, seq):
    t, d = h.shape
    ne = aff_row.shape[0]
    cap = EC_FACTOR * seq // ne
    group = GATHER_ROWS // cap
    n_groups = ne // group
    return pl.pallas_call(
        functools.partial(_dispatch_kernel, cap=cap),
        grid=(bsz, n_groups),
        in_specs=[pl.BlockSpec((n_groups, group, seq), lambda b, g: (0, 0, b)),
                  pl.BlockSpec((seq, ne), lambda b, g: (b, 0)),
                  pl.BlockSpec((seq, d), lambda b, g: (b, 0))],
        out_specs=[pl.BlockSpec((group, cap, d), lambda b, g: (g, b, 0)),
                   pl.BlockSpec((group, cap, 1), lambda b, g: (g, b, 0)),
                   pl.BlockSpec((seq, ne), lambda b, g: (b, 0))],
        out_shape=[jax.ShapeDtypeStruct((ne, bsz * cap, d), BF),
                   jax.ShapeDtypeStruct((ne, bsz * cap, 1), F32),
                   jax.ShapeDtypeStruct((t, ne), F32)],
        scratch_shapes=[pltpu.VMEM((n_groups, group, seq), F32)],
        compiler_params=_params(2), name="ec_dispatch",
    )(aff_row.reshape(n_groups, group, t), aff_col, h)


def _experts_kernel(xp_ref, xs_ref, gp_ref, gs_ref, wg_ref, wu_ref, wd_ref, yp_ref, ys_ref, accp_ref, accs_ref):
    f = pl.program_id(1)
    tf = wg_ref.shape[1]
    w_gate_up = jnp.concatenate([wg_ref[...].astype(BF), wu_ref[...].astype(BF)], axis=1)
    wd = wd_ref[...].astype(BF)
    groups = ((xp_ref, gp_ref, yp_ref, accp_ref), (xs_ref, gs_ref, ys_ref, accs_ref))

    @pl.when(f == 0)
    def _():
        for _, _, _, acc_ref in groups:
            acc_ref[...] = jnp.zeros_like(acc_ref)

    for x_ref, _, _, acc_ref in groups:
        for r in range(0, x_ref.shape[0], EXPERT_ROWS):
            rows = slice(r, r + EXPERT_ROWS)
            au = jnp.dot(x_ref[rows, :], w_gate_up, preferred_element_type=F32)
            a, u = au[:, :tf], au[:, tf:]
            acc_ref[rows, :] += jnp.dot((a * _sigmoid(a) * u).astype(BF), wd, preferred_element_type=F32)

    @pl.when(f == pl.num_programs(1) - 1)
    def _():
        for _, g_ref, y_ref, acc_ref in groups:
            y_ref[...] = (acc_ref[...] * g_ref[...]).astype(BF)


def _experts(xe_p, xe_s, gate_p, gate_s, w_gate, w_up, w_down, layer):
    ne, mp, d = xe_p.shape
    ms = xe_s.shape[1]
    fdim = w_gate.shape[-1]
    tf = 512
    rows = lambda m, last: pl.BlockSpec((None, m, last), lambda e, f: (e, 0, 0))
    return pl.pallas_call(
        _experts_kernel, grid=(ne, fdim // tf),
        in_specs=[rows(mp, d), rows(ms, d), rows(mp, 1), rows(ms, 1),
                  pl.BlockSpec((None, None, d, tf), lambda e, f: (layer, e, 0, f)),
                  pl.BlockSpec((None, None, d, tf), lambda e, f: (layer, e, 0, f)),
                  pl.BlockSpec((None, None, tf, d), lambda e, f: (layer, e, f, 0))],
        out_specs=[rows(mp, d), rows(ms, d)],
        out_shape=[jax.ShapeDtypeStruct((ne, mp, d), BF), jax.ShapeDtypeStruct((ne, ms, d), BF)],
        scratch_shapes=[pltpu.VMEM((mp, d), F32), pltpu.VMEM((ms, d), F32)],
        compiler_params=_params(2), name="ec_experts",
    )(xe_p, xe_s, gate_p, gate_s, w_gate, w_up, w_down)


RANK_RADIX = 32


COMBINE_ROWS = 256


def _combine_kernel(y_ref, rank_ref, x_ref, g_ref, *refs, cap, final):
    post_refs, place_ref = refs[:-1], refs[-1]
    ne = y_ref.shape[0]
    n_slots = ne * cap

    @pl.when(pl.program_id(1) == 0)
    def _():
        rank = rank_ref[...]
        hi = jnp.floor(rank * (1.0 / RANK_RADIX))
        lo = rank - RANK_RADIX * hi
        lane = lax.broadcasted_iota(jnp.int32, (ne, n_slots), 1)
        expert = lax.broadcasted_iota(jnp.int32, (ne, n_slots), 0)
        own = (lane >= expert * cap) & (lane < (expert + 1) * cap)
        spread_hi = jnp.where(own, float(RANK_RADIX), 0.0).astype(BF)
        spread_lo = jnp.where(own, 1.0, 0.0).astype(BF)
        spread = (jnp.dot(hi.astype(BF), spread_hi, preferred_element_type=F32)
                  + jnp.dot(lo.astype(BF), spread_lo, preferred_element_type=F32))
        slot = (lax.broadcasted_iota(jnp.int32, (1, n_slots), 1) & (cap - 1)).astype(F32)
        place_ref[...] = jnp.where(spread == slot, 1.0, 0.0).astype(BF)

    ts = x_ref.shape[0]
    rows = pl.ds(pl.multiple_of(pl.program_id(1) * ts, ts), ts)
    y = y_ref[...].reshape(n_slots, y_ref.shape[2])
    x_new = x_ref[...] + g_ref[...] * jnp.dot(place_ref[rows, :], y, preferred_element_type=F32)
    if final:
        (gain_ref,), (out_ref,) = post_refs[:1], post_refs[1:]
        out_ref[...] = _rms(x_new) * gain_ref[...]
    else:
        (sh_ref, sc_ref), (xo_ref, h_ref) = post_refs[:2], post_refs[2:]
        xo_ref[...] = x_new
        h_ref[...] = (_rms(x_new) * (1.0 + sc_ref[...]) + sh_ref[...]).astype(BF)


def _combine(y, rank, x, gate, bsz, seq, next_shift_scale=None, final_gain=None):
    t, d = x.shape
    ne = y.shape[0]
    cap = EC_FACTOR * seq // ne
    assert seq <= RANK_RADIX * RANK_RADIX and cap & (cap - 1) == 0
    ts = min(COMBINE_ROWS, seq)
    nt = seq // ts
    final = final_gain is not None
    vec = lambda v: pl.BlockSpec((None, 1, d), lambda b, i: (b if v.shape[0] > 1 else 0, 0, 0))
    tile = pl.BlockSpec((ts, d), lambda b, i: (b * nt + i, 0))
    in_specs = [pl.BlockSpec((ne, cap, d), lambda b, i: (0, b, 0)),
                pl.BlockSpec((seq, ne), lambda b, i: (b, 0)), tile, vec(gate)]
    if final:
        args = [final_gain.reshape(1, d)]
        in_specs.append(pl.BlockSpec((1, d), lambda b, i: (0, 0)))
        out_specs, out_shape = tile, jax.ShapeDtypeStruct((t, d), F32)
    else:
        args = list(next_shift_scale)
        in_specs += [vec(v) for v in args]
        out_specs = [tile, tile]
        out_shape = [jax.ShapeDtypeStruct((t, d), F32), jax.ShapeDtypeStruct((t, d), BF)]
    return pl.pallas_call(
        functools.partial(_combine_kernel, cap=cap, final=final),
        grid=(bsz, nt), in_specs=in_specs, out_specs=out_specs, out_shape=out_shape,
        scratch_shapes=[pltpu.VMEM((seq, ne * cap), BF)],
        compiler_params=_params(2), name="ec_combine",
    )(y, rank, x, gate, *args)


def _rope_tables(n_tokens, rot_dim):
    rows = n_tokens // GRID_W
    row = jnp.repeat(jnp.arange(rows), GRID_W).astype(F32)
    col = jnp.tile(jnp.arange(GRID_W), rows).astype(F32)
    quarter = rot_dim // 4
    inv = ROPE_THETA ** (-jnp.arange(quarter, dtype=F32) / quarter)
    a_row, a_col = row[:, None] * inv, col[:, None] * inv
    cos_t = jnp.concatenate([jnp.cos(a_row), jnp.cos(a_row), jnp.cos(a_col), jnp.cos(a_col)], axis=-1)
    sin_t = jnp.concatenate([-jnp.sin(a_row), jnp.sin(a_row), -jnp.sin(a_col), jnp.sin(a_col)], axis=-1)
    return cos_t, sin_t


def kernel(x_prompt, x_sample, state_mlstm_c, state_mlstm_n, state_mlstm_m, cache_mla_ckv, cache_mla_krope,
           cache_gqa_k, cache_gqa_v, c, c_ctx, w_mod, b_mod, w_in_even, b_igate, b_fgate, g_mlstm, g_cq, w_uq,
           g_ckv, w_ukv, w_out_even, w_in_odd, g_qnorm, g_knorm, w_out_odd, w_router, w_expert_gate,
           w_expert_up, w_expert_down, g_final):
    d = D_MODEL
    bp, sp, _ = x_prompt.shape
    bs, ss, _ = x_sample.shape
    depth = w_mod.shape[0]
    nh = ML_HEADS
    streams = {"p": (bp, sp), "s": (bs, ss)}
    x = {"p": x_prompt.reshape(bp * sp, d), "s": x_sample.reshape(bs * ss, d)}

    c8 = jnp.concatenate([c_ctx[None], c, jnp.zeros((8 - 1 - bs, d), F32)], axis=0)
    mod_all = _mod_vectors(c8, w_mod, b_mod).reshape(depth, 8, 6, 1, d)

    def mod(layer, key, idx):
        rows = mod_all[layer, 0:1, idx] if key == "p" else mod_all[layer, 1:1 + bs, idx]
        return rows

    new_even, new_odd = [], []
    h_in = {key: _norm_mod(x[key], mod(0, key, 0), mod(0, key, 1), streams[key][1]) for key in streams}
    routed_in = {}
    for layer in range(depth):
        j = layer // 2
        if layer % 2 == 0:
            w_out = _to_bf16(w_out_even, j)
            w_in_rows = jnp.swapaxes(w_in_even, 1, 2)
            w_side = jnp.concatenate(
                [w_in_rows[j, QKVO_COLS + 4 * nh:],
                 w_in_rows[j, QKVO_COLS:QKVO_COLS + 4 * nh],
                 jnp.zeros((128 - MLA_ROPE - 4 * nh, d), F32)], axis=0)[None]
            w_q = w_uq[j].reshape(MLA_Q_RANK, MLA_HEADS, MLA_NOPE + MLA_ROPE)
            w_q = jnp.concatenate([w_q[:, :, :MLA_NOPE].reshape(MLA_Q_RANK, -1),
                                   w_q[:, :, MLA_NOPE:].reshape(MLA_Q_RANK, -1)], axis=1)
            bias_col = jnp.concatenate([b_igate[j].reshape(1, -1), b_fgate[j].reshape(1, -1)], axis=1)
            bias_row = bias_col.reshape(-1, 1)
            cos64, sin64 = _rope_tables(ss, MLA_ROPE)
            cos_q, sin_q = jnp.tile(cos64, (1, MLA_HEADS)), jnp.tile(sin64, (1, MLA_HEADS))
            pad = jnp.zeros((ss, 128 - MLA_ROPE), F32)
            cos_k, sin_k = jnp.concatenate([cos64, pad], axis=1), jnp.concatenate([sin64, pad], axis=1)
            kvc = _mm_resident(cache_mla_ckv[:, j].reshape(-1, MLA_KV_RANK), w_ukv, j, w_ukv.shape[-1], 512,
                               name="mla_ctx_expand")
            for key, (bsz, seq) in streams.items():
                h = h_in[key]
                qkvo = _mm_resident(h, w_in_rows, j, QKVO_COLS, 512, w_rows=True, name="even_in_main")
                side = _mm_resident(h, w_side, 0, SIDE_COLS, SIDE_COLS // 3, w_rows=True, name="even_in_side")
                gates = side[:, SIDE_COLS - 128 + GATE_LANE0:SIDE_COLS - 128 + GATE_LANE0 + 4 * nh]
                g_row = gates.reshape(-1, ML_CHUNK, 4 * nh).transpose(0, 2, 1)
                if key == "p":
                    c0 = jnp.zeros((bsz, 2, nh, ML_QK, ML_V), F32)
                    n0 = jnp.zeros((bsz, 2, nh, ML_QK, ML_V), F32)
                    m0 = jnp.zeros((bsz, 1, 2 * nh), F32)
                else:
                    c0 = state_mlstm_c[:, j]
                    n0 = jnp.broadcast_to(state_mlstm_n[:, j][..., None], c0.shape)
                    m0 = state_mlstm_m[:, j].reshape(bsz, 1, 2 * nh)
                hf, hb, c_fin, n_fin, m_fin = _mlstm(qkvo, side, g_row, bias_col, bias_row, c0, n0, m0, bsz, seq)
                y_ml = _mlstm_post(hf, hb, qkvo, g_mlstm[j])
                qa = _norm_mm(side, 0, g_cq[j], w_q, False, "mla_q_up")
                kv, ckv_n = _norm_mm(side, 1, g_ckv[j], w_ukv[j], True, "mla_kv_up")
                if key == "p":
                    y_a = _mla_attention(qa, kv, side, bsz, seq)
                    k_rope = side[:, SIDE_COLS - 128:SIDE_COLS - 128 + MLA_ROPE]
                    new_even.append((c_fin, n_fin[..., 0], m_fin.reshape(bsz, 2, nh),
                                     ckv_n.reshape(bsz, seq, -1), k_rope.reshape(bsz, seq, -1)))
                else:
                    y_a = _mla_attention(qa, kv, side, bsz, seq,
                                         ctx=(cos_q, sin_q, cos_k, sin_k, kvc, cache_mla_krope[:, j]))
                routed_in[key] = _out_proj_router([y_ml, y_a], w_out, x[key], mod(layer, key, 2), mod(layer, key, 3),
                                                  mod(layer, key, 4), w_router, layer, seq)
        else:
            w_out = _to_bf16(w_out_odd, j)
            cos_t, sin_t = _rope_tables(ss, GQA_HEAD_DIM)
            for key, (bsz, seq) in streams.items():
                qkv = _mm_resident(h_in[key], w_in_odd, j, w_in_odd.shape[-1], 512, name="odd_in")
                if key == "p":
                    o, k_n, v = _gqa_attention(qkv, g_qnorm[j], g_knorm[j], bsz, seq)
                    new_odd.append((k_n, v))
                else:
                    o = _gqa_attention(qkv, g_qnorm[j], g_knorm[j], bsz, seq,
                                       ctx=(cos_t, sin_t, cache_gqa_k, cache_gqa_v, j))
                routed_in[key] = _out_proj_router([o], w_out, x[key], mod(layer, key, 2), mod(layer, key, 3),
                                                  mod(layer, key, 4), w_router, layer, seq)
        routed = {}
        for key, (bsz, seq) in streams.items():
            x[key], h, aff = routed_in[key]
            routed[key] = _dispatch(aff.T, aff, h, bsz, seq)
        y_p, y_s = _experts(routed["p"][0], routed["s"][0], routed["p"][1], routed["s"][1],
                            w_expert_gate, w_expert_up, w_expert_down, layer)
        for key, y in (("p", y_p), ("s", y_s)):
            bsz, seq = streams[key]
            if layer + 1 < depth:
                x[key], h_in[key] = _combine(
                    y, routed[key][2], x[key], mod(layer, key, 5), bsz, seq,
                    next_shift_scale=(mod(layer + 1, key, 0), mod(layer + 1, key, 1)))
            else:
                x[key] = _combine(y, routed[key][2], x[key], mod(layer, key, 5), bsz, seq, final_gain=g_final)

    y_prompt = x["p"].reshape(bp, sp, d)
    y_sample = x["s"].reshape(bs, ss, d)
    new_c = jnp.stack([e[0] for e in new_even], axis=1)
    new_n = jnp.stack([e[1] for e in new_even], axis=1)
    new_m = jnp.stack([e[2] for e in new_even], axis=1)
    new_ckv = jnp.stack([e[3] for e in new_even], axis=1)
    new_krope = jnp.stack([e[4] for e in new_even], axis=1)
    new_k = jnp.stack([e[0] for e in new_odd], axis=1)
    new_v = jnp.stack([e[1] for e in new_odd], axis=1)
    return (y_prompt, y_sample, new_c, new_n, new_m, new_ckv, new_krope, new_k, new_v)
```

```python
import functools
import itertools

import jax
import jax.numpy as jnp
from jax import lax
from jax.experimental import pallas as pl
from jax.experimental.pallas import tpu as pltpu

BF = jnp.bfloat16
F32 = jnp.float32

D_MODEL = 2048
GRID_W = 64
ROPE_THETA = 10000.0
NORM_EPS = 1e-6
ML_HEADS = 8
ML_QK = 64
ML_V = 128
ML_CHUNK = 64
MLA_HEADS = 8
MLA_Q_RANK = 512
MLA_KV_RANK = 512
MLA_NOPE = 128
MLA_ROPE = 64
MLA_V = 128
GQA_HEADS = 16
GQA_KV_HEADS = 4
GQA_HEAD_DIM = 128
N_EXPERTS = 16
EXPERT_DIM = 1024
EC_FACTOR = 2

QKVO_COLS = 2 * ML_HEADS * ML_QK + 2 * ML_HEADS * ML_V
SIDE_COLS = MLA_Q_RANK + MLA_KV_RANK + 128
GATE_LANE0 = MLA_ROPE

VMEM_LIMIT_BYTES = 56 * 1024 * 1024
OUT_PROJ_ROWS = 512
MLSTM_SEQS = 4


def _params(n_axes):
    return pltpu.CompilerParams(dimension_semantics=("arbitrary",) * n_axes,
                                vmem_limit_bytes=VMEM_LIMIT_BYTES)


def _bdot(a, b):
    return jnp.dot(a.astype(BF), b.astype(BF), preferred_element_type=F32)


def _bdot_nt(a, b):
    return lax.dot_general(a.astype(BF), b.astype(BF), (((1,), (1,)), ((), ())),
                           preferred_element_type=F32)


def _bdot_tn(a, b):
    return lax.dot_general(a.astype(BF), b.astype(BF), (((0,), (0,)), ((), ())),
                           preferred_element_type=F32)


def _sigmoid(x):
    return 1.0 / (1.0 + jnp.exp(-x))


def _log_sigmoid(x):
    return jnp.minimum(x, 0.0) - jnp.log1p(jnp.exp(-jnp.abs(x)))


def _rms(x):
    return x * lax.rsqrt(jnp.mean(x * x, axis=-1, keepdims=True) + NORM_EPS)


def _rope(x, cos_t, sin_t, quarter):
    width = x.shape[-1]
    axis = x.ndim - 1
    lane = lax.broadcasted_iota(jnp.int32, x.shape, axis)
    partner = jnp.where((lane & quarter) == 0,
                        pltpu.roll(x, width - quarter, axis=axis),
                        pltpu.roll(x, quarter, axis=axis))
    return x * cos_t + partner * sin_t


def _mod_kernel(c_ref, w_ref, b_ref, o_ref):
    c = c_ref[...]
    o_ref[...] = _bdot(c * _sigmoid(c), w_ref[...]) + b_ref[...]


def _mod_vectors(c8, w_mod, b_mod):
    n_layers, k, n = w_mod.shape
    tn = 1024
    return pl.pallas_call(
        _mod_kernel,
        grid=(n_layers, n // tn),
        in_specs=[pl.BlockSpec((8, k), lambda l, j: (0, 0)),
                  pl.BlockSpec((None, k, tn), lambda l, j: (l, 0, j)),
                  pl.BlockSpec((None, 1, tn), lambda l, j: (l, 0, j))],
        out_specs=pl.BlockSpec((None, 8, tn), lambda l, j: (l, 0, j)),
        out_shape=jax.ShapeDtypeStruct((n_layers, 8, n), F32),
        compiler_params=_params(2),
        name="mod_vectors",
    )(c8, w_mod, b_mod.reshape(n_layers, 1, n))


def _norm_mod_kernel(x_ref, sh_ref, sc_ref, h_ref):
    h_ref[...] = (_rms(x_ref[...]) * (1.0 + sc_ref[...]) + sh_ref[...]).astype(BF)


def _batch_of_tile(n_vectors, tm, seq):
    if n_vectors == 1:
        return lambda i: 0
    assert seq % tm == 0
    return lambda i: (i * tm) // seq


def _norm_mod(x, shift, scale, seq):
    t, d = x.shape
    tm = OUT_PROJ_ROWS
    which = _batch_of_tile(shift.shape[0], tm, seq)
    vec = pl.BlockSpec((None, 1, d), lambda i: (which(i), 0, 0))
    x_spec = pl.BlockSpec((tm, d), lambda i: (i, 0))
    return pl.pallas_call(
        _norm_mod_kernel, grid=(t // tm,),
        in_specs=[x_spec, vec, vec], out_specs=x_spec,
        out_shape=jax.ShapeDtypeStruct((t, d), BF),
        compiler_params=_params(1), name="norm_mod",
    )(x, shift, scale)


def _to_bf16_kernel(w_ref, o_ref):
    o_ref[...] = w_ref[...].astype(BF)


def _to_bf16(w3, layer):
    _, k, n = w3.shape
    tk = 512
    return pl.pallas_call(
        _to_bf16_kernel, grid=(k // tk,),
        in_specs=[pl.BlockSpec((None, tk, n), lambda i: (layer, i, 0))],
        out_specs=pl.BlockSpec((tk, n), lambda i: (i, 0)),
        out_shape=jax.ShapeDtypeStruct((k, n), BF),
        compiler_params=_params(1), name="weight_to_bf16",
    )(w3)


def _out_proj_router_kernel(*refs, k_sizes):
    n_a = len(k_sizes)
    a_refs = refs[:n_a]
    w_ref, x_ref, g_ref, sh_ref, sc_ref, wr_ref, xo_ref, h_ref, aff_ref = refs[n_a:]
    w_router = wr_ref[...].astype(BF)
    chunk = OUT_PROJ_ROWS // 2
    for r in range(0, x_ref.shape[0], chunk):
        rows = slice(r, r + chunk)
        acc, off = None, 0
        for a_ref, ks in zip(a_refs, k_sizes):
            term = jnp.dot(a_ref[rows, :], w_ref[off:off + ks, :], preferred_element_type=F32)
            acc = term if acc is None else acc + term
            off += ks
        x_new = x_ref[rows, :] + g_ref[...] * acc
        xo_ref[rows, :] = x_new
        h = (_rms(x_new) * (1.0 + sc_ref[...]) + sh_ref[...]).astype(BF)
        h_ref[rows, :] = h
        logits = jnp.dot(h, w_router, preferred_element_type=F32)
        e = jnp.exp(logits - jnp.max(logits, axis=-1, keepdims=True))
        aff_ref[rows, :] = e / jnp.sum(e, axis=-1, keepdims=True)


def _out_proj_router(a_list, w_bf, x, gate, shift, scale, w_router, layer, seq):
    m, d = x.shape
    k_sizes = tuple(a.shape[1] for a in a_list)
    assert w_bf.shape == (sum(k_sizes), d) and all(a.dtype == BF for a in a_list)
    tm = OUT_PROJ_ROWS
    ne = w_router.shape[-1]
    which = _batch_of_tile(gate.shape[0], tm, seq)
    vec = pl.BlockSpec((None, 1, d), lambda i: (which(i), 0, 0))
    tile = pl.BlockSpec((tm, d), lambda i: (i, 0))
    in_specs = [pl.BlockSpec((tm, ks), lambda i: (i, 0)) for ks in k_sizes]
    in_specs += [pl.BlockSpec(w_bf.shape, lambda i: (0, 0), pipeline_mode=pl.Buffered(1)),
                 tile, vec, vec, vec, pl.BlockSpec((None, d, ne), lambda i: (layer, 0, 0))]
    return pl.pallas_call(
        functools.partial(_out_proj_router_kernel, k_sizes=k_sizes),
        grid=(m // tm,), in_specs=in_specs,
        out_specs=[tile, tile, pl.BlockSpec((tm, ne), lambda i: (i, 0))],
        out_shape=[jax.ShapeDtypeStruct((m, d), F32), jax.ShapeDtypeStruct((m, d), BF),
                   jax.ShapeDtypeStruct((m, ne), F32)],
        compiler_params=_params(1), name="out_proj_router",
    )(*a_list, w_bf, x, gate, shift, scale, w_router)


def _mm_resident_kernel(a_ref, w_ref, o_ref, *, w_rows, chunk):
    w = w_ref[...].astype(BF)
    for r in range(0, a_ref.shape[0], chunk):
        rows = slice(r, r + chunk)
        a = a_ref[rows, :].astype(BF)
        if w_rows:
            o_ref[rows, :] = lax.dot_general(a, w, (((1,), (1,)), ((), ())), preferred_element_type=F32)
        else:
            o_ref[rows, :] = jnp.dot(a, w, preferred_element_type=F32)


def _mm_resident(a, w3, layer, n_cols, tn, w_rows=False, name="mm_resident"):
    m, k = a.shape
    assert w3.shape[2 if w_rows else 1] == k and n_cols % tn == 0
    if w_rows:
        w_spec = pl.BlockSpec((None, tn, k), lambda j: (layer, j, 0))
    else:
        w_spec = pl.BlockSpec((None, k, tn), lambda j: (layer, 0, j))
    return pl.pallas_call(
        functools.partial(_mm_resident_kernel, w_rows=w_rows, chunk=min(512, m)),
        grid=(n_cols // tn,),
        in_specs=[pl.BlockSpec((m, k), lambda j: (0, 0), pipeline_mode=pl.Buffered(1)), w_spec],
        out_specs=pl.BlockSpec((m, tn), lambda j: (0, j)),
        out_shape=jax.ShapeDtypeStruct((m, n_cols), F32),
        compiler_params=_params(1), name=name,
    )(a, w3)


def _norm_mm_kernel(x_ref, g_ref, w_ref, *out_refs, with_normed):
    wbf_ref = out_refs[-1]

    @pl.when(pl.program_id(0) == 0)
    def _():
        wbf_ref[...] = w_ref[...].astype(BF)

    xn = _rms(x_ref[...]) * g_ref[...]
    out_refs[0][...] = jnp.dot(xn.astype(BF), wbf_ref[...], preferred_element_type=F32)
    if with_normed:
        out_refs[1][...] = xn


def _norm_mm(x, col_blk, gain, w, with_normed, name):
    t = x.shape[0]
    k, n = w.shape
    tm = 512
    out_specs = [pl.BlockSpec((tm, n), lambda i: (i, 0))]
    out_shape = [jax.ShapeDtypeStruct((t, n), F32)]
    if with_normed:
        out_specs.append(pl.BlockSpec((tm, k), lambda i: (i, 0)))
        out_shape.append(jax.ShapeDtypeStruct((t, k), F32))
    outs = pl.pallas_call(
        functools.partial(_norm_mm_kernel, with_normed=with_normed),
        grid=(t // tm,),
        in_specs=[pl.BlockSpec((tm, k), lambda i: (i, col_blk)),
                  pl.BlockSpec((1, k), lambda i: (0, 0)),
                  pl.BlockSpec((k, n), lambda i: (0, 0))],
        out_specs=out_specs, out_shape=out_shape,
        scratch_shapes=[pltpu.VMEM((k, n), BF)],
        compiler_params=_params(1), name=name,
    )(x, gain.reshape(1, k), w)
    return outs if with_normed else outs[0]


def _split3(x):
    hi = x.astype(BF)
    rest = x - hi.astype(F32)
    mid = rest.astype(BF)
    return hi, mid, (rest - mid.astype(F32)).astype(BF)


def _scan_max(x, reverse):
    n = x.shape[0]
    row = lax.broadcasted_iota(jnp.int32, x.shape, 0)
    k = 1
    while k < n:
        if reverse:
            shifted = jnp.where(row < n - k, pltpu.roll(x, n - k, axis=0), -jnp.inf)
        else:
            shifted = jnp.where(row >= k, pltpu.roll(x, k, axis=0), -jnp.inf)
        x = jnp.maximum(x, shifted)
        k *= 2
    return x


def _mlstm_kernel(qf_ref, kf_ref, vf_ref, qb_ref, kb_ref, vb_ref, gcf_ref, gcb_ref, grf_ref, grb_ref,
                  bc_ref, br_ref, c0_ref, n0_ref, m0_ref, hf_ref, hb_ref, c_ref, n_ref, m_ref):
    @pl.when(pl.program_id(1) == 0)
    def _():
        c_ref[...] = c0_ref[...]
        n_ref[...] = n0_ref[...]
        m_ref[...] = m0_ref[...]

    nh, lc = ML_HEADS, ML_CHUNK
    row = lax.broadcasted_iota(jnp.int32, (lc, lc), 0)
    col = lax.broadcasted_iota(jnp.int32, (lc, lc), 1)
    lower = col <= row
    upper = col >= row
    ones_v = jnp.ones((lc, ML_V), BF)
    directions = ((qf_ref, kf_ref, vf_ref, gcf_ref, grf_ref, hf_ref, lower, upper),
                  (qb_ref, kb_ref, vb_ref, gcb_ref, grb_ref, hb_ref, upper, lower))
    seqs = range(c_ref.shape[0])
    gate_terms = {}
    for r, (d, (_, _, _, gc_ref, gr_ref, _, allowed, allowed_t)) in itertools.product(seqs, enumerate(directions)):
        g_col = gc_ref[r, :, GATE_LANE0:GATE_LANE0 + 4 * nh] + bc_ref[...]
        g_row = gr_ref[r] + br_ref[...]
        i_col = g_col[:, nh * d:nh * (d + 1)]
        f_col = _log_sigmoid(g_col[:, 2 * nh + nh * d:2 * nh + nh * (d + 1)])
        i_row = g_row[nh * d:nh * (d + 1), :]
        f_row = _log_sigmoid(g_row[2 * nh + nh * d:2 * nh + nh * (d + 1), :])
        tri = jnp.where(allowed, 1.0, 0.0).astype(BF)
        tri_t = jnp.where(allowed_t, 1.0, 0.0).astype(BF)
        b_col = sum(jnp.dot(tri, part, preferred_element_type=F32) for part in _split3(f_col))
        b_row = sum(jnp.dot(part, tri_t, preferred_element_type=F32) for part in _split3(f_row))
        b_end = jnp.sum(f_col, axis=0, keepdims=True)
        m_prev = m_ref[r, :, nh * d:nh * (d + 1)]
        m_t = b_col + jnp.maximum(m_prev, _scan_max(i_col - b_col, reverse=d == 1))
        g_col_end = b_end - b_col + i_col
        m_new = jnp.maximum(b_end + m_prev, jnp.max(g_col_end, axis=0, keepdims=True))
        gate_terms[r, d] = dict(
            u=b_col - m_t, r_row=i_row - b_row, w_inter=jnp.exp(b_col + m_prev - m_t), floor=jnp.exp(-m_t),
            k_scale=jnp.exp(g_col_end - m_new), decay=jnp.exp(b_end + m_prev - m_new), m_new=m_new)
    work = []
    for r, (d, (q_ref, k_ref, v_ref, _, _, h_ref, allowed, _)) in itertools.product(seqs, enumerate(directions)):
        for h in range(nh):
            w = dict(r=r, d=d, h=h, h_ref=h_ref, allowed=allowed, g=gate_terms[r, d])
            w["q"] = (q_ref[r, :, h * ML_QK:(h + 1) * ML_QK] * (ML_QK ** -0.5)).astype(BF)
            w["k"] = k_ref[r, :, h * ML_QK:(h + 1) * ML_QK]
            w["v1"] = jnp.concatenate([v_ref[r, :, h * ML_V:(h + 1) * ML_V].astype(BF), ones_v], axis=1)
            work.append(w)
    for w in work:
        w["qk"] = _bdot_nt(w["q"], w["k"])
    for w in work:
        r, d, h = w["r"], w["d"], w["h"]
        w["c_prev"], w["n_prev"] = c_ref[r, d, h], n_ref[r, d, h]
        state = jnp.concatenate([w["c_prev"], w["n_prev"]], axis=1).astype(BF)
        w["q_state"] = jnp.dot(w["q"], state, preferred_element_type=F32)
    for w in work:
        h, g = w["h"], w["g"]
        kw = w["k"] * g["k_scale"][:, h:h + 1]
        w["kv"] = _bdot_tn(kw, w["v1"])
    for w in work:
        h, g = w["h"], w["g"]
        dmat = jnp.where(w["allowed"], g["u"][:, h:h + 1] + g["r_row"][h:h + 1, :], -jnp.inf)
        sw = w["qk"] * jnp.exp(dmat)
        sw_hi = sw.astype(BF)
        sw_lo = (sw - sw_hi.astype(F32)).astype(BF)
        w["pv"] = jnp.dot(sw_hi, w["v1"], preferred_element_type=F32)
        w["den_lo"] = jnp.dot(sw_lo, ones_v, preferred_element_type=F32)
    for w in work:
        r, d, h, g = w["r"], w["d"], w["h"], w["g"]
        w_inter = g["w_inter"][:, h:h + 1]
        num = w_inter * w["q_state"][:, :ML_V] + w["pv"][:, :ML_V]
        den = w_inter * w["q_state"][:, ML_V:] + (w["pv"][:, ML_V:] + w["den_lo"])
        w["h_ref"][r, :, h * ML_V:(h + 1) * ML_V] = num / jnp.maximum(jnp.abs(den), g["floor"][:, h:h + 1])
        decay = g["decay"][:, h:h + 1]
        c_ref[r, d, h] = decay * w["c_prev"] + w["kv"][:, :ML_V]
        n_ref[r, d, h] = decay * w["n_prev"] + w["kv"][:, ML_V:]
    for r in seqs:
        m_ref[r, :, :nh] = gate_terms[r, 0]["m_new"]
        m_ref[r, :, nh:] = gate_terms[r, 1]["m_new"]


def _mlstm(qkvo, side, g_row, b_col, b_row, c0, n0, m0, bsz, seq):
    t = qkvo.shape[0]
    nc = seq // ML_CHUNK
    lc = ML_CHUNK
    nq = ML_HEADS * ML_QK
    nv = ML_HEADS * ML_V
    side_blk = (SIDE_COLS - 128) // 128
    rb = min(MLSTM_SEQS, bsz)
    assert bsz % rb == 0
    fwd = lambda c: c
    bwd = lambda c: nc - 1 - c

    def specs(pos):
        return [pl.BlockSpec((rb, lc, nq), lambda b, c: (b, pos(c), 0)),
                pl.BlockSpec((rb, lc, nq), lambda b, c: (b, pos(c), 1)),
                pl.BlockSpec((rb, lc, nv), lambda b, c: (b, pos(c), 1))]

    state = lambda *shape: pl.BlockSpec((rb,) + shape, lambda b, c: (b,) + (0,) * len(shape))
    in_specs = specs(fwd) + specs(bwd) + [
        pl.BlockSpec((rb, lc, 128), lambda b, c: (b, fwd(c), side_blk)),
        pl.BlockSpec((rb, lc, 128), lambda b, c: (b, bwd(c), side_blk)),
        pl.BlockSpec((rb, None, 4 * ML_HEADS, lc), lambda b, c: (b, fwd(c), 0, 0)),
        pl.BlockSpec((rb, None, 4 * ML_HEADS, lc), lambda b, c: (b, bwd(c), 0, 0)),
        pl.BlockSpec((1, 4 * ML_HEADS), lambda b, c: (0, 0)),
        pl.BlockSpec((4 * ML_HEADS, 1), lambda b, c: (0, 0)),
        state(2, ML_HEADS, ML_QK, ML_V), state(2, ML_HEADS, ML_QK, ML_V), state(1, 2 * ML_HEADS)]
    out_specs = [pl.BlockSpec((rb, lc, nv), lambda b, c: (b, fwd(c), 0)),
                 pl.BlockSpec((rb, lc, nv), lambda b, c: (b, bwd(c), 0)),
                 state(2, ML_HEADS, ML_QK, ML_V), state(2, ML_HEADS, ML_QK, ML_V), state(1, 2 * ML_HEADS)]
    out_shape = [jax.ShapeDtypeStruct((bsz, seq, nv), F32), jax.ShapeDtypeStruct((bsz, seq, nv), F32),
                 jax.ShapeDtypeStruct((bsz, 2, ML_HEADS, ML_QK, ML_V), F32),
                 jax.ShapeDtypeStruct((bsz, 2, ML_HEADS, ML_QK, ML_V), F32),
                 jax.ShapeDtypeStruct((bsz, 1, 2 * ML_HEADS), F32)]
    qkvo3, side3 = qkvo.reshape(bsz, seq, -1), side.reshape(bsz, seq, -1)
    g_row4 = g_row.reshape(bsz, nc, 4 * ML_HEADS, lc)
    hf, hb, c_fin, n_fin, m_fin = pl.pallas_call(
        _mlstm_kernel, grid=(bsz // rb, nc), in_specs=in_specs, out_specs=out_specs, out_shape=out_shape,
        compiler_params=_params(2), name="mlstm",
    )(qkvo3, qkvo3, qkvo3, qkvo3, qkvo3, qkvo3, side3, side3, g_row4, g_row4, b_col, b_row, c0, n0, m0)
    return hf.reshape(t, nv), hb.reshape(t, nv), c_fin, n_fin, m_fin


def _mlstm_post_kernel(hf_ref, hb_ref, o_ref, g_ref, y_ref):
    for h in range(ML_HEADS):
        sl = slice(h * ML_V, (h + 1) * ML_V)
        hn = _rms(hf_ref[:, sl] + hb_ref[:, sl]) * g_ref[:, sl]
        y_ref[:, sl] = (hn * _sigmoid(o_ref[:, sl])).astype(BF)


def _mlstm_post(hf, hb, qkvo, gain):
    t, nv = hf.shape
    tm = 512
    blk = pl.BlockSpec((tm, nv), lambda i: (i, 0))
    return pl.pallas_call(
        _mlstm_post_kernel, grid=(t // tm,),
        in_specs=[blk, blk, pl.BlockSpec((tm, nv), lambda i: (i, 2)), pl.BlockSpec((1, nv), lambda i: (0, 0))],
        out_specs=blk, out_shape=jax.ShapeDtypeStruct((t, nv), BF),
        compiler_params=_params(1), name="mlstm_post",
    )(hf, hb, qkvo, gain.reshape(1, nv))


LOG2_E = 1.4426950408889634


def _softmax_terms(scores, scale):
    c = scale * LOG2_E
    scaled = [s * c for s in scores]
    m = functools.reduce(jnp.maximum, [jnp.max(s, axis=-1, keepdims=True) for s in scaled])
    e = [jnp.exp2(s - m) for s in scaled]
    den = functools.reduce(lambda a, b: a + b, [jnp.sum(x, axis=-1, keepdims=True) for x in e])
    return [x.astype(BF) for x in e], 1.0 / den


def _mla_kernel(*refs, with_ctx):
    if with_ctx:
        qa_ref, kv_ref, side_ref, cq_ref, sq_ref, ck_ref, sk_ref, kvc_ref, krc_ref, o_ref = refs
    else:
        qa_ref, kv_ref, side_ref, o_ref = refs
    scale = (MLA_NOPE + MLA_ROPE) ** -0.5
    nope_cols = MLA_HEADS * MLA_NOPE
    q_rope = qa_ref[:, nope_cols:]
    k_rope = side_ref[...]
    if with_ctx:
        q_rope_rot = _rope(q_rope, cq_ref[...], sq_ref[...], MLA_ROPE // 4).astype(BF)
        k_rope_rot = _rope(k_rope, ck_ref[...], sk_ref[...], MLA_ROPE // 4)[:, :MLA_ROPE].astype(BF)
        k_rope_ctx = krc_ref[...].astype(BF)
        q_rope = q_rope.astype(BF)
    else:
        q_rope_rot = q_rope.astype(BF)
        k_rope_rot = k_rope[:, :MLA_ROPE].astype(BF)
    heads = range(MLA_HEADS)
    rope_cols = [slice(h * MLA_ROPE, (h + 1) * MLA_ROPE) for h in heads]
    kv0 = [h * (MLA_NOPE + MLA_V) for h in heads]
    q_n = [qa_ref[:, h * MLA_NOPE:(h + 1) * MLA_NOPE].astype(BF) for h in heads]
    scores = [[_bdot_nt(q_n[h], kv_ref[:, kv0[h]:kv0[h] + MLA_NOPE])
               + _bdot_nt(q_rope_rot[:, rope_cols[h]], k_rope_rot)] for h in heads]
    if with_ctx:
        for h in heads:
            scores[h].append(_bdot_nt(q_n[h], kvc_ref[:, kv0[h]:kv0[h] + MLA_NOPE])
                             + _bdot_nt(q_rope[:, rope_cols[h]], k_rope_ctx))
    weights = [_softmax_terms(scores[h], scale) for h in heads]
    for h in heads:
        (e, inv_den), v0 = weights[h], kv0[h] + MLA_NOPE
        out = jnp.dot(e[0], kv_ref[:, v0:v0 + MLA_V].astype(BF), preferred_element_type=F32)
        if with_ctx:
            out = out + jnp.dot(e[1], kvc_ref[:, v0:v0 + MLA_V].astype(BF), preferred_element_type=F32)
        o_ref[:, h * MLA_V:(h + 1) * MLA_V] = (out * inv_den).astype(BF)


def _mla_attention(qa, kv, side, bsz, seq, ctx=None):
    t = qa.shape[0]
    tq = 256
    nq = seq // tq
    side_blk = (SIDE_COLS - 128) // 128
    nkv = MLA_HEADS * (MLA_NOPE + MLA_V)
    in_specs = [pl.BlockSpec((tq, qa.shape[1]), lambda b, i: (b * nq + i, 0)),
                pl.BlockSpec((seq, nkv), lambda b, i: (b, 0)),
                pl.BlockSpec((seq, 128), lambda b, i: (b, side_blk))]
    args = [qa, kv, side]
    if ctx is not None:
        cos_q, sin_q, cos_k, sin_k, kvc, krc = ctx
        past = krc.shape[1]
        in_specs += [pl.BlockSpec((tq, cos_q.shape[1]), lambda b, i: (i, 0)),
                     pl.BlockSpec((tq, cos_q.shape[1]), lambda b, i: (i, 0)),
                     pl.BlockSpec((seq, 128), lambda b, i: (0, 0)),
                     pl.BlockSpec((seq, 128), lambda b, i: (0, 0)),
                     pl.BlockSpec((past, nkv), lambda b, i: (b, 0)),
                     pl.BlockSpec((None, past, MLA_ROPE), lambda b, i: (b, 0, 0))]
        args += [cos_q, sin_q, cos_k, sin_k, kvc, krc]
    nout = MLA_HEADS * MLA_V
    return pl.pallas_call(
        functools.partial(_mla_kernel, with_ctx=ctx is not None),
        grid=(bsz, nq), in_specs=in_specs,
        out_specs=pl.BlockSpec((tq, nout), lambda b, i: (b * nq + i, 0)),
        out_shape=jax.ShapeDtypeStruct((t, nout), BF),
        compiler_params=_params(2), name="mla_attention",
    )(*args)


def _gqa_kernel(*refs, with_ctx):
    if with_ctx:
        q_ref, k_ref, v_ref, gq_ref, gk_ref, cq_ref, sq_ref, ck_ref, sk_ref, kc_ref, vc_ref, o_ref, ksrc_ref = refs
    else:
        q_ref, k_ref, v_ref, gq_ref, gk_ref, o_ref, kn_ref, vo_ref, ksrc_ref = refs
    hd = GQA_HEAD_DIM
    scale = hd ** -0.5

    @pl.when(pl.program_id(2) == 0)
    def _():
        k_n = _rms(k_ref[...]) * gk_ref[...]
        if with_ctx:
            ksrc_ref[...] = _rope(k_n, ck_ref[...], sk_ref[...], hd // 4).astype(BF)
        else:
            ksrc_ref[...] = k_n.astype(BF)
            kn_ref[...] = k_n
            vo_ref[...] = v_ref[...]

    k_src = ksrc_ref[...]
    v = v_ref[...].astype(BF)
    if with_ctx:
        k_ctx, v_ctx = kc_ref[...].astype(BF), vc_ref[...].astype(BF)
    heads = range(GQA_HEADS // GQA_KV_HEADS)
    cols = [slice(r * hd, (r + 1) * hd) for r in heads]
    q_n = [_rms(q_ref[:, cols[r]]) * gq_ref[...] for r in heads]
    if with_ctx:
        scores = [[_bdot_nt(_rope(q_n[r], cq_ref[...], sq_ref[...], hd // 4), k_src), _bdot_nt(q_n[r], k_ctx)]
                  for r in heads]
    else:
        scores = [[_bdot_nt(q_n[r], k_src)] for r in heads]
    weights = [_softmax_terms(scores[r], scale) for r in heads]
    for r in heads:
        e, inv_den = weights[r]
        out = jnp.dot(e[0], v, preferred_element_type=F32)
        if with_ctx:
            out = out + jnp.dot(e[1], v_ctx, preferred_element_type=F32)
        o_ref[:, cols[r]] = (out * inv_den).astype(BF)


def _gqa_attention(qkv, g_q, g_k, bsz, seq, ctx=None):
    t = qkv.shape[0]
    hd = GQA_HEAD_DIM
    rep = GQA_HEADS // GQA_KV_HEADS
    tq = 256
    nq = seq // tq
    in_specs = [pl.BlockSpec((tq, rep * hd), lambda b, g, i: (b * nq + i, g)),
                pl.BlockSpec((seq, hd), lambda b, g, i: (b, GQA_HEADS + g)),
                pl.BlockSpec((seq, hd), lambda b, g, i: (b, GQA_HEADS + GQA_KV_HEADS + g)),
                pl.BlockSpec((1, hd), lambda b, g, i: (0, 0)),
                pl.BlockSpec((1, hd), lambda b, g, i: (0, 0))]
    args = [qkv, qkv, qkv, g_q.reshape(1, hd), g_k.reshape(1, hd)]
    o_spec = pl.BlockSpec((tq, rep * hd), lambda b, g, i: (b * nq + i, g))
    o_shape = jax.ShapeDtypeStruct((t, GQA_HEADS * hd), BF)
    if ctx is not None:
        cos_t, sin_t, kc, vc, j = ctx
        past = kc.shape[3]
        cache = pl.BlockSpec((None, None, None, past, hd), lambda b, g, i: (b, j, g, 0, 0))
        in_specs += [pl.BlockSpec((tq, hd), lambda b, g, i: (i, 0)),
                     pl.BlockSpec((tq, hd), lambda b, g, i: (i, 0)),
                     pl.BlockSpec((seq, hd), lambda b, g, i: (0, 0)),
                     pl.BlockSpec((seq, hd), lambda b, g, i: (0, 0)),
                     cache, cache]
        args += [cos_t, sin_t, cos_t, sin_t, kc, vc]
        out_specs, out_shape = o_spec, o_shape
    else:
        head_major = pl.BlockSpec((None, None, seq, hd), lambda b, g, i: (b, g, 0, 0))
        out_specs = [o_spec, head_major, head_major]
        kv_shape = jax.ShapeDtypeStruct((bsz, GQA_KV_HEADS, seq, hd), F32)
        out_shape = [o_shape, kv_shape, kv_shape]
    return pl.pallas_call(
        functools.partial(_gqa_kernel, with_ctx=ctx is not None),
        grid=(bsz, GQA_KV_HEADS, nq), in_specs=in_specs, out_specs=out_specs, out_shape=out_shape,
        scratch_shapes=[pltpu.VMEM((seq, hd), BF)],
        compiler_params=_params(3), name="gqa_attention",
    )(*args)


RANK_BLOCK = 256
GATHER_ROWS = 512
EXPERT_ROWS = 256


def _dispatch_kernel(ar_ref, ac_ref, h_ref, xe_ref, gate_ref, rankc_ref, rankr_ref, *, cap):
    n_groups, group, s = ar_ref.shape
    g = pl.program_id(1)

    @pl.when(g == 0)
    def _():
        blk = min(RANK_BLOCK, s)
        rankr_ref[...] = jnp.zeros_like(rankr_ref)
        ones_sub = jnp.ones((8, blk), BF)
        ones_lane = jnp.ones((s, 128), BF)
        for j in range(s // blk):
            rows = pl.ds(j * blk, blk)
            i0 = lax.broadcasted_iota(jnp.int32, (blk, s), 0) + j * blk
            i1 = lax.broadcasted_iota(jnp.int32, (blk, s), 1)
            sub_first = jnp.where(i0 < i1, 1.0, 0.0)
            for e in range(n_groups * group):
                eg, ei = e // group, e % group
                a_row = ar_ref[eg, ei:ei + 1, :]
                a_col = ac_ref[rows, e:e + 1]
                ahead = jnp.where(a_col > a_row, 1.0, jnp.where(a_col >= a_row, sub_first, 0.0)).astype(BF)
                rankr_ref[eg, ei:ei + 1, :] += jnp.dot(ones_sub, ahead, preferred_element_type=F32)[:1, :]
                rankc_ref[rows, e:e + 1] = (s - 1.0) - jnp.dot(ahead, ones_lane, preferred_element_type=F32)[:, :1]

    slot = lax.broadcasted_iota(jnp.int32, (group, cap, s), 1).astype(F32)
    pick = rankr_ref[g][:, None, :] == slot
    onehot = jnp.where(pick, 1.0, 0.0).reshape(group * cap, s).astype(BF)
    rows_f32 = jnp.dot(onehot, h_ref[...], preferred_element_type=F32)
    xe_ref[...] = rows_f32.reshape(group, cap, -1).astype(BF)
    gate_ref[...] = jnp.sum(jnp.where(pick, ar_ref[g][:, None, :], 0.0), axis=2, keepdims=True)


def _dispatch(aff_row, aff_col, h, bsz, seq):
    t, d = h.shape
    ne = aff_row.shape[0]
    cap = EC_FACTOR * seq // ne
    group = GATHER_ROWS // cap
    n_groups = ne // group
    return pl.pallas_call(
        functools.partial(_dispatch_kernel, cap=cap),
        grid=(bsz, n_groups),
        in_specs=[pl.BlockSpec((n_groups, group, seq), lambda b, g: (0, 0, b)),
                  pl.BlockSpec((seq, ne), lambda b, g: (b, 0)),
                  pl.BlockSpec((seq, d), lambda b, g: (b, 0))],
        out_specs=[pl.BlockSpec((group, cap, d), lambda b, g: (g, b, 0)),
                   pl.BlockSpec((group, cap, 1), lambda b, g: (g, b, 0)),
                   pl.BlockSpec((seq, ne), lambda b, g: (b, 0))],
        out_shape=[jax.ShapeDtypeStruct((ne, bsz * cap, d), BF),
                   jax.ShapeDtypeStruct((ne, bsz * cap, 1), F32),
                   jax.ShapeDtypeStruct((t, ne), F32)],
        scratch_shapes=[pltpu.VMEM((n_groups, group, seq), F32)],
        compiler_params=_params(2), name="ec_dispatch",
    )(aff_row.reshape(n_groups, group, t), aff_col, h)


def _experts_kernel(xp_ref, xs_ref, gp_ref, gs_ref, wg_ref, wu_ref, wd_ref, yp_ref, ys_ref, accp_ref, accs_ref):
    f = pl.program_id(1)
    tf = wg_ref.shape[1]
    w_gate_up = jnp.concatenate([wg_ref[...].astype(BF), wu_ref[...].astype(BF)], axis=1)
    wd = wd_ref[...].astype(BF)
    groups = ((xp_ref, gp_ref, yp_ref, accp_ref), (xs_ref, gs_ref, ys_ref, accs_ref))

    @pl.when(f == 0)
    def _():
        for _, _, _, acc_ref in groups:
            acc_ref[...] = jnp.zeros_like(acc_ref)

    for x_ref, _, _, acc_ref in groups:
        for r in range(0, x_ref.shape[0], EXPERT_ROWS):
            rows = slice(r, r + EXPERT_ROWS)
            au = jnp.dot(x_ref[rows, :], w_gate_up, preferred_element_type=F32)
            a, u = au[:, :tf], au[:, tf:]
            acc_ref[rows, :] += jnp.dot((a * _sigmoid(a) * u).astype(BF), wd, preferred_element_type=F32)

    @pl.when(f == pl.num_programs(1) - 1)
    def _():
        for _, g_ref, y_ref, acc_ref in groups:
            y_ref[...] = (acc_ref[...] * g_ref[...]).astype(BF)


def _experts(xe_p, xe_s, gate_p, gate_s, w_gate, w_up, w_down, layer):
    ne, mp, d = xe_p.shape
    ms = xe_s.shape[1]
    fdim = w_gate.shape[-1]
    tf = 512
    rows = lambda m, last: pl.BlockSpec((None, m, last), lambda e, f: (e, 0, 0))
    return pl.pallas_call(
        _experts_kernel, grid=(ne, fdim // tf),
        in_specs=[rows(mp, d), rows(ms, d), rows(mp, 1), rows(ms, 1),
                  pl.BlockSpec((None, None, d, tf), lambda e, f: (layer, e, 0, f)),
                  pl.BlockSpec((None, None, d, tf), lambda e, f: (layer, e, 0, f)),
                  pl.BlockSpec((None, None, tf, d), lambda e, f: (layer, e, f, 0))],
        out_specs=[rows(mp, d), rows(ms, d)],
        out_shape=[jax.ShapeDtypeStruct((ne, mp, d), BF), jax.ShapeDtypeStruct((ne, ms, d), BF)],
        scratch_shapes=[pltpu.VMEM((mp, d), F32), pltpu.VMEM((ms, d), F32)],
        compiler_params=_params(2), name="ec_experts",
    )(xe_p, xe_s, gate_p, gate_s, w_gate, w_up, w_down)


RANK_RADIX = 32


COMBINE_ROWS = 256


def _combine_kernel(y_ref, rank_ref, x_ref, g_ref, *refs, cap, final):
    post_refs, place_ref = refs[:-1], refs[-1]
    ne = y_ref.shape[0]
    n_slots = ne * cap

    @pl.when(pl.program_id(1) == 0)
    def _():
        rank = rank_ref[...]
        hi = jnp.floor(rank * (1.0 / RANK_RADIX))
        lo = rank - RANK_RADIX * hi
        lane = lax.broadcasted_iota(jnp.int32, (ne, n_slots), 1)
        expert = lax.broadcasted_iota(jnp.int32, (ne, n_slots), 0)
        own = (lane >= expert * cap) & (lane < (expert + 1) * cap)
        spread_hi = jnp.where(own, float(RANK_RADIX), 0.0).astype(BF)
        spread_lo = jnp.where(own, 1.0, 0.0).astype(BF)
        spread = (jnp.dot(hi.astype(BF), spread_hi, preferred_element_type=F32)
                  + jnp.dot(lo.astype(BF), spread_lo, preferred_element_type=F32))
        slot = (lax.broadcasted_iota(jnp.int32, (1, n_slots), 1) & (cap - 1)).astype(F32)
        place_ref[...] = jnp.where(spread == slot, 1.0, 0.0).astype(BF)

    ts = x_ref.shape[0]
    rows = pl.ds(pl.multiple_of(pl.program_id(1) * ts, ts), ts)
    y = y_ref[...].reshape(n_slots, y_ref.shape[2])
    x_new = x_ref[...] + g_ref[...] * jnp.dot(place_ref[rows, :], y, preferred_element_type=F32)
    if final:
        (gain_ref,), (out_ref,) = post_refs[:1], post_refs[1:]
        out_ref[...] = _rms(x_new) * gain_ref[...]
    else:
        (sh_ref, sc_ref), (xo_ref, h_ref) = post_refs[:2], post_refs[2:]
        xo_ref[...] = x_new
        h_ref[...] = (_rms(x_new) * (1.0 + sc_ref[...]) + sh_ref[...]).astype(BF)


def _combine(y, rank, x, gate, bsz, seq, next_shift_scale=None, final_gain=None):
    t, d = x.shape
    ne = y.shape[0]
    cap = EC_FACTOR * seq // ne
    assert seq <= RANK_RADIX * RANK_RADIX and cap & (cap - 1) == 0
    ts = min(COMBINE_ROWS, seq)
    nt = seq // ts
    final = final_gain is not None
    vec = lambda v: pl.BlockSpec((None, 1, d), lambda b, i: (b if v.shape[0] > 1 else 0, 0, 0))
    tile = pl.BlockSpec((ts, d), lambda b, i: (b * nt + i, 0))
    in_specs = [pl.BlockSpec((ne, cap, d), lambda b, i: (0, b, 0)),
                pl.BlockSpec((seq, ne), lambda b, i: (b, 0)), tile, vec(gate)]
    if final:
        args = [final_gain.reshape(1, d)]
        in_specs.append(pl.BlockSpec((1, d), lambda b, i: (0, 0)))
        out_specs, out_shape = tile, jax.ShapeDtypeStruct((t, d), F32)
    else:
        args = list(next_shift_scale)
        in_specs += [vec(v) for v in args]
        out_specs = [tile, tile]
        out_shape = [jax.ShapeDtypeStruct((t, d), F32), jax.ShapeDtypeStruct((t, d), BF)]
    return pl.pallas_call(
        functools.partial(_combine_kernel, cap=cap, final=final),
        grid=(bsz, nt), in_specs=in_specs, out_specs=out_specs, out_shape=out_shape,
        scratch_shapes=[pltpu.VMEM((seq, ne * cap), BF)],
        compiler_params=_params(2), name="ec_combine",
    )(y, rank, x, gate, *args)


def _rope_tables(n_tokens, rot_dim):
    rows = n_tokens // GRID_W
    row = jnp.repeat(jnp.arange(rows), GRID_W).astype(F32)
    col = jnp.tile(jnp.arange(GRID_W), rows).astype(F32)
    quarter = rot_dim // 4
    inv = ROPE_THETA ** (-jnp.arange(quarter, dtype=F32) / quarter)
    a_row, a_col = row[:, None] * inv, col[:, None] * inv
    cos_t = jnp.concatenate([jnp.cos(a_row), jnp.cos(a_row), jnp.cos(a_col), jnp.cos(a_col)], axis=-1)
    sin_t = jnp.concatenate([-jnp.sin(a_row), jnp.sin(a_row), -jnp.sin(a_col), jnp.sin(a_col)], axis=-1)
    return cos_t, sin_t


def kernel(x_prompt, x_sample, state_mlstm_c, state_mlstm_n, state_mlstm_m, cache_mla_ckv, cache_mla_krope,
           cache_gqa_k, cache_gqa_v, c, c_ctx, w_mod, b_mod, w_in_even, b_igate, b_fgate, g_mlstm, g_cq, w_uq,
           g_ckv, w_ukv, w_out_even, w_in_odd, g_qnorm, g_knorm, w_out_odd, w_router, w_expert_gate,
           w_expert_up, w_expert_down, g_final):
    d = D_MODEL
    bp, sp, _ = x_prompt.shape
    bs, ss, _ = x_sample.shape
    depth = w_mod.shape[0]
    nh = ML_HEADS
    streams = {"p": (bp, sp), "s": (bs, ss)}
    x = {"p": x_prompt.reshape(bp * sp, d), "s": x_sample.reshape(bs * ss, d)}

    c8 = jnp.concatenate([c_ctx[None], c, jnp.zeros((8 - 1 - bs, d), F32)], axis=0)
    mod_all = _mod_vectors(c8, w_mod, b_mod).reshape(depth, 8, 6, 1, d)

    def mod(layer, key, idx):
        rows = mod_all[layer, 0:1, idx] if key == "p" else mod_all[layer, 1:1 + bs, idx]
        return rows

    new_even, new_odd = [], []
    h_in = {key: _norm_mod(x[key], mod(0, key, 0), mod(0, key, 1), streams[key][1]) for key in streams}
    routed_in = {}
    for layer in range(depth):
        j = layer // 2
        if layer % 2 == 0:
            w_out = _to_bf16(w_out_even, j)
            w_in_rows = jnp.swapaxes(w_in_even, 1, 2)
            w_side = jnp.concatenate(
                [w_in_rows[j, QKVO_COLS + 4 * nh:],
                 w_in_rows[j, QKVO_COLS:QKVO_COLS + 4 * nh],
                 jnp.zeros((128 - MLA_ROPE - 4 * nh, d), F32)], axis=0)[None]
            w_q = w_uq[j].reshape(MLA_Q_RANK, MLA_HEADS, MLA_NOPE + MLA_ROPE)
            w_q = jnp.concatenate([w_q[:, :, :MLA_NOPE].reshape(MLA_Q_RANK, -1),
                                   w_q[:, :, MLA_NOPE:].reshape(MLA_Q_RANK, -1)], axis=1)
            bias_col = jnp.concatenate([b_igate[j].reshape(1, -1), b_fgate[j].reshape(1, -1)], axis=1)
            bias_row = bias_col.reshape(-1, 1)
            cos64, sin64 = _rope_tables(ss, MLA_ROPE)
            cos_q, sin_q = jnp.tile(cos64, (1, MLA_HEADS)), jnp.tile(sin64, (1, MLA_HEADS))
            pad = jnp.zeros((ss, 128 - MLA_ROPE), F32)
            cos_k, sin_k = jnp.concatenate([cos64, pad], axis=1), jnp.concatenate([sin64, pad], axis=1)
            kvc = _mm_resident(cache_mla_ckv[:, j].reshape(-1, MLA_KV_RANK), w_ukv, j, w_ukv.shape[-1], 512,
                               name="mla_ctx_expand")
            for key, (bsz, seq) in streams.items():
                h = h_in[key]
                qkvo = _mm_resident(h, w_in_rows, j, QKVO_COLS, 512, w_rows=True, name="even_in_main")
                side = _mm_resident(h, w_side, 0, SIDE_COLS, SIDE_COLS // 3, w_rows=True, name="even_in_side")
                gates = side[:, SIDE_COLS - 128 + GATE_LANE0:SIDE_COLS - 128 + GATE_LANE0 + 4 * nh]
                g_row = gates.reshape(-1, ML_CHUNK, 4 * nh).transpose(0, 2, 1)
                if key == "p":
                    c0 = jnp.zeros((bsz, 2, nh, ML_QK, ML_V), F32)
                    n0 = jnp.zeros((bsz, 2, nh, ML_QK, ML_V), F32)
                    m0 = jnp.zeros((bsz, 1, 2 * nh), F32)
                else:
                    c0 = state_mlstm_c[:, j]
                    n0 = jnp.broadcast_to(state_mlstm_n[:, j][..., None], c0.shape)
                    m0 = state_mlstm_m[:, j].reshape(bsz, 1, 2 * nh)
                hf, hb, c_fin, n_fin, m_fin = _mlstm(qkvo, side, g_row, bias_col, bias_row, c0, n0, m0, bsz, seq)
                y_ml = _mlstm_post(hf, hb, qkvo, g_mlstm[j])
                qa = _norm_mm(side, 0, g_cq[j], w_q, False, "mla_q_up")
                kv, ckv_n = _norm_mm(side, 1, g_ckv[j], w_ukv[j], True, "mla_kv_up")
                if key == "p":
                    y_a = _mla_attention(qa, kv, side, bsz, seq)
                    k_rope = side[:, SIDE_COLS - 128:SIDE_COLS - 128 + MLA_ROPE]
                    new_even.append((c_fin, n_fin[..., 0], m_fin.reshape(bsz, 2, nh),
                                     ckv_n.reshape(bsz, seq, -1), k_rope.reshape(bsz, seq, -1)))
                else:
                    y_a = _mla_attention(qa, kv, side, bsz, seq,
                                         ctx=(cos_q, sin_q, cos_k, sin_k, kvc, cache_mla_krope[:, j]))
                routed_in[key] = _out_proj_router([y_ml, y_a], w_out, x[key], mod(layer, key, 2), mod(layer, key, 3),
                                                  mod(layer, key, 4), w_router, layer, seq)
        else:
            w_out = _to_bf16(w_out_odd, j)
            cos_t, sin_t = _rope_tables(ss, GQA_HEAD_DIM)
            for key, (bsz, seq) in streams.items():
                qkv = _mm_resident(h_in[key], w_in_odd, j, w_in_odd.shape[-1], 512, name="odd_in")
                if key == "p":
                    o, k_n, v = _gqa_attention(qkv, g_qnorm[j], g_knorm[j], bsz, seq)
                    new_odd.append((k_n, v))
                else:
                    o = _gqa_attention(qkv, g_qnorm[j], g_knorm[j], bsz, seq,
                                       ctx=(cos_t, sin_t, cache_gqa_k, cache_gqa_v, j))
                routed_in[key] = _out_proj_router([o], w_out, x[key], mod(layer, key, 2), mod(layer, key, 3),
                                                  mod(layer, key, 4), w_router, layer, seq)
        routed = {}
        for key, (bsz, seq) in streams.items():
            x[key], h, aff = routed_in[key]
            routed[key] = _dispatch(aff.T, aff, h, bsz, seq)
        y_p, y_s = _experts(routed["p"][0], routed["s"][0], routed["p"][1], routed["s"][1],
                            w_expert_gate, w_expert_up, w_expert_down, layer)
        for key, y in (("p", y_p), ("s", y_s)):
            bsz, seq = streams[key]
            if layer + 1 < depth:
                x[key], h_in[key] = _combine(
                    y, routed[key][2], x[key], mod(layer, key, 5), bsz, seq,
                    next_shift_scale=(mod(layer + 1, key, 0), mod(layer + 1, key, 1)))
            else:
                x[key] = _combine(y, routed[key][2], x[key], mod(layer, key, 5), bsz, seq, final_gain=g_final)

    y_prompt = x["p"].reshape(bp, sp, d)
    y_sample = x["s"].reshape(bs, ss, d)
    new_c = jnp.stack([e[0] for e in new_even], axis=1)
    new_n = jnp.stack([e[1] for e in new_even], axis=1)
    new_m = jnp.stack([e[2] for e in new_even], axis=1)
    new_ckv = jnp.stack([e[3] for e in new_even], axis=1)
    new_krope = jnp.stack([e[4] for e in new_even], axis=1)
    new_k = jnp.stack([e[0] for e in new_odd], axis=1)
    new_v = jnp.stack([e[1] for e in new_odd], axis=1)
    return (y_prompt, y_sample, new_c, new_n, new_m, new_ckv, new_krope, new_k, new_v)
```

```python
import functools
import itertools

import jax
import jax.numpy as jnp
from jax import lax
from jax.experimental import pallas as pl
from jax.experimental.pallas import tpu as pltpu

BF = jnp.bfloat16
F32 = jnp.float32

D_MODEL = 2048
GRID_W = 64
ROPE_THETA = 10000.0
NORM_EPS = 1e-6
ML_HEADS = 8
ML_QK = 64
ML_V = 128
ML_CHUNK = 64
MLA_HEADS = 8
MLA_Q_RANK = 512
MLA_KV_RANK = 512
MLA_NOPE = 128
MLA_ROPE = 64
MLA_V = 128
GQA_HEADS = 16
GQA_KV_HEADS = 4
GQA_HEAD_DIM = 128
N_EXPERTS = 16
EXPERT_DIM = 1024
EC_FACTOR = 2

QKVO_COLS = 2 * ML_HEADS * ML_QK + 2 * ML_HEADS * ML_V
SIDE_COLS = MLA_Q_RANK + MLA_KV_RANK + 128
GATE_LANE0 = MLA_ROPE

VMEM_LIMIT_BYTES = 56 * 1024 * 1024
OUT_PROJ_ROWS = 512
MLSTM_SEQS = 4
MOD_COLS = 1024
IN_PROJ_COLS = 512
IN_PROJ_ROW_CHUNK = 512
CAST_ROWS = 512
ROW_TILE = 512
ATTN_Q_ROWS = 256
EXPERT_HIDDEN_COLS = 512
EXPERT_ROWS = 256
RANK_BLOCK = 256
GATHER_ROWS = 512
COMBINE_ROWS = 256


def _params(n_axes):
    return pltpu.CompilerParams(dimension_semantics=("arbitrary",) * n_axes,
                                vmem_limit_bytes=VMEM_LIMIT_BYTES)


def _bdot(a, b):
    return jnp.dot(a.astype(BF), b.astype(BF), preferred_element_type=F32)


def _bdot_nt(a, b):
    return lax.dot_general(a.astype(BF), b.astype(BF), (((1,), (1,)), ((), ())),
                           preferred_element_type=F32)


def _bdot_tn(a, b):
    return lax.dot_general(a.astype(BF), b.astype(BF), (((0,), (0,)), ((), ())),
                           preferred_element_type=F32)


def _sigmoid(x):
    return 1.0 / (1.0 + jnp.exp(-x))


def _log_sigmoid(x):
    return jnp.minimum(x, 0.0) - jnp.log1p(jnp.exp(-jnp.abs(x)))


def _rms(x):
    return x * lax.rsqrt(jnp.mean(x * x, axis=-1, keepdims=True) + NORM_EPS)


def _rope(x, cos_t, sin_t, quarter):
    width = x.shape[-1]
    axis = x.ndim - 1
    lane = lax.broadcasted_iota(jnp.int32, x.shape, axis)
    partner = jnp.where((lane & quarter) == 0,
                        pltpu.roll(x, width - quarter, axis=axis),
                        pltpu.roll(x, quarter, axis=axis))
    return x * cos_t + partner * sin_t


def _mod_kernel(c_ref, w_ref, b_ref, o_ref):
    c = c_ref[...]
    o_ref[...] = _bdot(c * _sigmoid(c), w_ref[...]) + b_ref[...]


def _mod_vectors(c8, w_mod, b_mod):
    n_layers, k, n = w_mod.shape
    tn = MOD_COLS
    return pl.pallas_call(
        _mod_kernel,
        grid=(n_layers, n // tn),
        in_specs=[pl.BlockSpec((8, k), lambda l, j: (0, 0)),
                  pl.BlockSpec((None, k, tn), lambda l, j: (l, 0, j)),
                  pl.BlockSpec((None, 1, tn), lambda l, j: (l, 0, j))],
        out_specs=pl.BlockSpec((None, 8, tn), lambda l, j: (l, 0, j)),
        out_shape=jax.ShapeDtypeStruct((n_layers, 8, n), F32),
        compiler_params=_params(2),
        name="mod_vectors",
    )(c8, w_mod, b_mod.reshape(n_layers, 1, n))


def _norm_mod_kernel(x_ref, sh_ref, sc_ref, h_ref):
    h_ref[...] = (_rms(x_ref[...]) * (1.0 + sc_ref[...]) + sh_ref[...]).astype(BF)


def _batch_of_tile(n_vectors, tm, seq):
    if n_vectors == 1:
        return lambda i: 0
    assert seq % tm == 0
    return lambda i: (i * tm) // seq


def _norm_mod(x, shift, scale, seq):
    t, d = x.shape
    tm = OUT_PROJ_ROWS
    which = _batch_of_tile(shift.shape[0], tm, seq)
    vec = pl.BlockSpec((None, 1, d), lambda i: (which(i), 0, 0))
    x_spec = pl.BlockSpec((tm, d), lambda i: (i, 0))
    return pl.pallas_call(
        _norm_mod_kernel, grid=(t // tm,),
        in_specs=[x_spec, vec, vec], out_specs=x_spec,
        out_shape=jax.ShapeDtypeStruct((t, d), BF),
        compiler_params=_params(1), name="norm_mod",
    )(x, shift, scale)


def _to_bf16_kernel(w_ref, o_ref):
    o_ref[...] = w_ref[...].astype(BF)


def _to_bf16(w3, layer):
    _, k, n = w3.shape
    tk = CAST_ROWS
    return pl.pallas_call(
        _to_bf16_kernel, grid=(k // tk,),
        in_specs=[pl.BlockSpec((None, tk, n), lambda i: (layer, i, 0))],
        out_specs=pl.BlockSpec((tk, n), lambda i: (i, 0)),
        out_shape=jax.ShapeDtypeStruct((k, n), BF),
        compiler_params=_params(1), name="weight_to_bf16",
    )(w3)


def _out_proj_router_kernel(*refs, k_sizes):
    n_a = len(k_sizes)
    a_refs = refs[:n_a]
    w_ref, x_ref, g_ref, sh_ref, sc_ref, wr_ref, xo_ref, h_ref, aff_ref = refs[n_a:]
    w_router = wr_ref[...].astype(BF)
    chunk = OUT_PROJ_ROWS // 2
    for r in range(0, x_ref.shape[0], chunk):
        rows = slice(r, r + chunk)
        acc, off = None, 0
        for a_ref, ks in zip(a_refs, k_sizes):
            term = jnp.dot(a_ref[rows, :], w_ref[off:off + ks, :], preferred_element_type=F32)
            acc = term if acc is None else acc + term
            off += ks
        x_new = x_ref[rows, :] + g_ref[...] * acc
        xo_ref[rows, :] = x_new
        h = (_rms(x_new) * (1.0 + sc_ref[...]) + sh_ref[...]).astype(BF)
        h_ref[rows, :] = h
        logits = jnp.dot(h, w_router, preferred_element_type=F32)
        e = jnp.exp(logits - jnp.max(logits, axis=-1, keepdims=True))
        aff_ref[rows, :] = e / jnp.sum(e, axis=-1, keepdims=True)


def _out_proj_router(a_list, w_bf, x, gate, shift, scale, w_router, layer, seq):
    m, d = x.shape
    k_sizes = tuple(a.shape[1] for a in a_list)
    assert w_bf.shape == (sum(k_sizes), d) and all(a.dtype == BF for a in a_list)
    tm = OUT_PROJ_ROWS
    ne = w_router.shape[-1]
    which = _batch_of_tile(gate.shape[0], tm, seq)
    vec = pl.BlockSpec((None, 1, d), lambda i: (which(i), 0, 0))
    tile = pl.BlockSpec((tm, d), lambda i: (i, 0))
    in_specs = [pl.BlockSpec((tm, ks), lambda i: (i, 0)) for ks in k_sizes]
    in_specs += [pl.BlockSpec(w_bf.shape, lambda i: (0, 0), pipeline_mode=pl.Buffered(1)),
                 tile, vec, vec, vec, pl.BlockSpec((None, d, ne), lambda i: (layer, 0, 0))]
    return pl.pallas_call(
        functools.partial(_out_proj_router_kernel, k_sizes=k_sizes),
        grid=(m // tm,), in_specs=in_specs,
        out_specs=[tile, tile, pl.BlockSpec((tm, ne), lambda i: (i, 0))],
        out_shape=[jax.ShapeDtypeStruct((m, d), F32), jax.ShapeDtypeStruct((m, d), BF),
                   jax.ShapeDtypeStruct((m, ne), F32)],
        compiler_params=_params(1), name="out_proj_router",
    )(*a_list, w_bf, x, gate, shift, scale, w_router)


def _mm_resident_kernel(a_ref, w_ref, o_ref, *, w_rows, chunk):
    w = w_ref[...].astype(BF)
    for r in range(0, a_ref.shape[0], chunk):
        rows = slice(r, r + chunk)
        a = a_ref[rows, :].astype(BF)
        if w_rows:
            o_ref[rows, :] = lax.dot_general(a, w, (((1,), (1,)), ((), ())), preferred_element_type=F32)
        else:
            o_ref[rows, :] = jnp.dot(a, w, preferred_element_type=F32)


def _mm_resident(a, w3, layer, n_cols, tn, w_rows=False, name="mm_resident"):
    m, k = a.shape
    assert w3.shape[2 if w_rows else 1] == k and n_cols % tn == 0
    if w_rows:
        w_spec = pl.BlockSpec((None, tn, k), lambda j: (layer, j, 0))
    else:
        w_spec = pl.BlockSpec((None, k, tn), lambda j: (layer, 0, j))
    return pl.pallas_call(
        functools.partial(_mm_resident_kernel, w_rows=w_rows, chunk=min(IN_PROJ_ROW_CHUNK, m)),
        grid=(n_cols // tn,),
        in_specs=[pl.BlockSpec((m, k), lambda j: (0, 0), pipeline_mode=pl.Buffered(1)), w_spec],
        out_specs=pl.BlockSpec((m, tn), lambda j: (0, j)),
        out_shape=jax.ShapeDtypeStruct((m, n_cols), F32),
        compiler_params=_params(1), name=name,
    )(a, w3)


def _norm_mm_kernel(x_ref, g_ref, w_ref, *out_refs, with_normed):
    wbf_ref = out_refs[-1]

    @pl.when(pl.program_id(0) == 0)
    def _():
        wbf_ref[...] = w_ref[...].astype(BF)

    xn = _rms(x_ref[...]) * g_ref[...]
    out_refs[0][...] = jnp.dot(xn.astype(BF), wbf_ref[...], preferred_element_type=F32)
    if with_normed:
        out_refs[1][...] = xn


def _norm_mm(x, col_blk, gain, w, with_normed, name):
    t = x.shape[0]
    k, n = w.shape
    tm = ROW_TILE
    out_specs = [pl.BlockSpec((tm, n), lambda i: (i, 0))]
    out_shape = [jax.ShapeDtypeStruct((t, n), F32)]
    if with_normed:
        out_specs.append(pl.BlockSpec((tm, k), lambda i: (i, 0)))
        out_shape.append(jax.ShapeDtypeStruct((t, k), F32))
    outs = pl.pallas_call(
        functools.partial(_norm_mm_kernel, with_normed=with_normed),
        grid=(t // tm,),
        in_specs=[pl.BlockSpec((tm, k), lambda i: (i, col_blk)),
                  pl.BlockSpec((1, k), lambda i: (0, 0)),
                  pl.BlockSpec((k, n), lambda i: (0, 0))],
        out_specs=out_specs, out_shape=out_shape,
        scratch_shapes=[pltpu.VMEM((k, n), BF)],
        compiler_params=_params(1), name=name,
    )(x, gain.reshape(1, k), w)
    return outs if with_normed else outs[0]


def _split3(x):
    hi = x.astype(BF)
    rest = x - hi.astype(F32)
    mid = rest.astype(BF)
    return hi, mid, (rest - mid.astype(F32)).astype(BF)


def _scan_max(x, reverse):
    n = x.shape[0]
    row = lax.broadcasted_iota(jnp.int32, x.shape, 0)
    k = 1
    while k < n:
        if reverse:
            shifted = jnp.where(row < n - k, pltpu.roll(x, n - k, axis=0), -jnp.inf)
        else:
            shifted = jnp.where(row >= k, pltpu.roll(x, k, axis=0), -jnp.inf)
        x = jnp.maximum(x, shifted)
        k *= 2
    return x


def _mlstm_kernel(qf_ref, kf_ref, vf_ref, qb_ref, kb_ref, vb_ref, gcf_ref, gcb_ref, grf_ref, grb_ref,
                  bc_ref, br_ref, *refs):
    hf_ref, hb_ref, c_ref, n_ref, m_ref = refs[-5:]

    @pl.when(pl.program_id(1) == 0)
    def _():
        for state_ref, init_ref in zip((c_ref, n_ref, m_ref), refs[:-5] or (None,) * 3):
            state_ref[...] = jnp.zeros_like(state_ref) if init_ref is None else init_ref[...]

    nh, lc = ML_HEADS, ML_CHUNK
    row = lax.broadcasted_iota(jnp.int32, (lc, lc), 0)
    col = lax.broadcasted_iota(jnp.int32, (lc, lc), 1)
    lower = col <= row
    upper = col >= row
    ones_v = jnp.ones((lc, ML_V), BF)
    directions = ((qf_ref, kf_ref, vf_ref, gcf_ref, grf_ref, hf_ref, lower, upper),
                  (qb_ref, kb_ref, vb_ref, gcb_ref, grb_ref, hb_ref, upper, lower))
    seqs = range(c_ref.shape[0])
    gate_terms = {}
    for r, (d, (_, _, _, gc_ref, gr_ref, _, allowed, allowed_t)) in itertools.product(seqs, enumerate(directions)):
        g_col = gc_ref[r, :, GATE_LANE0:GATE_LANE0 + 4 * nh] + bc_ref[...]
        g_row = gr_ref[r] + br_ref[...]
        i_col = g_col[:, nh * d:nh * (d + 1)]
        f_col = _log_sigmoid(g_col[:, 2 * nh + nh * d:2 * nh + nh * (d + 1)])
        i_row = g_row[nh * d:nh * (d + 1), :]
        f_row = _log_sigmoid(g_row[2 * nh + nh * d:2 * nh + nh * (d + 1), :])
        tri = jnp.where(allowed, 1.0, 0.0).astype(BF)
        tri_t = jnp.where(allowed_t, 1.0, 0.0).astype(BF)
        b_col = sum(jnp.dot(tri, part, preferred_element_type=F32) for part in _split3(f_col))
        b_row = sum(jnp.dot(part, tri_t, preferred_element_type=F32) for part in _split3(f_row))
        b_end = jnp.sum(f_col, axis=0, keepdims=True)
        m_prev = m_ref[r, :, nh * d:nh * (d + 1)]
        m_t = b_col + jnp.maximum(m_prev, _scan_max(i_col - b_col, reverse=d == 1))
        g_col_end = b_end - b_col + i_col
        m_new = jnp.maximum(b_end + m_prev, jnp.max(g_col_end, axis=0, keepdims=True))
        gate_terms[r, d] = dict(
            u=b_col - m_t, r_row=i_row - b_row, w_inter=jnp.exp(b_col + m_prev - m_t), floor=jnp.exp(-m_t),
            k_scale=jnp.exp(g_col_end - m_new), decay=jnp.exp(b_end + m_prev - m_new), m_new=m_new)
    work = []
    for r, (d, (q_ref, k_ref, v_ref, _, _, h_ref, allowed, _)) in itertools.product(seqs, enumerate(directions)):
        for h in range(nh):
            w = dict(r=r, d=d, h=h, h_ref=h_ref, allowed=allowed, g=gate_terms[r, d])
            w["q"] = (q_ref[r, :, h * ML_QK:(h + 1) * ML_QK] * (ML_QK ** -0.5)).astype(BF)
            w["k"] = k_ref[r, :, h * ML_QK:(h + 1) * ML_QK]
            w["v1"] = jnp.concatenate([v_ref[r, :, h * ML_V:(h + 1) * ML_V].astype(BF), ones_v], axis=1)
            work.append(w)
    for w in work:
        w["qk"] = _bdot_nt(w["q"], w["k"])
    for w in work:
        r, d, h = w["r"], w["d"], w["h"]
        w["c_prev"], w["n_prev"] = c_ref[r, d, h], n_ref[r, d, h]
        state = jnp.concatenate([w["c_prev"], w["n_prev"]], axis=1).astype(BF)
        w["q_state"] = jnp.dot(w["q"], state, preferred_element_type=F32)
    for w in work:
        h, g = w["h"], w["g"]
        kw = w["k"] * g["k_scale"][:, h:h + 1]
        w["kv"] = _bdot_tn(kw, w["v1"])
    for w in work:
        h, g = w["h"], w["g"]
        dmat = jnp.where(w["allowed"], g["u"][:, h:h + 1] + g["r_row"][h:h + 1, :], -jnp.inf)
        sw = w["qk"] * jnp.exp(dmat)
        sw_hi = sw.astype(BF)
        sw_lo = (sw - sw_hi.astype(F32)).astype(BF)
        w["pv"] = jnp.dot(sw_hi, w["v1"], preferred_element_type=F32)
        w["den_lo"] = jnp.dot(sw_lo, ones_v, preferred_element_type=F32)
    for w in work:
        r, d, h, g = w["r"], w["d"], w["h"], w["g"]
        w_inter = g["w_inter"][:, h:h + 1]
        num = w_inter * w["q_state"][:, :ML_V] + w["pv"][:, :ML_V]
        den = w_inter * w["q_state"][:, ML_V:] + (w["pv"][:, ML_V:] + w["den_lo"])
        w["h_ref"][r, :, h * ML_V:(h + 1) * ML_V] = num / jnp.maximum(jnp.abs(den), g["floor"][:, h:h + 1])
        decay = g["decay"][:, h:h + 1]
        c_ref[r, d, h] = decay * w["c_prev"] + w["kv"][:, :ML_V]
        n_ref[r, d, h] = decay * w["n_prev"] + w["kv"][:, ML_V:]
    for r in seqs:
        m_ref[r, :, :nh] = gate_terms[r, 0]["m_new"]
        m_ref[r, :, nh:] = gate_terms[r, 1]["m_new"]


def _mlstm(qkvo, side, g_row, b_col, b_row, init, bsz, seq):
    t = qkvo.shape[0]
    nc = seq // ML_CHUNK
    lc = ML_CHUNK
    nq = ML_HEADS * ML_QK
    nv = ML_HEADS * ML_V
    side_blk = (SIDE_COLS - 128) // 128
    rb = min(MLSTM_SEQS, bsz)
    assert bsz % rb == 0
    fwd = lambda c: c
    bwd = lambda c: nc - 1 - c

    def specs(pos):
        return [pl.BlockSpec((rb, lc, nq), lambda b, c: (b, pos(c), 0)),
                pl.BlockSpec((rb, lc, nq), lambda b, c: (b, pos(c), 1)),
                pl.BlockSpec((rb, lc, nv), lambda b, c: (b, pos(c), 1))]

    state = lambda *shape: pl.BlockSpec((rb,) + shape, lambda b, c: (b,) + (0,) * len(shape))
    state_specs = [state(2, ML_HEADS, ML_QK, ML_V), state(2, ML_HEADS, ML_QK, ML_V), state(1, 2 * ML_HEADS)]
    in_specs = specs(fwd) + specs(bwd) + [
        pl.BlockSpec((rb, lc, 128), lambda b, c: (b, fwd(c), side_blk)),
        pl.BlockSpec((rb, lc, 128), lambda b, c: (b, bwd(c), side_blk)),
        pl.BlockSpec((rb, None, 4 * ML_HEADS, lc), lambda b, c: (b, fwd(c), 0, 0)),
        pl.BlockSpec((rb, None, 4 * ML_HEADS, lc), lambda b, c: (b, bwd(c), 0, 0)),
        pl.BlockSpec((1, 4 * ML_HEADS), lambda b, c: (0, 0)),
        pl.BlockSpec((4 * ML_HEADS, 1), lambda b, c: (0, 0))]
    init = () if init is None else tuple(init)
    in_specs += state_specs[:len(init)]
    out_specs = [pl.BlockSpec((rb, lc, nv), lambda b, c: (b, fwd(c), 0)),
                 pl.BlockSpec((rb, lc, nv), lambda b, c: (b, bwd(c), 0))] + state_specs
    out_shape = [jax.ShapeDtypeStruct((bsz, seq, nv), F32), jax.ShapeDtypeStruct((bsz, seq, nv), F32),
                 jax.ShapeDtypeStruct((bsz, 2, ML_HEADS, ML_QK, ML_V), F32),
                 jax.ShapeDtypeStruct((bsz, 2, ML_HEADS, ML_QK, ML_V), F32),
                 jax.ShapeDtypeStruct((bsz, 1, 2 * ML_HEADS), F32)]
    qkvo3, side3 = qkvo.reshape(bsz, seq, -1), side.reshape(bsz, seq, -1)
    g_row4 = g_row.reshape(bsz, nc, 4 * ML_HEADS, lc)
    hf, hb, c_fin, n_fin, m_fin = pl.pallas_call(
        _mlstm_kernel, grid=(bsz // rb, nc), in_specs=in_specs, out_specs=out_specs, out_shape=out_shape,
        compiler_params=_params(2), name="mlstm",
    )(qkvo3, qkvo3, qkvo3, qkvo3, qkvo3, qkvo3, side3, side3, g_row4, g_row4, b_col, b_row, *init)
    return hf.reshape(t, nv), hb.reshape(t, nv), c_fin, n_fin, m_fin


def _mlstm_post_kernel(hf_ref, hb_ref, o_ref, g_ref, y_ref):
    for h in range(ML_HEADS):
        sl = slice(h * ML_V, (h + 1) * ML_V)
        hn = _rms(hf_ref[:, sl] + hb_ref[:, sl]) * g_ref[:, sl]
        y_ref[:, sl] = (hn * _sigmoid(o_ref[:, sl])).astype(BF)


def _mlstm_post(hf, hb, qkvo, gain):
    t, nv = hf.shape
    tm = ROW_TILE
    blk = pl.BlockSpec((tm, nv), lambda i: (i, 0))
    return pl.pallas_call(
        _mlstm_post_kernel, grid=(t // tm,),
        in_specs=[blk, blk, pl.BlockSpec((tm, nv), lambda i: (i, 2)), pl.BlockSpec((1, nv), lambda i: (0, 0))],
        out_specs=blk, out_shape=jax.ShapeDtypeStruct((t, nv), BF),
        compiler_params=_params(1), name="mlstm_post",
    )(hf, hb, qkvo, gain.reshape(1, nv))


LOG2_E = 1.4426950408889634


def _softmax_terms(scores, scale):
    c = scale * LOG2_E
    scaled = [s * c for s in scores]
    m = functools.reduce(jnp.maximum, [jnp.max(s, axis=-1, keepdims=True) for s in scaled])
    e = [jnp.exp2(s - m) for s in scaled]
    den = functools.reduce(lambda a, b: a + b, [jnp.sum(x, axis=-1, keepdims=True) for x in e])
    return [x.astype(BF) for x in e], 1.0 / den


def _mla_kernel(*refs, with_ctx):
    if with_ctx:
        qa_ref, kv_ref, side_ref, cq_ref, sq_ref, ck_ref, sk_ref, kvc_ref, krc_ref, o_ref = refs
    else:
        qa_ref, kv_ref, side_ref, o_ref = refs
    scale = (MLA_NOPE + MLA_ROPE) ** -0.5
    nope_cols = MLA_HEADS * MLA_NOPE
    q_rope = qa_ref[:, nope_cols:]
    k_rope = side_ref[...]
    if with_ctx:
        q_rope_rot = _rope(q_rope, cq_ref[...], sq_ref[...], MLA_ROPE // 4).astype(BF)
        k_rope_rot = _rope(k_rope, ck_ref[...], sk_ref[...], MLA_ROPE // 4)[:, :MLA_ROPE].astype(BF)
        k_rope_ctx = krc_ref[...].astype(BF)
        q_rope = q_rope.astype(BF)
    else:
        q_rope_rot = q_rope.astype(BF)
        k_rope_rot = k_rope[:, :MLA_ROPE].astype(BF)
    heads = range(MLA_HEADS)
    rope_cols = [slice(h * MLA_ROPE, (h + 1) * MLA_ROPE) for h in heads]
    kv0 = [h * (MLA_NOPE + MLA_V) for h in heads]
    q_n = [qa_ref[:, h * MLA_NOPE:(h + 1) * MLA_NOPE].astype(BF) for h in heads]
    scores = [[_bdot_nt(q_n[h], kv_ref[:, kv0[h]:kv0[h] + MLA_NOPE])
               + _bdot_nt(q_rope_rot[:, rope_cols[h]], k_rope_rot)] for h in heads]
    if with_ctx:
        for h in heads:
            scores[h].append(_bdot_nt(q_n[h], kvc_ref[:, kv0[h]:kv0[h] + MLA_NOPE])
                             + _bdot_nt(q_rope[:, rope_cols[h]], k_rope_ctx))
    weights = [_softmax_terms(scores[h], scale) for h in heads]
    for h in heads:
        (e, inv_den), v0 = weights[h], kv0[h] + MLA_NOPE
        out = jnp.dot(e[0], kv_ref[:, v0:v0 + MLA_V].astype(BF), preferred_element_type=F32)
        if with_ctx:
            out = out + jnp.dot(e[1], kvc_ref[:, v0:v0 + MLA_V].astype(BF), preferred_element_type=F32)
        o_ref[:, h * MLA_V:(h + 1) * MLA_V] = (out * inv_den).astype(BF)


def _mla_attention(qa, kv, side, bsz, seq, ctx=None):
    t = qa.shape[0]
    tq = ATTN_Q_ROWS
    nq = seq // tq
    side_blk = (SIDE_COLS - 128) // 128
    nkv = MLA_HEADS * (MLA_NOPE + MLA_V)
    in_specs = [pl.BlockSpec((tq, qa.shape[1]), lambda b, i: (b * nq + i, 0)),
                pl.BlockSpec((seq, nkv), lambda b, i: (b, 0)),
                pl.BlockSpec((seq, 128), lambda b, i: (b, side_blk))]
    args = [qa, kv, side]
    if ctx is not None:
        cos_q, sin_q, cos_k, sin_k, kvc, krc = ctx
        past = krc.shape[1]
        in_specs += [pl.BlockSpec((tq, cos_q.shape[1]), lambda b, i: (i, 0)),
                     pl.BlockSpec((tq, cos_q.shape[1]), lambda b, i: (i, 0)),
                     pl.BlockSpec((seq, 128), lambda b, i: (0, 0)),
                     pl.BlockSpec((seq, 128), lambda b, i: (0, 0)),
                     pl.BlockSpec((past, nkv), lambda b, i: (b, 0)),
                     pl.BlockSpec((None, past, MLA_ROPE), lambda b, i: (b, 0, 0))]
        args += [cos_q, sin_q, cos_k, sin_k, kvc, krc]
    nout = MLA_HEADS * MLA_V
    return pl.pallas_call(
        functools.partial(_mla_kernel, with_ctx=ctx is not None),
        grid=(bsz, nq), in_specs=in_specs,
        out_specs=pl.BlockSpec((tq, nout), lambda b, i: (b * nq + i, 0)),
        out_shape=jax.ShapeDtypeStruct((t, nout), BF),
        compiler_params=_params(2), name="mla_attention",
    )(*args)


def _gqa_kernel(*refs, with_ctx):
    if with_ctx:
        q_ref, k_ref, v_ref, gq_ref, gk_ref, cq_ref, sq_ref, ck_ref, sk_ref, kc_ref, vc_ref, o_ref, ksrc_ref = refs
    else:
        q_ref, k_ref, v_ref, gq_ref, gk_ref, o_ref, kn_ref, vo_ref, ksrc_ref = refs
    hd = GQA_HEAD_DIM
    scale = hd ** -0.5
    rep = GQA_HEADS // GQA_KV_HEADS
    n_kv = k_ref.shape[1] // hd
    kv_cols = [slice(g * hd, (g + 1) * hd) for g in range(n_kv)]

    @pl.when(pl.program_id(2) == 0)
    def _():
        for g in range(n_kv):
            k_n = _rms(k_ref[:, kv_cols[g]]) * gk_ref[...]
            if with_ctx:
                ksrc_ref[:, kv_cols[g]] = _rope(k_n, ck_ref[...], sk_ref[...], hd // 4).astype(BF)
            else:
                ksrc_ref[:, kv_cols[g]] = k_n.astype(BF)
                kn_ref[g] = k_n
                vo_ref[g] = v_ref[:, kv_cols[g]]

    v = [v_ref[:, kv_cols[g]].astype(BF) for g in range(n_kv)]
    if with_ctx:
        k_ctx, v_ctx = kc_ref[...].astype(BF), vc_ref[...].astype(BF)
    heads = range(n_kv * rep)
    cols = [slice(r * hd, (r + 1) * hd) for r in heads]
    q_n = [_rms(q_ref[:, cols[r]]) * gq_ref[...] for r in heads]
    k_src = [ksrc_ref[:, kv_cols[r // rep]] for r in heads]
    if with_ctx:
        scores = [[_bdot_nt(_rope(q_n[r], cq_ref[...], sq_ref[...], hd // 4), k_src[r]), _bdot_nt(q_n[r], k_ctx)]
                  for r in heads]
    else:
        scores = [[_bdot_nt(q_n[r], k_src[r])] for r in heads]
    weights = [_softmax_terms(scores[r], scale) for r in heads]
    for r in heads:
        e, inv_den = weights[r]
        out = jnp.dot(e[0], v[r // rep], preferred_element_type=F32)
        if with_ctx:
            out = out + jnp.dot(e[1], v_ctx, preferred_element_type=F32)
        o_ref[:, cols[r]] = (out * inv_den).astype(BF)


def _gqa_attention(qkv, g_q, g_k, bsz, seq, ctx=None):
    t = qkv.shape[0]
    hd = GQA_HEAD_DIM
    rep = GQA_HEADS // GQA_KV_HEADS
    tq = ATTN_Q_ROWS
    nq = seq // tq
    n_kv = 1 if ctx is not None else GQA_KV_HEADS
    k_blk0 = GQA_HEADS // n_kv
    in_specs = [pl.BlockSpec((tq, n_kv * rep * hd), lambda b, g, i: (b * nq + i, g)),
                pl.BlockSpec((seq, n_kv * hd), lambda b, g, i: (b, k_blk0 + g)),
                pl.BlockSpec((seq, n_kv * hd), lambda b, g, i: (b, k_blk0 + GQA_KV_HEADS // n_kv + g)),
                pl.BlockSpec((1, hd), lambda b, g, i: (0, 0)),
                pl.BlockSpec((1, hd), lambda b, g, i: (0, 0))]
    args = [qkv, qkv, qkv, g_q.reshape(1, hd), g_k.reshape(1, hd)]
    o_spec = pl.BlockSpec((tq, n_kv * rep * hd), lambda b, g, i: (b * nq + i, g))
    o_shape = jax.ShapeDtypeStruct((t, GQA_HEADS * hd), BF)
    if ctx is not None:
        cos_t, sin_t, kc, vc, j = ctx
        past = kc.shape[3]
        cache = pl.BlockSpec((None, None, None, past, hd), lambda b, g, i: (b, j, g, 0, 0))
        in_specs += [pl.BlockSpec((tq, hd), lambda b, g, i: (i, 0)),
                     pl.BlockSpec((tq, hd), lambda b, g, i: (i, 0)),
                     pl.BlockSpec((seq, hd), lambda b, g, i: (0, 0)),
                     pl.BlockSpec((seq, hd), lambda b, g, i: (0, 0)),
                     cache, cache]
        args += [cos_t, sin_t, cos_t, sin_t, kc, vc]
        out_specs, out_shape = o_spec, o_shape
    else:
        head_major = pl.BlockSpec((None, n_kv, seq, hd), lambda b, g, i: (b, g, 0, 0))
        out_specs = [o_spec, head_major, head_major]
        kv_shape = jax.ShapeDtypeStruct((bsz, GQA_KV_HEADS, seq, hd), F32)
        out_shape = [o_shape, kv_shape, kv_shape]
    return pl.pallas_call(
        functools.partial(_gqa_kernel, with_ctx=ctx is not None),
        grid=(bsz, GQA_KV_HEADS // n_kv, nq), in_specs=in_specs, out_specs=out_specs, out_shape=out_shape,
        scratch_shapes=[pltpu.VMEM((seq, n_kv * hd), BF)],
        compiler_params=_params(3), name="gqa_attention",
    )(*args)


def _dispatch_kernel(ar_ref, ac_ref, h_ref, xe_ref, gate_ref, rankc_ref, rankr_ref, *, cap):
    n_groups, group, s = ar_ref.shape
    g = pl.program_id(1)

    @pl.when(g == 0)
    def _():
        blk = min(RANK_BLOCK, s)
        rankr_ref[...] = jnp.zeros_like(rankr_ref)
        ones_sub = jnp.ones((8, blk), BF)
        ones_lane = jnp.ones((s, 128), BF)
        for j in range(s // blk):
            rows = pl.ds(j * blk, blk)
            i0 = lax.broadcasted_iota(jnp.int32, (blk, s), 0) + j * blk
            i1 = lax.broadcasted_iota(jnp.int32, (blk, s), 1)
            sub_first = jnp.where(i0 < i1, 1.0, 0.0)
            for e in range(n_groups * group):
                eg, ei = e // group, e % group
                a_row = ar_ref[eg, ei:ei + 1, :]
                a_col = ac_ref[rows, e:e + 1]
                ahead = jnp.where(a_col > a_row, 1.0, jnp.where(a_col >= a_row, sub_first, 0.0)).astype(BF)
                rankr_ref[eg, ei:ei + 1, :] += jnp.dot(ones_sub, ahead, preferred_element_type=F32)[:1, :]
                rankc_ref[rows, e:e + 1] = (s - 1.0) - jnp.dot(ahead, ones_lane, preferred_element_type=F32)[:, :1]

    slot = lax.broadcasted_iota(jnp.int32, (group, cap, s), 1).astype(F32)
    pick = rankr_ref[g][:, None, :] == slot
    onehot = jnp.where(pick, 1.0, 0.0).reshape(group * cap, s).astype(BF)
    rows_f32 = jnp.dot(onehot, h_ref[...], preferred_element_type=F32)
    xe_ref[...] = rows_f32.reshape(group, cap, -1).astype(BF)
    gate_ref[...] = jnp.sum(jnp.where(pick, ar_ref[g][:, None, :], 0.0), axis=2, keepdims=True)


def _dispatch(aff_row, aff_col, h, bsz, seq):
    t, d = h.shape
    ne = aff_row.shape[0]
    cap = EC_FACTOR * seq // ne
    group = GATHER_ROWS // cap
    n_groups = ne // group
    return pl.pallas_call(
        functools.partial(_dispatch_kernel, cap=cap),
        grid=(bsz, n_groups),
        in_specs=[pl.BlockSpec((n_groups, group, seq), lambda b, g: (0, 0, b)),
                  pl.BlockSpec((seq, ne), lambda b, g: (b, 0)),
                  pl.BlockSpec((seq, d), lambda b, g: (b, 0))],
        out_specs=[pl.BlockSpec((group, cap, d), lambda b, g: (g, b, 0)),
                   pl.BlockSpec((group, cap, 1), lambda b, g: (g, b, 0)),
                   pl.BlockSpec((seq, ne), lambda b, g: (b, 0))],
        out_shape=[jax.ShapeDtypeStruct((ne, bsz * cap, d), BF),
                   jax.ShapeDtypeStruct((ne, bsz * cap, 1), F32),
                   jax.ShapeDtypeStruct((t, ne), F32)],
        scratch_shapes=[pltpu.VMEM((n_groups, group, seq), F32)],
        compiler_params=_params(2), name="ec_dispatch",
    )(aff_row.reshape(n_groups, group, t), aff_col, h)


def _experts_kernel(xp_ref, xs_ref, gp_ref, gs_ref, wg_ref, wu_ref, wd_ref, yp_ref, ys_ref, accp_ref, accs_ref):
    f = pl.program_id(1)
    tf = wg_ref.shape[1]
    w_gate_up = jnp.concatenate([wg_ref[...].astype(BF), wu_ref[...].astype(BF)], axis=1)
    wd = wd_ref[...].astype(BF)
    groups = ((xp_ref, gp_ref, yp_ref, accp_ref), (xs_ref, gs_ref, ys_ref, accs_ref))

    @pl.when(f == 0)
    def _():
        for _, _, _, acc_ref in groups:
            acc_ref[...] = jnp.zeros_like(acc_ref)

    for x_ref, _, _, acc_ref in groups:
        for r in range(0, x_ref.shape[0], EXPERT_ROWS):
            rows = slice(r, r + EXPERT_ROWS)
            au = jnp.dot(x_ref[rows, :], w_gate_up, preferred_element_type=F32)
            a, u = au[:, :tf], au[:, tf:]
            acc_ref[rows, :] += jnp.dot((a * _sigmoid(a) * u).astype(BF), wd, preferred_element_type=F32)

    @pl.when(f == pl.num_programs(1) - 1)
    def _():
        for _, g_ref, y_ref, acc_ref in groups:
            y_ref[...] = (acc_ref[...] * g_ref[...]).astype(BF)


def _experts(xe_p, xe_s, gate_p, gate_s, w_gate, w_up, w_down, layer):
    ne, mp, d = xe_p.shape
    ms = xe_s.shape[1]
    fdim = w_gate.shape[-1]
    tf = EXPERT_HIDDEN_COLS
    rows = lambda m, last: pl.BlockSpec((None, m, last), lambda e, f: (e, 0, 0))
    return pl.pallas_call(
        _experts_kernel, grid=(ne, fdim // tf),
        in_specs=[rows(mp, d), rows(ms, d), rows(mp, 1), rows(ms, 1),
                  pl.BlockSpec((None, None, d, tf), lambda e, f: (layer, e, 0, f)),
                  pl.BlockSpec((None, None, d, tf), lambda e, f: (layer, e, 0, f)),
                  pl.BlockSpec((None, None, tf, d), lambda e, f: (layer, e, f, 0))],
        out_specs=[rows(mp, d), rows(ms, d)],
        out_shape=[jax.ShapeDtypeStruct((ne, mp, d), BF), jax.ShapeDtypeStruct((ne, ms, d), BF)],
        scratch_shapes=[pltpu.VMEM((mp, d), F32), pltpu.VMEM((ms, d), F32)],
        compiler_params=_params(2), name="ec_experts",
    )(xe_p, xe_s, gate_p, gate_s, w_gate, w_up, w_down)


RANK_RADIX = 32


def _combine_kernel(y_ref, rank_ref, x_ref, g_ref, *refs, cap, final):
    post_refs, place_ref = refs[:-1], refs[-1]
    ne = y_ref.shape[0]
    n_slots = ne * cap

    @pl.when(pl.program_id(1) == 0)
    def _():
        rank = rank_ref[...]
        hi = jnp.floor(rank * (1.0 / RANK_RADIX))
        lo = rank - RANK_RADIX * hi
        lane = lax.broadcasted_iota(jnp.int32, (ne, n_slots), 1)
        expert = lax.broadcasted_iota(jnp.int32, (ne, n_slots), 0)
        own = (lane >= expert * cap) & (lane < (expert + 1) * cap)
        spread_hi = jnp.where(own, float(RANK_RADIX), 0.0).astype(BF)
        spread_lo = jnp.where(own, 1.0, 0.0).astype(BF)
        spread = (jnp.dot(hi.astype(BF), spread_hi, preferred_element_type=F32)
                  + jnp.dot(lo.astype(BF), spread_lo, preferred_element_type=F32))
        slot = (lax.broadcasted_iota(jnp.int32, (1, n_slots), 1) & (cap - 1)).astype(F32)
        place_ref[...] = jnp.where(spread == slot, 1.0, 0.0).astype(BF)

    ts = x_ref.shape[0]
    rows = pl.ds(pl.multiple_of(pl.program_id(1) * ts, ts), ts)
    y = y_ref[...].reshape(n_slots, y_ref.shape[2])
    x_new = x_ref[...] + g_ref[...] * jnp.dot(place_ref[rows, :], y, preferred_element_type=F32)
    if final:
        (gain_ref,), (out_ref,) = post_refs[:1], post_refs[1:]
        out_ref[...] = _rms(x_new) * gain_ref[...]
    else:
        (sh_ref, sc_ref), (xo_ref, h_ref) = post_refs[:2], post_refs[2:]
        xo_ref[...] = x_new
        h_ref[...] = (_rms(x_new) * (1.0 + sc_ref[...]) + sh_ref[...]).astype(BF)


def _combine(y, rank, x, gate, bsz, seq, next_shift_scale=None, final_gain=None):
    t, d = x.shape
    ne = y.shape[0]
    cap = EC_FACTOR * seq // ne
    assert seq <= RANK_RADIX * RANK_RADIX and cap & (cap - 1) == 0
    ts = min(COMBINE_ROWS, seq)
    nt = seq // ts
    final = final_gain is not None
    vec = lambda v: pl.BlockSpec((None, 1, d), lambda b, i: (b if v.shape[0] > 1 else 0, 0, 0))
    tile = pl.BlockSpec((ts, d), lambda b, i: (b * nt + i, 0))
    in_specs = [pl.BlockSpec((ne, cap, d), lambda b, i: (0, b, 0)),
                pl.BlockSpec((seq, ne), lambda b, i: (b, 0)), tile, vec(gate)]
    if final:
        args = [final_gain.reshape(1, d)]
        in_specs.append(pl.BlockSpec((1, d), lambda b, i: (0, 0)))
        out_specs, out_shape = tile, jax.ShapeDtypeStruct((t, d), F32)
    else:
        args = list(next_shift_scale)
        in_specs += [vec(v) for v in args]
        out_specs = [tile, tile]
        out_shape = [jax.ShapeDtypeStruct((t, d), F32), jax.ShapeDtypeStruct((t, d), BF)]
    return pl.pallas_call(
        functools.partial(_combine_kernel, cap=cap, final=final),
        grid=(bsz, nt), in_specs=in_specs, out_specs=out_specs, out_shape=out_shape,
        scratch_shapes=[pltpu.VMEM((seq, ne * cap), BF)],
        compiler_params=_params(2), name="ec_combine",
    )(y, rank, x, gate, *args)


def _rope_tables(n_tokens, rot_dim):
    rows = n_tokens // GRID_W
    row = jnp.repeat(jnp.arange(rows), GRID_W).astype(F32)
    col = jnp.tile(jnp.arange(GRID_W), rows).astype(F32)
    quarter = rot_dim // 4
    inv = ROPE_THETA ** (-jnp.arange(quarter, dtype=F32) / quarter)
    a_row, a_col = row[:, None] * inv, col[:, None] * inv
    cos_t = jnp.concatenate([jnp.cos(a_row), jnp.cos(a_row), jnp.cos(a_col), jnp.cos(a_col)], axis=-1)
    sin_t = jnp.concatenate([-jnp.sin(a_row), jnp.sin(a_row), -jnp.sin(a_col), jnp.sin(a_col)], axis=-1)
    return cos_t, sin_t


def kernel(x_prompt, x_sample, state_mlstm_c, state_mlstm_n, state_mlstm_m, cache_mla_ckv, cache_mla_krope,
           cache_gqa_k, cache_gqa_v, c, c_ctx, w_mod, b_mod, w_in_even, b_igate, b_fgate, g_mlstm, g_cq, w_uq,
           g_ckv, w_ukv, w_out_even, w_in_odd, g_qnorm, g_knorm, w_out_odd, w_router, w_expert_gate,
           w_expert_up, w_expert_down, g_final):
    d = D_MODEL
    bp, sp, _ = x_prompt.shape
    bs, ss, _ = x_sample.shape
    depth = w_mod.shape[0]
    nh = ML_HEADS
    streams = {"p": (bp, sp), "s": (bs, ss)}
    x = {"p": x_prompt.reshape(bp * sp, d), "s": x_sample.reshape(bs * ss, d)}

    c8 = jnp.concatenate([c_ctx[None], c, jnp.zeros((8 - 1 - bs, d), F32)], axis=0)
    mod_all = _mod_vectors(c8, w_mod, b_mod).reshape(depth, 8, 6, 1, d)

    def mod(layer, key, idx):
        rows = mod_all[layer, 0:1, idx] if key == "p" else mod_all[layer, 1:1 + bs, idx]
        return rows

    new_even, new_odd = [], []
    h_in = {key: _norm_mod(x[key], mod(0, key, 0), mod(0, key, 1), streams[key][1]) for key in streams}
    routed_in = {}
    for layer in range(depth):
        j = layer // 2
        if layer % 2 == 0:
            w_out = _to_bf16(w_out_even, j)
            w_in_rows = jnp.swapaxes(w_in_even, 1, 2)
            w_side = jnp.concatenate(
                [w_in_rows[j, QKVO_COLS + 4 * nh:],
                 w_in_rows[j, QKVO_COLS:QKVO_COLS + 4 * nh],
                 jnp.zeros((128 - MLA_ROPE - 4 * nh, d), F32)], axis=0)[None]
            w_q = w_uq[j].reshape(MLA_Q_RANK, MLA_HEADS, MLA_NOPE + MLA_ROPE)
            w_q = jnp.concatenate([w_q[:, :, :MLA_NOPE].reshape(MLA_Q_RANK, -1),
                                   w_q[:, :, MLA_NOPE:].reshape(MLA_Q_RANK, -1)], axis=1)
            bias_col = jnp.concatenate([b_igate[j].reshape(1, -1), b_fgate[j].reshape(1, -1)], axis=1)
            bias_row = bias_col.reshape(-1, 1)
            cos64, sin64 = _rope_tables(ss, MLA_ROPE)
            cos_q, sin_q = jnp.tile(cos64, (1, MLA_HEADS)), jnp.tile(sin64, (1, MLA_HEADS))
            pad = jnp.zeros((ss, 128 - MLA_ROPE), F32)
            cos_k, sin_k = jnp.concatenate([cos64, pad], axis=1), jnp.concatenate([sin64, pad], axis=1)
            kvc = _mm_resident(cache_mla_ckv[:, j].reshape(-1, MLA_KV_RANK), w_ukv, j, w_ukv.shape[-1], IN_PROJ_COLS,
                               name="mla_ctx_expand")
            for key, (bsz, seq) in streams.items():
                h = h_in[key]
                qkvo = _mm_resident(h, w_in_rows, j, QKVO_COLS, IN_PROJ_COLS, w_rows=True, name="even_in_main")
                side = _mm_resident(h, w_side, 0, SIDE_COLS, SIDE_COLS // 3, w_rows=True, name="even_in_side")
                gates = side[:, SIDE_COLS - 128 + GATE_LANE0:SIDE_COLS - 128 + GATE_LANE0 + 4 * nh]
                g_row = gates.reshape(-1, ML_CHUNK, 4 * nh).transpose(0, 2, 1)
                if key == "p":
                    init = None
                else:
                    c0 = state_mlstm_c[:, j]
                    init = (c0, jnp.broadcast_to(state_mlstm_n[:, j][..., None], c0.shape),
                            state_mlstm_m[:, j].reshape(bsz, 1, 2 * nh))
                hf, hb, c_fin, n_fin, m_fin = _mlstm(qkvo, side, g_row, bias_col, bias_row, init, bsz, seq)
                y_ml = _mlstm_post(hf, hb, qkvo, g_mlstm[j])
                qa = _norm_mm(side, 0, g_cq[j], w_q, False, "mla_q_up")
                kv, ckv_n = _norm_mm(side, 1, g_ckv[j], w_ukv[j], True, "mla_kv_up")
                if key == "p":
                    y_a = _mla_attention(qa, kv, side, bsz, seq)
                    k_rope = side[:, SIDE_COLS - 128:SIDE_COLS - 128 + MLA_ROPE]
                    new_even.append((c_fin, n_fin[..., 0], m_fin.reshape(bsz, 2, nh),
                                     ckv_n.reshape(bsz, seq, -1), k_rope.reshape(bsz, seq, -1)))
                else:
                    y_a = _mla_attention(qa, kv, side, bsz, seq,
                                         ctx=(cos_q, sin_q, cos_k, sin_k, kvc, cache_mla_krope[:, j]))
                routed_in[key] = _out_proj_router([y_ml, y_a], w_out, x[key], mod(layer, key, 2), mod(layer, key, 3),
                                                  mod(layer, key, 4), w_router, layer, seq)
        else:
            w_out = _to_bf16(w_out_odd, j)
            cos_t, sin_t = _rope_tables(ss, GQA_HEAD_DIM)
            for key, (bsz, seq) in streams.items():
                qkv = _mm_resident(h_in[key], w_in_odd, j, w_in_odd.shape[-1], IN_PROJ_COLS, name="odd_in")
                if key == "p":
                    o, k_n, v = _gqa_attention(qkv, g_qnorm[j], g_knorm[j], bsz, seq)
                    new_odd.append((k_n, v))
                else:
                    o = _gqa_attention(qkv, g_qnorm[j], g_knorm[j], bsz, seq,
                                       ctx=(cos_t, sin_t, cache_gqa_k, cache_gqa_v, j))
                routed_in[key] = _out_proj_router([o], w_out, x[key], mod(layer, key, 2), mod(layer, key, 3),
                                                  mod(layer, key, 4), w_router, layer, seq)
        routed = {}
        for key, (bsz, seq) in streams.items():
            x[key], h, aff = routed_in[key]
            routed[key] = _dispatch(aff.T, aff, h, bsz, seq)
        y_p, y_s = _experts(routed["p"][0], routed["s"][0], routed["p"][1], routed["s"][1],
                            w_expert_gate, w_expert_up, w_expert_down, layer)
        for key, y in (("p", y_p), ("s", y_s)):
            bsz, seq = streams[key]
            if layer + 1 < depth:
                x[key], h_in[key] = _combine(
                    y, routed[key][2], x[key], mod(layer, key, 5), bsz, seq,
                    next_shift_scale=(mod(layer + 1, key, 0), mod(layer + 1, key, 1)))
            else:
                x[key] = _combine(y, routed[key][2], x[key], mod(layer, key, 5), bsz, seq, final_gain=g_final)

    y_prompt = x["p"].reshape(bp, sp, d)
    y_sample = x["s"].reshape(bs, ss, d)
    new_c = jnp.stack([e[0] for e in new_even], axis=1)
    new_n = jnp.stack([e[1] for e in new_even], axis=1)
    new_m = jnp.stack([e[2] for e in new_even], axis=1)
    new_ckv = jnp.stack([e[3] for e in new_even], axis=1)
    new_krope = jnp.stack([e[4] for e in new_even], axis=1)
    new_k = jnp.stack([e[0] for e in new_odd], axis=1)
    new_v = jnp.stack([e[1] for e in new_odd], axis=1)
    return (y_prompt, y_sample, new_c, new_n, new_m, new_ckv, new_krope, new_k, new_v)
```

```python
import functools
import itertools

import jax
import jax.numpy as jnp
from jax import lax
from jax.experimental import pallas as pl
from jax.experimental.pallas import tpu as pltpu

BF = jnp.bfloat16
F32 = jnp.float32

D_MODEL = 2048
GRID_W = 64
ROPE_THETA = 10000.0
NORM_EPS = 1e-6
ML_HEADS = 8
ML_QK = 64
ML_V = 128
ML_CHUNK = 64
MLA_HEADS = 8
MLA_Q_RANK = 512
MLA_KV_RANK = 512
MLA_NOPE = 128
MLA_ROPE = 64
MLA_V = 128
GQA_HEADS = 16
GQA_KV_HEADS = 4
GQA_HEAD_DIM = 128
N_EXPERTS = 16
EXPERT_DIM = 1024
EC_FACTOR = 2

QKVO_COLS = 2 * ML_HEADS * ML_QK + 2 * ML_HEADS * ML_V
SIDE_COLS = MLA_Q_RANK + MLA_KV_RANK + 128
GATE_LANE0 = MLA_ROPE

VMEM_LIMIT_BYTES = 56 * 1024 * 1024
OUT_PROJ_ROWS = 512
MLSTM_SEQS = 4
MOD_COLS = 1024
IN_PROJ_COLS = 512
IN_PROJ_ROW_CHUNK = 512
CAST_ROWS = 512
ROW_TILE = 512
ATTN_Q_ROWS = 256
EXPERT_HIDDEN_COLS = 512
EXPERT_ROWS = 256
RANK_BLOCK = 256
GATHER_ROWS = 512
COMBINE_ROWS = 256


def _params(n_axes):
    return pltpu.CompilerParams(dimension_semantics=("arbitrary",) * n_axes,
                                vmem_limit_bytes=VMEM_LIMIT_BYTES)


def _bdot(a, b):
    return jnp.dot(a.astype(BF), b.astype(BF), preferred_element_type=F32)


def _bdot_nt(a, b):
    return lax.dot_general(a.astype(BF), b.astype(BF), (((1,), (1,)), ((), ())),
                           preferred_element_type=F32)


def _bdot_tn(a, b):
    return lax.dot_general(a.astype(BF), b.astype(BF), (((0,), (0,)), ((), ())),
                           preferred_element_type=F32)


def _sigmoid(x):
    return 1.0 / (1.0 + jnp.exp(-x))


def _log_sigmoid(x):
    return jnp.minimum(x, 0.0) - jnp.log1p(jnp.exp(-jnp.abs(x)))


def _rms(x):
    return x * lax.rsqrt(jnp.mean(x * x, axis=-1, keepdims=True) + NORM_EPS)


def _rope(x, cos_t, sin_t, quarter):
    width = x.shape[-1]
    axis = x.ndim - 1
    lane = lax.broadcasted_iota(jnp.int32, x.shape, axis)
    partner = jnp.where((lane & quarter) == 0,
                        pltpu.roll(x, width - quarter, axis=axis),
                        pltpu.roll(x, quarter, axis=axis))
    return x * cos_t + partner * sin_t


def _mod_kernel(c_ref, w_ref, b_ref, o_ref):
    c = c_ref[...]
    o_ref[...] = _bdot(c * _sigmoid(c), w_ref[...]) + b_ref[...]


def _mod_vectors(c8, w_mod, b_mod):
    n_layers, k, n = w_mod.shape
    tn = MOD_COLS
    return pl.pallas_call(
        _mod_kernel,
        grid=(n_layers, n // tn),
        in_specs=[pl.BlockSpec((8, k), lambda l, j: (0, 0)),
                  pl.BlockSpec((None, k, tn), lambda l, j: (l, 0, j)),
                  pl.BlockSpec((None, 1, tn), lambda l, j: (l, 0, j))],
        out_specs=pl.BlockSpec((None, 8, tn), lambda l, j: (l, 0, j)),
        out_shape=jax.ShapeDtypeStruct((n_layers, 8, n), F32),
        compiler_params=_params(2),
        name="mod_vectors",
    )(c8, w_mod, b_mod.reshape(n_layers, 1, n))


def _norm_mod_kernel(x_ref, sh_ref, sc_ref, h_ref):
    h_ref[...] = (_rms(x_ref[...]) * (1.0 + sc_ref[...]) + sh_ref[...]).astype(BF)


def _batch_of_tile(n_vectors, tm, seq):
    if n_vectors == 1:
        return lambda i: 0
    assert seq % tm == 0
    return lambda i: (i * tm) // seq


def _norm_mod(x, shift, scale, seq):
    t, d = x.shape
    tm = OUT_PROJ_ROWS
    which = _batch_of_tile(shift.shape[0], tm, seq)
    vec = pl.BlockSpec((None, 1, d), lambda i: (which(i), 0, 0))
    x_spec = pl.BlockSpec((tm, d), lambda i: (i, 0))
    return pl.pallas_call(
        _norm_mod_kernel, grid=(t // tm,),
        in_specs=[x_spec, vec, vec], out_specs=x_spec,
        out_shape=jax.ShapeDtypeStruct((t, d), BF),
        compiler_params=_params(1), name="norm_mod",
    )(x, shift, scale)


def _to_bf16_kernel(w_ref, o_ref):
    o_ref[...] = w_ref[...].astype(BF)


def _to_bf16(w3, layer):
    _, k, n = w3.shape
    tk = CAST_ROWS
    return pl.pallas_call(
        _to_bf16_kernel, grid=(k // tk,),
        in_specs=[pl.BlockSpec((None, tk, n), lambda i: (layer, i, 0))],
        out_specs=pl.BlockSpec((tk, n), lambda i: (i, 0)),
        out_shape=jax.ShapeDtypeStruct((k, n), BF),
        compiler_params=_params(1), name="weight_to_bf16",
    )(w3)


def _out_proj_router_kernel(*refs, k_sizes):
    n_a = len(k_sizes)
    a_refs = refs[:n_a]
    w_ref, x_ref, g_ref, sh_ref, sc_ref, wr_ref, xo_ref, h_ref, aff_ref = refs[n_a:]
    w_router = wr_ref[...].astype(BF)
    chunk = OUT_PROJ_ROWS // 2
    for r in range(0, x_ref.shape[0], chunk):
        rows = slice(r, r + chunk)
        acc, off = None, 0
        for a_ref, ks in zip(a_refs, k_sizes):
            term = jnp.dot(a_ref[rows, :], w_ref[off:off + ks, :], preferred_element_type=F32)
            acc = term if acc is None else acc + term
            off += ks
        x_new = x_ref[rows, :] + g_ref[...] * acc
        xo_ref[rows, :] = x_new
        h = (_rms(x_new) * (1.0 + sc_ref[...]) + sh_ref[...]).astype(BF)
        h_ref[rows, :] = h
        logits = jnp.dot(h, w_router, preferred_element_type=F32)
        e = jnp.exp(logits - jnp.max(logits, axis=-1, keepdims=True))
        aff_ref[rows, :] = e / jnp.sum(e, axis=-1, keepdims=True)


def _out_proj_router(a_list, w_bf, x, gate, shift, scale, w_router, layer, seq):
    m, d = x.shape
    k_sizes = tuple(a.shape[1] for a in a_list)
    assert w_bf.shape == (sum(k_sizes), d) and all(a.dtype == BF for a in a_list)
    tm = OUT_PROJ_ROWS
    ne = w_router.shape[-1]
    which = _batch_of_tile(gate.shape[0], tm, seq)
    vec = pl.BlockSpec((None, 1, d), lambda i: (which(i), 0, 0))
    tile = pl.BlockSpec((tm, d), lambda i: (i, 0))
    in_specs = [pl.BlockSpec((tm, ks), lambda i: (i, 0)) for ks in k_sizes]
    in_specs += [pl.BlockSpec(w_bf.shape, lambda i: (0, 0), pipeline_mode=pl.Buffered(1)),
                 tile, vec, vec, vec, pl.BlockSpec((None, d, ne), lambda i: (layer, 0, 0))]
    return pl.pallas_call(
        functools.partial(_out_proj_router_kernel, k_sizes=k_sizes),
        grid=(m // tm,), in_specs=in_specs,
        out_specs=[tile, tile, pl.BlockSpec((tm, ne), lambda i: (i, 0))],
        out_shape=[jax.ShapeDtypeStruct((m, d), F32), jax.ShapeDtypeStruct((m, d), BF),
                   jax.ShapeDtypeStruct((m, ne), F32)],
        compiler_params=_params(1), name="out_proj_router",
    )(*a_list, w_bf, x, gate, shift, scale, w_router)


def _mm_resident_kernel(a_ref, w_ref, o_ref, *, w_rows, chunk):
    w = w_ref[...].astype(BF)
    for r in range(0, a_ref.shape[0], chunk):
        rows = slice(r, r + chunk)
        a = a_ref[rows, :].astype(BF)
        if w_rows:
            o_ref[rows, :] = lax.dot_general(a, w, (((1,), (1,)), ((), ())), preferred_element_type=F32)
        else:
            o_ref[rows, :] = jnp.dot(a, w, preferred_element_type=F32)


def _mm_resident(a, w3, layer, n_cols, tn, w_rows=False, name="mm_resident"):
    m, k = a.shape
    assert w3.shape[2 if w_rows else 1] == k and n_cols % tn == 0
    if w_rows:
        w_spec = pl.BlockSpec((None, tn, k), lambda j: (layer, j, 0))
    else:
        w_spec = pl.BlockSpec((None, k, tn), lambda j: (layer, 0, j))
    return pl.pallas_call(
        functools.partial(_mm_resident_kernel, w_rows=w_rows, chunk=min(IN_PROJ_ROW_CHUNK, m)),
        grid=(n_cols // tn,),
        in_specs=[pl.BlockSpec((m, k), lambda j: (0, 0), pipeline_mode=pl.Buffered(1)), w_spec],
        out_specs=pl.BlockSpec((m, tn), lambda j: (0, j)),
        out_shape=jax.ShapeDtypeStruct((m, n_cols), F32),
        compiler_params=_params(1), name=name,
    )(a, w3)


def _norm_mm_kernel(x_ref, g_ref, w_ref, *out_refs, with_normed):
    wbf_ref = out_refs[-1]

    @pl.when(pl.program_id(0) == 0)
    def _():
        wbf_ref[...] = w_ref[...].astype(BF)

    xn = _rms(x_ref[...]) * g_ref[...]
    out_refs[0][...] = jnp.dot(xn.astype(BF), wbf_ref[...], preferred_element_type=F32)
    if with_normed:
        out_refs[1][...] = xn


def _norm_mm(x, col_blk, gain, w, with_normed, name):
    t = x.shape[0]
    k, n = w.shape
    tm = ROW_TILE
    out_specs = [pl.BlockSpec((tm, n), lambda i: (i, 0))]
    out_shape = [jax.ShapeDtypeStruct((t, n), F32)]
    if with_normed:
        out_specs.append(pl.BlockSpec((tm, k), lambda i: (i, 0)))
        out_shape.append(jax.ShapeDtypeStruct((t, k), F32))
    outs = pl.pallas_call(
        functools.partial(_norm_mm_kernel, with_normed=with_normed),
        grid=(t // tm,),
        in_specs=[pl.BlockSpec((tm, k), lambda i: (i, col_blk)),
                  pl.BlockSpec((1, k), lambda i: (0, 0)),
                  pl.BlockSpec((k, n), lambda i: (0, 0))],
        out_specs=out_specs, out_shape=out_shape,
        scratch_shapes=[pltpu.VMEM((k, n), BF)],
        compiler_params=_params(1), name=name,
    )(x, gain.reshape(1, k), w)
    return outs if with_normed else outs[0]


def _split3(x):
    hi = x.astype(BF)
    rest = x - hi.astype(F32)
    mid = rest.astype(BF)
    return hi, mid, (rest - mid.astype(F32)).astype(BF)


def _scan_max(x, reverse):
    n = x.shape[0]
    row = lax.broadcasted_iota(jnp.int32, x.shape, 0)
    k = 1
    while k < n:
        if reverse:
            shifted = jnp.where(row < n - k, pltpu.roll(x, n - k, axis=0), -jnp.inf)
        else:
            shifted = jnp.where(row >= k, pltpu.roll(x, k, axis=0), -jnp.inf)
        x = jnp.maximum(x, shifted)
        k *= 2
    return x


def _mlstm_kernel(qf_ref, kf_ref, vf_ref, qb_ref, kb_ref, vb_ref, gcf_ref, gcb_ref, grf_ref, grb_ref,
                  bc_ref, br_ref, *refs):
    hf_ref, hb_ref, c_ref, n_ref, m_ref = refs[-5:]

    @pl.when(pl.program_id(1) == 0)
    def _():
        for state_ref, init_ref in zip((c_ref, n_ref, m_ref), refs[:-5] or (None,) * 3):
            state_ref[...] = jnp.zeros_like(state_ref) if init_ref is None else init_ref[...]

    nh, lc = ML_HEADS, ML_CHUNK
    row = lax.broadcasted_iota(jnp.int32, (lc, lc), 0)
    col = lax.broadcasted_iota(jnp.int32, (lc, lc), 1)
    lower = col <= row
    upper = col >= row
    ones_v = jnp.ones((lc, ML_V), BF)
    directions = ((qf_ref, kf_ref, vf_ref, gcf_ref, grf_ref, hf_ref, lower, upper),
                  (qb_ref, kb_ref, vb_ref, gcb_ref, grb_ref, hb_ref, upper, lower))
    seqs = range(c_ref.shape[0])
    gate_terms = {}
    for r, (d, (_, _, _, gc_ref, gr_ref, _, allowed, allowed_t)) in itertools.product(seqs, enumerate(directions)):
        g_col = gc_ref[r, :, GATE_LANE0:GATE_LANE0 + 4 * nh] + bc_ref[...]
        g_row = gr_ref[r] + br_ref[...]
        i_col = g_col[:, nh * d:nh * (d + 1)]
        f_col = _log_sigmoid(g_col[:, 2 * nh + nh * d:2 * nh + nh * (d + 1)])
        i_row = g_row[nh * d:nh * (d + 1), :]
        f_row = _log_sigmoid(g_row[2 * nh + nh * d:2 * nh + nh * (d + 1), :])
        tri = jnp.where(allowed, 1.0, 0.0).astype(BF)
        tri_t = jnp.where(allowed_t, 1.0, 0.0).astype(BF)
        b_col = sum(jnp.dot(tri, part, preferred_element_type=F32) for part in _split3(f_col))
        b_row = sum(jnp.dot(part, tri_t, preferred_element_type=F32) for part in _split3(f_row))
        b_end = jnp.sum(f_col, axis=0, keepdims=True)
        m_prev = m_ref[r, :, nh * d:nh * (d + 1)]
        m_t = b_col + jnp.maximum(m_prev, _scan_max(i_col - b_col, reverse=d == 1))
        g_col_end = b_end - b_col + i_col
        m_new = jnp.maximum(b_end + m_prev, jnp.max(g_col_end, axis=0, keepdims=True))
        gate_terms[r, d] = dict(
            u=b_col - m_t, r_row=i_row - b_row, w_inter=jnp.exp(b_col + m_prev - m_t), floor=jnp.exp(-m_t),
            k_scale=jnp.exp(g_col_end - m_new), decay=jnp.exp(b_end + m_prev - m_new), m_new=m_new)
    work = []
    for r, (d, (q_ref, k_ref, v_ref, _, _, h_ref, allowed, _)) in itertools.product(seqs, enumerate(directions)):
        for h in range(nh):
            w = dict(r=r, d=d, h=h, h_ref=h_ref, allowed=allowed, g=gate_terms[r, d])
            w["q"] = (q_ref[r, :, h * ML_QK:(h + 1) * ML_QK] * (ML_QK ** -0.5)).astype(BF)
            w["k"] = k_ref[r, :, h * ML_QK:(h + 1) * ML_QK]
            w["v1"] = jnp.concatenate([v_ref[r, :, h * ML_V:(h + 1) * ML_V].astype(BF), ones_v], axis=1)
            work.append(w)
    for w in work:
        w["qk"] = _bdot_nt(w["q"], w["k"])
    for w in work:
        r, d, h = w["r"], w["d"], w["h"]
        w["c_prev"], w["n_prev"] = c_ref[r, d, h], n_ref[r, d, h]
        state = jnp.concatenate([w["c_prev"], w["n_prev"]], axis=1).astype(BF)
        w["q_state"] = jnp.dot(w["q"], state, preferred_element_type=F32)
    for w in work:
        h, g = w["h"], w["g"]
        kw = w["k"] * g["k_scale"][:, h:h + 1]
        w["kv"] = _bdot_tn(kw, w["v1"])
    for w in work:
        h, g = w["h"], w["g"]
        dmat = jnp.where(w["allowed"], g["u"][:, h:h + 1] + g["r_row"][h:h + 1, :], -jnp.inf)
        sw = w["qk"] * jnp.exp(dmat)
        sw_hi = sw.astype(BF)
        sw_lo = (sw - sw_hi.astype(F32)).astype(BF)
        w["pv"] = jnp.dot(sw_hi, w["v1"], preferred_element_type=F32)
        w["den_lo"] = jnp.dot(sw_lo, ones_v, preferred_element_type=F32)
    for w in work:
        r, d, h, g = w["r"], w["d"], w["h"], w["g"]
        w_inter = g["w_inter"][:, h:h + 1]
        num = w_inter * w["q_state"][:, :ML_V] + w["pv"][:, :ML_V]
        den = w_inter * w["q_state"][:, ML_V:] + (w["pv"][:, ML_V:] + w["den_lo"])
        w["h_ref"][r, :, h * ML_V:(h + 1) * ML_V] = num / jnp.maximum(jnp.abs(den), g["floor"][:, h:h + 1])
        decay = g["decay"][:, h:h + 1]
        c_ref[r, d, h] = decay * w["c_prev"] + w["kv"][:, :ML_V]
        n_ref[r, d, h] = decay * w["n_prev"] + w["kv"][:, ML_V:]
    for r in seqs:
        m_ref[r, :, :nh] = gate_terms[r, 0]["m_new"]
        m_ref[r, :, nh:] = gate_terms[r, 1]["m_new"]


def _mlstm(qkvo, side, g_row, b_col, b_row, init, bsz, seq):
    t = qkvo.shape[0]
    nc = seq // ML_CHUNK
    lc = ML_CHUNK
    nq = ML_HEADS * ML_QK
    nv = ML_HEADS * ML_V
    side_blk = (SIDE_COLS - 128) // 128
    rb = min(MLSTM_SEQS, bsz)
    assert bsz % rb == 0
    fwd = lambda c: c
    bwd = lambda c: nc - 1 - c

    def specs(pos):
        return [pl.BlockSpec((rb, lc, nq), lambda b, c: (b, pos(c), 0)),
                pl.BlockSpec((rb, lc, nq), lambda b, c: (b, pos(c), 1)),
                pl.BlockSpec((rb, lc, nv), lambda b, c: (b, pos(c), 1))]

    state = lambda *shape: pl.BlockSpec((rb,) + shape, lambda b, c: (b,) + (0,) * len(shape))
    state_specs = [state(2, ML_HEADS, ML_QK, ML_V), state(2, ML_HEADS, ML_QK, ML_V), state(1, 2 * ML_HEADS)]
    in_specs = specs(fwd) + specs(bwd) + [
        pl.BlockSpec((rb, lc, 128), lambda b, c: (b, fwd(c), side_blk)),
        pl.BlockSpec((rb, lc, 128), lambda b, c: (b, bwd(c), side_blk)),
        pl.BlockSpec((rb, None, 4 * ML_HEADS, lc), lambda b, c: (b, fwd(c), 0, 0)),
        pl.BlockSpec((rb, None, 4 * ML_HEADS, lc), lambda b, c: (b, bwd(c), 0, 0)),
        pl.BlockSpec((1, 4 * ML_HEADS), lambda b, c: (0, 0)),
        pl.BlockSpec((4 * ML_HEADS, 1), lambda b, c: (0, 0))]
    init = () if init is None else tuple(init)
    in_specs += state_specs[:len(init)]
    out_specs = [pl.BlockSpec((rb, lc, nv), lambda b, c: (b, fwd(c), 0)),
                 pl.BlockSpec((rb, lc, nv), lambda b, c: (b, bwd(c), 0))] + state_specs
    out_shape = [jax.ShapeDtypeStruct((bsz, seq, nv), F32), jax.ShapeDtypeStruct((bsz, seq, nv), F32),
                 jax.ShapeDtypeStruct((bsz, 2, ML_HEADS, ML_QK, ML_V), F32),
                 jax.ShapeDtypeStruct((bsz, 2, ML_HEADS, ML_QK, ML_V), F32),
                 jax.ShapeDtypeStruct((bsz, 1, 2 * ML_HEADS), F32)]
    qkvo3, side3 = qkvo.reshape(bsz, seq, -1), side.reshape(bsz, seq, -1)
    g_row4 = g_row.reshape(bsz, nc, 4 * ML_HEADS, lc)
    hf, hb, c_fin, n_fin, m_fin = pl.pallas_call(
        _mlstm_kernel, grid=(bsz // rb, nc), in_specs=in_specs, out_specs=out_specs, out_shape=out_shape,
        compiler_params=_params(2), name="mlstm",
    )(qkvo3, qkvo3, qkvo3, qkvo3, qkvo3, qkvo3, side3, side3, g_row4, g_row4, b_col, b_row, *init)
    return hf.reshape(t, nv), hb.reshape(t, nv), c_fin, n_fin, m_fin


def _mlstm_post_kernel(hf_ref, hb_ref, o_ref, g_ref, y_ref):
    for h in range(ML_HEADS):
        sl = slice(h * ML_V, (h + 1) * ML_V)
        hn = _rms(hf_ref[:, sl] + hb_ref[:, sl]) * g_ref[:, sl]
        y_ref[:, sl] = (hn * _sigmoid(o_ref[:, sl])).astype(BF)


def _mlstm_post(hf, hb, qkvo, gain):
    t, nv = hf.shape
    tm = ROW_TILE
    blk = pl.BlockSpec((tm, nv), lambda i: (i, 0))
    return pl.pallas_call(
        _mlstm_post_kernel, grid=(t // tm,),
        in_specs=[blk, blk, pl.BlockSpec((tm, nv), lambda i: (i, 2)), pl.BlockSpec((1, nv), lambda i: (0, 0))],
        out_specs=blk, out_shape=jax.ShapeDtypeStruct((t, nv), BF),
        compiler_params=_params(1), name="mlstm_post",
    )(hf, hb, qkvo, gain.reshape(1, nv))


LOG2_E = 1.4426950408889634


def _softmax_terms(scores, scale):
    c = scale * LOG2_E
    scaled = [s * c for s in scores]
    m = functools.reduce(jnp.maximum, [jnp.max(s, axis=-1, keepdims=True) for s in scaled])
    e = [jnp.exp2(s - m) for s in scaled]
    den = functools.reduce(lambda a, b: a + b, [jnp.sum(x, axis=-1, keepdims=True) for x in e])
    return [x.astype(BF) for x in e], 1.0 / den


def _walk_heads(n_heads, scores_fn, softmax_fn, values_fn, lag):
    scores, weights = {}, {}
    for t in range(n_heads + 2 * lag):
        if t < n_heads:
            scores[t] = scores_fn(t)
        if 0 <= t - lag < n_heads:
            weights[t - lag] = softmax_fn(t - lag, scores.pop(t - lag))
        if 0 <= t - 2 * lag < n_heads:
            values_fn(t - 2 * lag, weights.pop(t - 2 * lag))


def _mla_kernel(*refs, with_ctx):
    if with_ctx:
        qa_ref, kv_ref, side_ref, cq_ref, sq_ref, ck_ref, sk_ref, kvc_ref, krc_ref, o_ref = refs
    else:
        qa_ref, kv_ref, side_ref, o_ref = refs
    scale = (MLA_NOPE + MLA_ROPE) ** -0.5
    nope_cols = MLA_HEADS * MLA_NOPE
    q_rope = qa_ref[:, nope_cols:]
    k_rope = side_ref[...]
    if with_ctx:
        q_rope_rot = _rope(q_rope, cq_ref[...], sq_ref[...], MLA_ROPE // 4).astype(BF)
        k_rope_rot = _rope(k_rope, ck_ref[...], sk_ref[...], MLA_ROPE // 4)[:, :MLA_ROPE].astype(BF)
        k_rope_ctx = krc_ref[...].astype(BF)
        q_rope = q_rope.astype(BF)
    else:
        q_rope_rot = q_rope.astype(BF)
        k_rope_rot = k_rope[:, :MLA_ROPE].astype(BF)
    heads = range(MLA_HEADS)
    rope_cols = [slice(h * MLA_ROPE, (h + 1) * MLA_ROPE) for h in heads]
    kv0 = [h * (MLA_NOPE + MLA_V) for h in heads]
    q_n = [qa_ref[:, h * MLA_NOPE:(h + 1) * MLA_NOPE].astype(BF) for h in heads]

    def scores(h):
        blocks = [_bdot_nt(q_n[h], kv_ref[:, kv0[h]:kv0[h] + MLA_NOPE])
                  + _bdot_nt(q_rope_rot[:, rope_cols[h]], k_rope_rot)]
        if with_ctx:
            blocks.append(_bdot_nt(q_n[h], kvc_ref[:, kv0[h]:kv0[h] + MLA_NOPE])
                          + _bdot_nt(q_rope[:, rope_cols[h]], k_rope_ctx))
        return blocks

    def weighted_values(h, weights):
        (e, inv_den), v0 = weights, kv0[h] + MLA_NOPE
        out = jnp.dot(e[0], kv_ref[:, v0:v0 + MLA_V].astype(BF), preferred_element_type=F32)
        if with_ctx:
            out = out + jnp.dot(e[1], kvc_ref[:, v0:v0 + MLA_V].astype(BF), preferred_element_type=F32)
        o_ref[:, h * MLA_V:(h + 1) * MLA_V] = (out * inv_den).astype(BF)

    _walk_heads(MLA_HEADS, scores, lambda h, s: _softmax_terms(s, scale), weighted_values,
                lag=2 if with_ctx else MLA_HEADS)


def _mla_attention(qa, kv, side, bsz, seq, ctx=None):
    t = qa.shape[0]
    tq = ATTN_Q_ROWS
    nq = seq // tq
    side_blk = (SIDE_COLS - 128) // 128
    nkv = MLA_HEADS * (MLA_NOPE + MLA_V)
    in_specs = [pl.BlockSpec((tq, qa.shape[1]), lambda b, i: (b * nq + i, 0)),
                pl.BlockSpec((seq, nkv), lambda b, i: (b, 0)),
                pl.BlockSpec((seq, 128), lambda b, i: (b, side_blk))]
    args = [qa, kv, side]
    if ctx is not None:
        cos_q, sin_q, cos_k, sin_k, kvc, krc = ctx
        past = krc.shape[1]
        in_specs += [pl.BlockSpec((tq, cos_q.shape[1]), lambda b, i: (i, 0)),
                     pl.BlockSpec((tq, cos_q.shape[1]), lambda b, i: (i, 0)),
                     pl.BlockSpec((seq, 128), lambda b, i: (0, 0)),
                     pl.BlockSpec((seq, 128), lambda b, i: (0, 0)),
                     pl.BlockSpec((past, nkv), lambda b, i: (b, 0)),
                     pl.BlockSpec((None, past, MLA_ROPE), lambda b, i: (b, 0, 0))]
        args += [cos_q, sin_q, cos_k, sin_k, kvc, krc]
    nout = MLA_HEADS * MLA_V
    return pl.pallas_call(
        functools.partial(_mla_kernel, with_ctx=ctx is not None),
        grid=(bsz, nq), in_specs=in_specs,
        out_specs=pl.BlockSpec((tq, nout), lambda b, i: (b * nq + i, 0)),
        out_shape=jax.ShapeDtypeStruct((t, nout), BF),
        compiler_params=_params(2), name="mla_attention",
    )(*args)


def _gqa_kernel(*refs, with_ctx):
    if with_ctx:
        q_ref, k_ref, v_ref, gq_ref, gk_ref, cq_ref, sq_ref, ck_ref, sk_ref, kc_ref, vc_ref, o_ref, ksrc_ref = refs
    else:
        q_ref, k_ref, v_ref, gq_ref, gk_ref, o_ref, kn_ref, vo_ref, ksrc_ref = refs
    hd = GQA_HEAD_DIM
    scale = hd ** -0.5
    rep = GQA_HEADS // GQA_KV_HEADS
    n_kv = k_ref.shape[1] // hd
    kv_cols = [slice(g * hd, (g + 1) * hd) for g in range(n_kv)]

    @pl.when(pl.program_id(2) == 0)
    def _():
        for g in range(n_kv):
            k_n = _rms(k_ref[:, kv_cols[g]]) * gk_ref[...]
            if with_ctx:
                ksrc_ref[:, kv_cols[g]] = _rope(k_n, ck_ref[...], sk_ref[...], hd // 4).astype(BF)
            else:
                ksrc_ref[:, kv_cols[g]] = k_n.astype(BF)
                kn_ref[g] = k_n
                vo_ref[g] = v_ref[:, kv_cols[g]]

    v = [v_ref[:, kv_cols[g]].astype(BF) for g in range(n_kv)]
    if with_ctx:
        k_ctx, v_ctx = kc_ref[...].astype(BF), vc_ref[...].astype(BF)
    heads = range(n_kv * rep)
    cols = [slice(r * hd, (r + 1) * hd) for r in heads]
    q_n = [_rms(q_ref[:, cols[r]]) * gq_ref[...] for r in heads]
    k_src = [ksrc_ref[:, kv_cols[r // rep]] for r in heads]
    def scores(r):
        if with_ctx:
            return [_bdot_nt(_rope(q_n[r], cq_ref[...], sq_ref[...], hd // 4), k_src[r]), _bdot_nt(q_n[r], k_ctx)]
        return [_bdot_nt(q_n[r], k_src[r])]

    def weighted_values(r, weights):
        e, inv_den = weights
        out = jnp.dot(e[0], v[r // rep], preferred_element_type=F32)
        if with_ctx:
            out = out + jnp.dot(e[1], v_ctx, preferred_element_type=F32)
        o_ref[:, cols[r]] = (out * inv_den).astype(BF)

    _walk_heads(len(heads), scores, lambda r, s: _softmax_terms(s, scale), weighted_values, lag=1)


def _gqa_attention(qkv, g_q, g_k, bsz, seq, ctx=None):
    t = qkv.shape[0]
    hd = GQA_HEAD_DIM
    rep = GQA_HEADS // GQA_KV_HEADS
    tq = ATTN_Q_ROWS
    nq = seq // tq
    n_kv = 1 if ctx is not None else GQA_KV_HEADS
    k_blk0 = GQA_HEADS // n_kv
    in_specs = [pl.BlockSpec((tq, n_kv * rep * hd), lambda b, g, i: (b * nq + i, g)),
                pl.BlockSpec((seq, n_kv * hd), lambda b, g, i: (b, k_blk0 + g)),
                pl.BlockSpec((seq, n_kv * hd), lambda b, g, i: (b, k_blk0 + GQA_KV_HEADS // n_kv + g)),
                pl.BlockSpec((1, hd), lambda b, g, i: (0, 0)),
                pl.BlockSpec((1, hd), lambda b, g, i: (0, 0))]
    args = [qkv, qkv, qkv, g_q.reshape(1, hd), g_k.reshape(1, hd)]
    o_spec = pl.BlockSpec((tq, n_kv * rep * hd), lambda b, g, i: (b * nq + i, g))
    o_shape = jax.ShapeDtypeStruct((t, GQA_HEADS * hd), BF)
    if ctx is not None:
        cos_t, sin_t, kc, vc, j = ctx
        past = kc.shape[3]
        cache = pl.BlockSpec((None, None, None, past, hd), lambda b, g, i: (b, j, g, 0, 0))
        in_specs += [pl.BlockSpec((tq, hd), lambda b, g, i: (i, 0)),
                     pl.BlockSpec((tq, hd), lambda b, g, i: (i, 0)),
                     pl.BlockSpec((seq, hd), lambda b, g, i: (0, 0)),
                     pl.BlockSpec((seq, hd), lambda b, g, i: (0, 0)),
                     cache, cache]
        args += [cos_t, sin_t, cos_t, sin_t, kc, vc]
        out_specs, out_shape = o_spec, o_shape
    else:
        head_major = pl.BlockSpec((None, n_kv, seq, hd), lambda b, g, i: (b, g, 0, 0))
        out_specs = [o_spec, head_major, head_major]
        kv_shape = jax.ShapeDtypeStruct((bsz, GQA_KV_HEADS, seq, hd), F32)
        out_shape = [o_shape, kv_shape, kv_shape]
    return pl.pallas_call(
        functools.partial(_gqa_kernel, with_ctx=ctx is not None),
        grid=(bsz, GQA_KV_HEADS // n_kv, nq), in_specs=in_specs, out_specs=out_specs, out_shape=out_shape,
        scratch_shapes=[pltpu.VMEM((seq, n_kv * hd), BF)],
        compiler_params=_params(3), name="gqa_attention",
    )(*args)


def _dispatch_kernel(ar_ref, ac_ref, h_ref, xe_ref, gate_ref, rankc_ref, rankr_ref, *, cap):
    n_groups, group, s = ar_ref.shape
    g = pl.program_id(1)

    @pl.when(g == 0)
    def _():
        blk = min(RANK_BLOCK, s)
        rankr_ref[...] = jnp.zeros_like(rankr_ref)
        ones_sub = jnp.ones((8, blk), BF)
        ones_lane = jnp.ones((s, 128), BF)
        for j in range(s // blk):
            rows = pl.ds(j * blk, blk)
            i0 = lax.broadcasted_iota(jnp.int32, (blk, s), 0) + j * blk
            i1 = lax.broadcasted_iota(jnp.int32, (blk, s), 1)
            sub_first = jnp.where(i0 < i1, 1.0, 0.0)
            for e in range(n_groups * group):
                eg, ei = e // group, e % group
                a_row = ar_ref[eg, ei:ei + 1, :]
                a_col = ac_ref[rows, e:e + 1]
                ahead = jnp.where(a_col > a_row, 1.0, jnp.where(a_col >= a_row, sub_first, 0.0)).astype(BF)
                rankr_ref[eg, ei:ei + 1, :] += jnp.dot(ones_sub, ahead, preferred_element_type=F32)[:1, :]
                rankc_ref[rows, e:e + 1] = (s - 1.0) - jnp.dot(ahead, ones_lane, preferred_element_type=F32)[:, :1]

    slot = lax.broadcasted_iota(jnp.int32, (group, cap, s), 1).astype(F32)
    pick = rankr_ref[g][:, None, :] == slot
    onehot = jnp.where(pick, 1.0, 0.0).reshape(group * cap, s).astype(BF)
    rows_f32 = jnp.dot(onehot, h_ref[...], preferred_element_type=F32)
    xe_ref[...] = rows_f32.reshape(group, cap, -1).astype(BF)
    gate_ref[...] = jnp.sum(jnp.where(pick, ar_ref[g][:, None, :], 0.0), axis=2, keepdims=True)


def _dispatch(aff_row, aff_col, h, bsz, seq):
    t, d = h.shape
    ne = aff_row.shape[0]
    cap = EC_FACTOR * seq // ne
    group = GATHER_ROWS // cap
    n_groups = ne // group
    return pl.pallas_call(
        functools.partial(_dispatch_kernel, cap=cap),
        grid=(bsz, n_groups),
        in_specs=[pl.BlockSpec((n_groups, group, seq), lambda b, g: (0, 0, b)),
                  pl.BlockSpec((seq, ne), lambda b, g: (b, 0)),
                  pl.BlockSpec((seq, d), lambda b, g: (b, 0))],
        out_specs=[pl.BlockSpec((group, cap, d), lambda b, g: (g, b, 0)),
                   pl.BlockSpec((group, cap, 1), lambda b, g: (g, b, 0)),
                   pl.BlockSpec((seq, ne), lambda b, g: (b, 0))],
        out_shape=[jax.ShapeDtypeStruct((ne, bsz * cap, d), BF),
                   jax.ShapeDtypeStruct((ne, bsz * cap, 1), F32),
                   jax.ShapeDtypeStruct((t, ne), F32)],
        scratch_shapes=[pltpu.VMEM((n_groups, group, seq), F32)],
        compiler_params=_params(2), name="ec_dispatch",
    )(aff_row.reshape(n_groups, group, t), aff_col, h)


def _experts_kernel(xp_ref, xs_ref, gp_ref, gs_ref, wg_ref, wu_ref, wd_ref, yp_ref, ys_ref, accp_ref, accs_ref):
    f = pl.program_id(1)
    tf = wg_ref.shape[1]
    w_gate_up = jnp.concatenate([wg_ref[...].astype(BF), wu_ref[...].astype(BF)], axis=1)
    wd = wd_ref[...].astype(BF)
    groups = ((xp_ref, gp_ref, yp_ref, accp_ref), (xs_ref, gs_ref, ys_ref, accs_ref))

    @pl.when(f == 0)
    def _():
        for _, _, _, acc_ref in groups:
            acc_ref[...] = jnp.zeros_like(acc_ref)

    for x_ref, _, _, acc_ref in groups:
        for r in range(0, x_ref.shape[0], EXPERT_ROWS):
            rows = slice(r, r + EXPERT_ROWS)
            au = jnp.dot(x_ref[rows, :], w_gate_up, preferred_element_type=F32)
            a, u = au[:, :tf], au[:, tf:]
            acc_ref[rows, :] += jnp.dot((a * _sigmoid(a) * u).astype(BF), wd, preferred_element_type=F32)

    @pl.when(f == pl.num_programs(1) - 1)
    def _():
        for _, g_ref, y_ref, acc_ref in groups:
            y_ref[...] = (acc_ref[...] * g_ref[...]).astype(BF)


def _experts(xe_p, xe_s, gate_p, gate_s, w_gate, w_up, w_down, layer):
    ne, mp, d = xe_p.shape
    ms = xe_s.shape[1]
    fdim = w_gate.shape[-1]
    tf = EXPERT_HIDDEN_COLS
    rows = lambda m, last: pl.BlockSpec((None, m, last), lambda e, f: (e, 0, 0))
    return pl.pallas_call(
        _experts_kernel, grid=(ne, fdim // tf),
        in_specs=[rows(mp, d), rows(ms, d), rows(mp, 1), rows(ms, 1),
                  pl.BlockSpec((None, None, d, tf), lambda e, f: (layer, e, 0, f)),
                  pl.BlockSpec((None, None, d, tf), lambda e, f: (layer, e, 0, f)),
                  pl.BlockSpec((None, None, tf, d), lambda e, f: (layer, e, f, 0))],
        out_specs=[rows(mp, d), rows(ms, d)],
        out_shape=[jax.ShapeDtypeStruct((ne, mp, d), BF), jax.ShapeDtypeStruct((ne, ms, d), BF)],
        scratch_shapes=[pltpu.VMEM((mp, d), F32), pltpu.VMEM((ms, d), F32)],
        compiler_params=_params(2), name="ec_experts",
    )(xe_p, xe_s, gate_p, gate_s, w_gate, w_up, w_down)


RANK_RADIX = 32


def _combine_kernel(y_ref, rank_ref, x_ref, g_ref, *refs, cap, final):
    post_refs, place_ref = refs[:-1], refs[-1]
    ne = y_ref.shape[0]
    n_slots = ne * cap

    @pl.when(pl.program_id(1) == 0)
    def _():
        rank = rank_ref[...]
        hi = jnp.floor(rank * (1.0 / RANK_RADIX))
        lo = rank - RANK_RADIX * hi
        lane = lax.broadcasted_iota(jnp.int32, (ne, n_slots), 1)
        expert = lax.broadcasted_iota(jnp.int32, (ne, n_slots), 0)
        own = (lane >= expert * cap) & (lane < (expert + 1) * cap)
        spread_hi = jnp.where(own, float(RANK_RADIX), 0.0).astype(BF)
        spread_lo = jnp.where(own, 1.0, 0.0).astype(BF)
        spread = (jnp.dot(hi.astype(BF), spread_hi, preferred_element_type=F32)
                  + jnp.dot(lo.astype(BF), spread_lo, preferred_element_type=F32))
        slot = (lax.broadcasted_iota(jnp.int32, (1, n_slots), 1) & (cap - 1)).astype(F32)
        place_ref[...] = jnp.where(spread == slot, 1.0, 0.0).astype(BF)

    ts = x_ref.shape[0]
    rows = pl.ds(pl.multiple_of(pl.program_id(1) * ts, ts), ts)
    y = y_ref[...].reshape(n_slots, y_ref.shape[2])
    x_new = x_ref[...] + g_ref[...] * jnp.dot(place_ref[rows, :], y, preferred_element_type=F32)
    if final:
        (gain_ref,), (out_ref,) = post_refs[:1], post_refs[1:]
        out_ref[...] = _rms(x_new) * gain_ref[...]
    else:
        (sh_ref, sc_ref), (xo_ref, h_ref) = post_refs[:2], post_refs[2:]
        xo_ref[...] = x_new
        h_ref[...] = (_rms(x_new) * (1.0 + sc_ref[...]) + sh_ref[...]).astype(BF)


def _combine(y, rank, x, gate, bsz, seq, next_shift_scale=None, final_gain=None):
    t, d = x.shape
    ne = y.shape[0]
    cap = EC_FACTOR * seq // ne
    assert seq <= RANK_RADIX * RANK_RADIX and cap & (cap - 1) == 0
    ts = min(COMBINE_ROWS, seq)
    nt = seq // ts
    final = final_gain is not None
    vec = lambda v: pl.BlockSpec((None, 1, d), lambda b, i: (b if v.shape[0] > 1 else 0, 0, 0))
    tile = pl.BlockSpec((ts, d), lambda b, i: (b * nt + i, 0))
    in_specs = [pl.BlockSpec((ne, cap, d), lambda b, i: (0, b, 0)),
                pl.BlockSpec((seq, ne), lambda b, i: (b, 0)), tile, vec(gate)]
    if final:
        args = [final_gain.reshape(1, d)]
        in_specs.append(pl.BlockSpec((1, d), lambda b, i: (0, 0)))
        out_specs, out_shape = tile, jax.ShapeDtypeStruct((t, d), F32)
    else:
        args = list(next_shift_scale)
        in_specs += [vec(v) for v in args]
        out_specs = [tile, tile]
        out_shape = [jax.ShapeDtypeStruct((t, d), F32), jax.ShapeDtypeStruct((t, d), BF)]
    return pl.pallas_call(
        functools.partial(_combine_kernel, cap=cap, final=final),
        grid=(bsz, nt), in_specs=in_specs, out_specs=out_specs, out_shape=out_shape,
        scratch_shapes=[pltpu.VMEM((seq, ne * cap), BF)],
        compiler_params=_params(2), name="ec_combine",
    )(y, rank, x, gate, *args)


def _rope_tables(n_tokens, rot_dim):
    rows = n_tokens // GRID_W
    row = jnp.repeat(jnp.arange(rows), GRID_W).astype(F32)
    col = jnp.tile(jnp.arange(GRID_W), rows).astype(F32)
    quarter = rot_dim // 4
    inv = ROPE_THETA ** (-jnp.arange(quarter, dtype=F32) / quarter)
    a_row, a_col = row[:, None] * inv, col[:, None] * inv
    cos_t = jnp.concatenate([jnp.cos(a_row), jnp.cos(a_row), jnp.cos(a_col), jnp.cos(a_col)], axis=-1)
    sin_t = jnp.concatenate([-jnp.sin(a_row), jnp.sin(a_row), -jnp.sin(a_col), jnp.sin(a_col)], axis=-1)
    return cos_t, sin_t


def kernel(x_prompt, x_sample, state_mlstm_c, state_mlstm_n, state_mlstm_m, cache_mla_ckv, cache_mla_krope,
           cache_gqa_k, cache_gqa_v, c, c_ctx, w_mod, b_mod, w_in_even, b_igate, b_fgate, g_mlstm, g_cq, w_uq,
           g_ckv, w_ukv, w_out_even, w_in_odd, g_qnorm, g_knorm, w_out_odd, w_router, w_expert_gate,
           w_expert_up, w_expert_down, g_final):
    d = D_MODEL
    bp, sp, _ = x_prompt.shape
    bs, ss, _ = x_sample.shape
    depth = w_mod.shape[0]
    nh = ML_HEADS
    streams = {"p": (bp, sp), "s": (bs, ss)}
    x = {"p": x_prompt.reshape(bp * sp, d), "s": x_sample.reshape(bs * ss, d)}

    c8 = jnp.concatenate([c_ctx[None], c, jnp.zeros((8 - 1 - bs, d), F32)], axis=0)
    mod_all = _mod_vectors(c8, w_mod, b_mod).reshape(depth, 8, 6, 1, d)

    def mod(layer, key, idx):
        rows = mod_all[layer, 0:1, idx] if key == "p" else mod_all[layer, 1:1 + bs, idx]
        return rows

    new_even, new_odd = [], []
    h_in = {key: _norm_mod(x[key], mod(0, key, 0), mod(0, key, 1), streams[key][1]) for key in streams}
    routed_in = {}
    for layer in range(depth):
        j = layer // 2
        if layer % 2 == 0:
            w_out = _to_bf16(w_out_even, j)
            w_in_rows = jnp.swapaxes(w_in_even, 1, 2)
            w_side = jnp.concatenate(
                [w_in_rows[j, QKVO_COLS + 4 * nh:],
                 w_in_rows[j, QKVO_COLS:QKVO_COLS + 4 * nh],
                 jnp.zeros((128 - MLA_ROPE - 4 * nh, d), F32)], axis=0)[None]
            w_q = w_uq[j].reshape(MLA_Q_RANK, MLA_HEADS, MLA_NOPE + MLA_ROPE)
            w_q = jnp.concatenate([w_q[:, :, :MLA_NOPE].reshape(MLA_Q_RANK, -1),
                                   w_q[:, :, MLA_NOPE:].reshape(MLA_Q_RANK, -1)], axis=1)
            bias_col = jnp.concatenate([b_igate[j].reshape(1, -1), b_fgate[j].reshape(1, -1)], axis=1)
            bias_row = bias_col.reshape(-1, 1)
            cos64, sin64 = _rope_tables(ss, MLA_ROPE)
            cos_q, sin_q = jnp.tile(cos64, (1, MLA_HEADS)), jnp.tile(sin64, (1, MLA_HEADS))
            pad = jnp.zeros((ss, 128 - MLA_ROPE), F32)
            cos_k, sin_k = jnp.concatenate([cos64, pad], axis=1), jnp.concatenate([sin64, pad], axis=1)
            kvc = _mm_resident(cache_mla_ckv[:, j].reshape(-1, MLA_KV_RANK), w_ukv, j, w_ukv.shape[-1], IN_PROJ_COLS,
                               name="mla_ctx_expand")
            for key, (bsz, seq) in streams.items():
                h = h_in[key]
                qkvo = _mm_resident(h, w_in_rows, j, QKVO_COLS, IN_PROJ_COLS, w_rows=True, name="even_in_main")
                side = _mm_resident(h, w_side, 0, SIDE_COLS, SIDE_COLS // 3, w_rows=True, name="even_in_side")
                gates = side[:, SIDE_COLS - 128 + GATE_LANE0:SIDE_COLS - 128 + GATE_LANE0 + 4 * nh]
                g_row = gates.reshape(-1, ML_CHUNK, 4 * nh).transpose(0, 2, 1)
                if key == "p":
                    init = None
                else:
                    c0 = state_mlstm_c[:, j]
                    init = (c0, jnp.broadcast_to(state_mlstm_n[:, j][..., None], c0.shape),
                            state_mlstm_m[:, j].reshape(bsz, 1, 2 * nh))
                hf, hb, c_fin, n_fin, m_fin = _mlstm(qkvo, side, g_row, bias_col, bias_row, init, bsz, seq)
                y_ml = _mlstm_post(hf, hb, qkvo, g_mlstm[j])
                qa = _norm_mm(side, 0, g_cq[j], w_q, False, "mla_q_up")
                kv, ckv_n = _norm_mm(side, 1, g_ckv[j], w_ukv[j], True, "mla_kv_up")
                if key == "p":
                    y_a = _mla_attention(qa, kv, side, bsz, seq)
                    k_rope = side[:, SIDE_COLS - 128:SIDE_COLS - 128 + MLA_ROPE]
                    new_even.append((c_fin, n_fin[..., 0], m_fin.reshape(bsz, 2, nh),
                                     ckv_n.reshape(bsz, seq, -1), k_rope.reshape(bsz, seq, -1)))
                else:
                    y_a = _mla_attention(qa, kv, side, bsz, seq,
                                         ctx=(cos_q, sin_q, cos_k, sin_k, kvc, cache_mla_krope[:, j]))
                routed_in[key] = _out_proj_router([y_ml, y_a], w_out, x[key], mod(layer, key, 2), mod(layer, key, 3),
                                                  mod(layer, key, 4), w_router, layer, seq)
        else:
            w_out = _to_bf16(w_out_odd, j)
            cos_t, sin_t = _rope_tables(ss, GQA_HEAD_DIM)
            for key, (bsz, seq) in streams.items():
                qkv = _mm_resident(h_in[key], w_in_odd, j, w_in_odd.shape[-1], IN_PROJ_COLS, name="odd_in")
                if key == "p":
                    o, k_n, v = _gqa_attention(qkv, g_qnorm[j], g_knorm[j], bsz, seq)
                    new_odd.append((k_n, v))
                else:
                    o = _gqa_attention(qkv, g_qnorm[j], g_knorm[j], bsz, seq,
                                       ctx=(cos_t, sin_t, cache_gqa_k, cache_gqa_v, j))
                routed_in[key] = _out_proj_router([o], w_out, x[key], mod(layer, key, 2), mod(layer, key, 3),
                                                  mod(layer, key, 4), w_router, layer, seq)
        routed = {}
        for key, (bsz, seq) in streams.items():
            x[key], h, aff = routed_in[key]
            routed[key] = _dispatch(aff.T, aff, h, bsz, seq)
        y_p, y_s = _experts(routed["p"][0], routed["s"][0], routed["p"][1], routed["s"][1],
                            w_expert_gate, w_expert_up, w_expert_down, layer)
        for key, y in (("p", y_p), ("s", y_s)):
            bsz, seq = streams[key]
            if layer + 1 < depth:
                x[key], h_in[key] = _combine(
                    y, routed[key][2], x[key], mod(layer, key, 5), bsz, seq,
                    next_shift_scale=(mod(layer + 1, key, 0), mod(layer + 1, key, 1)))
            else:
                x[key] = _combine(y, routed[key][2], x[key], mod(layer, key, 5), bsz, seq, final_gain=g_final)

    y_prompt = x["p"].reshape(bp, sp, d)
    y_sample = x["s"].reshape(bs, ss, d)
    new_c = jnp.stack([e[0] for e in new_even], axis=1)
    new_n = jnp.stack([e[1] for e in new_even], axis=1)
    new_m = jnp.stack([e[2] for e in new_even], axis=1)
    new_ckv = jnp.stack([e[3] for e in new_even], axis=1)
    new_krope = jnp.stack([e[4] for e in new_even], axis=1)
    new_k = jnp.stack([e[0] for e in new_odd], axis=1)
    new_v = jnp.stack([e[1] for e in new_odd], axis=1)
    return (y_prompt, y_sample, new_c, new_n, new_m, new_ckv, new_krope, new_k, new_v)
```

```python
import functools
import itertools

import jax
import jax.numpy as jnp
from jax import lax
from jax.experimental import pallas as pl
from jax.experimental.pallas import tpu as pltpu

BF = jnp.bfloat16
F32 = jnp.float32

D_MODEL = 2048
GRID_W = 64
ROPE_THETA = 10000.0
NORM_EPS = 1e-6
ML_HEADS = 8
ML_QK = 64
ML_V = 128
ML_CHUNK = 64
MLA_HEADS = 8
MLA_Q_RANK = 512
MLA_KV_RANK = 512
MLA_NOPE = 128
MLA_ROPE = 64
MLA_V = 128
GQA_HEADS = 16
GQA_KV_HEADS = 4
GQA_HEAD_DIM = 128
N_EXPERTS = 16
EXPERT_DIM = 1024
EC_FACTOR = 2

QKVO_COLS = 2 * ML_HEADS * ML_QK + 2 * ML_HEADS * ML_V
SIDE_COLS = MLA_Q_RANK + MLA_KV_RANK + 128
GATE_LANE0 = MLA_ROPE

VMEM_LIMIT_BYTES = 56 * 1024 * 1024
OUT_PROJ_ROWS = 512
MLSTM_SEQS = 4
MOD_COLS = 1024
IN_PROJ_COLS = 512
IN_PROJ_ROW_CHUNK = 512
CAST_ROWS = 512
ROW_TILE = 512
ATTN_Q_ROWS = 256
EXPERT_HIDDEN_COLS = 512
EXPERT_ROWS = 256
RANK_BLOCK = 256
GATHER_ROWS = 512
COMBINE_ROWS = 256


def _params(n_axes):
    return pltpu.CompilerParams(dimension_semantics=("arbitrary",) * n_axes,
                                vmem_limit_bytes=VMEM_LIMIT_BYTES)


def _bdot(a, b):
    return jnp.dot(a.astype(BF), b.astype(BF), preferred_element_type=F32)


def _bdot_nt(a, b):
    return lax.dot_general(a.astype(BF), b.astype(BF), (((1,), (1,)), ((), ())),
                           preferred_element_type=F32)


def _bdot_tn(a, b):
    return lax.dot_general(a.astype(BF), b.astype(BF), (((0,), (0,)), ((), ())),
                           preferred_element_type=F32)


def _sigmoid(x):
    return 1.0 / (1.0 + jnp.exp(-x))


def _log_sigmoid(x):
    return jnp.minimum(x, 0.0) - jnp.log1p(jnp.exp(-jnp.abs(x)))


def _rms(x):
    return x * lax.rsqrt(jnp.mean(x * x, axis=-1, keepdims=True) + NORM_EPS)


def _rope(x, cos_t, sin_t, quarter):
    width = x.shape[-1]
    axis = x.ndim - 1
    lane = lax.broadcasted_iota(jnp.int32, x.shape, axis)
    partner = jnp.where((lane & quarter) == 0,
                        pltpu.roll(x, width - quarter, axis=axis),
                        pltpu.roll(x, quarter, axis=axis))
    return x * cos_t + partner * sin_t


def _mod_kernel(c_ref, w_ref, b_ref, o_ref):
    c = c_ref[...]
    o_ref[...] = _bdot(c * _sigmoid(c), w_ref[...]) + b_ref[...]


def _mod_vectors(c8, w_mod, b_mod):
    n_layers, k, n = w_mod.shape
    tn = MOD_COLS
    return pl.pallas_call(
        _mod_kernel,
        grid=(n_layers, n // tn),
        in_specs=[pl.BlockSpec((8, k), lambda l, j: (0, 0)),
                  pl.BlockSpec((None, k, tn), lambda l, j: (l, 0, j)),
                  pl.BlockSpec((None, 1, tn), lambda l, j: (l, 0, j))],
        out_specs=pl.BlockSpec((None, 8, tn), lambda l, j: (l, 0, j)),
        out_shape=jax.ShapeDtypeStruct((n_layers, 8, n), F32),
        compiler_params=_params(2),
        name="mod_vectors",
    )(c8, w_mod, b_mod.reshape(n_layers, 1, n))


def _norm_mod_kernel(x_ref, sh_ref, sc_ref, h_ref):
    h_ref[...] = (_rms(x_ref[...]) * (1.0 + sc_ref[...]) + sh_ref[...]).astype(BF)


def _batch_of_tile(n_vectors, tm, seq):
    if n_vectors == 1:
        return lambda i: 0
    assert seq % tm == 0
    return lambda i: (i * tm) // seq


def _norm_mod(x, shift, scale, seq):
    t, d = x.shape
    tm = OUT_PROJ_ROWS
    which = _batch_of_tile(shift.shape[0], tm, seq)
    vec = pl.BlockSpec((None, 1, d), lambda i: (which(i), 0, 0))
    x_spec = pl.BlockSpec((tm, d), lambda i: (i, 0))
    return pl.pallas_call(
        _norm_mod_kernel, grid=(t // tm,),
        in_specs=[x_spec, vec, vec], out_specs=x_spec,
        out_shape=jax.ShapeDtypeStruct((t, d), BF),
        compiler_params=_params(1), name="norm_mod",
    )(x, shift, scale)


def _to_bf16_kernel(w_ref, o_ref):
    o_ref[...] = w_ref[...].astype(BF)


def _to_bf16(w3, layer):
    _, k, n = w3.shape
    tk = CAST_ROWS
    return pl.pallas_call(
        _to_bf16_kernel, grid=(k // tk,),
        in_specs=[pl.BlockSpec((None, tk, n), lambda i: (layer, i, 0))],
        out_specs=pl.BlockSpec((tk, n), lambda i: (i, 0)),
        out_shape=jax.ShapeDtypeStruct((k, n), BF),
        compiler_params=_params(1), name="weight_to_bf16",
    )(w3)


def _out_proj_router_kernel(*refs, k_sizes):
    n_a = len(k_sizes)
    a_refs = refs[:n_a]
    w_ref, x_ref, g_ref, sh_ref, sc_ref, wr_ref, xo_ref, h_ref, aff_ref, afft_ref = refs[n_a:]
    w_router = wr_ref[...].astype(BF)
    chunk = OUT_PROJ_ROWS // 2
    for r in range(0, x_ref.shape[0], chunk):
        rows = slice(r, r + chunk)
        acc, off = None, 0
        for a_ref, ks in zip(a_refs, k_sizes):
            term = jnp.dot(a_ref[rows, :], w_ref[off:off + ks, :], preferred_element_type=F32)
            acc = term if acc is None else acc + term
            off += ks
        x_new = x_ref[rows, :] + g_ref[...] * acc
        xo_ref[rows, :] = x_new
        h = (_rms(x_new) * (1.0 + sc_ref[...]) + sh_ref[...]).astype(BF)
        h_ref[rows, :] = h
        logits = jnp.dot(h, w_router, preferred_element_type=F32)
        e = jnp.exp(logits - jnp.max(logits, axis=-1, keepdims=True))
        aff = e / jnp.sum(e, axis=-1, keepdims=True)
        aff_ref[rows, :] = aff
        afft_ref[:, rows] = aff.T


def _out_proj_router(a_list, w_bf, x, gate, shift, scale, w_router, layer, seq):
    m, d = x.shape
    k_sizes = tuple(a.shape[1] for a in a_list)
    assert w_bf.shape == (sum(k_sizes), d) and all(a.dtype == BF for a in a_list)
    tm = OUT_PROJ_ROWS
    ne = w_router.shape[-1]
    which = _batch_of_tile(gate.shape[0], tm, seq)
    vec = pl.BlockSpec((None, 1, d), lambda i: (which(i), 0, 0))
    tile = pl.BlockSpec((tm, d), lambda i: (i, 0))
    in_specs = [pl.BlockSpec((tm, ks), lambda i: (i, 0)) for ks in k_sizes]
    in_specs += [pl.BlockSpec(w_bf.shape, lambda i: (0, 0), pipeline_mode=pl.Buffered(1)),
                 tile, vec, vec, vec, pl.BlockSpec((None, d, ne), lambda i: (layer, 0, 0))]
    return pl.pallas_call(
        functools.partial(_out_proj_router_kernel, k_sizes=k_sizes),
        grid=(m // tm,), in_specs=in_specs,
        out_specs=[tile, tile, pl.BlockSpec((tm, ne), lambda i: (i, 0)), pl.BlockSpec((ne, tm), lambda i: (0, i))],
        out_shape=[jax.ShapeDtypeStruct((m, d), F32), jax.ShapeDtypeStruct((m, d), BF),
                   jax.ShapeDtypeStruct((m, ne), F32), jax.ShapeDtypeStruct((ne, m), F32)],
        compiler_params=_params(1), name="out_proj_router",
    )(*a_list, w_bf, x, gate, shift, scale, w_router)


def _mm_resident_kernel(a_ref, w_ref, o_ref, *, w_rows, chunk):
    w = w_ref[...].astype(BF)
    for r in range(0, a_ref.shape[0], chunk):
        rows = slice(r, r + chunk)
        a = a_ref[rows, :].astype(BF)
        if w_rows:
            o_ref[rows, :] = lax.dot_general(a, w, (((1,), (1,)), ((), ())), preferred_element_type=F32)
        else:
            o_ref[rows, :] = jnp.dot(a, w, preferred_element_type=F32)


def _mm_resident(a, w3, layer, n_cols, tn, w_rows=False, name="mm_resident"):
    m, k = a.shape
    assert w3.shape[2 if w_rows else 1] == k and n_cols % tn == 0
    if w_rows:
        w_spec = pl.BlockSpec((None, tn, k), lambda j: (layer, j, 0))
    else:
        w_spec = pl.BlockSpec((None, k, tn), lambda j: (layer, 0, j))
    return pl.pallas_call(
        functools.partial(_mm_resident_kernel, w_rows=w_rows, chunk=min(IN_PROJ_ROW_CHUNK, m)),
        grid=(n_cols // tn,),
        in_specs=[pl.BlockSpec((m, k), lambda j: (0, 0), pipeline_mode=pl.Buffered(1)), w_spec],
        out_specs=pl.BlockSpec((m, tn), lambda j: (0, j)),
        out_shape=jax.ShapeDtypeStruct((m, n_cols), F32),
        compiler_params=_params(1), name=name,
    )(a, w3)


def _norm_mm_kernel(x_ref, g_ref, w_ref, *out_refs, with_normed):
    wbf_ref = out_refs[-1]

    @pl.when(pl.program_id(0) == 0)
    def _():
        wbf_ref[...] = w_ref[...].astype(BF)

    xn = _rms(x_ref[...]) * g_ref[...]
    out_refs[0][...] = jnp.dot(xn.astype(BF), wbf_ref[...], preferred_element_type=F32)
    if with_normed:
        out_refs[1][...] = xn


def _norm_mm(x, col_blk, gain, w, with_normed, name):
    t = x.shape[0]
    k, n = w.shape
    tm = ROW_TILE
    out_specs = [pl.BlockSpec((tm, n), lambda i: (i, 0))]
    out_shape = [jax.ShapeDtypeStruct((t, n), F32)]
    if with_normed:
        out_specs.append(pl.BlockSpec((tm, k), lambda i: (i, 0)))
        out_shape.append(jax.ShapeDtypeStruct((t, k), F32))
    outs = pl.pallas_call(
        functools.partial(_norm_mm_kernel, with_normed=with_normed),
        grid=(t // tm,),
        in_specs=[pl.BlockSpec((tm, k), lambda i: (i, col_blk)),
                  pl.BlockSpec((1, k), lambda i: (0, 0)),
                  pl.BlockSpec((k, n), lambda i: (0, 0))],
        out_specs=out_specs, out_shape=out_shape,
        scratch_shapes=[pltpu.VMEM((k, n), BF)],
        compiler_params=_params(1), name=name,
    )(x, gain.reshape(1, k), w)
    return outs if with_normed else outs[0]


def _split3(x):
    hi = x.astype(BF)
    rest = x - hi.astype(F32)
    mid = rest.astype(BF)
    return hi, mid, (rest - mid.astype(F32)).astype(BF)


def _scan_max(x, reverse):
    n = x.shape[0]
    row = lax.broadcasted_iota(jnp.int32, x.shape, 0)
    k = 1
    while k < n:
        if reverse:
            shifted = jnp.where(row < n - k, pltpu.roll(x, n - k, axis=0), -jnp.inf)
        else:
            shifted = jnp.where(row >= k, pltpu.roll(x, k, axis=0), -jnp.inf)
        x = jnp.maximum(x, shifted)
        k *= 2
    return x


def _mlstm_kernel(qf_ref, kf_ref, vf_ref, qb_ref, kb_ref, vb_ref, gcf_ref, gcb_ref, grf_ref, grb_ref,
                  bc_ref, br_ref, *refs):
    hf_ref, hb_ref, c_ref, n_out_ref, m_ref, n_ref = refs[-6:]

    @pl.when(pl.program_id(1) == 0)
    def _():
        for state_ref, init_ref in zip((c_ref, n_ref, m_ref), refs[:-6] or (None,) * 3):
            state_ref[...] = jnp.zeros_like(state_ref) if init_ref is None else init_ref[...]

    nh, lc = ML_HEADS, ML_CHUNK
    row = lax.broadcasted_iota(jnp.int32, (lc, lc), 0)
    col = lax.broadcasted_iota(jnp.int32, (lc, lc), 1)
    lower = col <= row
    upper = col >= row
    ones_v = jnp.ones((lc, ML_V), BF)
    directions = ((qf_ref, kf_ref, vf_ref, gcf_ref, grf_ref, hf_ref, lower, upper),
                  (qb_ref, kb_ref, vb_ref, gcb_ref, grb_ref, hb_ref, upper, lower))
    seqs = range(c_ref.shape[0])
    gate_terms = {}
    for r, (d, (_, _, _, gc_ref, gr_ref, _, allowed, allowed_t)) in itertools.product(seqs, enumerate(directions)):
        g_col = gc_ref[r, :, GATE_LANE0:GATE_LANE0 + 4 * nh] + bc_ref[...]
        g_row = gr_ref[r] + br_ref[...]
        i_col = g_col[:, nh * d:nh * (d + 1)]
        f_col = _log_sigmoid(g_col[:, 2 * nh + nh * d:2 * nh + nh * (d + 1)])
        i_row = g_row[nh * d:nh * (d + 1), :]
        f_row = _log_sigmoid(g_row[2 * nh + nh * d:2 * nh + nh * (d + 1), :])
        tri = jnp.where(allowed, 1.0, 0.0).astype(BF)
        tri_t = jnp.where(allowed_t, 1.0, 0.0).astype(BF)
        b_col = sum(jnp.dot(tri, part, preferred_element_type=F32) for part in _split3(f_col))
        b_row = sum(jnp.dot(part, tri_t, preferred_element_type=F32) for part in _split3(f_row))
        b_end = jnp.sum(f_col, axis=0, keepdims=True)
        m_prev = m_ref[r, :, nh * d:nh * (d + 1)]
        m_t = b_col + jnp.maximum(m_prev, _scan_max(i_col - b_col, reverse=d == 1))
        g_col_end = b_end - b_col + i_col
        m_new = jnp.maximum(b_end + m_prev, jnp.max(g_col_end, axis=0, keepdims=True))
        gate_terms[r, d] = dict(
            u=b_col - m_t, r_row=i_row - b_row, w_inter=jnp.exp(b_col + m_prev - m_t), floor=jnp.exp(-m_t),
            k_scale=jnp.exp(g_col_end - m_new), decay=jnp.exp(b_end + m_prev - m_new), m_new=m_new)
    work = []
    for r, (d, (q_ref, k_ref, v_ref, _, _, h_ref, allowed, _)) in itertools.product(seqs, enumerate(directions)):
        for h in range(nh):
            w = dict(r=r, d=d, h=h, h_ref=h_ref, allowed=allowed, g=gate_terms[r, d])
            w["q"] = (q_ref[r, :, h * ML_QK:(h + 1) * ML_QK] * (ML_QK ** -0.5)).astype(BF)
            w["k"] = k_ref[r, :, h * ML_QK:(h + 1) * ML_QK]
            w["v1"] = jnp.concatenate([v_ref[r, :, h * ML_V:(h + 1) * ML_V].astype(BF), ones_v], axis=1)
            work.append(w)
    for w in work:
        w["qk"] = _bdot_nt(w["q"], w["k"])
    for w in work:
        r, d, h = w["r"], w["d"], w["h"]
        w["c_prev"], w["n_prev"] = c_ref[r, d, h], n_ref[r, d, h]
        state = jnp.concatenate([w["c_prev"], w["n_prev"]], axis=1).astype(BF)
        w["q_state"] = jnp.dot(w["q"], state, preferred_element_type=F32)
    for w in work:
        h, g = w["h"], w["g"]
        kw = w["k"] * g["k_scale"][:, h:h + 1]
        w["kv"] = _bdot_tn(kw, w["v1"])
    for w in work:
        h, g = w["h"], w["g"]
        dmat = jnp.where(w["allowed"], g["u"][:, h:h + 1] + g["r_row"][h:h + 1, :], -jnp.inf)
        sw = w["qk"] * jnp.exp(dmat)
        sw_hi = sw.astype(BF)
        sw_lo = (sw - sw_hi.astype(F32)).astype(BF)
        w["pv"] = jnp.dot(sw_hi, w["v1"], preferred_element_type=F32)
        w["den_lo"] = jnp.dot(sw_lo, ones_v, preferred_element_type=F32)
    for w in work:
        r, d, h, g = w["r"], w["d"], w["h"], w["g"]
        w_inter = g["w_inter"][:, h:h + 1]
        num = w_inter * w["q_state"][:, :ML_V] + w["pv"][:, :ML_V]
        den = w_inter * w["q_state"][:, ML_V:] + (w["pv"][:, ML_V:] + w["den_lo"])
        w["h_ref"][r, :, h * ML_V:(h + 1) * ML_V] = num / jnp.maximum(jnp.abs(den), g["floor"][:, h:h + 1])
        decay = g["decay"][:, h:h + 1]
        c_ref[r, d, h] = decay * w["c_prev"] + w["kv"][:, :ML_V]
        n_new = decay * w["n_prev"] + w["kv"][:, ML_V:]
        n_ref[r, d, h] = n_new
        n_out_ref[r, d, h] = n_new[:, :1]
    for r in seqs:
        m_ref[r, :, :nh] = gate_terms[r, 0]["m_new"]
        m_ref[r, :, nh:] = gate_terms[r, 1]["m_new"]


def _mlstm(qkvo, side, g_row, b_col, b_row, init, bsz, seq):
    t = qkvo.shape[0]
    nc = seq // ML_CHUNK
    lc = ML_CHUNK
    nq = ML_HEADS * ML_QK
    nv = ML_HEADS * ML_V
    side_blk = (SIDE_COLS - 128) // 128
    rb = min(MLSTM_SEQS, bsz)
    assert bsz % rb == 0
    fwd = lambda c: c
    bwd = lambda c: nc - 1 - c

    def specs(pos):
        return [pl.BlockSpec((rb, lc, nq), lambda b, c: (b, pos(c), 0)),
                pl.BlockSpec((rb, lc, nq), lambda b, c: (b, pos(c), 1)),
                pl.BlockSpec((rb, lc, nv), lambda b, c: (b, pos(c), 1))]

    state = lambda *shape: pl.BlockSpec((rb,) + shape, lambda b, c: (b,) + (0,) * len(shape))
    state_specs = [state(2, ML_HEADS, ML_QK, ML_V), state(2, ML_HEADS, ML_QK, ML_V), state(1, 2 * ML_HEADS)]
    in_specs = specs(fwd) + specs(bwd) + [
        pl.BlockSpec((rb, lc, 128), lambda b, c: (b, fwd(c), side_blk)),
        pl.BlockSpec((rb, lc, 128), lambda b, c: (b, bwd(c), side_blk)),
        pl.BlockSpec((rb, None, 4 * ML_HEADS, lc), lambda b, c: (b, fwd(c), 0, 0)),
        pl.BlockSpec((rb, None, 4 * ML_HEADS, lc), lambda b, c: (b, bwd(c), 0, 0)),
        pl.BlockSpec((1, 4 * ML_HEADS), lambda b, c: (0, 0)),
        pl.BlockSpec((4 * ML_HEADS, 1), lambda b, c: (0, 0))]
    init = () if init is None else tuple(init)
    in_specs += state_specs[:len(init)]
    out_specs = [pl.BlockSpec((rb, lc, nv), lambda b, c: (b, fwd(c), 0)),
                 pl.BlockSpec((rb, lc, nv), lambda b, c: (b, bwd(c), 0)),
                 state_specs[0], state(2, ML_HEADS, ML_QK, 1), state_specs[2]]
    out_shape = [jax.ShapeDtypeStruct((bsz, seq, nv), F32), jax.ShapeDtypeStruct((bsz, seq, nv), F32),
                 jax.ShapeDtypeStruct((bsz, 2, ML_HEADS, ML_QK, ML_V), F32),
                 jax.ShapeDtypeStruct((bsz, 2, ML_HEADS, ML_QK, 1), F32),
                 jax.ShapeDtypeStruct((bsz, 1, 2 * ML_HEADS), F32)]
    qkvo3, side3 = qkvo.reshape(bsz, seq, -1), side.reshape(bsz, seq, -1)
    g_row4 = g_row.reshape(bsz, nc, 4 * ML_HEADS, lc)
    hf, hb, c_fin, n_fin, m_fin = pl.pallas_call(
        _mlstm_kernel, grid=(bsz // rb, nc), in_specs=in_specs, out_specs=out_specs, out_shape=out_shape,
        scratch_shapes=[pltpu.VMEM((rb, 2, ML_HEADS, ML_QK, ML_V), F32)],
        compiler_params=_params(2), name="mlstm",
    )(qkvo3, qkvo3, qkvo3, qkvo3, qkvo3, qkvo3, side3, side3, g_row4, g_row4, b_col, b_row, *init)
    return hf.reshape(t, nv), hb.reshape(t, nv), c_fin, n_fin.reshape(bsz, 2, ML_HEADS, ML_QK), m_fin


def _mlstm_post_kernel(hf_ref, hb_ref, o_ref, g_ref, y_ref):
    for h in range(ML_HEADS):
        sl = slice(h * ML_V, (h + 1) * ML_V)
        hn = _rms(hf_ref[:, sl] + hb_ref[:, sl]) * g_ref[:, sl]
        y_ref[:, sl] = (hn * _sigmoid(o_ref[:, sl])).astype(BF)


def _mlstm_post(hf, hb, qkvo, gain):
    t, nv = hf.shape
    tm = ROW_TILE
    blk = pl.BlockSpec((tm, nv), lambda i: (i, 0))
    return pl.pallas_call(
        _mlstm_post_kernel, grid=(t // tm,),
        in_specs=[blk, blk, pl.BlockSpec((tm, nv), lambda i: (i, 2)), pl.BlockSpec((1, nv), lambda i: (0, 0))],
        out_specs=blk, out_shape=jax.ShapeDtypeStruct((t, nv), BF),
        compiler_params=_params(1), name="mlstm_post",
    )(hf, hb, qkvo, gain.reshape(1, nv))


LOG2_E = 1.4426950408889634


def _softmax_terms(scores, scale):
    c = scale * LOG2_E
    scaled = [s * c for s in scores]
    m = functools.reduce(jnp.maximum, [jnp.max(s, axis=-1, keepdims=True) for s in scaled])
    e = [jnp.exp2(s - m) for s in scaled]
    den = functools.reduce(lambda a, b: a + b, [jnp.sum(x, axis=-1, keepdims=True) for x in e])
    return [x.astype(BF) for x in e], 1.0 / den


def _walk_heads(n_heads, scores_fn, softmax_fn, values_fn, lag):
    scores, weights = {}, {}
    for t in range(n_heads + 2 * lag):
        if t < n_heads:
            scores[t] = scores_fn(t)
        if 0 <= t - lag < n_heads:
            weights[t - lag] = softmax_fn(t - lag, scores.pop(t - lag))
        if 0 <= t - 2 * lag < n_heads:
            values_fn(t - 2 * lag, weights.pop(t - 2 * lag))


def _mla_kernel(*refs, with_ctx):
    if with_ctx:
        qa_ref, kv_ref, side_ref, cq_ref, sq_ref, ck_ref, sk_ref, kvc_ref, krc_ref, o_ref = refs
    else:
        qa_ref, kv_ref, side_ref, o_ref = refs
    scale = (MLA_NOPE + MLA_ROPE) ** -0.5
    nope_cols = MLA_HEADS * MLA_NOPE
    q_rope = qa_ref[:, nope_cols:]
    k_rope = side_ref[...]
    if with_ctx:
        q_rope_rot = _rope(q_rope, cq_ref[...], sq_ref[...], MLA_ROPE // 4).astype(BF)
        k_rope_rot = _rope(k_rope, ck_ref[...], sk_ref[...], MLA_ROPE // 4)[:, :MLA_ROPE].astype(BF)
        k_rope_ctx = krc_ref[...].astype(BF)
        q_rope = q_rope.astype(BF)
    else:
        q_rope_rot = q_rope.astype(BF)
        k_rope_rot = k_rope[:, :MLA_ROPE].astype(BF)
    heads = range(MLA_HEADS)
    rope_cols = [slice(h * MLA_ROPE, (h + 1) * MLA_ROPE) for h in heads]
    kv0 = [h * (MLA_NOPE + MLA_V) for h in heads]
    q_n = [qa_ref[:, h * MLA_NOPE:(h + 1) * MLA_NOPE].astype(BF) for h in heads]

    def scores(h):
        blocks = [_bdot_nt(q_n[h], kv_ref[:, kv0[h]:kv0[h] + MLA_NOPE])
                  + _bdot_nt(q_rope_rot[:, rope_cols[h]], k_rope_rot)]
        if with_ctx:
            blocks.append(_bdot_nt(q_n[h], kvc_ref[:, kv0[h]:kv0[h] + MLA_NOPE])
                          + _bdot_nt(q_rope[:, rope_cols[h]], k_rope_ctx))
        return blocks

    def weighted_values(h, weights):
        (e, inv_den), v0 = weights, kv0[h] + MLA_NOPE
        out = jnp.dot(e[0], kv_ref[:, v0:v0 + MLA_V].astype(BF), preferred_element_type=F32)
        if with_ctx:
            out = out + jnp.dot(e[1], kvc_ref[:, v0:v0 + MLA_V].astype(BF), preferred_element_type=F32)
        o_ref[:, h * MLA_V:(h + 1) * MLA_V] = (out * inv_den).astype(BF)

    _walk_heads(MLA_HEADS, scores, lambda h, s: _softmax_terms(s, scale), weighted_values,
                lag=2 if with_ctx else MLA_HEADS)


def _mla_attention(qa, kv, side, bsz, seq, ctx=None):
    t = qa.shape[0]
    tq = ATTN_Q_ROWS
    nq = seq // tq
    side_blk = (SIDE_COLS - 128) // 128
    nkv = MLA_HEADS * (MLA_NOPE + MLA_V)
    in_specs = [pl.BlockSpec((tq, qa.shape[1]), lambda b, i: (b * nq + i, 0)),
                pl.BlockSpec((seq, nkv), lambda b, i: (b, 0)),
                pl.BlockSpec((seq, 128), lambda b, i: (b, side_blk))]
    args = [qa, kv, side]
    if ctx is not None:
        cos_q, sin_q, cos_k, sin_k, kvc, krc = ctx
        past = krc.shape[1]
        in_specs += [pl.BlockSpec((tq, cos_q.shape[1]), lambda b, i: (i, 0)),
                     pl.BlockSpec((tq, cos_q.shape[1]), lambda b, i: (i, 0)),
                     pl.BlockSpec((seq, 128), lambda b, i: (0, 0)),
                     pl.BlockSpec((seq, 128), lambda b, i: (0, 0)),
                     pl.BlockSpec((past, nkv), lambda b, i: (b, 0)),
                     pl.BlockSpec((None, past, MLA_ROPE), lambda b, i: (b, 0, 0))]
        args += [cos_q, sin_q, cos_k, sin_k, kvc, krc]
    nout = MLA_HEADS * MLA_V
    return pl.pallas_call(
        functools.partial(_mla_kernel, with_ctx=ctx is not None),
        grid=(bsz, nq), in_specs=in_specs,
        out_specs=pl.BlockSpec((tq, nout), lambda b, i: (b * nq + i, 0)),
        out_shape=jax.ShapeDtypeStruct((t, nout), BF),
        compiler_params=_params(2), name="mla_attention",
    )(*args)


def _gqa_kernel(*refs, with_ctx):
    if with_ctx:
        q_ref, k_ref, v_ref, gq_ref, gk_ref, cq_ref, sq_ref, ck_ref, sk_ref, kc_ref, vc_ref, o_ref, ksrc_ref = refs
    else:
        q_ref, k_ref, v_ref, gq_ref, gk_ref, o_ref, kn_ref, vo_ref, ksrc_ref = refs
    hd = GQA_HEAD_DIM
    scale = hd ** -0.5
    rep = GQA_HEADS // GQA_KV_HEADS
    n_kv = k_ref.shape[1] // hd
    kv_cols = [slice(g * hd, (g + 1) * hd) for g in range(n_kv)]

    @pl.when(pl.program_id(2) == 0)
    def _():
        for g in range(n_kv):
            k_n = _rms(k_ref[:, kv_cols[g]]) * gk_ref[...]
            if with_ctx:
                ksrc_ref[:, kv_cols[g]] = _rope(k_n, ck_ref[...], sk_ref[...], hd // 4).astype(BF)
            else:
                ksrc_ref[:, kv_cols[g]] = k_n.astype(BF)
                kn_ref[g] = k_n
                vo_ref[g] = v_ref[:, kv_cols[g]]

    v = [v_ref[:, kv_cols[g]].astype(BF) for g in range(n_kv)]
    if with_ctx:
        k_ctx, v_ctx = kc_ref[...].astype(BF), vc_ref[...].astype(BF)
    heads = range(n_kv * rep)
    cols = [slice(r * hd, (r + 1) * hd) for r in heads]
    q_n = [_rms(q_ref[:, cols[r]]) * gq_ref[...] for r in heads]
    k_src = [ksrc_ref[:, kv_cols[r // rep]] for r in heads]
    def scores(r):
        if with_ctx:
            return [_bdot_nt(_rope(q_n[r], cq_ref[...], sq_ref[...], hd // 4), k_src[r]), _bdot_nt(q_n[r], k_ctx)]
        return [_bdot_nt(q_n[r], k_src[r])]

    def weighted_values(r, weights):
        e, inv_den = weights
        out = jnp.dot(e[0], v[r // rep], preferred_element_type=F32)
        if with_ctx:
            out = out + jnp.dot(e[1], v_ctx, preferred_element_type=F32)
        o_ref[:, cols[r]] = (out * inv_den).astype(BF)

    _walk_heads(len(heads), scores, lambda r, s: _softmax_terms(s, scale), weighted_values, lag=1)


def _gqa_attention(qkv, g_q, g_k, bsz, seq, ctx=None):
    t = qkv.shape[0]
    hd = GQA_HEAD_DIM
    rep = GQA_HEADS // GQA_KV_HEADS
    tq = ATTN_Q_ROWS
    nq = seq // tq
    n_kv = 1 if ctx is not None else GQA_KV_HEADS
    k_blk0 = GQA_HEADS // n_kv
    in_specs = [pl.BlockSpec((tq, n_kv * rep * hd), lambda b, g, i: (b * nq + i, g)),
                pl.BlockSpec((seq, n_kv * hd), lambda b, g, i: (b, k_blk0 + g)),
                pl.BlockSpec((seq, n_kv * hd), lambda b, g, i: (b, k_blk0 + GQA_KV_HEADS // n_kv + g)),
                pl.BlockSpec((1, hd), lambda b, g, i: (0, 0)),
                pl.BlockSpec((1, hd), lambda b, g, i: (0, 0))]
    args = [qkv, qkv, qkv, g_q.reshape(1, hd), g_k.reshape(1, hd)]
    o_spec = pl.BlockSpec((tq, n_kv * rep * hd), lambda b, g, i: (b * nq + i, g))
    o_shape = jax.ShapeDtypeStruct((t, GQA_HEADS * hd), BF)
    if ctx is not None:
        cos_t, sin_t, kc, vc, j = ctx
        past = kc.shape[3]
        cache = pl.BlockSpec((None, None, None, past, hd), lambda b, g, i: (b, j, g, 0, 0))
        in_specs += [pl.BlockSpec((tq, hd), lambda b, g, i: (i, 0)),
                     pl.BlockSpec((tq, hd), lambda b, g, i: (i, 0)),
                     pl.BlockSpec((seq, hd), lambda b, g, i: (0, 0)),
                     pl.BlockSpec((seq, hd), lambda b, g, i: (0, 0)),
                     cache, cache]
        args += [cos_t, sin_t, cos_t, sin_t, kc, vc]
        out_specs, out_shape = o_spec, o_shape
    else:
        head_major = pl.BlockSpec((None, n_kv, seq, hd), lambda b, g, i: (b, g, 0, 0))
        out_specs = [o_spec, head_major, head_major]
        kv_shape = jax.ShapeDtypeStruct((bsz, GQA_KV_HEADS, seq, hd), F32)
        out_shape = [o_shape, kv_shape, kv_shape]
    return pl.pallas_call(
        functools.partial(_gqa_kernel, with_ctx=ctx is not None),
        grid=(bsz, GQA_KV_HEADS // n_kv, nq), in_specs=in_specs, out_specs=out_specs, out_shape=out_shape,
        scratch_shapes=[pltpu.VMEM((seq, n_kv * hd), BF)],
        compiler_params=_params(3), name="gqa_attention",
    )(*args)


def _dispatch_kernel(ar_ref, ac_ref, h_ref, xe_ref, gate_ref, rankc_ref, rankr_ref, *, cap):
    n_groups, group, s = ar_ref.shape
    g = pl.program_id(1)

    @pl.when(g == 0)
    def _():
        blk = min(RANK_BLOCK, s)
        rankr_ref[...] = jnp.zeros_like(rankr_ref)
        ones_sub = jnp.ones((8, blk), BF)
        ones_lane = jnp.ones((s, 128), BF)
        for j in range(s // blk):
            rows = pl.ds(j * blk, blk)
            i0 = lax.broadcasted_iota(jnp.int32, (blk, s), 0) + j * blk
            i1 = lax.broadcasted_iota(jnp.int32, (blk, s), 1)
            sub_first = jnp.where(i0 < i1, 1.0, 0.0)
            for e in range(n_groups * group):
                eg, ei = e // group, e % group
                a_row = ar_ref[eg, ei:ei + 1, :]
                a_col = ac_ref[rows, e:e + 1]
                ahead = jnp.where(a_col > a_row, 1.0, jnp.where(a_col >= a_row, sub_first, 0.0)).astype(BF)
                rankr_ref[eg, ei:ei + 1, :] += jnp.dot(ones_sub, ahead, preferred_element_type=F32)[:1, :]
                rankc_ref[rows, e:e + 1] = (s - 1.0) - jnp.dot(ahead, ones_lane, preferred_element_type=F32)[:, :1]

    slot = lax.broadcasted_iota(jnp.int32, (group, cap, s), 1).astype(F32)
    pick = rankr_ref[g][:, None, :] == slot
    onehot = jnp.where(pick, 1.0, 0.0).reshape(group * cap, s).astype(BF)
    rows_f32 = jnp.dot(onehot, h_ref[...], preferred_element_type=F32)
    xe_ref[...] = rows_f32.reshape(group, cap, -1).astype(BF)
    gate_ref[...] = jnp.sum(jnp.where(pick, ar_ref[g][:, None, :], 0.0), axis=2, keepdims=True)


def _dispatch(aff_row, aff_col, h, bsz, seq):
    t, d = h.shape
    ne = aff_row.shape[0]
    cap = EC_FACTOR * seq // ne
    group = GATHER_ROWS // cap
    n_groups = ne // group
    return pl.pallas_call(
        functools.partial(_dispatch_kernel, cap=cap),
        grid=(bsz, n_groups),
        in_specs=[pl.BlockSpec((n_groups, group, seq), lambda b, g: (0, 0, b)),
                  pl.BlockSpec((seq, ne), lambda b, g: (b, 0)),
                  pl.BlockSpec((seq, d), lambda b, g: (b, 0))],
        out_specs=[pl.BlockSpec((group, cap, d), lambda b, g: (g, b, 0)),
                   pl.BlockSpec((group, cap, 1), lambda b, g: (g, b, 0)),
                   pl.BlockSpec((seq, ne), lambda b, g: (b, 0))],
        out_shape=[jax.ShapeDtypeStruct((ne, bsz * cap, d), BF),
                   jax.ShapeDtypeStruct((ne, bsz * cap, 1), F32),
                   jax.ShapeDtypeStruct((t, ne), F32)],
        scratch_shapes=[pltpu.VMEM((n_groups, group, seq), F32)],
        compiler_params=_params(2), name="ec_dispatch",
    )(aff_row.reshape(n_groups, group, t), aff_col, h)


def _experts_kernel(xp_ref, xs_ref, gp_ref, gs_ref, wg_ref, wu_ref, wd_ref, yp_ref, ys_ref, accp_ref, accs_ref):
    f = pl.program_id(1)
    tf = wg_ref.shape[1]
    w_gate_up = jnp.concatenate([wg_ref[...].astype(BF), wu_ref[...].astype(BF)], axis=1)
    wd = wd_ref[...].astype(BF)
    groups = ((xp_ref, gp_ref, yp_ref, accp_ref), (xs_ref, gs_ref, ys_ref, accs_ref))

    @pl.when(f == 0)
    def _():
        for _, _, _, acc_ref in groups:
            acc_ref[...] = jnp.zeros_like(acc_ref)

    for x_ref, _, _, acc_ref in groups:
        for r in range(0, x_ref.shape[0], EXPERT_ROWS):
            rows = slice(r, r + EXPERT_ROWS)
            au = jnp.dot(x_ref[rows, :], w_gate_up, preferred_element_type=F32)
            a, u = au[:, :tf], au[:, tf:]
            acc_ref[rows, :] += jnp.dot((a * _sigmoid(a) * u).astype(BF), wd, preferred_element_type=F32)

    @pl.when(f == pl.num_programs(1) - 1)
    def _():
        for _, g_ref, y_ref, acc_ref in groups:
            y_ref[...] = (acc_ref[...] * g_ref[...]).astype(BF)


def _experts(xe_p, xe_s, gate_p, gate_s, w_gate, w_up, w_down, layer):
    ne, mp, d = xe_p.shape
    ms = xe_s.shape[1]
    fdim = w_gate.shape[-1]
    tf = EXPERT_HIDDEN_COLS
    rows = lambda m, last: pl.BlockSpec((None, m, last), lambda e, f: (e, 0, 0))
    return pl.pallas_call(
        _experts_kernel, grid=(ne, fdim // tf),
        in_specs=[rows(mp, d), rows(ms, d), rows(mp, 1), rows(ms, 1),
                  pl.BlockSpec((None, None, d, tf), lambda e, f: (layer, e, 0, f)),
                  pl.BlockSpec((None, None, d, tf), lambda e, f: (layer, e, 0, f)),
                  pl.BlockSpec((None, None, tf, d), lambda e, f: (layer, e, f, 0))],
        out_specs=[rows(mp, d), rows(ms, d)],
        out_shape=[jax.ShapeDtypeStruct((ne, mp, d), BF), jax.ShapeDtypeStruct((ne, ms, d), BF)],
        scratch_shapes=[pltpu.VMEM((mp, d), F32), pltpu.VMEM((ms, d), F32)],
        compiler_params=_params(2), name="ec_experts",
    )(xe_p, xe_s, gate_p, gate_s, w_gate, w_up, w_down)


RANK_RADIX = 32


def _combine_kernel(y_ref, rank_ref, x_ref, g_ref, *refs, cap, final):
    post_refs, place_ref = refs[:-1], refs[-1]
    ne = y_ref.shape[0]
    n_slots = ne * cap

    @pl.when(pl.program_id(1) == 0)
    def _():
        rank = rank_ref[...]
        hi = jnp.floor(rank * (1.0 / RANK_RADIX))
        lo = rank - RANK_RADIX * hi
        lane = lax.broadcasted_iota(jnp.int32, (ne, n_slots), 1)
        expert = lax.broadcasted_iota(jnp.int32, (ne, n_slots), 0)
        own = (lane >= expert * cap) & (lane < (expert + 1) * cap)
        spread_hi = jnp.where(own, float(RANK_RADIX), 0.0).astype(BF)
        spread_lo = jnp.where(own, 1.0, 0.0).astype(BF)
        spread = (jnp.dot(hi.astype(BF), spread_hi, preferred_element_type=F32)
                  + jnp.dot(lo.astype(BF), spread_lo, preferred_element_type=F32))
        slot = (lax.broadcasted_iota(jnp.int32, (1, n_slots), 1) & (cap - 1)).astype(F32)
        place_ref[...] = jnp.where(spread == slot, 1.0, 0.0).astype(BF)

    ts = x_ref.shape[0]
    rows = pl.ds(pl.multiple_of(pl.program_id(1) * ts, ts), ts)
    y = y_ref[...].reshape(n_slots, y_ref.shape[2])
    x_new = x_ref[...] + g_ref[...] * jnp.dot(place_ref[rows, :], y, preferred_element_type=F32)
    if final:
        (gain_ref,), (out_ref,) = post_refs[:1], post_refs[1:]
        out_ref[...] = _rms(x_new) * gain_ref[...]
    else:
        (sh_ref, sc_ref), (xo_ref, h_ref) = post_refs[:2], post_refs[2:]
        xo_ref[...] = x_new
        h_ref[...] = (_rms(x_new) * (1.0 + sc_ref[...]) + sh_ref[...]).astype(BF)


def _combine(y, rank, x, gate, bsz, seq, next_shift_scale=None, final_gain=None):
    t, d = x.shape
    ne = y.shape[0]
    cap = EC_FACTOR * seq // ne
    assert seq <= RANK_RADIX * RANK_RADIX and cap & (cap - 1) == 0
    ts = min(COMBINE_ROWS, seq)
    nt = seq // ts
    final = final_gain is not None
    vec = lambda v: pl.BlockSpec((None, 1, d), lambda b, i: (b if v.shape[0] > 1 else 0, 0, 0))
    tile = pl.BlockSpec((ts, d), lambda b, i: (b * nt + i, 0))
    in_specs = [pl.BlockSpec((ne, cap, d), lambda b, i: (0, b, 0)),
                pl.BlockSpec((seq, ne), lambda b, i: (b, 0)), tile, vec(gate)]
    if final:
        args = [final_gain.reshape(1, d)]
        in_specs.append(pl.BlockSpec((1, d), lambda b, i: (0, 0)))
        out_specs, out_shape = tile, jax.ShapeDtypeStruct((t, d), F32)
    else:
        args = list(next_shift_scale)
        in_specs += [vec(v) for v in args]
        out_specs = [tile, tile]
        out_shape = [jax.ShapeDtypeStruct((t, d), F32), jax.ShapeDtypeStruct((t, d), BF)]
    return pl.pallas_call(
        functools.partial(_combine_kernel, cap=cap, final=final),
        grid=(bsz, nt), in_specs=in_specs, out_specs=out_specs, out_shape=out_shape,
        scratch_shapes=[pltpu.VMEM((seq, ne * cap), BF)],
        compiler_params=_params(2), name="ec_combine",
    )(y, rank, x, gate, *args)


def _rope_tables(n_tokens, rot_dim):
    rows = n_tokens // GRID_W
    row = jnp.repeat(jnp.arange(rows), GRID_W).astype(F32)
    col = jnp.tile(jnp.arange(GRID_W), rows).astype(F32)
    quarter = rot_dim // 4
    inv = ROPE_THETA ** (-jnp.arange(quarter, dtype=F32) / quarter)
    a_row, a_col = row[:, None] * inv, col[:, None] * inv
    cos_t = jnp.concatenate([jnp.cos(a_row), jnp.cos(a_row), jnp.cos(a_col), jnp.cos(a_col)], axis=-1)
    sin_t = jnp.concatenate([-jnp.sin(a_row), jnp.sin(a_row), -jnp.sin(a_col), jnp.sin(a_col)], axis=-1)
    return cos_t, sin_t


def kernel(x_prompt, x_sample, state_mlstm_c, state_mlstm_n, state_mlstm_m, cache_mla_ckv, cache_mla_krope,
           cache_gqa_k, cache_gqa_v, c, c_ctx, w_mod, b_mod, w_in_even, b_igate, b_fgate, g_mlstm, g_cq, w_uq,
           g_ckv, w_ukv, w_out_even, w_in_odd, g_qnorm, g_knorm, w_out_odd, w_router, w_expert_gate,
           w_expert_up, w_expert_down, g_final):
    d = D_MODEL
    bp, sp, _ = x_prompt.shape
    bs, ss, _ = x_sample.shape
    depth = w_mod.shape[0]
    nh = ML_HEADS
    streams = {"p": (bp, sp), "s": (bs, ss)}
    x = {"p": x_prompt.reshape(bp * sp, d), "s": x_sample.reshape(bs * ss, d)}

    c8 = jnp.concatenate([c_ctx[None], c, jnp.zeros((8 - 1 - bs, d), F32)], axis=0)
    mod_all = _mod_vectors(c8, w_mod, b_mod).reshape(depth, 8, 6, 1, d)

    def mod(layer, key, idx):
        rows = mod_all[layer, 0:1, idx] if key == "p" else mod_all[layer, 1:1 + bs, idx]
        return rows

    new_even, new_odd = [], []
    h_in = {key: _norm_mod(x[key], mod(0, key, 0), mod(0, key, 1), streams[key][1]) for key in streams}
    routed_in = {}
    for layer in range(depth):
        j = layer // 2
        if layer % 2 == 0:
            w_out = _to_bf16(w_out_even, j)
            w_in_rows = jnp.swapaxes(w_in_even, 1, 2)
            w_side = jnp.concatenate(
                [w_in_rows[j, QKVO_COLS + 4 * nh:],
                 w_in_rows[j, QKVO_COLS:QKVO_COLS + 4 * nh],
                 jnp.zeros((128 - MLA_ROPE - 4 * nh, d), F32)], axis=0)[None]
            w_q = w_uq[j].reshape(MLA_Q_RANK, MLA_HEADS, MLA_NOPE + MLA_ROPE)
            w_q = jnp.concatenate([w_q[:, :, :MLA_NOPE].reshape(MLA_Q_RANK, -1),
                                   w_q[:, :, MLA_NOPE:].reshape(MLA_Q_RANK, -1)], axis=1)
            bias_col = jnp.concatenate([b_igate[j].reshape(1, -1), b_fgate[j].reshape(1, -1)], axis=1)
            bias_row = bias_col.reshape(-1, 1)
            cos64, sin64 = _rope_tables(ss, MLA_ROPE)
            cos_q, sin_q = jnp.tile(cos64, (1, MLA_HEADS)), jnp.tile(sin64, (1, MLA_HEADS))
            pad = jnp.zeros((ss, 128 - MLA_ROPE), F32)
            cos_k, sin_k = jnp.concatenate([cos64, pad], axis=1), jnp.concatenate([sin64, pad], axis=1)
            kvc = _mm_resident(cache_mla_ckv[:, j].reshape(-1, MLA_KV_RANK), w_ukv, j, w_ukv.shape[-1], IN_PROJ_COLS,
                               name="mla_ctx_expand")
            for key, (bsz, seq) in streams.items():
                h = h_in[key]
                qkvo = _mm_resident(h, w_in_rows, j, QKVO_COLS, IN_PROJ_COLS, w_rows=True, name="even_in_main")
                side = _mm_resident(h, w_side, 0, SIDE_COLS, SIDE_COLS // 3, w_rows=True, name="even_in_side")
                gates = side[:, SIDE_COLS - 128 + GATE_LANE0:SIDE_COLS - 128 + GATE_LANE0 + 4 * nh]
                g_row = gates.reshape(-1, ML_CHUNK, 4 * nh).transpose(0, 2, 1)
                if key == "p":
                    init = None
                else:
                    c0 = state_mlstm_c[:, j]
                    init = (c0, jnp.broadcast_to(state_mlstm_n[:, j][..., None], c0.shape),
                            state_mlstm_m[:, j].reshape(bsz, 1, 2 * nh))
                hf, hb, c_fin, n_fin, m_fin = _mlstm(qkvo, side, g_row, bias_col, bias_row, init, bsz, seq)
                y_ml = _mlstm_post(hf, hb, qkvo, g_mlstm[j])
                qa = _norm_mm(side, 0, g_cq[j], w_q, False, "mla_q_up")
                kv, ckv_n = _norm_mm(side, 1, g_ckv[j], w_ukv[j], True, "mla_kv_up")
                if key == "p":
                    y_a = _mla_attention(qa, kv, side, bsz, seq)
                    k_rope = side[:, SIDE_COLS - 128:SIDE_COLS - 128 + MLA_ROPE]
                    new_even.append((c_fin, n_fin, m_fin.reshape(bsz, 2, nh),
                                     ckv_n.reshape(bsz, seq, -1), k_rope.reshape(bsz, seq, -1)))
                else:
                    y_a = _mla_attention(qa, kv, side, bsz, seq,
                                         ctx=(cos_q, sin_q, cos_k, sin_k, kvc, cache_mla_krope[:, j]))
                routed_in[key] = _out_proj_router([y_ml, y_a], w_out, x[key], mod(layer, key, 2), mod(layer, key, 3),
                                                  mod(layer, key, 4), w_router, layer, seq)
        else:
            w_out = _to_bf16(w_out_odd, j)
            cos_t, sin_t = _rope_tables(ss, GQA_HEAD_DIM)
            for key, (bsz, seq) in streams.items():
                qkv = _mm_resident(h_in[key], w_in_odd, j, w_in_odd.shape[-1], IN_PROJ_COLS, name="odd_in")
                if key == "p":
                    o, k_n, v = _gqa_attention(qkv, g_qnorm[j], g_knorm[j], bsz, seq)
                    new_odd.append((k_n, v))
                else:
                    o = _gqa_attention(qkv, g_qnorm[j], g_knorm[j], bsz, seq,
                                       ctx=(cos_t, sin_t, cache_gqa_k, cache_gqa_v, j))
                routed_in[key] = _out_proj_router([o], w_out, x[key], mod(layer, key, 2), mod(layer, key, 3),
                                                  mod(layer, key, 4), w_router, layer, seq)
        routed = {}
        for key, (bsz, seq) in streams.items():
            x[key], h, aff, aff_t = routed_in[key]
            routed[key] = _dispatch(aff_t, aff, h, bsz, seq)
        y_p, y_s = _experts(routed["p"][0], routed["s"][0], routed["p"][1], routed["s"][1],
                            w_expert_gate, w_expert_up, w_expert_down, layer)
        for key, y in (("p", y_p), ("s", y_s)):
            bsz, seq = streams[key]
            if layer + 1 < depth:
                x[key], h_in[key] = _combine(
                    y, routed[key][2], x[key], mod(layer, key, 5), bsz, seq,
                    next_shift_scale=(mod(layer + 1, key, 0), mod(layer + 1, key, 1)))
            else:
                x[key] = _combine(y, routed[key][2], x[key], mod(layer, key, 5), bsz, seq, final_gain=g_final)

    y_prompt = x["p"].reshape(bp, sp, d)
    y_sample = x["s"].reshape(bs, ss, d)
    new_c = jnp.stack([e[0] for e in new_even], axis=1)
    new_n = jnp.stack([e[1] for e in new_even], axis=1)
    new_m = jnp.stack([e[2] for e in new_even], axis=1)
    new_ckv = jnp.stack([e[3] for e in new_even], axis=1)
    new_krope = jnp.stack([e[4] for e in new_even], axis=1)
    new_k = jnp.stack([e[0] for e in new_odd], axis=1)
    new_v = jnp.stack([e[1] for e in new_odd], axis=1)
    return (y_prompt, y_sample, new_c, new_n, new_m, new_ckv, new_krope, new_k, new_v)
```

```python
import functools
import itertools

import jax
import jax.numpy as jnp
from jax import lax
from jax.experimental import pallas as pl
from jax.experimental.pallas import tpu as pltpu

BF = jnp.bfloat16
F32 = jnp.float32

D_MODEL = 2048
GRID_W = 64
ROPE_THETA = 10000.0
NORM_EPS = 1e-6
ML_HEADS = 8
ML_QK = 64
ML_V = 128
ML_CHUNK = 64
MLA_HEADS = 8
MLA_Q_RANK = 512
MLA_KV_RANK = 512
MLA_NOPE = 128
MLA_ROPE = 64
MLA_V = 128
GQA_HEADS = 16
GQA_KV_HEADS = 4
GQA_HEAD_DIM = 128
N_EXPERTS = 16
EXPERT_DIM = 1024
EC_FACTOR = 2

QKVO_COLS = 2 * ML_HEADS * ML_QK + 2 * ML_HEADS * ML_V
SIDE_COLS = MLA_Q_RANK + MLA_KV_RANK + 128
GATE_LANE0 = MLA_ROPE

VMEM_LIMIT_BYTES = 56 * 1024 * 1024
OUT_PROJ_ROWS = 512
MLSTM_SEQS = 4
MOD_COLS = 1024
IN_PROJ_COLS = 512
IN_PROJ_ROW_CHUNK = 512
CAST_ROWS = 512
ROW_TILE = 512
ATTN_Q_ROWS = 256
EXPERT_HIDDEN_COLS = 512
EXPERT_ROWS = 256
RANK_BLOCK = 256
GATHER_ROWS = 512
COMBINE_ROWS = 256


def _params(n_axes):
    return pltpu.CompilerParams(dimension_semantics=("arbitrary",) * n_axes,
                                vmem_limit_bytes=VMEM_LIMIT_BYTES)


def _bdot(a, b):
    return jnp.dot(a.astype(BF), b.astype(BF), preferred_element_type=F32)


def _bdot_nt(a, b):
    return lax.dot_general(a.astype(BF), b.astype(BF), (((1,), (1,)), ((), ())),
                           preferred_element_type=F32)


def _bdot_tn(a, b):
    return lax.dot_general(a.astype(BF), b.astype(BF), (((0,), (0,)), ((), ())),
                           preferred_element_type=F32)


def _sigmoid(x):
    return 1.0 / (1.0 + jnp.exp(-x))


def _log_sigmoid(x):
    return jnp.minimum(x, 0.0) - jnp.log1p(jnp.exp(-jnp.abs(x)))


def _rms(x):
    return x * lax.rsqrt(jnp.mean(x * x, axis=-1, keepdims=True) + NORM_EPS)


def _rope(x, cos_t, sin_t, quarter):
    width = x.shape[-1]
    axis = x.ndim - 1
    lane = lax.broadcasted_iota(jnp.int32, x.shape, axis)
    partner = jnp.where((lane & quarter) == 0,
                        pltpu.roll(x, width - quarter, axis=axis),
                        pltpu.roll(x, quarter, axis=axis))
    return x * cos_t + partner * sin_t


def _mod_kernel(c_ref, w_ref, b_ref, o_ref):
    c = c_ref[...]
    o_ref[...] = _bdot(c * _sigmoid(c), w_ref[...]) + b_ref[...]


def _mod_vectors(c8, w_mod, b_mod):
    n_layers, k, n = w_mod.shape
    tn = MOD_COLS
    return pl.pallas_call(
        _mod_kernel,
        grid=(n_layers, n // tn),
        in_specs=[pl.BlockSpec((8, k), lambda l, j: (0, 0)),
                  pl.BlockSpec((None, k, tn), lambda l, j: (l, 0, j)),
                  pl.BlockSpec((None, 1, tn), lambda l, j: (l, 0, j))],
        out_specs=pl.BlockSpec((None, 8, tn), lambda l, j: (l, 0, j)),
        out_shape=jax.ShapeDtypeStruct((n_layers, 8, n), F32),
        compiler_params=_params(2),
        name="mod_vectors",
    )(c8, w_mod, b_mod.reshape(n_layers, 1, n))


def _norm_mod_kernel(x_ref, sh_ref, sc_ref, h_ref):
    h_ref[...] = (_rms(x_ref[...]) * (1.0 + sc_ref[...]) + sh_ref[...]).astype(BF)


def _batch_of_tile(n_vectors, tm, seq):
    if n_vectors == 1:
        return lambda i: 0
    assert seq % tm == 0
    return lambda i: (i * tm) // seq


def _norm_mod(x, shift, scale, seq):
    t, d = x.shape
    tm = OUT_PROJ_ROWS
    which = _batch_of_tile(shift.shape[0], tm, seq)
    vec = pl.BlockSpec((None, 1, d), lambda i: (which(i), 0, 0))
    x_spec = pl.BlockSpec((tm, d), lambda i: (i, 0))
    return pl.pallas_call(
        _norm_mod_kernel, grid=(t // tm,),
        in_specs=[x_spec, vec, vec], out_specs=x_spec,
        out_shape=jax.ShapeDtypeStruct((t, d), BF),
        compiler_params=_params(1), name="norm_mod",
    )(x, shift, scale)


def _to_bf16_kernel(w_ref, o_ref):
    o_ref[...] = w_ref[...].astype(BF)


def _to_bf16(w3, layer):
    _, k, n = w3.shape
    tk = CAST_ROWS
    return pl.pallas_call(
        _to_bf16_kernel, grid=(k // tk,),
        in_specs=[pl.BlockSpec((None, tk, n), lambda i: (layer, i, 0))],
        out_specs=pl.BlockSpec((tk, n), lambda i: (i, 0)),
        out_shape=jax.ShapeDtypeStruct((k, n), BF),
        compiler_params=_params(1), name="weight_to_bf16",
    )(w3)


def _out_proj_router_kernel(*refs, k_sizes):
    n_a = len(k_sizes)
    a_refs = refs[:n_a]
    w_ref, x_ref, g_ref, sh_ref, sc_ref, wr_ref, xo_ref, h_ref, aff_ref, afft_ref = refs[n_a:]
    w_router = wr_ref[...].astype(BF)
    chunk = OUT_PROJ_ROWS // 2
    for r in range(0, x_ref.shape[0], chunk):
        rows = slice(r, r + chunk)
        acc, off = None, 0
        for a_ref, ks in zip(a_refs, k_sizes):
            term = jnp.dot(a_ref[rows, :], w_ref[off:off + ks, :], preferred_element_type=F32)
            acc = term if acc is None else acc + term
            off += ks
        x_new = x_ref[rows, :] + g_ref[...] * acc
        xo_ref[rows, :] = x_new
        h = (_rms(x_new) * (1.0 + sc_ref[...]) + sh_ref[...]).astype(BF)
        h_ref[rows, :] = h
        logits = jnp.dot(h, w_router, preferred_element_type=F32)
        e = jnp.exp(logits - jnp.max(logits, axis=-1, keepdims=True))
        aff = e / jnp.sum(e, axis=-1, keepdims=True)
        aff_ref[rows, :] = aff
        afft_ref[:, rows] = aff.T


def _out_proj_router(a_list, w_bf, x, gate, shift, scale, w_router, layer, seq):
    m, d = x.shape
    k_sizes = tuple(a.shape[1] for a in a_list)
    assert w_bf.shape == (sum(k_sizes), d) and all(a.dtype == BF for a in a_list)
    tm = OUT_PROJ_ROWS
    ne = w_router.shape[-1]
    which = _batch_of_tile(gate.shape[0], tm, seq)
    vec = pl.BlockSpec((None, 1, d), lambda i: (which(i), 0, 0))
    tile = pl.BlockSpec((tm, d), lambda i: (i, 0))
    in_specs = [pl.BlockSpec((tm, ks), lambda i: (i, 0)) for ks in k_sizes]
    in_specs += [pl.BlockSpec(w_bf.shape, lambda i: (0, 0), pipeline_mode=pl.Buffered(1)),
                 tile, vec, vec, vec, pl.BlockSpec((None, d, ne), lambda i: (layer, 0, 0))]
    return pl.pallas_call(
        functools.partial(_out_proj_router_kernel, k_sizes=k_sizes),
        grid=(m // tm,), in_specs=in_specs,
        out_specs=[tile, tile, pl.BlockSpec((tm, ne), lambda i: (i, 0)), pl.BlockSpec((ne, tm), lambda i: (0, i))],
        out_shape=[jax.ShapeDtypeStruct((m, d), F32), jax.ShapeDtypeStruct((m, d), BF),
                   jax.ShapeDtypeStruct((m, ne), F32), jax.ShapeDtypeStruct((ne, m), F32)],
        compiler_params=_params(1), name="out_proj_router",
    )(*a_list, w_bf, x, gate, shift, scale, w_router)


def _mm_resident_kernel(a_ref, w_ref, o_ref, *, w_rows, chunk):
    w = w_ref[...].astype(BF)
    for r in range(0, a_ref.shape[0], chunk):
        rows = slice(r, r + chunk)
        a = a_ref[rows, :].astype(BF)
        if w_rows:
            o_ref[rows, :] = lax.dot_general(a, w, (((1,), (1,)), ((), ())), preferred_element_type=F32)
        else:
            o_ref[rows, :] = jnp.dot(a, w, preferred_element_type=F32)


def _mm_resident(a, w3, layer, n_cols, tn, w_rows=False, name="mm_resident"):
    m, k = a.shape
    assert w3.shape[2 if w_rows else 1] == k and n_cols % tn == 0
    if w_rows:
        w_spec = pl.BlockSpec((None, tn, k), lambda j: (layer, j, 0))
    else:
        w_spec = pl.BlockSpec((None, k, tn), lambda j: (layer, 0, j))
    return pl.pallas_call(
        functools.partial(_mm_resident_kernel, w_rows=w_rows, chunk=min(IN_PROJ_ROW_CHUNK, m)),
        grid=(n_cols // tn,),
        in_specs=[pl.BlockSpec((m, k), lambda j: (0, 0), pipeline_mode=pl.Buffered(1)), w_spec],
        out_specs=pl.BlockSpec((m, tn), lambda j: (0, j)),
        out_shape=jax.ShapeDtypeStruct((m, n_cols), F32),
        compiler_params=_params(1), name=name,
    )(a, w3)


def _norm_mm_kernel(x_ref, g_ref, w_ref, *out_refs, with_normed):
    wbf_ref = out_refs[-1]

    @pl.when(pl.program_id(0) == 0)
    def _():
        wbf_ref[...] = w_ref[...].astype(BF)

    xn = _rms(x_ref[...]) * g_ref[...]
    out_refs[0][...] = jnp.dot(xn.astype(BF), wbf_ref[...], preferred_element_type=F32)
    if with_normed:
        out_refs[1][...] = xn


def _norm_mm(x, col_blk, gain, w, with_normed, name):
    t = x.shape[0]
    k, n = w.shape
    tm = ROW_TILE
    out_specs = [pl.BlockSpec((tm, n), lambda i: (i, 0))]
    out_shape = [jax.ShapeDtypeStruct((t, n), F32)]
    if with_normed:
        out_specs.append(pl.BlockSpec((tm, k), lambda i: (i, 0)))
        out_shape.append(jax.ShapeDtypeStruct((t, k), F32))
    outs = pl.pallas_call(
        functools.partial(_norm_mm_kernel, with_normed=with_normed),
        grid=(t // tm,),
        in_specs=[pl.BlockSpec((tm, k), lambda i: (i, col_blk)),
                  pl.BlockSpec((1, k), lambda i: (0, 0)),
                  pl.BlockSpec((k, n), lambda i: (0, 0))],
        out_specs=out_specs, out_shape=out_shape,
        scratch_shapes=[pltpu.VMEM((k, n), BF)],
        compiler_params=_params(1), name=name,
    )(x, gain.reshape(1, k), w)
    return outs if with_normed else outs[0]


def _split3(x):
    hi = x.astype(BF)
    rest = x - hi.astype(F32)
    mid = rest.astype(BF)
    return hi, mid, (rest - mid.astype(F32)).astype(BF)


def _scan_max(x, reverse):
    n = x.shape[0]
    row = lax.broadcasted_iota(jnp.int32, x.shape, 0)
    k = 1
    while k < n:
        if reverse:
            shifted = jnp.where(row < n - k, pltpu.roll(x, n - k, axis=0), -jnp.inf)
        else:
            shifted = jnp.where(row >= k, pltpu.roll(x, k, axis=0), -jnp.inf)
        x = jnp.maximum(x, shifted)
        k *= 2
    return x


def _mlstm_kernel(qf_ref, kf_ref, vf_ref, qb_ref, kb_ref, vb_ref, gcf_ref, gcb_ref, grf_ref, grb_ref,
                  bc_ref, br_ref, *refs):
    hf_ref, hb_ref, c_ref, n_out_ref, m_ref, n_ref = refs[-6:]

    @pl.when(pl.program_id(1) == 0)
    def _():
        for state_ref, init_ref in zip((c_ref, n_ref, m_ref), refs[:-6] or (None,) * 3):
            state_ref[...] = jnp.zeros_like(state_ref) if init_ref is None else init_ref[...]

    nh, lc = ML_HEADS, ML_CHUNK
    row = lax.broadcasted_iota(jnp.int32, (lc, lc), 0)
    col = lax.broadcasted_iota(jnp.int32, (lc, lc), 1)
    lower = col <= row
    upper = col >= row
    ones_v = jnp.ones((lc, ML_V), BF)
    directions = ((qf_ref, kf_ref, vf_ref, gcf_ref, grf_ref, hf_ref, lower, upper),
                  (qb_ref, kb_ref, vb_ref, gcb_ref, grb_ref, hb_ref, upper, lower))
    seqs = range(c_ref.shape[0])
    gate_terms = {}
    for r, (d, (_, _, _, gc_ref, gr_ref, _, allowed, allowed_t)) in itertools.product(seqs, enumerate(directions)):
        g_col = gc_ref[r, :, GATE_LANE0:GATE_LANE0 + 4 * nh] + bc_ref[...]
        g_row = gr_ref[r] + br_ref[...]
        i_col = g_col[:, nh * d:nh * (d + 1)]
        f_col = _log_sigmoid(g_col[:, 2 * nh + nh * d:2 * nh + nh * (d + 1)])
        i_row = g_row[nh * d:nh * (d + 1), :]
        f_row = _log_sigmoid(g_row[2 * nh + nh * d:2 * nh + nh * (d + 1), :])
        tri = jnp.where(allowed, 1.0, 0.0).astype(BF)
        tri_t = jnp.where(allowed_t, 1.0, 0.0).astype(BF)
        b_col = sum(jnp.dot(tri, part, preferred_element_type=F32) for part in _split3(f_col))
        b_row = sum(jnp.dot(part, tri_t, preferred_element_type=F32) for part in _split3(f_row))
        b_end = jnp.sum(f_col, axis=0, keepdims=True)
        m_prev = m_ref[r, :, nh * d:nh * (d + 1)]
        m_t = b_col + jnp.maximum(m_prev, _scan_max(i_col - b_col, reverse=d == 1))
        g_col_end = b_end - b_col + i_col
        m_new = jnp.maximum(b_end + m_prev, jnp.max(g_col_end, axis=0, keepdims=True))
        gate_terms[r, d] = dict(
            u=b_col - m_t, r_row=i_row - b_row, w_inter=jnp.exp(b_col + m_prev - m_t), floor=jnp.exp(-m_t),
            k_scale=jnp.exp(g_col_end - m_new), decay=jnp.exp(b_end + m_prev - m_new), m_new=m_new)
    work = []
    for r, (d, (q_ref, k_ref, v_ref, _, _, h_ref, allowed, _)) in itertools.product(seqs, enumerate(directions)):
        for h in range(nh):
            w = dict(r=r, d=d, h=h, h_ref=h_ref, allowed=allowed, g=gate_terms[r, d])
            w["q"] = (q_ref[r, :, h * ML_QK:(h + 1) * ML_QK] * (ML_QK ** -0.5)).astype(BF)
            w["k"] = k_ref[r, :, h * ML_QK:(h + 1) * ML_QK]
            w["v1"] = jnp.concatenate([v_ref[r, :, h * ML_V:(h + 1) * ML_V].astype(BF), ones_v], axis=1)
            work.append(w)
    for w in work:
        w["qk"] = _bdot_nt(w["q"], w["k"])
    for w in work:
        r, d, h = w["r"], w["d"], w["h"]
        w["c_prev"], w["n_prev"] = c_ref[r, d, h], n_ref[r, d, h]
        state = jnp.concatenate([w["c_prev"], w["n_prev"]], axis=1).astype(BF)
        w["q_state"] = jnp.dot(w["q"], state, preferred_element_type=F32)
    for w in work:
        h, g = w["h"], w["g"]
        kw = w["k"] * g["k_scale"][:, h:h + 1]
        w["kv"] = _bdot_tn(kw, w["v1"])
    for w in work:
        h, g = w["h"], w["g"]
        dmat = jnp.where(w["allowed"], g["u"][:, h:h + 1] + g["r_row"][h:h + 1, :], -jnp.inf)
        sw = w["qk"] * jnp.exp(dmat)
        sw_hi = sw.astype(BF)
        sw_lo = (sw - sw_hi.astype(F32)).astype(BF)
        w["pv"] = jnp.dot(sw_hi, w["v1"], preferred_element_type=F32)
        w["den_lo"] = jnp.dot(sw_lo, ones_v, preferred_element_type=F32)
    for w in work:
        r, d, h, g = w["r"], w["d"], w["h"], w["g"]
        w_inter = g["w_inter"][:, h:h + 1]
        num = w_inter * w["q_state"][:, :ML_V] + w["pv"][:, :ML_V]
        den = w_inter * w["q_state"][:, ML_V:] + (w["pv"][:, ML_V:] + w["den_lo"])
        w["h_ref"][r, :, h * ML_V:(h + 1) * ML_V] = num / jnp.maximum(jnp.abs(den), g["floor"][:, h:h + 1])
        decay = g["decay"][:, h:h + 1]
        c_ref[r, d, h] = decay * w["c_prev"] + w["kv"][:, :ML_V]
        n_ref[r, d, h] = decay * w["n_prev"] + w["kv"][:, ML_V:]
    for r in seqs:
        m_ref[r, :, :nh] = gate_terms[r, 0]["m_new"]
        m_ref[r, :, nh:] = gate_terms[r, 1]["m_new"]

    @pl.when(pl.program_id(1) == pl.num_programs(1) - 1)
    def _():
        for r, d, h in itertools.product(seqs, range(2), range(nh)):
            n_out_ref[r, d, h:h + 1, :] = n_ref[r, d, h].T[:1, :]


def _mlstm(qkvo, side, g_row, b_col, b_row, init, bsz, seq):
    t = qkvo.shape[0]
    nc = seq // ML_CHUNK
    lc = ML_CHUNK
    nq = ML_HEADS * ML_QK
    nv = ML_HEADS * ML_V
    side_blk = (SIDE_COLS - 128) // 128
    rb = min(MLSTM_SEQS, bsz)
    assert bsz % rb == 0
    fwd = lambda c: c
    bwd = lambda c: nc - 1 - c

    def specs(pos):
        return [pl.BlockSpec((rb, lc, nq), lambda b, c: (b, pos(c), 0)),
                pl.BlockSpec((rb, lc, nq), lambda b, c: (b, pos(c), 1)),
                pl.BlockSpec((rb, lc, nv), lambda b, c: (b, pos(c), 1))]

    state = lambda *shape: pl.BlockSpec((rb,) + shape, lambda b, c: (b,) + (0,) * len(shape))
    state_specs = [state(2, ML_HEADS, ML_QK, ML_V), state(2, ML_HEADS, ML_QK, ML_V), state(1, 2 * ML_HEADS)]
    in_specs = specs(fwd) + specs(bwd) + [
        pl.BlockSpec((rb, lc, 128), lambda b, c: (b, fwd(c), side_blk)),
        pl.BlockSpec((rb, lc, 128), lambda b, c: (b, bwd(c), side_blk)),
        pl.BlockSpec((rb, None, 4 * ML_HEADS, lc), lambda b, c: (b, fwd(c), 0, 0)),
        pl.BlockSpec((rb, None, 4 * ML_HEADS, lc), lambda b, c: (b, bwd(c), 0, 0)),
        pl.BlockSpec((1, 4 * ML_HEADS), lambda b, c: (0, 0)),
        pl.BlockSpec((4 * ML_HEADS, 1), lambda b, c: (0, 0))]
    init = () if init is None else tuple(init)
    in_specs += state_specs[:len(init)]
    out_specs = [pl.BlockSpec((rb, lc, nv), lambda b, c: (b, fwd(c), 0)),
                 pl.BlockSpec((rb, lc, nv), lambda b, c: (b, bwd(c), 0)),
                 state_specs[0], state(2, ML_HEADS, ML_QK), state_specs[2]]
    out_shape = [jax.ShapeDtypeStruct((bsz, seq, nv), F32), jax.ShapeDtypeStruct((bsz, seq, nv), F32),
                 jax.ShapeDtypeStruct((bsz, 2, ML_HEADS, ML_QK, ML_V), F32),
                 jax.ShapeDtypeStruct((bsz, 2, ML_HEADS, ML_QK), F32),
                 jax.ShapeDtypeStruct((bsz, 1, 2 * ML_HEADS), F32)]
    qkvo3, side3 = qkvo.reshape(bsz, seq, -1), side.reshape(bsz, seq, -1)
    g_row4 = g_row.reshape(bsz, nc, 4 * ML_HEADS, lc)
    hf, hb, c_fin, n_fin, m_fin = pl.pallas_call(
        _mlstm_kernel, grid=(bsz // rb, nc), in_specs=in_specs, out_specs=out_specs, out_shape=out_shape,
        scratch_shapes=[pltpu.VMEM((rb, 2, ML_HEADS, ML_QK, ML_V), F32)],
        compiler_params=_params(2), name="mlstm",
    )(qkvo3, qkvo3, qkvo3, qkvo3, qkvo3, qkvo3, side3, side3, g_row4, g_row4, b_col, b_row, *init)
    return hf.reshape(t, nv), hb.reshape(t, nv), c_fin, n_fin, m_fin


def _mlstm_post_kernel(hf_ref, hb_ref, o_ref, g_ref, y_ref):
    for h in range(ML_HEADS):
        sl = slice(h * ML_V, (h + 1) * ML_V)
        hn = _rms(hf_ref[:, sl] + hb_ref[:, sl]) * g_ref[:, sl]
        y_ref[:, sl] = (hn * _sigmoid(o_ref[:, sl])).astype(BF)


def _mlstm_post(hf, hb, qkvo, gain):
    t, nv = hf.shape
    tm = ROW_TILE
    blk = pl.BlockSpec((tm, nv), lambda i: (i, 0))
    return pl.pallas_call(
        _mlstm_post_kernel, grid=(t // tm,),
        in_specs=[blk, blk, pl.BlockSpec((tm, nv), lambda i: (i, 2)), pl.BlockSpec((1, nv), lambda i: (0, 0))],
        out_specs=blk, out_shape=jax.ShapeDtypeStruct((t, nv), BF),
        compiler_params=_params(1), name="mlstm_post",
    )(hf, hb, qkvo, gain.reshape(1, nv))


LOG2_E = 1.4426950408889634


def _softmax_terms(scores, scale):
    c = scale * LOG2_E
    scaled = [s * c for s in scores]
    m = functools.reduce(jnp.maximum, [jnp.max(s, axis=-1, keepdims=True) for s in scaled])
    e = [jnp.exp2(s - m) for s in scaled]
    den = functools.reduce(lambda a, b: a + b, [jnp.sum(x, axis=-1, keepdims=True) for x in e])
    return [x.astype(BF) for x in e], 1.0 / den


def _walk_heads(n_heads, scores_fn, softmax_fn, values_fn, lag):
    scores, weights = {}, {}
    for t in range(n_heads + 2 * lag):
        if t < n_heads:
            scores[t] = scores_fn(t)
        if 0 <= t - lag < n_heads:
            weights[t - lag] = softmax_fn(t - lag, scores.pop(t - lag))
        if 0 <= t - 2 * lag < n_heads:
            values_fn(t - 2 * lag, weights.pop(t - 2 * lag))


def _mla_kernel(*refs, with_ctx):
    if with_ctx:
        qa_ref, kv_ref, side_ref, cq_ref, sq_ref, ck_ref, sk_ref, kvc_ref, krc_ref, o_ref = refs
    else:
        qa_ref, kv_ref, side_ref, o_ref = refs
    scale = (MLA_NOPE + MLA_ROPE) ** -0.5
    nope_cols = MLA_HEADS * MLA_NOPE
    q_rope = qa_ref[:, nope_cols:]
    k_rope = side_ref[...]
    if with_ctx:
        q_rope_rot = _rope(q_rope, cq_ref[...], sq_ref[...], MLA_ROPE // 4).astype(BF)
        k_rope_rot = _rope(k_rope, ck_ref[...], sk_ref[...], MLA_ROPE // 4)[:, :MLA_ROPE].astype(BF)
        k_rope_ctx = krc_ref[...].astype(BF)
        q_rope = q_rope.astype(BF)
    else:
        q_rope_rot = q_rope.astype(BF)
        k_rope_rot = k_rope[:, :MLA_ROPE].astype(BF)
    heads = range(MLA_HEADS)
    rope_cols = [slice(h * MLA_ROPE, (h + 1) * MLA_ROPE) for h in heads]
    kv0 = [h * (MLA_NOPE + MLA_V) for h in heads]
    q_n = [qa_ref[:, h * MLA_NOPE:(h + 1) * MLA_NOPE].astype(BF) for h in heads]

    def scores(h):
        blocks = [_bdot_nt(q_n[h], kv_ref[:, kv0[h]:kv0[h] + MLA_NOPE])
                  + _bdot_nt(q_rope_rot[:, rope_cols[h]], k_rope_rot)]
        if with_ctx:
            blocks.append(_bdot_nt(q_n[h], kvc_ref[:, kv0[h]:kv0[h] + MLA_NOPE])
                          + _bdot_nt(q_rope[:, rope_cols[h]], k_rope_ctx))
        return blocks

    def weighted_values(h, weights):
        (e, inv_den), v0 = weights, kv0[h] + MLA_NOPE
        out = jnp.dot(e[0], kv_ref[:, v0:v0 + MLA_V].astype(BF), preferred_element_type=F32)
        if with_ctx:
            out = out + jnp.dot(e[1], kvc_ref[:, v0:v0 + MLA_V].astype(BF), preferred_element_type=F32)
        o_ref[:, h * MLA_V:(h + 1) * MLA_V] = (out * inv_den).astype(BF)

    _walk_heads(MLA_HEADS, scores, lambda h, s: _softmax_terms(s, scale), weighted_values,
                lag=2 if with_ctx else MLA_HEADS)


def _mla_attention(qa, kv, side, bsz, seq, ctx=None):
    t = qa.shape[0]
    tq = ATTN_Q_ROWS
    nq = seq // tq
    side_blk = (SIDE_COLS - 128) // 128
    nkv = MLA_HEADS * (MLA_NOPE + MLA_V)
    in_specs = [pl.BlockSpec((tq, qa.shape[1]), lambda b, i: (b * nq + i, 0)),
                pl.BlockSpec((seq, nkv), lambda b, i: (b, 0)),
                pl.BlockSpec((seq, 128), lambda b, i: (b, side_blk))]
    args = [qa, kv, side]
    if ctx is not None:
        cos_q, sin_q, cos_k, sin_k, kvc, krc = ctx
        past = krc.shape[1]
        in_specs += [pl.BlockSpec((tq, cos_q.shape[1]), lambda b, i: (i, 0)),
                     pl.BlockSpec((tq, cos_q.shape[1]), lambda b, i: (i, 0)),
                     pl.BlockSpec((seq, 128), lambda b, i: (0, 0)),
                     pl.BlockSpec((seq, 128), lambda b, i: (0, 0)),
                     pl.BlockSpec((past, nkv), lambda b, i: (b, 0)),
                     pl.BlockSpec((None, past, MLA_ROPE), lambda b, i: (b, 0, 0))]
        args += [cos_q, sin_q, cos_k, sin_k, kvc, krc]
    nout = MLA_HEADS * MLA_V
    return pl.pallas_call(
        functools.partial(_mla_kernel, with_ctx=ctx is not None),
        grid=(bsz, nq), in_specs=in_specs,
        out_specs=pl.BlockSpec((tq, nout), lambda b, i: (b * nq + i, 0)),
        out_shape=jax.ShapeDtypeStruct((t, nout), BF),
        compiler_params=_params(2), name="mla_attention",
    )(*args)


def _gqa_kernel(*refs, with_ctx):
    if with_ctx:
        q_ref, k_ref, v_ref, gq_ref, gk_ref, cq_ref, sq_ref, ck_ref, sk_ref, kc_ref, vc_ref, o_ref, ksrc_ref = refs
    else:
        q_ref, k_ref, v_ref, gq_ref, gk_ref, o_ref, kn_ref, vo_ref, ksrc_ref = refs
    hd = GQA_HEAD_DIM
    scale = hd ** -0.5
    rep = GQA_HEADS // GQA_KV_HEADS
    n_kv = k_ref.shape[1] // hd
    kv_cols = [slice(g * hd, (g + 1) * hd) for g in range(n_kv)]

    @pl.when(pl.program_id(2) == 0)
    def _():
        for g in range(n_kv):
            k_n = _rms(k_ref[:, kv_cols[g]]) * gk_ref[...]
            if with_ctx:
                ksrc_ref[:, kv_cols[g]] = _rope(k_n, ck_ref[...], sk_ref[...], hd // 4).astype(BF)
            else:
                ksrc_ref[:, kv_cols[g]] = k_n.astype(BF)
                kn_ref[g] = k_n
                vo_ref[g] = v_ref[:, kv_cols[g]]

    v = [v_ref[:, kv_cols[g]].astype(BF) for g in range(n_kv)]
    if with_ctx:
        k_ctx, v_ctx = kc_ref[...].astype(BF), vc_ref[...].astype(BF)
    heads = range(n_kv * rep)
    cols = [slice(r * hd, (r + 1) * hd) for r in heads]
    q_n = [_rms(q_ref[:, cols[r]]) * gq_ref[...] for r in heads]
    k_src = [ksrc_ref[:, kv_cols[r // rep]] for r in heads]
    def scores(r):
        if with_ctx:
            return [_bdot_nt(_rope(q_n[r], cq_ref[...], sq_ref[...], hd // 4), k_src[r]), _bdot_nt(q_n[r], k_ctx)]
        return [_bdot_nt(q_n[r], k_src[r])]

    def weighted_values(r, weights):
        e, inv_den = weights
        out = jnp.dot(e[0], v[r // rep], preferred_element_type=F32)
        if with_ctx:
            out = out + jnp.dot(e[1], v_ctx, preferred_element_type=F32)
        o_ref[:, cols[r]] = (out * inv_den).astype(BF)

    _walk_heads(len(heads), scores, lambda r, s: _softmax_terms(s, scale), weighted_values, lag=1)


def _gqa_attention(qkv, g_q, g_k, bsz, seq, ctx=None):
    t = qkv.shape[0]
    hd = GQA_HEAD_DIM
    rep = GQA_HEADS // GQA_KV_HEADS
    tq = ATTN_Q_ROWS
    nq = seq // tq
    n_kv = 1 if ctx is not None else GQA_KV_HEADS
    k_blk0 = GQA_HEADS // n_kv
    in_specs = [pl.BlockSpec((tq, n_kv * rep * hd), lambda b, g, i: (b * nq + i, g)),
                pl.BlockSpec((seq, n_kv * hd), lambda b, g, i: (b, k_blk0 + g)),
                pl.BlockSpec((seq, n_kv * hd), lambda b, g, i: (b, k_blk0 + GQA_KV_HEADS // n_kv + g)),
                pl.BlockSpec((1, hd), lambda b, g, i: (0, 0)),
                pl.BlockSpec((1, hd), lambda b, g, i: (0, 0))]
    args = [qkv, qkv, qkv, g_q.reshape(1, hd), g_k.reshape(1, hd)]
    o_spec = pl.BlockSpec((tq, n_kv * rep * hd), lambda b, g, i: (b * nq + i, g))
    o_shape = jax.ShapeDtypeStruct((t, GQA_HEADS * hd), BF)
    if ctx is not None:
        cos_t, sin_t, kc, vc, j = ctx
        past = kc.shape[3]
        cache = pl.BlockSpec((None, None, None, past, hd), lambda b, g, i: (b, j, g, 0, 0))
        in_specs += [pl.BlockSpec((tq, hd), lambda b, g, i: (i, 0)),
                     pl.BlockSpec((tq, hd), lambda b, g, i: (i, 0)),
                     pl.BlockSpec((seq, hd), lambda b, g, i: (0, 0)),
                     pl.BlockSpec((seq, hd), lambda b, g, i: (0, 0)),
                     cache, cache]
        args += [cos_t, sin_t, cos_t, sin_t, kc, vc]
        out_specs, out_shape = o_spec, o_shape
    else:
        head_major = pl.BlockSpec((None, n_kv, seq, hd), lambda b, g, i: (b, g, 0, 0))
        out_specs = [o_spec, head_major, head_major]
        kv_shape = jax.ShapeDtypeStruct((bsz, GQA_KV_HEADS, seq, hd), F32)
        out_shape = [o_shape, kv_shape, kv_shape]
    return pl.pallas_call(
        functools.partial(_gqa_kernel, with_ctx=ctx is not None),
        grid=(bsz, GQA_KV_HEADS // n_kv, nq), in_specs=in_specs, out_specs=out_specs, out_shape=out_shape,
        scratch_shapes=[pltpu.VMEM((seq, n_kv * hd), BF)],
        compiler_params=_params(3), name="gqa_attention",
    )(*args)


def _dispatch_kernel(ar_ref, ac_ref, h_ref, xe_ref, gate_ref, rankc_ref, rankr_ref, below_ref, *, cap):
    n_groups, group, s = ar_ref.shape
    g = pl.program_id(1)

    @pl.when(g == 0)
    def _():
        blk = min(RANK_BLOCK, s)
        nb = s // blk
        below_ref[...] = jnp.zeros_like(below_ref)
        rankc_ref[...] = jnp.zeros_like(rankc_ref)
        ones_sub = jnp.ones((8, blk), BF)
        ones_lane = jnp.ones((blk, 128), BF)
        i0 = lax.broadcasted_iota(jnp.int32, (blk, blk), 0)
        i1 = lax.broadcasted_iota(jnp.int32, (blk, blk), 1)
        sub_first = jnp.where(i0 < i1, 1.0, 0.0)
        for e in range(n_groups * group):
            eg, ei = e // group, e % group
            for j in range(nb):
                rows = pl.ds(j * blk, blk)
                a_col = ac_ref[rows, e:e + 1]
                for k in range(j, nb):
                    cols = pl.ds(k * blk, blk)
                    a_row = ar_ref[eg, ei:ei + 1, cols]
                    if j == k:
                        ahead = jnp.where(a_col > a_row, 1.0, jnp.where(a_col >= a_row, sub_first, 0.0))
                    else:
                        ahead = jnp.where(a_col >= a_row, 1.0, 0.0)
                    ahead = ahead.astype(BF)
                    below_ref[e:e + 1, cols] += jnp.dot(ones_sub, ahead, preferred_element_type=F32)[:1, :]
                    if j < k:
                        rankc_ref[rows, e:e + 1] += blk - jnp.dot(ahead, ones_lane, preferred_element_type=F32)[:, :1]
        below, above = below_ref[...], rankc_ref[...]
        rank_rows = below + above.T
        for gi in range(n_groups):
            rankr_ref[gi] = rank_rows[gi * group:(gi + 1) * group, :]
        rankc_ref[...] = above + below.T

    slot = lax.broadcasted_iota(jnp.int32, (group, cap, s), 1).astype(F32)
    pick = rankr_ref[g][:, None, :] == slot
    onehot = jnp.where(pick, 1.0, 0.0).reshape(group * cap, s).astype(BF)
    rows_f32 = jnp.dot(onehot, h_ref[...], preferred_element_type=F32)
    xe_ref[...] = rows_f32.reshape(group, cap, -1).astype(BF)
    gate_ref[...] = jnp.sum(jnp.where(pick, ar_ref[g][:, None, :], 0.0), axis=2, keepdims=True)


def _dispatch(aff_row, aff_col, h, bsz, seq):
    t, d = h.shape
    ne = aff_row.shape[0]
    cap = EC_FACTOR * seq // ne
    group = GATHER_ROWS // cap
    n_groups = ne // group
    return pl.pallas_call(
        functools.partial(_dispatch_kernel, cap=cap),
        grid=(bsz, n_groups),
        in_specs=[pl.BlockSpec((n_groups, group, seq), lambda b, g: (0, 0, b)),
                  pl.BlockSpec((seq, ne), lambda b, g: (b, 0)),
                  pl.BlockSpec((seq, d), lambda b, g: (b, 0))],
        out_specs=[pl.BlockSpec((group, cap, d), lambda b, g: (g, b, 0)),
                   pl.BlockSpec((group, cap, 1), lambda b, g: (g, b, 0)),
                   pl.BlockSpec((seq, ne), lambda b, g: (b, 0))],
        out_shape=[jax.ShapeDtypeStruct((ne, bsz * cap, d), BF),
                   jax.ShapeDtypeStruct((ne, bsz * cap, 1), F32),
                   jax.ShapeDtypeStruct((t, ne), F32)],
        scratch_shapes=[pltpu.VMEM((n_groups, group, seq), F32), pltpu.VMEM((ne, seq), F32)],
        compiler_params=_params(2), name="ec_dispatch",
    )(aff_row.reshape(n_groups, group, t), aff_col, h)


def _experts_kernel(xp_ref, xs_ref, gp_ref, gs_ref, wg_ref, wu_ref, wd_ref, yp_ref, ys_ref, accp_ref, accs_ref):
    f = pl.program_id(1)
    tf = wg_ref.shape[1]
    w_gate_up = jnp.concatenate([wg_ref[...].astype(BF), wu_ref[...].astype(BF)], axis=1)
    wd = wd_ref[...].astype(BF)
    groups = ((xp_ref, gp_ref, yp_ref, accp_ref), (xs_ref, gs_ref, ys_ref, accs_ref))

    @pl.when(f == 0)
    def _():
        for _, _, _, acc_ref in groups:
            acc_ref[...] = jnp.zeros_like(acc_ref)

    for x_ref, _, _, acc_ref in groups:
        for r in range(0, x_ref.shape[0], EXPERT_ROWS):
            rows = slice(r, r + EXPERT_ROWS)
            au = jnp.dot(x_ref[rows, :], w_gate_up, preferred_element_type=F32)
            a, u = au[:, :tf], au[:, tf:]
            acc_ref[rows, :] += jnp.dot((a * _sigmoid(a) * u).astype(BF), wd, preferred_element_type=F32)

    @pl.when(f == pl.num_programs(1) - 1)
    def _():
        for _, g_ref, y_ref, acc_ref in groups:
            y_ref[...] = (acc_ref[...] * g_ref[...]).astype(BF)


def _experts(xe_p, xe_s, gate_p, gate_s, w_gate, w_up, w_down, layer):
    ne, mp, d = xe_p.shape
    ms = xe_s.shape[1]
    fdim = w_gate.shape[-1]
    tf = EXPERT_HIDDEN_COLS
    rows = lambda m, last: pl.BlockSpec((None, m, last), lambda e, f: (e, 0, 0))
    return pl.pallas_call(
        _experts_kernel, grid=(ne, fdim // tf),
        in_specs=[rows(mp, d), rows(ms, d), rows(mp, 1), rows(ms, 1),
                  pl.BlockSpec((None, None, d, tf), lambda e, f: (layer, e, 0, f)),
                  pl.BlockSpec((None, None, d, tf), lambda e, f: (layer, e, 0, f)),
                  pl.BlockSpec((None, None, tf, d), lambda e, f: (layer, e, f, 0))],
        out_specs=[rows(mp, d), rows(ms, d)],
        out_shape=[jax.ShapeDtypeStruct((ne, mp, d), BF), jax.ShapeDtypeStruct((ne, ms, d), BF)],
        scratch_shapes=[pltpu.VMEM((mp, d), F32), pltpu.VMEM((ms, d), F32)],
        compiler_params=_params(2), name="ec_experts",
    )(xe_p, xe_s, gate_p, gate_s, w_gate, w_up, w_down)


RANK_RADIX = 32


def _combine_kernel(y_ref, rank_ref, x_ref, g_ref, *refs, cap, final):
    post_refs, place_ref = refs[:-1], refs[-1]
    ne = y_ref.shape[0]
    n_slots = ne * cap

    @pl.when(pl.program_id(1) == 0)
    def _():
        rank = rank_ref[...]
        hi = jnp.floor(rank * (1.0 / RANK_RADIX))
        lo = rank - RANK_RADIX * hi
        lane = lax.broadcasted_iota(jnp.int32, (ne, n_slots), 1)
        expert = lax.broadcasted_iota(jnp.int32, (ne, n_slots), 0)
        own = (lane >= expert * cap) & (lane < (expert + 1) * cap)
        spread_hi = jnp.where(own, float(RANK_RADIX), 0.0).astype(BF)
        spread_lo = jnp.where(own, 1.0, 0.0).astype(BF)
        spread = (jnp.dot(hi.astype(BF), spread_hi, preferred_element_type=F32)
                  + jnp.dot(lo.astype(BF), spread_lo, preferred_element_type=F32))
        slot = (lax.broadcasted_iota(jnp.int32, (1, n_slots), 1) & (cap - 1)).astype(F32)
        place_ref[...] = jnp.where(spread == slot, 1.0, 0.0).astype(BF)

    ts = x_ref.shape[0]
    rows = pl.ds(pl.multiple_of(pl.program_id(1) * ts, ts), ts)
    y = y_ref[...].reshape(n_slots, y_ref.shape[2])
    x_new = x_ref[...] + g_ref[...] * jnp.dot(place_ref[rows, :], y, preferred_element_type=F32)
    if final:
        (gain_ref,), (out_ref,) = post_refs[:1], post_refs[1:]
        out_ref[...] = _rms(x_new) * gain_ref[...]
    else:
        (sh_ref, sc_ref), (xo_ref, h_ref) = post_refs[:2], post_refs[2:]
        xo_ref[...] = x_new
        h_ref[...] = (_rms(x_new) * (1.0 + sc_ref[...]) + sh_ref[...]).astype(BF)


def _combine(y, rank, x, gate, bsz, seq, next_shift_scale=None, final_gain=None):
    t, d = x.shape
    ne = y.shape[0]
    cap = EC_FACTOR * seq // ne
    assert seq <= RANK_RADIX * RANK_RADIX and cap & (cap - 1) == 0
    ts = min(COMBINE_ROWS, seq)
    nt = seq // ts
    final = final_gain is not None
    vec = lambda v: pl.BlockSpec((None, 1, d), lambda b, i: (b if v.shape[0] > 1 else 0, 0, 0))
    tile = pl.BlockSpec((ts, d), lambda b, i: (b * nt + i, 0))
    in_specs = [pl.BlockSpec((ne, cap, d), lambda b, i: (0, b, 0)),
                pl.BlockSpec((seq, ne), lambda b, i: (b, 0)), tile, vec(gate)]
    if final:
        args = [final_gain.reshape(1, d)]
        in_specs.append(pl.BlockSpec((1, d), lambda b, i: (0, 0)))
        out_specs, out_shape = tile, jax.ShapeDtypeStruct((t, d), F32)
    else:
        args = list(next_shift_scale)
        in_specs += [vec(v) for v in args]
        out_specs = [tile, tile]
        out_shape = [jax.ShapeDtypeStruct((t, d), F32), jax.ShapeDtypeStruct((t, d), BF)]
    return pl.pallas_call(
        functools.partial(_combine_kernel, cap=cap, final=final),
        grid=(bsz, nt), in_specs=in_specs, out_specs=out_specs, out_shape=out_shape,
        scratch_shapes=[pltpu.VMEM((seq, ne * cap), BF)],
        compiler_params=_params(2), name="ec_combine",
    )(y, rank, x, gate, *args)


def _rope_tables(n_tokens, rot_dim):
    rows = n_tokens // GRID_W
    row = jnp.repeat(jnp.arange(rows), GRID_W).astype(F32)
    col = jnp.tile(jnp.arange(GRID_W), rows).astype(F32)
    quarter = rot_dim // 4
    inv = ROPE_THETA ** (-jnp.arange(quarter, dtype=F32) / quarter)
    a_row, a_col = row[:, None] * inv, col[:, None] * inv
    cos_t = jnp.concatenate([jnp.cos(a_row), jnp.cos(a_row), jnp.cos(a_col), jnp.cos(a_col)], axis=-1)
    sin_t = jnp.concatenate([-jnp.sin(a_row), jnp.sin(a_row), -jnp.sin(a_col), jnp.sin(a_col)], axis=-1)
    return cos_t, sin_t


def kernel(x_prompt, x_sample, state_mlstm_c, state_mlstm_n, state_mlstm_m, cache_mla_ckv, cache_mla_krope,
           cache_gqa_k, cache_gqa_v, c, c_ctx, w_mod, b_mod, w_in_even, b_igate, b_fgate, g_mlstm, g_cq, w_uq,
           g_ckv, w_ukv, w_out_even, w_in_odd, g_qnorm, g_knorm, w_out_odd, w_router, w_expert_gate,
           w_expert_up, w_expert_down, g_final):
    d = D_MODEL
    bp, sp, _ = x_prompt.shape
    bs, ss, _ = x_sample.shape
    depth = w_mod.shape[0]
    nh = ML_HEADS
    streams = {"p": (bp, sp), "s": (bs, ss)}
    x = {"p": x_prompt.reshape(bp * sp, d), "s": x_sample.reshape(bs * ss, d)}

    c8 = jnp.concatenate([c_ctx[None], c, jnp.zeros((8 - 1 - bs, d), F32)], axis=0)
    mod_all = _mod_vectors(c8, w_mod, b_mod).reshape(depth, 8, 6, 1, d)

    def mod(layer, key, idx):
        rows = mod_all[layer, 0:1, idx] if key == "p" else mod_all[layer, 1:1 + bs, idx]
        return rows

    new_even, new_odd = [], []
    h_in = {key: _norm_mod(x[key], mod(0, key, 0), mod(0, key, 1), streams[key][1]) for key in streams}
    routed_in = {}
    for layer in range(depth):
        j = layer // 2
        if layer % 2 == 0:
            w_out = _to_bf16(w_out_even, j)
            w_in_rows = jnp.swapaxes(w_in_even, 1, 2)
            w_side = jnp.concatenate(
                [w_in_rows[j, QKVO_COLS + 4 * nh:],
                 w_in_rows[j, QKVO_COLS:QKVO_COLS + 4 * nh],
                 jnp.zeros((128 - MLA_ROPE - 4 * nh, d), F32)], axis=0)[None]
            w_q = w_uq[j].reshape(MLA_Q_RANK, MLA_HEADS, MLA_NOPE + MLA_ROPE)
            w_q = jnp.concatenate([w_q[:, :, :MLA_NOPE].reshape(MLA_Q_RANK, -1),
                                   w_q[:, :, MLA_NOPE:].reshape(MLA_Q_RANK, -1)], axis=1)
            bias_col = jnp.concatenate([b_igate[j].reshape(1, -1), b_fgate[j].reshape(1, -1)], axis=1)
            bias_row = bias_col.reshape(-1, 1)
            cos64, sin64 = _rope_tables(ss, MLA_ROPE)
            cos_q, sin_q = jnp.tile(cos64, (1, MLA_HEADS)), jnp.tile(sin64, (1, MLA_HEADS))
            pad = jnp.zeros((ss, 128 - MLA_ROPE), F32)
            cos_k, sin_k = jnp.concatenate([cos64, pad], axis=1), jnp.concatenate([sin64, pad], axis=1)
            kvc = _mm_resident(cache_mla_ckv[:, j].reshape(-1, MLA_KV_RANK), w_ukv, j, w_ukv.shape[-1], IN_PROJ_COLS,
                               name="mla_ctx_expand")
            for key, (bsz, seq) in streams.items():
                h = h_in[key]
                qkvo = _mm_resident(h, w_in_rows, j, QKVO_COLS, IN_PROJ_COLS, w_rows=True, name="even_in_main")
                side = _mm_resident(h, w_side, 0, SIDE_COLS, SIDE_COLS // 3, w_rows=True, name="even_in_side")
                gates = side[:, SIDE_COLS - 128 + GATE_LANE0:SIDE_COLS - 128 + GATE_LANE0 + 4 * nh]
                g_row = gates.reshape(-1, ML_CHUNK, 4 * nh).transpose(0, 2, 1)
                if key == "p":
                    init = None
                else:
                    c0 = state_mlstm_c[:, j]
                    init = (c0, jnp.broadcast_to(state_mlstm_n[:, j][..., None], c0.shape),
                            state_mlstm_m[:, j].reshape(bsz, 1, 2 * nh))
                hf, hb, c_fin, n_fin, m_fin = _mlstm(qkvo, side, g_row, bias_col, bias_row, init, bsz, seq)
                y_ml = _mlstm_post(hf, hb, qkvo, g_mlstm[j])
                qa = _norm_mm(side, 0, g_cq[j], w_q, False, "mla_q_up")
                kv, ckv_n = _norm_mm(side, 1, g_ckv[j], w_ukv[j], True, "mla_kv_up")
                if key == "p":
                    y_a = _mla_attention(qa, kv, side, bsz, seq)
                    k_rope = side[:, SIDE_COLS - 128:SIDE_COLS - 128 + MLA_ROPE]
                    new_even.append((c_fin, n_fin, m_fin.reshape(bsz, 2, nh),
                                     ckv_n.reshape(bsz, seq, -1), k_rope.reshape(bsz, seq, -1)))
                else:
                    y_a = _mla_attention(qa, kv, side, bsz, seq,
                                         ctx=(cos_q, sin_q, cos_k, sin_k, kvc, cache_mla_krope[:, j]))
                routed_in[key] = _out_proj_router([y_ml, y_a], w_out, x[key], mod(layer, key, 2), mod(layer, key, 3),
                                                  mod(layer, key, 4), w_router, layer, seq)
        else:
            w_out = _to_bf16(w_out_odd, j)
            cos_t, sin_t = _rope_tables(ss, GQA_HEAD_DIM)
            for key, (bsz, seq) in streams.items():
                qkv = _mm_resident(h_in[key], w_in_odd, j, w_in_odd.shape[-1], IN_PROJ_COLS, name="odd_in")
                if key == "p":
                    o, k_n, v = _gqa_attention(qkv, g_qnorm[j], g_knorm[j], bsz, seq)
                    new_odd.append((k_n, v))
                else:
                    o = _gqa_attention(qkv, g_qnorm[j], g_knorm[j], bsz, seq,
                                       ctx=(cos_t, sin_t, cache_gqa_k, cache_gqa_v, j))
                routed_in[key] = _out_proj_router([o], w_out, x[key], mod(layer, key, 2), mod(layer, key, 3),
                                                  mod(layer, key, 4), w_router, layer, seq)
        routed = {}
        for key, (bsz, seq) in streams.items():
            x[key], h, aff, aff_t = routed_in[key]
            routed[key] = _dispatch(aff_t, aff, h, bsz, seq)
        y_p, y_s = _experts(routed["p"][0], routed["s"][0], routed["p"][1], routed["s"][1],
                            w_expert_gate, w_expert_up, w_expert_down, layer)
        for key, y in (("p", y_p), ("s", y_s)):
            bsz, seq = streams[key]
            if layer + 1 < depth:
                x[key], h_in[key] = _combine(
                    y, routed[key][2], x[key], mod(layer, key, 5), bsz, seq,
                    next_shift_scale=(mod(layer + 1, key, 0), mod(layer + 1, key, 1)))
            else:
                x[key] = _combine(y, routed[key][2], x[key], mod(layer, key, 5), bsz, seq, final_gain=g_final)

    y_prompt = x["p"].reshape(bp, sp, d)
    y_sample = x["s"].reshape(bs, ss, d)
    new_c = jnp.stack([e[0] for e in new_even], axis=1)
    new_n = jnp.stack([e[1] for e in new_even], axis=1)
    new_m = jnp.stack([e[2] for e in new_even], axis=1)
    new_ckv = jnp.stack([e[3] for e in new_even], axis=1)
    new_krope = jnp.stack([e[4] for e in new_even], axis=1)
    new_k = jnp.stack([e[0] for e in new_odd], axis=1)
    new_v = jnp.stack([e[1] for e in new_odd], axis=1)
    return (y_prompt, y_sample, new_c, new_n, new_m, new_ckv, new_krope, new_k, new_v)
```

```python
import functools
import itertools

import jax
import jax.numpy as jnp
from jax import lax
from jax.experimental import pallas as pl
from jax.experimental.pallas import tpu as pltpu

BF = jnp.bfloat16
F32 = jnp.float32

D_MODEL = 2048
GRID_W = 64
ROPE_THETA = 10000.0
NORM_EPS = 1e-6
ML_HEADS = 8
ML_QK = 64
ML_V = 128
ML_CHUNK = 64
MLA_HEADS = 8
MLA_Q_RANK = 512
MLA_KV_RANK = 512
MLA_NOPE = 128
MLA_ROPE = 64
MLA_V = 128
GQA_HEADS = 16
GQA_KV_HEADS = 4
GQA_HEAD_DIM = 128
EC_FACTOR = 2

QKVO_COLS = 2 * ML_HEADS * ML_QK + 2 * ML_HEADS * ML_V
SIDE_COLS = MLA_Q_RANK + MLA_KV_RANK + 128
GATE_LANE0 = MLA_ROPE

VMEM_LIMIT_BYTES = 56 * 1024 * 1024
OUT_PROJ_ROWS = 512
MLSTM_SEQS = 4
MOD_COLS = 1024
IN_PROJ_COLS = 512
IN_PROJ_ROW_CHUNK = 512
CAST_ROWS = 512
ROW_TILE = 512
ATTN_Q_ROWS = 256
EXPERT_HIDDEN_COLS = 512
EXPERT_ROWS = 256
RANK_BLOCK = 256
GATHER_ROWS = 512
COMBINE_ROWS = 256


def _params(n_axes):
    return pltpu.CompilerParams(dimension_semantics=("arbitrary",) * n_axes,
                                vmem_limit_bytes=VMEM_LIMIT_BYTES)


def _bdot(a, b):
    return jnp.dot(a.astype(BF), b.astype(BF), preferred_element_type=F32)


def _bdot_nt(a, b):
    return lax.dot_general(a.astype(BF), b.astype(BF), (((1,), (1,)), ((), ())),
                           preferred_element_type=F32)


def _bdot_tn(a, b):
    return lax.dot_general(a.astype(BF), b.astype(BF), (((0,), (0,)), ((), ())),
                           preferred_element_type=F32)


def _sigmoid(x):
    return 1.0 / (1.0 + jnp.exp(-x))


def _log_sigmoid(x):
    return jnp.minimum(x, 0.0) - jnp.log1p(jnp.exp(-jnp.abs(x)))


def _rms(x):
    return x * lax.rsqrt(jnp.mean(x * x, axis=-1, keepdims=True) + NORM_EPS)


def _rope(x, cos_t, sin_t, quarter):
    width = x.shape[-1]
    axis = x.ndim - 1
    lane = lax.broadcasted_iota(jnp.int32, x.shape, axis)
    partner = jnp.where((lane & quarter) == 0,
                        pltpu.roll(x, width - quarter, axis=axis),
                        pltpu.roll(x, quarter, axis=axis))
    return x * cos_t + partner * sin_t


def _mod_kernel(c_ref, w_ref, b_ref, o_ref):
    c = c_ref[...]
    o_ref[...] = _bdot(c * _sigmoid(c), w_ref[...]) + b_ref[...]


def _mod_vectors(c8, w_mod, b_mod):
    n_layers, k, n = w_mod.shape
    tn = MOD_COLS
    return pl.pallas_call(
        _mod_kernel,
        grid=(n_layers, n // tn),
        in_specs=[pl.BlockSpec((8, k), lambda l, j: (0, 0)),
                  pl.BlockSpec((None, k, tn), lambda l, j: (l, 0, j)),
                  pl.BlockSpec((None, 1, tn), lambda l, j: (l, 0, j))],
        out_specs=pl.BlockSpec((None, 8, tn), lambda l, j: (l, 0, j)),
        out_shape=jax.ShapeDtypeStruct((n_layers, 8, n), F32),
        compiler_params=_params(2),
        name="mod_vectors",
    )(c8, w_mod, b_mod.reshape(n_layers, 1, n))


def _norm_mod_kernel(x_ref, sh_ref, sc_ref, h_ref):
    h_ref[...] = (_rms(x_ref[...]) * (1.0 + sc_ref[...]) + sh_ref[...]).astype(BF)


def _batch_of_tile(n_vectors, tm, seq):
    if n_vectors == 1:
        return lambda i: 0
    assert seq % tm == 0
    return lambda i: (i * tm) // seq


def _norm_mod(x, shift, scale, seq):
    t, d = x.shape
    tm = OUT_PROJ_ROWS
    which = _batch_of_tile(shift.shape[0], tm, seq)
    vec = pl.BlockSpec((None, 1, d), lambda i: (which(i), 0, 0))
    x_spec = pl.BlockSpec((tm, d), lambda i: (i, 0))
    return pl.pallas_call(
        _norm_mod_kernel, grid=(t // tm,),
        in_specs=[x_spec, vec, vec], out_specs=x_spec,
        out_shape=jax.ShapeDtypeStruct((t, d), BF),
        compiler_params=_params(1), name="norm_mod",
    )(x, shift, scale)


def _to_bf16_kernel(w_ref, o_ref):
    o_ref[...] = w_ref[...].astype(BF)


def _to_bf16(w3, layer):
    _, k, n = w3.shape
    tk = CAST_ROWS
    return pl.pallas_call(
        _to_bf16_kernel, grid=(k // tk,),
        in_specs=[pl.BlockSpec((None, tk, n), lambda i: (layer, i, 0))],
        out_specs=pl.BlockSpec((tk, n), lambda i: (i, 0)),
        out_shape=jax.ShapeDtypeStruct((k, n), BF),
        compiler_params=_params(1), name="weight_to_bf16",
    )(w3)


def _out_proj_router_kernel(*refs, k_sizes):
    n_a = len(k_sizes)
    a_refs = refs[:n_a]
    w_ref, x_ref, g_ref, sh_ref, sc_ref, wr_ref, xo_ref, h_ref, aff_ref, afft_ref = refs[n_a:]
    w_router = wr_ref[...].astype(BF)
    chunk = OUT_PROJ_ROWS // 2
    for r in range(0, x_ref.shape[0], chunk):
        rows = slice(r, r + chunk)
        acc, off = None, 0
        for a_ref, ks in zip(a_refs, k_sizes):
            term = jnp.dot(a_ref[rows, :], w_ref[off:off + ks, :], preferred_element_type=F32)
            acc = term if acc is None else acc + term
            off += ks
        x_new = x_ref[rows, :] + g_ref[...] * acc
        xo_ref[rows, :] = x_new
        h = (_rms(x_new) * (1.0 + sc_ref[...]) + sh_ref[...]).astype(BF)
        h_ref[rows, :] = h
        logits = jnp.dot(h, w_router, preferred_element_type=F32)
        e = jnp.exp(logits - jnp.max(logits, axis=-1, keepdims=True))
        aff = e / jnp.sum(e, axis=-1, keepdims=True)
        aff_ref[rows, :] = aff
        afft_ref[:, rows] = aff.T


def _out_proj_router(a_list, w_bf, x, gate, shift, scale, w_router, layer, seq):
    m, d = x.shape
    k_sizes = tuple(a.shape[1] for a in a_list)
    assert w_bf.shape == (sum(k_sizes), d) and all(a.dtype == BF for a in a_list)
    tm = OUT_PROJ_ROWS
    ne = w_router.shape[-1]
    which = _batch_of_tile(gate.shape[0], tm, seq)
    vec = pl.BlockSpec((None, 1, d), lambda i: (which(i), 0, 0))
    tile = pl.BlockSpec((tm, d), lambda i: (i, 0))
    in_specs = [pl.BlockSpec((tm, ks), lambda i: (i, 0)) for ks in k_sizes]
    in_specs += [pl.BlockSpec(w_bf.shape, lambda i: (0, 0), pipeline_mode=pl.Buffered(1)),
                 tile, vec, vec, vec, pl.BlockSpec((None, d, ne), lambda i: (layer, 0, 0))]
    return pl.pallas_call(
        functools.partial(_out_proj_router_kernel, k_sizes=k_sizes),
        grid=(m // tm,), in_specs=in_specs,
        out_specs=[tile, tile, pl.BlockSpec((tm, ne), lambda i: (i, 0)), pl.BlockSpec((ne, tm), lambda i: (0, i))],
        out_shape=[jax.ShapeDtypeStruct((m, d), F32), jax.ShapeDtypeStruct((m, d), BF),
                   jax.ShapeDtypeStruct((m, ne), F32), jax.ShapeDtypeStruct((ne, m), F32)],
        compiler_params=_params(1), name="out_proj_router",
    )(*a_list, w_bf, x, gate, shift, scale, w_router)


def _mm_resident_kernel(a_ref, w_ref, o_ref, *, w_rows, chunk):
    w = w_ref[...].astype(BF)
    for r in range(0, a_ref.shape[0], chunk):
        rows = slice(r, r + chunk)
        a = a_ref[rows, :].astype(BF)
        if w_rows:
            out = lax.dot_general(a, w, (((1,), (1,)), ((), ())), preferred_element_type=F32)
        else:
            out = jnp.dot(a, w, preferred_element_type=F32)
        o_ref[rows, :] = out.astype(o_ref.dtype)


def _mm_resident(a, w3, layer, n_cols, tn, w_rows=False, name="mm_resident", out_dtype=F32):
    m, k = a.shape
    assert w3.shape[2 if w_rows else 1] == k and n_cols % tn == 0
    if w_rows:
        w_spec = pl.BlockSpec((None, tn, k), lambda j: (layer, j, 0))
    else:
        w_spec = pl.BlockSpec((None, k, tn), lambda j: (layer, 0, j))
    return pl.pallas_call(
        functools.partial(_mm_resident_kernel, w_rows=w_rows, chunk=min(IN_PROJ_ROW_CHUNK, m)),
        grid=(n_cols // tn,),
        in_specs=[pl.BlockSpec((m, k), lambda j: (0, 0), pipeline_mode=pl.Buffered(1)), w_spec],
        out_specs=pl.BlockSpec((m, tn), lambda j: (0, j)),
        out_shape=jax.ShapeDtypeStruct((m, n_cols), out_dtype),
        compiler_params=_params(1), name=name,
    )(a, w3)


def _norm_mm_kernel(x_ref, g_ref, w_ref, *out_refs, with_normed):
    wbf_ref = out_refs[-1]

    @pl.when(pl.program_id(0) == 0)
    def _():
        wbf_ref[...] = w_ref[...].astype(BF)

    xn = _rms(x_ref[...]) * g_ref[...]
    out_refs[0][...] = jnp.dot(xn.astype(BF), wbf_ref[...], preferred_element_type=F32).astype(out_refs[0].dtype)
    if with_normed:
        out_refs[1][...] = xn


def _norm_mm(x, col_blk, gain, w, with_normed, name, out_dtype=F32):
    t = x.shape[0]
    k, n = w.shape
    tm = ROW_TILE
    out_specs = [pl.BlockSpec((tm, n), lambda i: (i, 0))]
    out_shape = [jax.ShapeDtypeStruct((t, n), out_dtype)]
    if with_normed:
        out_specs.append(pl.BlockSpec((tm, k), lambda i: (i, 0)))
        out_shape.append(jax.ShapeDtypeStruct((t, k), F32))
    outs = pl.pallas_call(
        functools.partial(_norm_mm_kernel, with_normed=with_normed),
        grid=(t // tm,),
        in_specs=[pl.BlockSpec((tm, k), lambda i: (i, col_blk)),
                  pl.BlockSpec((1, k), lambda i: (0, 0)),
                  pl.BlockSpec((k, n), lambda i: (0, 0))],
        out_specs=out_specs, out_shape=out_shape,
        scratch_shapes=[pltpu.VMEM((k, n), BF)],
        compiler_params=_params(1), name=name,
    )(x, gain.reshape(1, k), w)
    return outs if with_normed else outs[0]


def _split3(x):
    hi = x.astype(BF)
    rest = x - hi.astype(F32)
    mid = rest.astype(BF)
    return hi, mid, (rest - mid.astype(F32)).astype(BF)


def _scan_max(x, reverse):
    n = x.shape[0]
    row = lax.broadcasted_iota(jnp.int32, x.shape, 0)
    k = 1
    while k < n:
        if reverse:
            shifted = jnp.where(row < n - k, pltpu.roll(x, n - k, axis=0), -jnp.inf)
        else:
            shifted = jnp.where(row >= k, pltpu.roll(x, k, axis=0), -jnp.inf)
        x = jnp.maximum(x, shifted)
        k *= 2
    return x


def _mlstm_kernel(qf_ref, kf_ref, vf_ref, qb_ref, kb_ref, vb_ref, gcf_ref, gcb_ref, grf_ref, grb_ref,
                  bc_ref, br_ref, *refs):
    hf_ref, hb_ref, c_ref, n_out_ref, m_ref, n_ref = refs[-6:]

    @pl.when(pl.program_id(1) == 0)
    def _():
        for state_ref, init_ref in zip((c_ref, n_ref, m_ref), refs[:-6] or (None,) * 3):
            state_ref[...] = jnp.zeros_like(state_ref) if init_ref is None else init_ref[...]

    nh, lc = ML_HEADS, ML_CHUNK
    row = lax.broadcasted_iota(jnp.int32, (lc, lc), 0)
    col = lax.broadcasted_iota(jnp.int32, (lc, lc), 1)
    lower = col <= row
    upper = col >= row
    ones_v = jnp.ones((lc, ML_V), BF)
    directions = ((qf_ref, kf_ref, vf_ref, gcf_ref, grf_ref, hf_ref, lower, upper),
                  (qb_ref, kb_ref, vb_ref, gcb_ref, grb_ref, hb_ref, upper, lower))
    seqs = range(c_ref.shape[0])
    gate_terms = {}
    for r, (d, (_, _, _, gc_ref, gr_ref, _, allowed, allowed_t)) in itertools.product(seqs, enumerate(directions)):
        g_col = gc_ref[r, :, GATE_LANE0:GATE_LANE0 + 4 * nh] + bc_ref[...]
        g_row = gr_ref[r] + br_ref[...]
        i_col = g_col[:, nh * d:nh * (d + 1)]
        f_col = _log_sigmoid(g_col[:, 2 * nh + nh * d:2 * nh + nh * (d + 1)])
        i_row = g_row[nh * d:nh * (d + 1), :]
        f_row = _log_sigmoid(g_row[2 * nh + nh * d:2 * nh + nh * (d + 1), :])
        tri = jnp.where(allowed, 1.0, 0.0).astype(BF)
        tri_t = jnp.where(allowed_t, 1.0, 0.0).astype(BF)
        b_col = sum(jnp.dot(tri, part, preferred_element_type=F32) for part in _split3(f_col))
        b_row = sum(jnp.dot(part, tri_t, preferred_element_type=F32) for part in _split3(f_row))
        b_end = jnp.sum(f_col, axis=0, keepdims=True)
        m_prev = m_ref[r, :, nh * d:nh * (d + 1)]
        m_t = b_col + jnp.maximum(m_prev, _scan_max(i_col - b_col, reverse=d == 1))
        g_col_end = b_end - b_col + i_col
        m_new = jnp.maximum(b_end + m_prev, jnp.max(g_col_end, axis=0, keepdims=True))
        gate_terms[r, d] = dict(
            u=b_col - m_t, r_row=i_row - b_row, w_inter=jnp.exp(b_col + m_prev - m_t), floor=jnp.exp(-m_t),
            k_scale=jnp.exp(g_col_end - m_new), decay=jnp.exp(b_end + m_prev - m_new), m_new=m_new)
    work = []
    for r, (d, (q_ref, k_ref, v_ref, _, _, h_ref, allowed, _)) in itertools.product(seqs, enumerate(directions)):
        for h in range(nh):
            w = dict(r=r, d=d, h=h, h_ref=h_ref, allowed=allowed, g=gate_terms[r, d])
            w["q"] = (q_ref[r, :, h * ML_QK:(h + 1) * ML_QK] * (ML_QK ** -0.5)).astype(BF)
            w["k"] = k_ref[r, :, h * ML_QK:(h + 1) * ML_QK]
            w["v1"] = jnp.concatenate([v_ref[r, :, h * ML_V:(h + 1) * ML_V].astype(BF), ones_v], axis=1)
            work.append(w)
    for w in work:
        w["qk"] = _bdot_nt(w["q"], w["k"])
    for w in work:
        r, d, h = w["r"], w["d"], w["h"]
        w["c_prev"], w["n_prev"] = c_ref[r, d, h], n_ref[r, d, h]
        state = jnp.concatenate([w["c_prev"], w["n_prev"]], axis=1).astype(BF)
        w["q_state"] = jnp.dot(w["q"], state, preferred_element_type=F32)
    for w in work:
        h, g = w["h"], w["g"]
        kw = w["k"] * g["k_scale"][:, h:h + 1]
        w["kv"] = _bdot_tn(kw, w["v1"])
    for w in work:
        h, g = w["h"], w["g"]
        dmat = jnp.where(w["allowed"], g["u"][:, h:h + 1] + g["r_row"][h:h + 1, :], -jnp.inf)
        sw = w["qk"] * jnp.exp(dmat)
        sw_hi = sw.astype(BF)
        sw_lo = (sw - sw_hi.astype(F32)).astype(BF)
        w["pv"] = jnp.dot(sw_hi, w["v1"], preferred_element_type=F32)
        w["den_lo"] = jnp.dot(sw_lo, ones_v, preferred_element_type=F32)
    for w in work:
        r, d, h, g = w["r"], w["d"], w["h"], w["g"]
        w_inter = g["w_inter"][:, h:h + 1]
        num = w_inter * w["q_state"][:, :ML_V] + w["pv"][:, :ML_V]
        den = w_inter * w["q_state"][:, ML_V:] + (w["pv"][:, ML_V:] + w["den_lo"])
        w["h_ref"][r, :, h * ML_V:(h + 1) * ML_V] = num / jnp.maximum(jnp.abs(den), g["floor"][:, h:h + 1])
        decay = g["decay"][:, h:h + 1]
        c_ref[r, d, h] = decay * w["c_prev"] + w["kv"][:, :ML_V]
        n_ref[r, d, h] = decay * w["n_prev"] + w["kv"][:, ML_V:]
    for r in seqs:
        m_ref[r, :, :nh] = gate_terms[r, 0]["m_new"]
        m_ref[r, :, nh:] = gate_terms[r, 1]["m_new"]

    @pl.when(pl.program_id(1) == pl.num_programs(1) - 1)
    def _():
        for r, d, h in itertools.product(seqs, range(2), range(nh)):
            n_out_ref[r, d, h:h + 1, :] = n_ref[r, d, h].T[:1, :]


def _mlstm(qkvo, side, g_row, b_col, b_row, init, bsz, seq):
    t = qkvo.shape[0]
    nc = seq // ML_CHUNK
    lc = ML_CHUNK
    nq = ML_HEADS * ML_QK
    nv = ML_HEADS * ML_V
    side_blk = (SIDE_COLS - 128) // 128
    rb = min(MLSTM_SEQS, bsz)
    assert bsz % rb == 0
    fwd = lambda c: c
    bwd = lambda c: nc - 1 - c

    def specs(pos):
        return [pl.BlockSpec((rb, lc, nq), lambda b, c: (b, pos(c), 0)),
                pl.BlockSpec((rb, lc, nq), lambda b, c: (b, pos(c), 1)),
                pl.BlockSpec((rb, lc, nv), lambda b, c: (b, pos(c), 1))]

    state = lambda *shape: pl.BlockSpec((rb,) + shape, lambda b, c: (b,) + (0,) * len(shape))
    state_specs = [state(2, ML_HEADS, ML_QK, ML_V), state(2, ML_HEADS, ML_QK, ML_V), state(1, 2 * ML_HEADS)]
    in_specs = specs(fwd) + specs(bwd) + [
        pl.BlockSpec((rb, lc, 128), lambda b, c: (b, fwd(c), side_blk)),
        pl.BlockSpec((rb, lc, 128), lambda b, c: (b, bwd(c), side_blk)),
        pl.BlockSpec((rb, None, 4 * ML_HEADS, lc), lambda b, c: (b, fwd(c), 0, 0)),
        pl.BlockSpec((rb, None, 4 * ML_HEADS, lc), lambda b, c: (b, bwd(c), 0, 0)),
        pl.BlockSpec((1, 4 * ML_HEADS), lambda b, c: (0, 0)),
        pl.BlockSpec((4 * ML_HEADS, 1), lambda b, c: (0, 0))]
    init = () if init is None else tuple(init)
    in_specs += state_specs[:len(init)]
    out_specs = [pl.BlockSpec((rb, lc, nv), lambda b, c: (b, fwd(c), 0)),
                 pl.BlockSpec((rb, lc, nv), lambda b, c: (b, bwd(c), 0)),
                 state_specs[0], state(2, ML_HEADS, ML_QK), state_specs[2]]
    out_shape = [jax.ShapeDtypeStruct((bsz, seq, nv), F32), jax.ShapeDtypeStruct((bsz, seq, nv), F32),
                 jax.ShapeDtypeStruct((bsz, 2, ML_HEADS, ML_QK, ML_V), F32),
                 jax.ShapeDtypeStruct((bsz, 2, ML_HEADS, ML_QK), F32),
                 jax.ShapeDtypeStruct((bsz, 1, 2 * ML_HEADS), F32)]
    qkvo3, side3 = qkvo.reshape(bsz, seq, -1), side.reshape(bsz, seq, -1)
    g_row4 = g_row.reshape(bsz, nc, 4 * ML_HEADS, lc)
    hf, hb, c_fin, n_fin, m_fin = pl.pallas_call(
        _mlstm_kernel, grid=(bsz // rb, nc), in_specs=in_specs, out_specs=out_specs, out_shape=out_shape,
        scratch_shapes=[pltpu.VMEM((rb, 2, ML_HEADS, ML_QK, ML_V), F32)],
        compiler_params=_params(2), name="mlstm",
    )(qkvo3, qkvo3, qkvo3, qkvo3, qkvo3, qkvo3, side3, side3, g_row4, g_row4, b_col, b_row, *init)
    return hf.reshape(t, nv), hb.reshape(t, nv), c_fin, n_fin, m_fin


def _mlstm_post_kernel(hf_ref, hb_ref, o_ref, g_ref, y_ref):
    for h in range(ML_HEADS):
        sl = slice(h * ML_V, (h + 1) * ML_V)
        hn = _rms(hf_ref[:, sl] + hb_ref[:, sl]) * g_ref[:, sl]
        y_ref[:, sl] = (hn * _sigmoid(o_ref[:, sl])).astype(BF)


def _mlstm_post(hf, hb, qkvo, gain):
    t, nv = hf.shape
    tm = ROW_TILE
    blk = pl.BlockSpec((tm, nv), lambda i: (i, 0))
    return pl.pallas_call(
        _mlstm_post_kernel, grid=(t // tm,),
        in_specs=[blk, blk, pl.BlockSpec((tm, nv), lambda i: (i, 2)), pl.BlockSpec((1, nv), lambda i: (0, 0))],
        out_specs=blk, out_shape=jax.ShapeDtypeStruct((t, nv), BF),
        compiler_params=_params(1), name="mlstm_post",
    )(hf, hb, qkvo, gain.reshape(1, nv))


LOG2_E = 1.4426950408889634


def _softmax_terms(scores, scale):
    c = scale * LOG2_E
    scaled = [s * c for s in scores]
    m = functools.reduce(jnp.maximum, [jnp.max(s, axis=-1, keepdims=True) for s in scaled])
    e = [jnp.exp2(s - m) for s in scaled]
    den = functools.reduce(lambda a, b: a + b, [jnp.sum(x, axis=-1, keepdims=True) for x in e])
    return [x.astype(BF) for x in e], 1.0 / den


def _walk_heads(n_heads, scores_fn, softmax_fn, values_fn, lag):
    scores, weights = {}, {}
    for t in range(n_heads + 2 * lag):
        if t < n_heads:
            scores[t] = scores_fn(t)
        if 0 <= t - lag < n_heads:
            weights[t - lag] = softmax_fn(t - lag, scores.pop(t - lag))
        if 0 <= t - 2 * lag < n_heads:
            values_fn(t - 2 * lag, weights.pop(t - 2 * lag))


def _mla_kernel(*refs, with_ctx):
    if with_ctx:
        qa_ref, kv_ref, side_ref, cq_ref, sq_ref, ck_ref, sk_ref, kvc_ref, krc_ref, o_ref = refs
    else:
        qa_ref, kv_ref, side_ref, o_ref = refs
    scale = (MLA_NOPE + MLA_ROPE) ** -0.5
    nope_cols = MLA_HEADS * MLA_NOPE
    q_rope = qa_ref[:, nope_cols:]
    k_rope = side_ref[...]
    if with_ctx:
        q_rope_rot = _rope(q_rope, cq_ref[...], sq_ref[...], MLA_ROPE // 4).astype(BF)
        k_rope_rot = _rope(k_rope, ck_ref[...], sk_ref[...], MLA_ROPE // 4)[:, :MLA_ROPE].astype(BF)
        k_rope_ctx = krc_ref[...].astype(BF)
        q_rope = q_rope.astype(BF)
    else:
        q_rope_rot = q_rope.astype(BF)
        k_rope_rot = k_rope[:, :MLA_ROPE].astype(BF)
    heads = range(MLA_HEADS)
    rope_cols = [slice(h * MLA_ROPE, (h + 1) * MLA_ROPE) for h in heads]
    kv0 = [h * (MLA_NOPE + MLA_V) for h in heads]
    q_n = [qa_ref[:, h * MLA_NOPE:(h + 1) * MLA_NOPE].astype(BF) for h in heads]

    def scores(h):
        blocks = [_bdot_nt(q_n[h], kv_ref[:, kv0[h]:kv0[h] + MLA_NOPE])
                  + _bdot_nt(q_rope_rot[:, rope_cols[h]], k_rope_rot)]
        if with_ctx:
            blocks.append(_bdot_nt(q_n[h], kvc_ref[:, kv0[h]:kv0[h] + MLA_NOPE])
                          + _bdot_nt(q_rope[:, rope_cols[h]], k_rope_ctx))
        return blocks

    def weighted_values(h, weights):
        (e, inv_den), v0 = weights, kv0[h] + MLA_NOPE
        out = jnp.dot(e[0], kv_ref[:, v0:v0 + MLA_V].astype(BF), preferred_element_type=F32)
        if with_ctx:
            out = out + jnp.dot(e[1], kvc_ref[:, v0:v0 + MLA_V].astype(BF), preferred_element_type=F32)
        o_ref[:, h * MLA_V:(h + 1) * MLA_V] = (out * inv_den).astype(BF)

    _walk_heads(MLA_HEADS, scores, lambda h, s: _softmax_terms(s, scale), weighted_values,
                lag=2 if with_ctx else MLA_HEADS)


def _mla_attention(qa, kv, side, bsz, seq, ctx=None):
    t = qa.shape[0]
    tq = ATTN_Q_ROWS
    nq = seq // tq
    side_blk = (SIDE_COLS - 128) // 128
    nkv = MLA_HEADS * (MLA_NOPE + MLA_V)
    in_specs = [pl.BlockSpec((tq, qa.shape[1]), lambda b, i: (b * nq + i, 0)),
                pl.BlockSpec((seq, nkv), lambda b, i: (b, 0)),
                pl.BlockSpec((seq, 128), lambda b, i: (b, side_blk))]
    args = [qa, kv, side]
    if ctx is not None:
        cos_q, sin_q, cos_k, sin_k, kvc, krc = ctx
        past = krc.shape[1]
        in_specs += [pl.BlockSpec((tq, cos_q.shape[1]), lambda b, i: (i, 0)),
                     pl.BlockSpec((tq, cos_q.shape[1]), lambda b, i: (i, 0)),
                     pl.BlockSpec((seq, 128), lambda b, i: (0, 0)),
                     pl.BlockSpec((seq, 128), lambda b, i: (0, 0)),
                     pl.BlockSpec((past, nkv), lambda b, i: (b, 0)),
                     pl.BlockSpec((None, past, MLA_ROPE), lambda b, i: (b, 0, 0))]
        args += [cos_q, sin_q, cos_k, sin_k, kvc, krc]
    nout = MLA_HEADS * MLA_V
    return pl.pallas_call(
        functools.partial(_mla_kernel, with_ctx=ctx is not None),
        grid=(bsz, nq), in_specs=in_specs,
        out_specs=pl.BlockSpec((tq, nout), lambda b, i: (b * nq + i, 0)),
        out_shape=jax.ShapeDtypeStruct((t, nout), BF),
        compiler_params=_params(2), name="mla_attention",
    )(*args)


def _gqa_kernel(*refs, with_ctx):
    if with_ctx:
        q_ref, k_ref, v_ref, gq_ref, gk_ref, cq_ref, sq_ref, ck_ref, sk_ref, kc_ref, vc_ref, o_ref, ksrc_ref = refs
    else:
        q_ref, k_ref, v_ref, gq_ref, gk_ref, o_ref, kn_ref, vo_ref, ksrc_ref = refs
    hd = GQA_HEAD_DIM
    scale = hd ** -0.5
    rep = GQA_HEADS // GQA_KV_HEADS
    n_kv = k_ref.shape[1] // hd
    kv_cols = [slice(g * hd, (g + 1) * hd) for g in range(n_kv)]

    @pl.when(pl.program_id(2) == 0)
    def _():
        for g in range(n_kv):
            k_n = _rms(k_ref[:, kv_cols[g]]) * gk_ref[...]
            if with_ctx:
                ksrc_ref[:, kv_cols[g]] = _rope(k_n, ck_ref[...], sk_ref[...], hd // 4).astype(BF)
            else:
                ksrc_ref[:, kv_cols[g]] = k_n.astype(BF)
                kn_ref[g] = k_n
                vo_ref[g] = v_ref[:, kv_cols[g]]

    v = [v_ref[:, kv_cols[g]].astype(BF) for g in range(n_kv)]
    if with_ctx:
        k_ctx, v_ctx = kc_ref[...].astype(BF), vc_ref[...].astype(BF)
    heads = range(n_kv * rep)
    cols = [slice(r * hd, (r + 1) * hd) for r in heads]
    q_n = [_rms(q_ref[:, cols[r]]) * gq_ref[...] for r in heads]
    k_src = [ksrc_ref[:, kv_cols[r // rep]] for r in heads]
    def scores(r):
        if with_ctx:
            return [_bdot_nt(_rope(q_n[r], cq_ref[...], sq_ref[...], hd // 4), k_src[r]), _bdot_nt(q_n[r], k_ctx)]
        return [_bdot_nt(q_n[r], k_src[r])]

    def weighted_values(r, weights):
        e, inv_den = weights
        out = jnp.dot(e[0], v[r // rep], preferred_element_type=F32)
        if with_ctx:
            out = out + jnp.dot(e[1], v_ctx, preferred_element_type=F32)
        o_ref[:, cols[r]] = (out * inv_den).astype(BF)

    _walk_heads(len(heads), scores, lambda r, s: _softmax_terms(s, scale), weighted_values, lag=1)


def _gqa_attention(qkv, g_q, g_k, bsz, seq, ctx=None):
    t = qkv.shape[0]
    hd = GQA_HEAD_DIM
    rep = GQA_HEADS // GQA_KV_HEADS
    tq = ATTN_Q_ROWS
    nq = seq // tq
    n_kv = 1 if ctx is not None else GQA_KV_HEADS
    k_blk0 = GQA_HEADS // n_kv
    in_specs = [pl.BlockSpec((tq, n_kv * rep * hd), lambda b, g, i: (b * nq + i, g)),
                pl.BlockSpec((seq, n_kv * hd), lambda b, g, i: (b, k_blk0 + g)),
                pl.BlockSpec((seq, n_kv * hd), lambda b, g, i: (b, k_blk0 + GQA_KV_HEADS // n_kv + g)),
                pl.BlockSpec((1, hd), lambda b, g, i: (0, 0)),
                pl.BlockSpec((1, hd), lambda b, g, i: (0, 0))]
    args = [qkv, qkv, qkv, g_q.reshape(1, hd), g_k.reshape(1, hd)]
    o_spec = pl.BlockSpec((tq, n_kv * rep * hd), lambda b, g, i: (b * nq + i, g))
    o_shape = jax.ShapeDtypeStruct((t, GQA_HEADS * hd), BF)
    if ctx is not None:
        cos_t, sin_t, kc, vc, j = ctx
        past = kc.shape[3]
        cache = pl.BlockSpec((None, None, None, past, hd), lambda b, g, i: (b, j, g, 0, 0))
        in_specs += [pl.BlockSpec((tq, hd), lambda b, g, i: (i, 0)),
                     pl.BlockSpec((tq, hd), lambda b, g, i: (i, 0)),
                     pl.BlockSpec((seq, hd), lambda b, g, i: (0, 0)),
                     pl.BlockSpec((seq, hd), lambda b, g, i: (0, 0)),
                     cache, cache]
        args += [cos_t, sin_t, cos_t, sin_t, kc, vc]
        out_specs, out_shape = o_spec, o_shape
    else:
        head_major = pl.BlockSpec((None, n_kv, seq, hd), lambda b, g, i: (b, g, 0, 0))
        out_specs = [o_spec, head_major, head_major]
        kv_shape = jax.ShapeDtypeStruct((bsz, GQA_KV_HEADS, seq, hd), F32)
        out_shape = [o_shape, kv_shape, kv_shape]
    return pl.pallas_call(
        functools.partial(_gqa_kernel, with_ctx=ctx is not None),
        grid=(bsz, GQA_KV_HEADS // n_kv, nq), in_specs=in_specs, out_specs=out_specs, out_shape=out_shape,
        scratch_shapes=[pltpu.VMEM((seq, n_kv * hd), BF)],
        compiler_params=_params(3), name="gqa_attention",
    )(*args)


def _dispatch_kernel(ar_ref, ac_ref, h_ref, xe_ref, gate_ref, rankc_ref, rankr_ref, below_ref, *, cap):
    n_groups, group, s = ar_ref.shape
    g = pl.program_id(1)

    @pl.when(g == 0)
    def _():
        blk = min(RANK_BLOCK, s)
        nb = s // blk
        below_ref[...] = jnp.zeros_like(below_ref)
        rankc_ref[...] = jnp.zeros_like(rankc_ref)
        ones_sub = jnp.ones((8, blk), BF)
        ones_lane = jnp.ones((blk, 128), BF)
        i0 = lax.broadcasted_iota(jnp.int32, (blk, blk), 0)
        i1 = lax.broadcasted_iota(jnp.int32, (blk, blk), 1)
        sub_first = jnp.where(i0 < i1, 1.0, 0.0)
        for e in range(n_groups * group):
            eg, ei = e // group, e % group
            for j in range(nb):
                rows = pl.ds(j * blk, blk)
                a_col = ac_ref[rows, e:e + 1]
                for k in range(j, nb):
                    cols = pl.ds(k * blk, blk)
                    a_row = ar_ref[eg, ei:ei + 1, cols]
                    if j == k:
                        ahead = jnp.where(a_col > a_row, 1.0, jnp.where(a_col >= a_row, sub_first, 0.0))
                    else:
                        ahead = jnp.where(a_col >= a_row, 1.0, 0.0)
                    ahead = ahead.astype(BF)
                    below_ref[e:e + 1, cols] += jnp.dot(ones_sub, ahead, preferred_element_type=F32)[:1, :]
                    if j < k:
                        rankc_ref[rows, e:e + 1] += blk - jnp.dot(ahead, ones_lane, preferred_element_type=F32)[:, :1]
        below, above = below_ref[...], rankc_ref[...]
        rank_rows = below + above.T
        for gi in range(n_groups):
            rankr_ref[gi] = rank_rows[gi * group:(gi + 1) * group, :]
        rankc_ref[...] = above + below.T

    slot = lax.broadcasted_iota(jnp.int32, (group, cap, s), 1).astype(F32)
    pick = rankr_ref[g][:, None, :] == slot
    onehot = jnp.where(pick, 1.0, 0.0).reshape(group * cap, s).astype(BF)
    rows_f32 = jnp.dot(onehot, h_ref[...], preferred_element_type=F32)
    xe_ref[...] = rows_f32.reshape(group, cap, -1).astype(BF)
    gate_ref[...] = jnp.sum(jnp.where(pick, ar_ref[g][:, None, :], 0.0), axis=2, keepdims=True)


def _dispatch(aff_row, aff_col, h, bsz, seq):
    t, d = h.shape
    ne = aff_row.shape[0]
    cap = EC_FACTOR * seq // ne
    group = GATHER_ROWS // cap
    n_groups = ne // group
    return pl.pallas_call(
        functools.partial(_dispatch_kernel, cap=cap),
        grid=(bsz, n_groups),
        in_specs=[pl.BlockSpec((n_groups, group, seq), lambda b, g: (0, 0, b)),
                  pl.BlockSpec((seq, ne), lambda b, g: (b, 0)),
                  pl.BlockSpec((seq, d), lambda b, g: (b, 0))],
        out_specs=[pl.BlockSpec((group, cap, d), lambda b, g: (g, b, 0)),
                   pl.BlockSpec((group, cap, 1), lambda b, g: (g, b, 0)),
                   pl.BlockSpec((seq, ne), lambda b, g: (b, 0))],
        out_shape=[jax.ShapeDtypeStruct((ne, bsz * cap, d), BF),
                   jax.ShapeDtypeStruct((ne, bsz * cap, 1), F32),
                   jax.ShapeDtypeStruct((t, ne), F32)],
        scratch_shapes=[pltpu.VMEM((n_groups, group, seq), F32), pltpu.VMEM((ne, seq), F32)],
        compiler_params=_params(2), name="ec_dispatch",
    )(aff_row.reshape(n_groups, group, t), aff_col, h)


def _experts_kernel(xp_ref, xs_ref, gp_ref, gs_ref, wg_ref, wu_ref, wd_ref, yp_ref, ys_ref, accp_ref, accs_ref):
    f = pl.program_id(1)
    tf = wg_ref.shape[1]
    w_gate_up = jnp.concatenate([wg_ref[...].astype(BF), wu_ref[...].astype(BF)], axis=1)
    wd = wd_ref[...].astype(BF)
    groups = ((xp_ref, gp_ref, yp_ref, accp_ref), (xs_ref, gs_ref, ys_ref, accs_ref))

    @pl.when(f == 0)
    def _():
        for _, _, _, acc_ref in groups:
            acc_ref[...] = jnp.zeros_like(acc_ref)

    for x_ref, _, _, acc_ref in groups:
        for r in range(0, x_ref.shape[0], EXPERT_ROWS):
            rows = slice(r, r + EXPERT_ROWS)
            au = jnp.dot(x_ref[rows, :], w_gate_up, preferred_element_type=F32)
            a, u = au[:, :tf], au[:, tf:]
            acc_ref[rows, :] += jnp.dot((a * _sigmoid(a) * u).astype(BF), wd, preferred_element_type=F32)

    @pl.when(f == pl.num_programs(1) - 1)
    def _():
        for _, g_ref, y_ref, acc_ref in groups:
            y_ref[...] = (acc_ref[...] * g_ref[...]).astype(BF)


def _experts(xe_p, xe_s, gate_p, gate_s, w_gate, w_up, w_down, layer):
    ne, mp, d = xe_p.shape
    ms = xe_s.shape[1]
    fdim = w_gate.shape[-1]
    tf = EXPERT_HIDDEN_COLS
    rows = lambda m, last: pl.BlockSpec((None, m, last), lambda e, f: (e, 0, 0))
    return pl.pallas_call(
        _experts_kernel, grid=(ne, fdim // tf),
        in_specs=[rows(mp, d), rows(ms, d), rows(mp, 1), rows(ms, 1),
                  pl.BlockSpec((None, None, d, tf), lambda e, f: (layer, e, 0, f)),
                  pl.BlockSpec((None, None, d, tf), lambda e, f: (layer, e, 0, f)),
                  pl.BlockSpec((None, None, tf, d), lambda e, f: (layer, e, f, 0))],
        out_specs=[rows(mp, d), rows(ms, d)],
        out_shape=[jax.ShapeDtypeStruct((ne, mp, d), BF), jax.ShapeDtypeStruct((ne, ms, d), BF)],
        scratch_shapes=[pltpu.VMEM((mp, d), F32), pltpu.VMEM((ms, d), F32)],
        compiler_params=_params(2), name="ec_experts",
    )(xe_p, xe_s, gate_p, gate_s, w_gate, w_up, w_down)


RANK_RADIX = 32


def _combine_kernel(y_ref, rank_ref, x_ref, g_ref, *refs, cap, final):
    post_refs, place_ref = refs[:-1], refs[-1]
    ne = y_ref.shape[0]
    n_slots = ne * cap

    @pl.when(pl.program_id(1) == 0)
    def _():
        rank = rank_ref[...]
        hi = jnp.floor(rank * (1.0 / RANK_RADIX))
        lo = rank - RANK_RADIX * hi
        lane = lax.broadcasted_iota(jnp.int32, (ne, n_slots), 1)
        expert = lax.broadcasted_iota(jnp.int32, (ne, n_slots), 0)
        own = (lane >= expert * cap) & (lane < (expert + 1) * cap)
        spread_hi = jnp.where(own, float(RANK_RADIX), 0.0).astype(BF)
        spread_lo = jnp.where(own, 1.0, 0.0).astype(BF)
        spread = (jnp.dot(hi.astype(BF), spread_hi, preferred_element_type=F32)
                  + jnp.dot(lo.astype(BF), spread_lo, preferred_element_type=F32))
        slot = (lax.broadcasted_iota(jnp.int32, (1, n_slots), 1) & (cap - 1)).astype(F32)
        place_ref[...] = jnp.where(spread == slot, 1.0, 0.0).astype(BF)

    ts = x_ref.shape[0]
    rows = pl.ds(pl.multiple_of(pl.program_id(1) * ts, ts), ts)
    y = y_ref[...].reshape(n_slots, y_ref.shape[2])
    x_new = x_ref[...] + g_ref[...] * jnp.dot(place_ref[rows, :], y, preferred_element_type=F32)
    if final:
        (gain_ref,), (out_ref,) = post_refs[:1], post_refs[1:]
        out_ref[...] = _rms(x_new) * gain_ref[...]
    else:
        (sh_ref, sc_ref), (xo_ref, h_ref) = post_refs[:2], post_refs[2:]
        xo_ref[...] = x_new
        h_ref[...] = (_rms(x_new) * (1.0 + sc_ref[...]) + sh_ref[...]).astype(BF)


def _combine(y, rank, x, gate, bsz, seq, next_shift_scale=None, final_gain=None):
    t, d = x.shape
    ne = y.shape[0]
    cap = EC_FACTOR * seq // ne
    assert seq <= RANK_RADIX * RANK_RADIX and cap & (cap - 1) == 0
    ts = min(COMBINE_ROWS, seq)
    nt = seq // ts
    final = final_gain is not None
    vec = lambda v: pl.BlockSpec((None, 1, d), lambda b, i: (b if v.shape[0] > 1 else 0, 0, 0))
    tile = pl.BlockSpec((ts, d), lambda b, i: (b * nt + i, 0))
    in_specs = [pl.BlockSpec((ne, cap, d), lambda b, i: (0, b, 0)),
                pl.BlockSpec((seq, ne), lambda b, i: (b, 0)), tile, vec(gate)]
    if final:
        args = [final_gain.reshape(1, d)]
        in_specs.append(pl.BlockSpec((1, d), lambda b, i: (0, 0)))
        out_specs, out_shape = tile, jax.ShapeDtypeStruct((t, d), F32)
    else:
        args = list(next_shift_scale)
        in_specs += [vec(v) for v in args]
        out_specs = [tile, tile]
        out_shape = [jax.ShapeDtypeStruct((t, d), F32), jax.ShapeDtypeStruct((t, d), BF)]
    return pl.pallas_call(
        functools.partial(_combine_kernel, cap=cap, final=final),
        grid=(bsz, nt), in_specs=in_specs, out_specs=out_specs, out_shape=out_shape,
        scratch_shapes=[pltpu.VMEM((seq, ne * cap), BF)],
        compiler_params=_params(2), name="ec_combine",
    )(y, rank, x, gate, *args)


def _rope_tables(n_tokens, rot_dim):
    rows = n_tokens // GRID_W
    row = jnp.repeat(jnp.arange(rows), GRID_W).astype(F32)
    col = jnp.tile(jnp.arange(GRID_W), rows).astype(F32)
    quarter = rot_dim // 4
    inv = ROPE_THETA ** (-jnp.arange(quarter, dtype=F32) / quarter)
    a_row, a_col = row[:, None] * inv, col[:, None] * inv
    cos_t = jnp.concatenate([jnp.cos(a_row), jnp.cos(a_row), jnp.cos(a_col), jnp.cos(a_col)], axis=-1)
    sin_t = jnp.concatenate([-jnp.sin(a_row), jnp.sin(a_row), -jnp.sin(a_col), jnp.sin(a_col)], axis=-1)
    return cos_t, sin_t


def kernel(x_prompt, x_sample, state_mlstm_c, state_mlstm_n, state_mlstm_m, cache_mla_ckv, cache_mla_krope,
           cache_gqa_k, cache_gqa_v, c, c_ctx, w_mod, b_mod, w_in_even, b_igate, b_fgate, g_mlstm, g_cq, w_uq,
           g_ckv, w_ukv, w_out_even, w_in_odd, g_qnorm, g_knorm, w_out_odd, w_router, w_expert_gate,
           w_expert_up, w_expert_down, g_final):
    d = D_MODEL
    bp, sp, _ = x_prompt.shape
    bs, ss, _ = x_sample.shape
    depth = w_mod.shape[0]
    nh = ML_HEADS
    streams = {"p": (bp, sp), "s": (bs, ss)}
    x = {"p": x_prompt.reshape(bp * sp, d), "s": x_sample.reshape(bs * ss, d)}

    c8 = jnp.concatenate([c_ctx[None], c, jnp.zeros((8 - 1 - bs, d), F32)], axis=0)
    mod_all = _mod_vectors(c8, w_mod, b_mod).reshape(depth, 8, 6, 1, d)

    def mod(layer, key, idx):
        rows = mod_all[layer, 0:1, idx] if key == "p" else mod_all[layer, 1:1 + bs, idx]
        return rows

    new_even, new_odd = [], []
    h_in = {key: _norm_mod(x[key], mod(0, key, 0), mod(0, key, 1), streams[key][1]) for key in streams}
    routed_in = {}
    for layer in range(depth):
        j = layer // 2
        if layer % 2 == 0:
            w_out = _to_bf16(w_out_even, j)
            w_in_rows = jnp.swapaxes(w_in_even, 1, 2)
            w_side = jnp.concatenate(
                [w_in_rows[j, QKVO_COLS + 4 * nh:],
                 w_in_rows[j, QKVO_COLS:QKVO_COLS + 4 * nh],
                 jnp.zeros((128 - MLA_ROPE - 4 * nh, d), F32)], axis=0)[None]
            w_q = w_uq[j].reshape(MLA_Q_RANK, MLA_HEADS, MLA_NOPE + MLA_ROPE)
            w_q = jnp.concatenate([w_q[:, :, :MLA_NOPE].reshape(MLA_Q_RANK, -1),
                                   w_q[:, :, MLA_NOPE:].reshape(MLA_Q_RANK, -1)], axis=1)
            bias_col = jnp.concatenate([b_igate[j].reshape(1, -1), b_fgate[j].reshape(1, -1)], axis=1)
            bias_row = bias_col.reshape(-1, 1)
            cos64, sin64 = _rope_tables(ss, MLA_ROPE)
            cos_q, sin_q = jnp.tile(cos64, (1, MLA_HEADS)), jnp.tile(sin64, (1, MLA_HEADS))
            pad = jnp.zeros((ss, 128 - MLA_ROPE), F32)
            cos_k, sin_k = jnp.concatenate([cos64, pad], axis=1), jnp.concatenate([sin64, pad], axis=1)
            kvc = _mm_resident(cache_mla_ckv[:, j].reshape(-1, MLA_KV_RANK), w_ukv, j, w_ukv.shape[-1], IN_PROJ_COLS,
                               name="mla_ctx_expand", out_dtype=BF)
            for key, (bsz, seq) in streams.items():
                h = h_in[key]
                qkvo = _mm_resident(h, w_in_rows, j, QKVO_COLS, IN_PROJ_COLS, w_rows=True, name="even_in_main")
                side = _mm_resident(h, w_side, 0, SIDE_COLS, SIDE_COLS // 3, w_rows=True, name="even_in_side")
                gates = side[:, SIDE_COLS - 128 + GATE_LANE0:SIDE_COLS - 128 + GATE_LANE0 + 4 * nh]
                g_row = gates.reshape(-1, ML_CHUNK, 4 * nh).transpose(0, 2, 1)
                if key == "p":
                    init = None
                else:
                    c0 = state_mlstm_c[:, j]
                    init = (c0, jnp.broadcast_to(state_mlstm_n[:, j][..., None], c0.shape),
                            state_mlstm_m[:, j].reshape(bsz, 1, 2 * nh))
                hf, hb, c_fin, n_fin, m_fin = _mlstm(qkvo, side, g_row, bias_col, bias_row, init, bsz, seq)
                y_ml = _mlstm_post(hf, hb, qkvo, g_mlstm[j])
                qa = _norm_mm(side, 0, g_cq[j], w_q, False, "mla_q_up", out_dtype=BF if key == "p" else F32)
                kv, ckv_n = _norm_mm(side, 1, g_ckv[j], w_ukv[j], True, "mla_kv_up", out_dtype=BF)
                if key == "p":
                    y_a = _mla_attention(qa, kv, side, bsz, seq)
                    k_rope = side[:, SIDE_COLS - 128:SIDE_COLS - 128 + MLA_ROPE]
                    new_even.append((c_fin, n_fin, m_fin.reshape(bsz, 2, nh),
                                     ckv_n.reshape(bsz, seq, -1), k_rope.reshape(bsz, seq, -1)))
                else:
                    y_a = _mla_attention(qa, kv, side, bsz, seq,
                                         ctx=(cos_q, sin_q, cos_k, sin_k, kvc, cache_mla_krope[:, j]))
                routed_in[key] = _out_proj_router([y_ml, y_a], w_out, x[key], mod(layer, key, 2), mod(layer, key, 3),
                                                  mod(layer, key, 4), w_router, layer, seq)
        else:
            w_out = _to_bf16(w_out_odd, j)
            cos_t, sin_t = _rope_tables(ss, GQA_HEAD_DIM)
            for key, (bsz, seq) in streams.items():
                qkv = _mm_resident(h_in[key], w_in_odd, j, w_in_odd.shape[-1], IN_PROJ_COLS, name="odd_in")
                if key == "p":
                    o, k_n, v = _gqa_attention(qkv, g_qnorm[j], g_knorm[j], bsz, seq)
                    new_odd.append((k_n, v))
                else:
                    o = _gqa_attention(qkv, g_qnorm[j], g_knorm[j], bsz, seq,
                                       ctx=(cos_t, sin_t, cache_gqa_k, cache_gqa_v, j))
                routed_in[key] = _out_proj_router([o], w_out, x[key], mod(layer, key, 2), mod(layer, key, 3),
                                                  mod(layer, key, 4), w_router, layer, seq)
        routed = {}
        for key, (bsz, seq) in streams.items():
            x[key], h, aff, aff_t = routed_in[key]
            routed[key] = _dispatch(aff_t, aff, h, bsz, seq)
        y_p, y_s = _experts(routed["p"][0], routed["s"][0], routed["p"][1], routed["s"][1],
                            w_expert_gate, w_expert_up, w_expert_down, layer)
        for key, y in (("p", y_p), ("s", y_s)):
            bsz, seq = streams[key]
            if layer + 1 < depth:
                x[key], h_in[key] = _combine(
                    y, routed[key][2], x[key], mod(layer, key, 5), bsz, seq,
                    next_shift_scale=(mod(layer + 1, key, 0), mod(layer + 1, key, 1)))
            else:
                x[key] = _combine(y, routed[key][2], x[key], mod(layer, key, 5), bsz, seq, final_gain=g_final)

    y_prompt = x["p"].reshape(bp, sp, d)
    y_sample = x["s"].reshape(bs, ss, d)
    new_c = jnp.stack([e[0] for e in new_even], axis=1)
    new_n = jnp.stack([e[1] for e in new_even], axis=1)
    new_m = jnp.stack([e[2] for e in new_even], axis=1)
    new_ckv = jnp.stack([e[3] for e in new_even], axis=1)
    new_krope = jnp.stack([e[4] for e in new_even], axis=1)
    new_k = jnp.stack([e[0] for e in new_odd], axis=1)
    new_v = jnp.stack([e[1] for e in new_odd], axis=1)
    return (y_prompt, y_sample, new_c, new_n, new_m, new_ckv, new_krope, new_k, new_v)
```

```python
import functools
import itertools

import jax
import jax.numpy as jnp
from jax import lax
from jax.experimental import pallas as pl
from jax.experimental.pallas import tpu as pltpu

BF = jnp.bfloat16
F32 = jnp.float32

D_MODEL = 2048
GRID_W = 64
ROPE_THETA = 10000.0
NORM_EPS = 1e-6
ML_HEADS = 8
ML_QK = 64
ML_V = 128
ML_CHUNK = 64
MLA_HEADS = 8
MLA_Q_RANK = 512
MLA_KV_RANK = 512
MLA_NOPE = 128
MLA_ROPE = 64
MLA_V = 128
GQA_HEADS = 16
GQA_KV_HEADS = 4
GQA_HEAD_DIM = 128
EC_FACTOR = 2

QKVO_COLS = 2 * ML_HEADS * ML_QK + 2 * ML_HEADS * ML_V
SIDE_COLS = MLA_Q_RANK + MLA_KV_RANK + 128
GATE_LANE0 = MLA_ROPE

VMEM_LIMIT_BYTES = 56 * 1024 * 1024
OUT_PROJ_ROWS = 512
MLSTM_SEQS = 4
MOD_COLS = 1024
IN_PROJ_COLS = 512
IN_PROJ_ROW_CHUNK = 512
CAST_ROWS = 512
ROW_TILE = 512
ATTN_Q_ROWS = 256
EXPERT_HIDDEN_COLS = 512
EXPERT_ROWS = 256
RANK_BLOCK = 256
GATHER_ROWS = 512
COMBINE_ROWS = 256


def _params(n_axes):
    return pltpu.CompilerParams(dimension_semantics=("arbitrary",) * n_axes,
                                vmem_limit_bytes=VMEM_LIMIT_BYTES)


def _bdot(a, b):
    return jnp.dot(a.astype(BF), b.astype(BF), preferred_element_type=F32)


def _bdot_nt(a, b):
    return lax.dot_general(a.astype(BF), b.astype(BF), (((1,), (1,)), ((), ())),
                           preferred_element_type=F32)


def _bdot_tn(a, b):
    return lax.dot_general(a.astype(BF), b.astype(BF), (((0,), (0,)), ((), ())),
                           preferred_element_type=F32)


def _sigmoid(x):
    return 1.0 / (1.0 + jnp.exp(-x))


def _log_sigmoid(x):
    return jnp.minimum(x, 0.0) - jnp.log1p(jnp.exp(-jnp.abs(x)))


def _rms(x):
    return x * lax.rsqrt(jnp.mean(x * x, axis=-1, keepdims=True) + NORM_EPS)


def _rope(x, cos_t, sin_t, quarter):
    width = x.shape[-1]
    axis = x.ndim - 1
    lane = lax.broadcasted_iota(jnp.int32, x.shape, axis)
    partner = jnp.where((lane & quarter) == 0,
                        pltpu.roll(x, width - quarter, axis=axis),
                        pltpu.roll(x, quarter, axis=axis))
    return x * cos_t + partner * sin_t


def _mod_kernel(c_ref, w_ref, b_ref, o_ref):
    c = c_ref[...]
    o_ref[...] = _bdot(c * _sigmoid(c), w_ref[...]) + b_ref[...]


def _mod_vectors(c8, w_mod, b_mod):
    n_layers, k, n = w_mod.shape
    tn = MOD_COLS
    return pl.pallas_call(
        _mod_kernel,
        grid=(n_layers, n // tn),
        in_specs=[pl.BlockSpec((8, k), lambda l, j: (0, 0)),
                  pl.BlockSpec((None, k, tn), lambda l, j: (l, 0, j)),
                  pl.BlockSpec((None, 1, tn), lambda l, j: (l, 0, j))],
        out_specs=pl.BlockSpec((None, 8, tn), lambda l, j: (l, 0, j)),
        out_shape=jax.ShapeDtypeStruct((n_layers, 8, n), F32),
        compiler_params=_params(2),
        name="mod_vectors",
    )(c8, w_mod, b_mod.reshape(n_layers, 1, n))


def _norm_mod_kernel(x_ref, sh_ref, sc_ref, h_ref):
    h_ref[...] = (_rms(x_ref[...]) * (1.0 + sc_ref[...]) + sh_ref[...]).astype(BF)


def _batch_of_tile(n_vectors, tm, seq):
    if n_vectors == 1:
        return lambda i: 0
    assert seq % tm == 0
    return lambda i: (i * tm) // seq


def _norm_mod(x, shift, scale, seq):
    t, d = x.shape
    tm = OUT_PROJ_ROWS
    which = _batch_of_tile(shift.shape[0], tm, seq)
    vec = pl.BlockSpec((None, 1, d), lambda i: (which(i), 0, 0))
    x_spec = pl.BlockSpec((tm, d), lambda i: (i, 0))
    return pl.pallas_call(
        _norm_mod_kernel, grid=(t // tm,),
        in_specs=[x_spec, vec, vec], out_specs=x_spec,
        out_shape=jax.ShapeDtypeStruct((t, d), BF),
        compiler_params=_params(1), name="norm_mod",
    )(x, shift, scale)


def _to_bf16_kernel(w_ref, o_ref):
    o_ref[...] = w_ref[...].astype(BF)


def _to_bf16(w3, layer):
    _, k, n = w3.shape
    tk = CAST_ROWS
    return pl.pallas_call(
        _to_bf16_kernel, grid=(k // tk,),
        in_specs=[pl.BlockSpec((None, tk, n), lambda i: (layer, i, 0))],
        out_specs=pl.BlockSpec((tk, n), lambda i: (i, 0)),
        out_shape=jax.ShapeDtypeStruct((k, n), BF),
        compiler_params=_params(1), name="weight_to_bf16",
    )(w3)


def _mlstm_gated_output(hf_ref, hb_ref, o_ref, g_ref, rows):
    parts = []
    for h in range(ML_HEADS):
        sl = slice(h * ML_V, (h + 1) * ML_V)
        hn = _rms(hf_ref[rows, sl] + hb_ref[rows, sl]) * g_ref[:, sl]
        parts.append((hn * _sigmoid(o_ref[rows, sl])).astype(BF))
    return jnp.concatenate(parts, axis=1)


def _out_proj_router_kernel(*refs, k_sizes, with_mlstm):
    n_lead = 4 if with_mlstm else 0
    n_a = len(k_sizes) - (1 if with_mlstm else 0)
    a_refs = refs[n_lead:n_lead + n_a]
    w_ref, x_ref, g_ref, sh_ref, sc_ref, wr_ref, xo_ref, h_ref, aff_ref, afft_ref = refs[n_lead + n_a:]
    w_router = wr_ref[...].astype(BF)
    chunk = OUT_PROJ_ROWS // 2
    for r in range(0, x_ref.shape[0], chunk):
        rows = slice(r, r + chunk)
        operands = [a_ref[rows, :] for a_ref in a_refs]
        if with_mlstm:
            operands.insert(0, _mlstm_gated_output(*refs[:4], rows))
        acc, off = None, 0
        for a, ks in zip(operands, k_sizes):
            term = jnp.dot(a, w_ref[off:off + ks, :], preferred_element_type=F32)
            acc = term if acc is None else acc + term
            off += ks
        x_new = x_ref[rows, :] + g_ref[...] * acc
        xo_ref[rows, :] = x_new
        h = (_rms(x_new) * (1.0 + sc_ref[...]) + sh_ref[...]).astype(BF)
        h_ref[rows, :] = h
        logits = jnp.dot(h, w_router, preferred_element_type=F32)
        e = jnp.exp(logits - jnp.max(logits, axis=-1, keepdims=True))
        aff = e / jnp.sum(e, axis=-1, keepdims=True)
        aff_ref[rows, :] = aff
        afft_ref[:, rows] = aff.T


def _out_proj_router(a_list, w_bf, x, gate, shift, scale, w_router, layer, seq, mlstm=None):
    m, d = x.shape
    k_sizes = tuple(a.shape[1] for a in a_list)
    assert all(a.dtype == BF for a in a_list)
    tm = OUT_PROJ_ROWS
    ne = w_router.shape[-1]
    which = _batch_of_tile(gate.shape[0], tm, seq)
    vec = pl.BlockSpec((None, 1, d), lambda i: (which(i), 0, 0))
    tile = pl.BlockSpec((tm, d), lambda i: (i, 0))
    lead_args, lead_specs = [], []
    if mlstm is not None:
        hf, hb, qkvo, gain = mlstm
        nv = hf.shape[1]
        scan = pl.BlockSpec((tm, nv), lambda i: (i, 0))
        lead_args = [hf, hb, qkvo, gain.reshape(1, nv)]
        lead_specs = [scan, scan, pl.BlockSpec((tm, nv), lambda i: (i, QKVO_COLS // nv - 1)),
                      pl.BlockSpec((1, nv), lambda i: (0, 0))]
        k_sizes = (nv,) + k_sizes
    assert w_bf.shape == (sum(k_sizes), d)
    in_specs = lead_specs + [pl.BlockSpec((tm, a.shape[1]), lambda i: (i, 0)) for a in a_list]
    in_specs += [pl.BlockSpec(w_bf.shape, lambda i: (0, 0), pipeline_mode=pl.Buffered(1)),
                 tile, vec, vec, vec, pl.BlockSpec((None, d, ne), lambda i: (layer, 0, 0))]
    return pl.pallas_call(
        functools.partial(_out_proj_router_kernel, k_sizes=k_sizes, with_mlstm=mlstm is not None),
        grid=(m // tm,), in_specs=in_specs,
        out_specs=[tile, tile, pl.BlockSpec((tm, ne), lambda i: (i, 0)), pl.BlockSpec((ne, tm), lambda i: (0, i))],
        out_shape=[jax.ShapeDtypeStruct((m, d), F32), jax.ShapeDtypeStruct((m, d), BF),
                   jax.ShapeDtypeStruct((m, ne), F32), jax.ShapeDtypeStruct((ne, m), F32)],
        compiler_params=_params(1), name="out_proj_router",
    )(*lead_args, *a_list, w_bf, x, gate, shift, scale, w_router)


def _mm_resident_kernel(a_ref, w_ref, o_ref, *, w_rows, chunk):
    w = w_ref[...].astype(BF)
    for r in range(0, a_ref.shape[0], chunk):
        rows = slice(r, r + chunk)
        a = a_ref[rows, :].astype(BF)
        if w_rows:
            out = lax.dot_general(a, w, (((1,), (1,)), ((), ())), preferred_element_type=F32)
        else:
            out = jnp.dot(a, w, preferred_element_type=F32)
        o_ref[rows, :] = out.astype(o_ref.dtype)


def _mm_resident(a, w3, layer, n_cols, tn, w_rows=False, name="mm_resident", out_dtype=F32):
    m, k = a.shape
    assert w3.shape[2 if w_rows else 1] == k and n_cols % tn == 0
    if w_rows:
        w_spec = pl.BlockSpec((None, tn, k), lambda j: (layer, j, 0))
    else:
        w_spec = pl.BlockSpec((None, k, tn), lambda j: (layer, 0, j))
    return pl.pallas_call(
        functools.partial(_mm_resident_kernel, w_rows=w_rows, chunk=min(IN_PROJ_ROW_CHUNK, m)),
        grid=(n_cols // tn,),
        in_specs=[pl.BlockSpec((m, k), lambda j: (0, 0), pipeline_mode=pl.Buffered(1)), w_spec],
        out_specs=pl.BlockSpec((m, tn), lambda j: (0, j)),
        out_shape=jax.ShapeDtypeStruct((m, n_cols), out_dtype),
        compiler_params=_params(1), name=name,
    )(a, w3)


def _norm_mm_kernel(x_ref, g_ref, w_ref, *out_refs, with_normed):
    wbf_ref = out_refs[-1]

    @pl.when(pl.program_id(0) == 0)
    def _():
        wbf_ref[...] = w_ref[...].astype(BF)

    xn = _rms(x_ref[...]) * g_ref[...]
    out_refs[0][...] = jnp.dot(xn.astype(BF), wbf_ref[...], preferred_element_type=F32).astype(out_refs[0].dtype)
    if with_normed:
        out_refs[1][...] = xn


def _norm_mm(x, col_blk, gain, w, with_normed, name, out_dtype=F32):
    t = x.shape[0]
    k, n = w.shape
    tm = ROW_TILE
    out_specs = [pl.BlockSpec((tm, n), lambda i: (i, 0))]
    out_shape = [jax.ShapeDtypeStruct((t, n), out_dtype)]
    if with_normed:
        out_specs.append(pl.BlockSpec((tm, k), lambda i: (i, 0)))
        out_shape.append(jax.ShapeDtypeStruct((t, k), F32))
    outs = pl.pallas_call(
        functools.partial(_norm_mm_kernel, with_normed=with_normed),
        grid=(t // tm,),
        in_specs=[pl.BlockSpec((tm, k), lambda i: (i, col_blk)),
                  pl.BlockSpec((1, k), lambda i: (0, 0)),
                  pl.BlockSpec((k, n), lambda i: (0, 0))],
        out_specs=out_specs, out_shape=out_shape,
        scratch_shapes=[pltpu.VMEM((k, n), BF)],
        compiler_params=_params(1), name=name,
    )(x, gain.reshape(1, k), w)
    return outs if with_normed else outs[0]


def _split3(x):
    hi = x.astype(BF)
    rest = x - hi.astype(F32)
    mid = rest.astype(BF)
    return hi, mid, (rest - mid.astype(F32)).astype(BF)


def _scan_max(x, reverse):
    n = x.shape[0]
    row = lax.broadcasted_iota(jnp.int32, x.shape, 0)
    k = 1
    while k < n:
        if reverse:
            shifted = jnp.where(row < n - k, pltpu.roll(x, n - k, axis=0), -jnp.inf)
        else:
            shifted = jnp.where(row >= k, pltpu.roll(x, k, axis=0), -jnp.inf)
        x = jnp.maximum(x, shifted)
        k *= 2
    return x


def _mlstm_kernel(qf_ref, kf_ref, vf_ref, qb_ref, kb_ref, vb_ref, gcf_ref, gcb_ref, grf_ref, grb_ref,
                  bc_ref, br_ref, *refs):
    hf_ref, hb_ref, c_ref, n_out_ref, m_ref, n_ref = refs[-6:]

    @pl.when(pl.program_id(1) == 0)
    def _():
        for state_ref, init_ref in zip((c_ref, n_ref, m_ref), refs[:-6] or (None,) * 3):
            state_ref[...] = jnp.zeros_like(state_ref) if init_ref is None else init_ref[...]

    nh, lc = ML_HEADS, ML_CHUNK
    row = lax.broadcasted_iota(jnp.int32, (lc, lc), 0)
    col = lax.broadcasted_iota(jnp.int32, (lc, lc), 1)
    lower = col <= row
    upper = col >= row
    ones_v = jnp.ones((lc, ML_V), BF)
    directions = ((qf_ref, kf_ref, vf_ref, gcf_ref, grf_ref, hf_ref, lower, upper),
                  (qb_ref, kb_ref, vb_ref, gcb_ref, grb_ref, hb_ref, upper, lower))
    seqs = range(c_ref.shape[0])
    gate_terms = {}
    for r, (d, (_, _, _, gc_ref, gr_ref, _, allowed, allowed_t)) in itertools.product(seqs, enumerate(directions)):
        g_col = gc_ref[r, :, GATE_LANE0:GATE_LANE0 + 4 * nh] + bc_ref[...]
        g_row = gr_ref[r] + br_ref[...]
        i_col = g_col[:, nh * d:nh * (d + 1)]
        f_col = _log_sigmoid(g_col[:, 2 * nh + nh * d:2 * nh + nh * (d + 1)])
        i_row = g_row[nh * d:nh * (d + 1), :]
        f_row = _log_sigmoid(g_row[2 * nh + nh * d:2 * nh + nh * (d + 1), :])
        tri = jnp.where(allowed, 1.0, 0.0).astype(BF)
        tri_t = jnp.where(allowed_t, 1.0, 0.0).astype(BF)
        b_col = sum(jnp.dot(tri, part, preferred_element_type=F32) for part in _split3(f_col))
        b_row = sum(jnp.dot(part, tri_t, preferred_element_type=F32) for part in _split3(f_row))
        b_end = jnp.sum(f_col, axis=0, keepdims=True)
        m_prev = m_ref[r, :, nh * d:nh * (d + 1)]
        m_t = b_col + jnp.maximum(m_prev, _scan_max(i_col - b_col, reverse=d == 1))
        g_col_end = b_end - b_col + i_col
        m_new = jnp.maximum(b_end + m_prev, jnp.max(g_col_end, axis=0, keepdims=True))
        gate_terms[r, d] = dict(
            u=b_col - m_t, r_row=i_row - b_row, w_inter=jnp.exp(b_col + m_prev - m_t), floor=jnp.exp(-m_t),
            k_scale=jnp.exp(g_col_end - m_new), decay=jnp.exp(b_end + m_prev - m_new), m_new=m_new)
    work = []
    for r, (d, (q_ref, k_ref, v_ref, _, _, h_ref, allowed, _)) in itertools.product(seqs, enumerate(directions)):
        for h in range(nh):
            w = dict(r=r, d=d, h=h, h_ref=h_ref, allowed=allowed, g=gate_terms[r, d])
            w["q"] = (q_ref[r, :, h * ML_QK:(h + 1) * ML_QK] * (ML_QK ** -0.5)).astype(BF)
            w["k"] = k_ref[r, :, h * ML_QK:(h + 1) * ML_QK]
            w["v1"] = jnp.concatenate([v_ref[r, :, h * ML_V:(h + 1) * ML_V].astype(BF), ones_v], axis=1)
            work.append(w)
    for w in work:
        w["qk"] = _bdot_nt(w["q"], w["k"])
    for w in work:
        r, d, h = w["r"], w["d"], w["h"]
        w["c_prev"], w["n_prev"] = c_ref[r, d, h], n_ref[r, d, h]
        state = jnp.concatenate([w["c_prev"], w["n_prev"]], axis=1).astype(BF)
        w["q_state"] = jnp.dot(w["q"], state, preferred_element_type=F32)
    for w in work:
        h, g = w["h"], w["g"]
        kw = w["k"] * g["k_scale"][:, h:h + 1]
        w["kv"] = _bdot_tn(kw, w["v1"])
    for w in work:
        h, g = w["h"], w["g"]
        dmat = jnp.where(w["allowed"], g["u"][:, h:h + 1] + g["r_row"][h:h + 1, :], -jnp.inf)
        sw = w["qk"] * jnp.exp(dmat)
        sw_hi = sw.astype(BF)
        sw_lo = (sw - sw_hi.astype(F32)).astype(BF)
        w["pv"] = jnp.dot(sw_hi, w["v1"], preferred_element_type=F32)
        w["den_lo"] = jnp.dot(sw_lo, ones_v, preferred_element_type=F32)
    for w in work:
        r, d, h, g = w["r"], w["d"], w["h"], w["g"]
        w_inter = g["w_inter"][:, h:h + 1]
        num = w_inter * w["q_state"][:, :ML_V] + w["pv"][:, :ML_V]
        den = w_inter * w["q_state"][:, ML_V:] + (w["pv"][:, ML_V:] + w["den_lo"])
        w["h_ref"][r, :, h * ML_V:(h + 1) * ML_V] = num / jnp.maximum(jnp.abs(den), g["floor"][:, h:h + 1])
        decay = g["decay"][:, h:h + 1]
        c_ref[r, d, h] = decay * w["c_prev"] + w["kv"][:, :ML_V]
        n_ref[r, d, h] = decay * w["n_prev"] + w["kv"][:, ML_V:]
    for r in seqs:
        m_ref[r, :, :nh] = gate_terms[r, 0]["m_new"]
        m_ref[r, :, nh:] = gate_terms[r, 1]["m_new"]

    @pl.when(pl.program_id(1) == pl.num_programs(1) - 1)
    def _():
        for r, d, h in itertools.product(seqs, range(2), range(nh)):
            n_out_ref[r, d, h:h + 1, :] = n_ref[r, d, h].T[:1, :]


def _mlstm(qkvo, side, g_row, b_col, b_row, init, bsz, seq):
    t = qkvo.shape[0]
    nc = seq // ML_CHUNK
    lc = ML_CHUNK
    nq = ML_HEADS * ML_QK
    nv = ML_HEADS * ML_V
    side_blk = (SIDE_COLS - 128) // 128
    rb = min(MLSTM_SEQS, bsz)
    assert bsz % rb == 0
    fwd = lambda c: c
    bwd = lambda c: nc - 1 - c

    def specs(pos):
        return [pl.BlockSpec((rb, lc, nq), lambda b, c: (b, pos(c), 0)),
                pl.BlockSpec((rb, lc, nq), lambda b, c: (b, pos(c), 1)),
                pl.BlockSpec((rb, lc, nv), lambda b, c: (b, pos(c), 1))]

    state = lambda *shape: pl.BlockSpec((rb,) + shape, lambda b, c: (b,) + (0,) * len(shape))
    state_specs = [state(2, ML_HEADS, ML_QK, ML_V), state(2, ML_HEADS, ML_QK, ML_V), state(1, 2 * ML_HEADS)]
    in_specs = specs(fwd) + specs(bwd) + [
        pl.BlockSpec((rb, lc, 128), lambda b, c: (b, fwd(c), side_blk)),
        pl.BlockSpec((rb, lc, 128), lambda b, c: (b, bwd(c), side_blk)),
        pl.BlockSpec((rb, None, 4 * ML_HEADS, lc), lambda b, c: (b, fwd(c), 0, 0)),
        pl.BlockSpec((rb, None, 4 * ML_HEADS, lc), lambda b, c: (b, bwd(c), 0, 0)),
        pl.BlockSpec((1, 4 * ML_HEADS), lambda b, c: (0, 0)),
        pl.BlockSpec((4 * ML_HEADS, 1), lambda b, c: (0, 0))]
    init = () if init is None else tuple(init)
    in_specs += state_specs[:len(init)]
    out_specs = [pl.BlockSpec((rb, lc, nv), lambda b, c: (b, fwd(c), 0)),
                 pl.BlockSpec((rb, lc, nv), lambda b, c: (b, bwd(c), 0)),
                 state_specs[0], state(2, ML_HEADS, ML_QK), state_specs[2]]
    out_shape = [jax.ShapeDtypeStruct((bsz, seq, nv), F32), jax.ShapeDtypeStruct((bsz, seq, nv), F32),
                 jax.ShapeDtypeStruct((bsz, 2, ML_HEADS, ML_QK, ML_V), F32),
                 jax.ShapeDtypeStruct((bsz, 2, ML_HEADS, ML_QK), F32),
                 jax.ShapeDtypeStruct((bsz, 1, 2 * ML_HEADS), F32)]
    qkvo3, side3 = qkvo.reshape(bsz, seq, -1), side.reshape(bsz, seq, -1)
    g_row4 = g_row.reshape(bsz, nc, 4 * ML_HEADS, lc)
    hf, hb, c_fin, n_fin, m_fin = pl.pallas_call(
        _mlstm_kernel, grid=(bsz // rb, nc), in_specs=in_specs, out_specs=out_specs, out_shape=out_shape,
        scratch_shapes=[pltpu.VMEM((rb, 2, ML_HEADS, ML_QK, ML_V), F32)],
        compiler_params=_params(2), name="mlstm",
    )(qkvo3, qkvo3, qkvo3, qkvo3, qkvo3, qkvo3, side3, side3, g_row4, g_row4, b_col, b_row, *init)
    return hf.reshape(t, nv), hb.reshape(t, nv), c_fin, n_fin, m_fin


LOG2_E = 1.4426950408889634


def _softmax_terms(scores, scale):
    c = scale * LOG2_E
    scaled = [s * c for s in scores]
    m = functools.reduce(jnp.maximum, [jnp.max(s, axis=-1, keepdims=True) for s in scaled])
    e = [jnp.exp2(s - m) for s in scaled]
    den = functools.reduce(lambda a, b: a + b, [jnp.sum(x, axis=-1, keepdims=True) for x in e])
    return [x.astype(BF) for x in e], 1.0 / den


def _walk_heads(n_heads, scores_fn, softmax_fn, values_fn, lag):
    scores, weights = {}, {}
    for t in range(n_heads + 2 * lag):
        if t < n_heads:
            scores[t] = scores_fn(t)
        if 0 <= t - lag < n_heads:
            weights[t - lag] = softmax_fn(t - lag, scores.pop(t - lag))
        if 0 <= t - 2 * lag < n_heads:
            values_fn(t - 2 * lag, weights.pop(t - 2 * lag))


def _mla_kernel(*refs, with_ctx):
    if with_ctx:
        qa_ref, kv_ref, side_ref, cq_ref, sq_ref, ck_ref, sk_ref, kvc_ref, krc_ref, o_ref = refs
    else:
        qa_ref, kv_ref, side_ref, o_ref = refs
    scale = (MLA_NOPE + MLA_ROPE) ** -0.5
    nope_cols = MLA_HEADS * MLA_NOPE
    q_rope = qa_ref[:, nope_cols:]
    k_rope = side_ref[...]
    if with_ctx:
        q_rope_rot = _rope(q_rope, cq_ref[...], sq_ref[...], MLA_ROPE // 4).astype(BF)
        k_rope_rot = _rope(k_rope, ck_ref[...], sk_ref[...], MLA_ROPE // 4)[:, :MLA_ROPE].astype(BF)
        k_rope_ctx = krc_ref[...].astype(BF)
        q_rope = q_rope.astype(BF)
    else:
        q_rope_rot = q_rope.astype(BF)
        k_rope_rot = k_rope[:, :MLA_ROPE].astype(BF)
    heads = range(MLA_HEADS)
    rope_cols = [slice(h * MLA_ROPE, (h + 1) * MLA_ROPE) for h in heads]
    kv0 = [h * (MLA_NOPE + MLA_V) for h in heads]
    q_n = [qa_ref[:, h * MLA_NOPE:(h + 1) * MLA_NOPE].astype(BF) for h in heads]

    def scores(h):
        blocks = [_bdot_nt(q_n[h], kv_ref[:, kv0[h]:kv0[h] + MLA_NOPE])
                  + _bdot_nt(q_rope_rot[:, rope_cols[h]], k_rope_rot)]
        if with_ctx:
            blocks.append(_bdot_nt(q_n[h], kvc_ref[:, kv0[h]:kv0[h] + MLA_NOPE])
                          + _bdot_nt(q_rope[:, rope_cols[h]], k_rope_ctx))
        return blocks

    def weighted_values(h, weights):
        (e, inv_den), v0 = weights, kv0[h] + MLA_NOPE
        out = jnp.dot(e[0], kv_ref[:, v0:v0 + MLA_V].astype(BF), preferred_element_type=F32)
        if with_ctx:
            out = out + jnp.dot(e[1], kvc_ref[:, v0:v0 + MLA_V].astype(BF), preferred_element_type=F32)
        o_ref[:, h * MLA_V:(h + 1) * MLA_V] = (out * inv_den).astype(BF)

    _walk_heads(MLA_HEADS, scores, lambda h, s: _softmax_terms(s, scale), weighted_values,
                lag=2 if with_ctx else MLA_HEADS)


def _mla_attention(qa, kv, side, bsz, seq, ctx=None):
    t = qa.shape[0]
    tq = ATTN_Q_ROWS
    nq = seq // tq
    side_blk = (SIDE_COLS - 128) // 128
    nkv = MLA_HEADS * (MLA_NOPE + MLA_V)
    in_specs = [pl.BlockSpec((tq, qa.shape[1]), lambda b, i: (b * nq + i, 0)),
                pl.BlockSpec((seq, nkv), lambda b, i: (b, 0)),
                pl.BlockSpec((seq, 128), lambda b, i: (b, side_blk))]
    args = [qa, kv, side]
    if ctx is not None:
        cos_q, sin_q, cos_k, sin_k, kvc, krc = ctx
        past = krc.shape[1]
        in_specs += [pl.BlockSpec((tq, cos_q.shape[1]), lambda b, i: (i, 0)),
                     pl.BlockSpec((tq, cos_q.shape[1]), lambda b, i: (i, 0)),
                     pl.BlockSpec((seq, 128), lambda b, i: (0, 0)),
                     pl.BlockSpec((seq, 128), lambda b, i: (0, 0)),
                     pl.BlockSpec((past, nkv), lambda b, i: (b, 0)),
                     pl.BlockSpec((None, past, MLA_ROPE), lambda b, i: (b, 0, 0))]
        args += [cos_q, sin_q, cos_k, sin_k, kvc, krc]
    nout = MLA_HEADS * MLA_V
    return pl.pallas_call(
        functools.partial(_mla_kernel, with_ctx=ctx is not None),
        grid=(bsz, nq), in_specs=in_specs,
        out_specs=pl.BlockSpec((tq, nout), lambda b, i: (b * nq + i, 0)),
        out_shape=jax.ShapeDtypeStruct((t, nout), BF),
        compiler_params=_params(2), name="mla_attention",
    )(*args)


def _gqa_kernel(*refs, with_ctx):
    if with_ctx:
        q_ref, k_ref, v_ref, gq_ref, gk_ref, cq_ref, sq_ref, ck_ref, sk_ref, kc_ref, vc_ref, o_ref, ksrc_ref = refs
    else:
        q_ref, k_ref, v_ref, gq_ref, gk_ref, o_ref, kn_ref, vo_ref, ksrc_ref = refs
    hd = GQA_HEAD_DIM
    scale = hd ** -0.5
    rep = GQA_HEADS // GQA_KV_HEADS
    n_kv = k_ref.shape[1] // hd
    kv_cols = [slice(g * hd, (g + 1) * hd) for g in range(n_kv)]

    @pl.when(pl.program_id(2) == 0)
    def _():
        for g in range(n_kv):
            k_n = _rms(k_ref[:, kv_cols[g]]) * gk_ref[...]
            if with_ctx:
                ksrc_ref[:, kv_cols[g]] = _rope(k_n, ck_ref[...], sk_ref[...], hd // 4).astype(BF)
            else:
                ksrc_ref[:, kv_cols[g]] = k_n.astype(BF)
                kn_ref[g] = k_n
                vo_ref[g] = v_ref[:, kv_cols[g]]

    v = [v_ref[:, kv_cols[g]].astype(BF) for g in range(n_kv)]
    if with_ctx:
        k_ctx, v_ctx = kc_ref[...].astype(BF), vc_ref[...].astype(BF)
    heads = range(n_kv * rep)
    cols = [slice(r * hd, (r + 1) * hd) for r in heads]
    q_n = [_rms(q_ref[:, cols[r]]) * gq_ref[...] for r in heads]
    k_src = [ksrc_ref[:, kv_cols[r // rep]] for r in heads]
    def scores(r):
        if with_ctx:
            return [_bdot_nt(_rope(q_n[r], cq_ref[...], sq_ref[...], hd // 4), k_src[r]), _bdot_nt(q_n[r], k_ctx)]
        return [_bdot_nt(q_n[r], k_src[r])]

    def weighted_values(r, weights):
        e, inv_den = weights
        out = jnp.dot(e[0], v[r // rep], preferred_element_type=F32)
        if with_ctx:
            out = out + jnp.dot(e[1], v_ctx, preferred_element_type=F32)
        o_ref[:, cols[r]] = (out * inv_den).astype(BF)

    _walk_heads(len(heads), scores, lambda r, s: _softmax_terms(s, scale), weighted_values, lag=1)


def _gqa_attention(qkv, g_q, g_k, bsz, seq, ctx=None):
    t = qkv.shape[0]
    hd = GQA_HEAD_DIM
    rep = GQA_HEADS // GQA_KV_HEADS
    tq = ATTN_Q_ROWS
    nq = seq // tq
    n_kv = 1 if ctx is not None else GQA_KV_HEADS
    k_blk0 = GQA_HEADS // n_kv
    in_specs = [pl.BlockSpec((tq, n_kv * rep * hd), lambda b, g, i: (b * nq + i, g)),
                pl.BlockSpec((seq, n_kv * hd), lambda b, g, i: (b, k_blk0 + g)),
                pl.BlockSpec((seq, n_kv * hd), lambda b, g, i: (b, k_blk0 + GQA_KV_HEADS // n_kv + g)),
                pl.BlockSpec((1, hd), lambda b, g, i: (0, 0)),
                pl.BlockSpec((1, hd), lambda b, g, i: (0, 0))]
    args = [qkv, qkv, qkv, g_q.reshape(1, hd), g_k.reshape(1, hd)]
    o_spec = pl.BlockSpec((tq, n_kv * rep * hd), lambda b, g, i: (b * nq + i, g))
    o_shape = jax.ShapeDtypeStruct((t, GQA_HEADS * hd), BF)
    if ctx is not None:
        cos_t, sin_t, kc, vc, j = ctx
        past = kc.shape[3]
        cache = pl.BlockSpec((None, None, None, past, hd), lambda b, g, i: (b, j, g, 0, 0))
        in_specs += [pl.BlockSpec((tq, hd), lambda b, g, i: (i, 0)),
                     pl.BlockSpec((tq, hd), lambda b, g, i: (i, 0)),
                     pl.BlockSpec((seq, hd), lambda b, g, i: (0, 0)),
                     pl.BlockSpec((seq, hd), lambda b, g, i: (0, 0)),
                     cache, cache]
        args += [cos_t, sin_t, cos_t, sin_t, kc, vc]
        out_specs, out_shape = o_spec, o_shape
    else:
        head_major = pl.BlockSpec((None, n_kv, seq, hd), lambda b, g, i: (b, g, 0, 0))
        out_specs = [o_spec, head_major, head_major]
        kv_shape = jax.ShapeDtypeStruct((bsz, GQA_KV_HEADS, seq, hd), F32)
        out_shape = [o_shape, kv_shape, kv_shape]
    return pl.pallas_call(
        functools.partial(_gqa_kernel, with_ctx=ctx is not None),
        grid=(bsz, GQA_KV_HEADS // n_kv, nq), in_specs=in_specs, out_specs=out_specs, out_shape=out_shape,
        scratch_shapes=[pltpu.VMEM((seq, n_kv * hd), BF)],
        compiler_params=_params(3), name="gqa_attention",
    )(*args)


def _dispatch_kernel(ar_ref, ac_ref, h_ref, xe_ref, gate_ref, rankc_ref, rankr_ref, below_ref, *, cap):
    n_groups, group, s = ar_ref.shape
    g = pl.program_id(1)

    @pl.when(g == 0)
    def _():
        blk = min(RANK_BLOCK, s)
        nb = s // blk
        below_ref[...] = jnp.zeros_like(below_ref)
        rankc_ref[...] = jnp.zeros_like(rankc_ref)
        ones_sub = jnp.ones((8, blk), BF)
        ones_lane = jnp.ones((blk, 128), BF)
        i0 = lax.broadcasted_iota(jnp.int32, (blk, blk), 0)
        i1 = lax.broadcasted_iota(jnp.int32, (blk, blk), 1)
        sub_first = jnp.where(i0 < i1, 1.0, 0.0)
        for e in range(n_groups * group):
            eg, ei = e // group, e % group
            for j in range(nb):
                rows = pl.ds(j * blk, blk)
                a_col = ac_ref[rows, e:e + 1]
                for k in range(j, nb):
                    cols = pl.ds(k * blk, blk)
                    a_row = ar_ref[eg, ei:ei + 1, cols]
                    if j == k:
                        ahead = jnp.where(a_col > a_row, 1.0, jnp.where(a_col >= a_row, sub_first, 0.0))
                    else:
                        ahead = jnp.where(a_col >= a_row, 1.0, 0.0)
                    ahead = ahead.astype(BF)
                    below_ref[e:e + 1, cols] += jnp.dot(ones_sub, ahead, preferred_element_type=F32)[:1, :]
                    if j < k:
                        rankc_ref[rows, e:e + 1] += blk - jnp.dot(ahead, ones_lane, preferred_element_type=F32)[:, :1]
        below, above = below_ref[...], rankc_ref[...]
        rank_rows = below + above.T
        for gi in range(n_groups):
            rankr_ref[gi] = rank_rows[gi * group:(gi + 1) * group, :]
        rankc_ref[...] = above + below.T

    slot = lax.broadcasted_iota(jnp.int32, (group, cap, s), 1).astype(F32)
    pick = rankr_ref[g][:, None, :] == slot
    onehot = jnp.where(pick, 1.0, 0.0).reshape(group * cap, s).astype(BF)
    rows_f32 = jnp.dot(onehot, h_ref[...], preferred_element_type=F32)
    xe_ref[...] = rows_f32.reshape(group, cap, -1).astype(BF)
    gate_ref[...] = jnp.sum(jnp.where(pick, ar_ref[g][:, None, :], 0.0), axis=2, keepdims=True)


def _dispatch(aff_row, aff_col, h, bsz, seq):
    t, d = h.shape
    ne = aff_row.shape[0]
    cap = EC_FACTOR * seq // ne
    group = GATHER_ROWS // cap
    n_groups = ne // group
    return pl.pallas_call(
        functools.partial(_dispatch_kernel, cap=cap),
        grid=(bsz, n_groups),
        in_specs=[pl.BlockSpec((n_groups, group, seq), lambda b, g: (0, 0, b)),
                  pl.BlockSpec((seq, ne), lambda b, g: (b, 0)),
                  pl.BlockSpec((seq, d), lambda b, g: (b, 0))],
        out_specs=[pl.BlockSpec((group, cap, d), lambda b, g: (g, b, 0)),
                   pl.BlockSpec((group, cap, 1), lambda b, g: (g, b, 0)),
                   pl.BlockSpec((seq, ne), lambda b, g: (b, 0))],
        out_shape=[jax.ShapeDtypeStruct((ne, bsz * cap, d), BF),
                   jax.ShapeDtypeStruct((ne, bsz * cap, 1), F32),
                   jax.ShapeDtypeStruct((t, ne), F32)],
        scratch_shapes=[pltpu.VMEM((n_groups, group, seq), F32), pltpu.VMEM((ne, seq), F32)],
        compiler_params=_params(2), name="ec_dispatch",
    )(aff_row.reshape(n_groups, group, t), aff_col, h)


def _experts_kernel(xp_ref, xs_ref, gp_ref, gs_ref, wg_ref, wu_ref, wd_ref, yp_ref, ys_ref, accp_ref, accs_ref):
    f = pl.program_id(1)
    tf = wg_ref.shape[1]
    w_gate_up = jnp.concatenate([wg_ref[...].astype(BF), wu_ref[...].astype(BF)], axis=1)
    wd = wd_ref[...].astype(BF)
    groups = ((xp_ref, gp_ref, yp_ref, accp_ref), (xs_ref, gs_ref, ys_ref, accs_ref))

    @pl.when(f == 0)
    def _():
        for _, _, _, acc_ref in groups:
            acc_ref[...] = jnp.zeros_like(acc_ref)

    for x_ref, _, _, acc_ref in groups:
        for r in range(0, x_ref.shape[0], EXPERT_ROWS):
            rows = slice(r, r + EXPERT_ROWS)
            au = jnp.dot(x_ref[rows, :], w_gate_up, preferred_element_type=F32)
            a, u = au[:, :tf], au[:, tf:]
            acc_ref[rows, :] += jnp.dot((a * _sigmoid(a) * u).astype(BF), wd, preferred_element_type=F32)

    @pl.when(f == pl.num_programs(1) - 1)
    def _():
        for _, g_ref, y_ref, acc_ref in groups:
            y_ref[...] = (acc_ref[...] * g_ref[...]).astype(BF)


def _experts(xe_p, xe_s, gate_p, gate_s, w_gate, w_up, w_down, layer):
    ne, mp, d = xe_p.shape
    ms = xe_s.shape[1]
    fdim = w_gate.shape[-1]
    tf = EXPERT_HIDDEN_COLS
    rows = lambda m, last: pl.BlockSpec((None, m, last), lambda e, f: (e, 0, 0))
    return pl.pallas_call(
        _experts_kernel, grid=(ne, fdim // tf),
        in_specs=[rows(mp, d), rows(ms, d), rows(mp, 1), rows(ms, 1),
                  pl.BlockSpec((None, None, d, tf), lambda e, f: (layer, e, 0, f)),
                  pl.BlockSpec((None, None, d, tf), lambda e, f: (layer, e, 0, f)),
                  pl.BlockSpec((None, None, tf, d), lambda e, f: (layer, e, f, 0))],
        out_specs=[rows(mp, d), rows(ms, d)],
        out_shape=[jax.ShapeDtypeStruct((ne, mp, d), BF), jax.ShapeDtypeStruct((ne, ms, d), BF)],
        scratch_shapes=[pltpu.VMEM((mp, d), F32), pltpu.VMEM((ms, d), F32)],
        compiler_params=_params(2), name="ec_experts",
    )(xe_p, xe_s, gate_p, gate_s, w_gate, w_up, w_down)


RANK_RADIX = 32


def _combine_kernel(y_ref, rank_ref, x_ref, g_ref, *refs, cap, final):
    post_refs, place_ref = refs[:-1], refs[-1]
    ne = y_ref.shape[0]
    n_slots = ne * cap

    @pl.when(pl.program_id(1) == 0)
    def _():
        rank = rank_ref[...]
        hi = jnp.floor(rank * (1.0 / RANK_RADIX))
        lo = rank - RANK_RADIX * hi
        lane = lax.broadcasted_iota(jnp.int32, (ne, n_slots), 1)
        expert = lax.broadcasted_iota(jnp.int32, (ne, n_slots), 0)
        own = (lane >= expert * cap) & (lane < (expert + 1) * cap)
        spread_hi = jnp.where(own, float(RANK_RADIX), 0.0).astype(BF)
        spread_lo = jnp.where(own, 1.0, 0.0).astype(BF)
        spread = (jnp.dot(hi.astype(BF), spread_hi, preferred_element_type=F32)
                  + jnp.dot(lo.astype(BF), spread_lo, preferred_element_type=F32))
        slot = (lax.broadcasted_iota(jnp.int32, (1, n_slots), 1) & (cap - 1)).astype(F32)
        place_ref[...] = jnp.where(spread == slot, 1.0, 0.0).astype(BF)

    ts = x_ref.shape[0]
    rows = pl.ds(pl.multiple_of(pl.program_id(1) * ts, ts), ts)
    y = y_ref[...].reshape(n_slots, y_ref.shape[2])
    x_new = x_ref[...] + g_ref[...] * jnp.dot(place_ref[rows, :], y, preferred_element_type=F32)
    if final:
        (gain_ref,), (out_ref,) = post_refs[:1], post_refs[1:]
        out_ref[...] = _rms(x_new) * gain_ref[...]
    else:
        (sh_ref, sc_ref), (xo_ref, h_ref) = post_refs[:2], post_refs[2:]
        xo_ref[...] = x_new
        h_ref[...] = (_rms(x_new) * (1.0 + sc_ref[...]) + sh_ref[...]).astype(BF)


def _combine(y, rank, x, gate, bsz, seq, next_shift_scale=None, final_gain=None):
    t, d = x.shape
    ne = y.shape[0]
    cap = EC_FACTOR * seq // ne
    assert seq <= RANK_RADIX * RANK_RADIX and cap & (cap - 1) == 0
    ts = min(COMBINE_ROWS, seq)
    nt = seq // ts
    final = final_gain is not None
    vec = lambda v: pl.BlockSpec((None, 1, d), lambda b, i: (b if v.shape[0] > 1 else 0, 0, 0))
    tile = pl.BlockSpec((ts, d), lambda b, i: (b * nt + i, 0))
    in_specs = [pl.BlockSpec((ne, cap, d), lambda b, i: (0, b, 0)),
                pl.BlockSpec((seq, ne), lambda b, i: (b, 0)), tile, vec(gate)]
    if final:
        args = [final_gain.reshape(1, d)]
        in_specs.append(pl.BlockSpec((1, d), lambda b, i: (0, 0)))
        out_specs, out_shape = tile, jax.ShapeDtypeStruct((t, d), F32)
    else:
        args = list(next_shift_scale)
        in_specs += [vec(v) for v in args]
        out_specs = [tile, tile]
        out_shape = [jax.ShapeDtypeStruct((t, d), F32), jax.ShapeDtypeStruct((t, d), BF)]
    return pl.pallas_call(
        functools.partial(_combine_kernel, cap=cap, final=final),
        grid=(bsz, nt), in_specs=in_specs, out_specs=out_specs, out_shape=out_shape,
        scratch_shapes=[pltpu.VMEM((seq, ne * cap), BF)],
        compiler_params=_params(2), name="ec_combine",
    )(y, rank, x, gate, *args)


def _rope_tables(n_tokens, rot_dim):
    rows = n_tokens // GRID_W
    row = jnp.repeat(jnp.arange(rows), GRID_W).astype(F32)
    col = jnp.tile(jnp.arange(GRID_W), rows).astype(F32)
    quarter = rot_dim // 4
    inv = ROPE_THETA ** (-jnp.arange(quarter, dtype=F32) / quarter)
    a_row, a_col = row[:, None] * inv, col[:, None] * inv
    cos_t = jnp.concatenate([jnp.cos(a_row), jnp.cos(a_row), jnp.cos(a_col), jnp.cos(a_col)], axis=-1)
    sin_t = jnp.concatenate([-jnp.sin(a_row), jnp.sin(a_row), -jnp.sin(a_col), jnp.sin(a_col)], axis=-1)
    return cos_t, sin_t


def kernel(x_prompt, x_sample, state_mlstm_c, state_mlstm_n, state_mlstm_m, cache_mla_ckv, cache_mla_krope,
           cache_gqa_k, cache_gqa_v, c, c_ctx, w_mod, b_mod, w_in_even, b_igate, b_fgate, g_mlstm, g_cq, w_uq,
           g_ckv, w_ukv, w_out_even, w_in_odd, g_qnorm, g_knorm, w_out_odd, w_router, w_expert_gate,
           w_expert_up, w_expert_down, g_final):
    d = D_MODEL
    bp, sp, _ = x_prompt.shape
    bs, ss, _ = x_sample.shape
    depth = w_mod.shape[0]
    nh = ML_HEADS
    streams = {"p": (bp, sp), "s": (bs, ss)}
    x = {"p": x_prompt.reshape(bp * sp, d), "s": x_sample.reshape(bs * ss, d)}

    c8 = jnp.concatenate([c_ctx[None], c, jnp.zeros((8 - 1 - bs, d), F32)], axis=0)
    mod_all = _mod_vectors(c8, w_mod, b_mod).reshape(depth, 8, 6, 1, d)

    def mod(layer, key, idx):
        rows = mod_all[layer, 0:1, idx] if key == "p" else mod_all[layer, 1:1 + bs, idx]
        return rows

    new_even, new_odd = [], []
    h_in = {key: _norm_mod(x[key], mod(0, key, 0), mod(0, key, 1), streams[key][1]) for key in streams}
    routed_in = {}
    for layer in range(depth):
        j = layer // 2
        if layer % 2 == 0:
            w_out = _to_bf16(w_out_even, j)
            w_in_rows = jnp.swapaxes(w_in_even, 1, 2)
            w_side = jnp.concatenate(
                [w_in_rows[j, QKVO_COLS + 4 * nh:],
                 w_in_rows[j, QKVO_COLS:QKVO_COLS + 4 * nh],
                 jnp.zeros((128 - MLA_ROPE - 4 * nh, d), F32)], axis=0)[None]
            w_q = w_uq[j].reshape(MLA_Q_RANK, MLA_HEADS, MLA_NOPE + MLA_ROPE)
            w_q = jnp.concatenate([w_q[:, :, :MLA_NOPE].reshape(MLA_Q_RANK, -1),
                                   w_q[:, :, MLA_NOPE:].reshape(MLA_Q_RANK, -1)], axis=1)
            bias_col = jnp.concatenate([b_igate[j].reshape(1, -1), b_fgate[j].reshape(1, -1)], axis=1)
            bias_row = bias_col.reshape(-1, 1)
            cos64, sin64 = _rope_tables(ss, MLA_ROPE)
            cos_q, sin_q = jnp.tile(cos64, (1, MLA_HEADS)), jnp.tile(sin64, (1, MLA_HEADS))
            pad = jnp.zeros((ss, 128 - MLA_ROPE), F32)
            cos_k, sin_k = jnp.concatenate([cos64, pad], axis=1), jnp.concatenate([sin64, pad], axis=1)
            kvc = _mm_resident(cache_mla_ckv[:, j].reshape(-1, MLA_KV_RANK), w_ukv, j, w_ukv.shape[-1], IN_PROJ_COLS,
                               name="mla_ctx_expand", out_dtype=BF)
            for key, (bsz, seq) in streams.items():
                h = h_in[key]
                qkvo = _mm_resident(h, w_in_rows, j, QKVO_COLS, IN_PROJ_COLS, w_rows=True, name="even_in_main")
                side = _mm_resident(h, w_side, 0, SIDE_COLS, SIDE_COLS // 3, w_rows=True, name="even_in_side")
                gates = side[:, SIDE_COLS - 128 + GATE_LANE0:SIDE_COLS - 128 + GATE_LANE0 + 4 * nh]
                g_row = gates.reshape(-1, ML_CHUNK, 4 * nh).transpose(0, 2, 1)
                if key == "p":
                    init = None
                else:
                    c0 = state_mlstm_c[:, j]
                    init = (c0, jnp.broadcast_to(state_mlstm_n[:, j][..., None], c0.shape),
                            state_mlstm_m[:, j].reshape(bsz, 1, 2 * nh))
                hf, hb, c_fin, n_fin, m_fin = _mlstm(qkvo, side, g_row, bias_col, bias_row, init, bsz, seq)
                qa = _norm_mm(side, 0, g_cq[j], w_q, False, "mla_q_up", out_dtype=BF if key == "p" else F32)
                kv, ckv_n = _norm_mm(side, 1, g_ckv[j], w_ukv[j], True, "mla_kv_up", out_dtype=BF)
                if key == "p":
                    y_a = _mla_attention(qa, kv, side, bsz, seq)
                    k_rope = side[:, SIDE_COLS - 128:SIDE_COLS - 128 + MLA_ROPE]
                    new_even.append((c_fin, n_fin, m_fin.reshape(bsz, 2, nh),
                                     ckv_n.reshape(bsz, seq, -1), k_rope.reshape(bsz, seq, -1)))
                else:
                    y_a = _mla_attention(qa, kv, side, bsz, seq,
                                         ctx=(cos_q, sin_q, cos_k, sin_k, kvc, cache_mla_krope[:, j]))
                routed_in[key] = _out_proj_router([y_a], w_out, x[key], mod(layer, key, 2), mod(layer, key, 3),
                                                  mod(layer, key, 4), w_router, layer, seq,
                                                  mlstm=(hf, hb, qkvo, g_mlstm[j]))
        else:
            w_out = _to_bf16(w_out_odd, j)
            cos_t, sin_t = _rope_tables(ss, GQA_HEAD_DIM)
            for key, (bsz, seq) in streams.items():
                qkv = _mm_resident(h_in[key], w_in_odd, j, w_in_odd.shape[-1], IN_PROJ_COLS, name="odd_in")
                if key == "p":
                    o, k_n, v = _gqa_attention(qkv, g_qnorm[j], g_knorm[j], bsz, seq)
                    new_odd.append((k_n, v))
                else:
                    o = _gqa_attention(qkv, g_qnorm[j], g_knorm[j], bsz, seq,
                                       ctx=(cos_t, sin_t, cache_gqa_k, cache_gqa_v, j))
                routed_in[key] = _out_proj_router([o], w_out, x[key], mod(layer, key, 2), mod(layer, key, 3),
                                                  mod(layer, key, 4), w_router, layer, seq)
        routed = {}
        for key, (bsz, seq) in streams.items():
            x[key], h, aff, aff_t = routed_in[key]
            routed[key] = _dispatch(aff_t, aff, h, bsz, seq)
        y_p, y_s = _experts(routed["p"][0], routed["s"][0], routed["p"][1], routed["s"][1],
                            w_expert_gate, w_expert_up, w_expert_down, layer)
        for key, y in (("p", y_p), ("s", y_s)):
            bsz, seq = streams[key]
            if layer + 1 < depth:
                x[key], h_in[key] = _combine(
                    y, routed[key][2], x[key], mod(layer, key, 5), bsz, seq,
                    next_shift_scale=(mod(layer + 1, key, 0), mod(layer + 1, key, 1)))
            else:
                x[key] = _combine(y, routed[key][2], x[key], mod(layer, key, 5), bsz, seq, final_gain=g_final)

    y_prompt = x["p"].reshape(bp, sp, d)
    y_sample = x["s"].reshape(bs, ss, d)
    new_c = jnp.stack([e[0] for e in new_even], axis=1)
    new_n = jnp.stack([e[1] for e in new_even], axis=1)
    new_m = jnp.stack([e[2] for e in new_even], axis=1)
    new_ckv = jnp.stack([e[3] for e in new_even], axis=1)
    new_krope = jnp.stack([e[4] for e in new_even], axis=1)
    new_k = jnp.stack([e[0] for e in new_odd], axis=1)
    new_v = jnp.stack([e[1] for e in new_odd], axis=1)
    return (y_prompt, y_sample, new_c, new_n, new_m, new_ckv, new_krope, new_k, new_v)
```

```python
import functools
import itertools

import jax
import jax.numpy as jnp
from jax import lax
from jax.experimental import pallas as pl
from jax.experimental.pallas import tpu as pltpu

BF = jnp.bfloat16
F32 = jnp.float32

D_MODEL = 2048
GRID_W = 64
ROPE_THETA = 10000.0
NORM_EPS = 1e-6
ML_HEADS = 8
ML_QK = 64
ML_V = 128
ML_CHUNK = 64
MLA_HEADS = 8
MLA_Q_RANK = 512
MLA_KV_RANK = 512
MLA_NOPE = 128
MLA_ROPE = 64
MLA_V = 128
GQA_HEADS = 16
GQA_KV_HEADS = 4
GQA_HEAD_DIM = 128
EC_FACTOR = 2

QKVO_COLS = 2 * ML_HEADS * ML_QK + 2 * ML_HEADS * ML_V
SIDE_COLS = MLA_Q_RANK + MLA_KV_RANK + 128
GATE_LANE0 = MLA_ROPE

VMEM_LIMIT_BYTES = 56 * 1024 * 1024
OUT_PROJ_ROWS = 512
MLSTM_SEQS = 4
MOD_COLS = 2048
IN_PROJ_COLS = 512
IN_PROJ_ROW_CHUNK = 512
CAST_ROWS = 1024
ROW_TILE = 1024
ATTN_Q_ROWS = 256
EXPERT_HIDDEN_COLS = 512
EXPERT_ROWS = 256
RANK_BLOCK = 256
GATHER_ROWS = 512
COMBINE_ROWS = 256


def _params(n_axes):
    return pltpu.CompilerParams(dimension_semantics=("arbitrary",) * n_axes,
                                vmem_limit_bytes=VMEM_LIMIT_BYTES)


def _bdot(a, b):
    return jnp.dot(a.astype(BF), b.astype(BF), preferred_element_type=F32)


def _bdot_nt(a, b):
    return lax.dot_general(a.astype(BF), b.astype(BF), (((1,), (1,)), ((), ())),
                           preferred_element_type=F32)


def _bdot_tn(a, b):
    return lax.dot_general(a.astype(BF), b.astype(BF), (((0,), (0,)), ((), ())),
                           preferred_element_type=F32)


def _sigmoid(x):
    return 1.0 / (1.0 + jnp.exp(-x))


def _log_sigmoid(x):
    return jnp.minimum(x, 0.0) - jnp.log1p(jnp.exp(-jnp.abs(x)))


def _rms(x):
    return x * lax.rsqrt(jnp.mean(x * x, axis=-1, keepdims=True) + NORM_EPS)


def _rope(x, cos_t, sin_t, quarter):
    width = x.shape[-1]
    axis = x.ndim - 1
    lane = lax.broadcasted_iota(jnp.int32, x.shape, axis)
    partner = jnp.where((lane & quarter) == 0,
                        pltpu.roll(x, width - quarter, axis=axis),
                        pltpu.roll(x, quarter, axis=axis))
    return x * cos_t + partner * sin_t


def _mod_kernel(c_ref, w_ref, b_ref, o_ref):
    c = c_ref[...]
    o_ref[...] = _bdot(c * _sigmoid(c), w_ref[...]) + b_ref[...]


def _mod_vectors(c8, w_mod, b_mod):
    n_layers, k, n = w_mod.shape
    tn = MOD_COLS
    return pl.pallas_call(
        _mod_kernel,
        grid=(n_layers, n // tn),
        in_specs=[pl.BlockSpec((8, k), lambda l, j: (0, 0)),
                  pl.BlockSpec((None, k, tn), lambda l, j: (l, 0, j)),
                  pl.BlockSpec((None, 1, tn), lambda l, j: (l, 0, j))],
        out_specs=pl.BlockSpec((None, 8, tn), lambda l, j: (l, 0, j)),
        out_shape=jax.ShapeDtypeStruct((n_layers, 8, n), F32),
        compiler_params=_params(2),
        name="mod_vectors",
    )(c8, w_mod, b_mod.reshape(n_layers, 1, n))


def _norm_mod_kernel(x_ref, sh_ref, sc_ref, h_ref):
    h_ref[...] = (_rms(x_ref[...]) * (1.0 + sc_ref[...]) + sh_ref[...]).astype(BF)


def _batch_of_tile(n_vectors, tm, seq):
    if n_vectors == 1:
        return lambda i: 0
    assert seq % tm == 0
    return lambda i: (i * tm) // seq


def _norm_mod(x, shift, scale, seq):
    t, d = x.shape
    tm = OUT_PROJ_ROWS
    which = _batch_of_tile(shift.shape[0], tm, seq)
    vec = pl.BlockSpec((None, 1, d), lambda i: (which(i), 0, 0))
    x_spec = pl.BlockSpec((tm, d), lambda i: (i, 0))
    return pl.pallas_call(
        _norm_mod_kernel, grid=(t // tm,),
        in_specs=[x_spec, vec, vec], out_specs=x_spec,
        out_shape=jax.ShapeDtypeStruct((t, d), BF),
        compiler_params=_params(1), name="norm_mod",
    )(x, shift, scale)


def _to_bf16_kernel(w_ref, o_ref):
    o_ref[...] = w_ref[...].astype(BF)


def _to_bf16(w3, layer):
    _, k, n = w3.shape
    tk = CAST_ROWS
    return pl.pallas_call(
        _to_bf16_kernel, grid=(k // tk,),
        in_specs=[pl.BlockSpec((None, tk, n), lambda i: (layer, i, 0))],
        out_specs=pl.BlockSpec((tk, n), lambda i: (i, 0)),
        out_shape=jax.ShapeDtypeStruct((k, n), BF),
        compiler_params=_params(1), name="weight_to_bf16",
    )(w3)


def _mlstm_gated_output(hf_ref, hb_ref, o_ref, g_ref, rows):
    parts = []
    for h in range(ML_HEADS):
        sl = slice(h * ML_V, (h + 1) * ML_V)
        hn = _rms(hf_ref[rows, sl] + hb_ref[rows, sl]) * g_ref[:, sl]
        parts.append((hn * _sigmoid(o_ref[rows, sl])).astype(BF))
    return jnp.concatenate(parts, axis=1)


def _out_proj_router_kernel(*refs, k_sizes, with_mlstm):
    n_lead = 4 if with_mlstm else 0
    n_a = len(k_sizes) - (1 if with_mlstm else 0)
    a_refs = refs[n_lead:n_lead + n_a]
    w_ref, x_ref, g_ref, sh_ref, sc_ref, wr_ref, xo_ref, h_ref, aff_ref, afft_ref = refs[n_lead + n_a:]
    w_router = wr_ref[...].astype(BF)
    chunk = OUT_PROJ_ROWS // 2
    for r in range(0, x_ref.shape[0], chunk):
        rows = slice(r, r + chunk)
        operands = [a_ref[rows, :] for a_ref in a_refs]
        if with_mlstm:
            operands.insert(0, _mlstm_gated_output(*refs[:4], rows))
        acc, off = None, 0
        for a, ks in zip(operands, k_sizes):
            term = jnp.dot(a, w_ref[off:off + ks, :], preferred_element_type=F32)
            acc = term if acc is None else acc + term
            off += ks
        x_new = x_ref[rows, :] + g_ref[...] * acc
        xo_ref[rows, :] = x_new
        h = (_rms(x_new) * (1.0 + sc_ref[...]) + sh_ref[...]).astype(BF)
        h_ref[rows, :] = h
        logits = jnp.dot(h, w_router, preferred_element_type=F32)
        e = jnp.exp(logits - jnp.max(logits, axis=-1, keepdims=True))
        aff = e / jnp.sum(e, axis=-1, keepdims=True)
        aff_ref[rows, :] = aff
        afft_ref[:, rows] = aff.T


def _out_proj_router(a_list, w_bf, x, gate, shift, scale, w_router, layer, seq, mlstm=None):
    m, d = x.shape
    k_sizes = tuple(a.shape[1] for a in a_list)
    assert all(a.dtype == BF for a in a_list)
    tm = OUT_PROJ_ROWS
    ne = w_router.shape[-1]
    which = _batch_of_tile(gate.shape[0], tm, seq)
    vec = pl.BlockSpec((None, 1, d), lambda i: (which(i), 0, 0))
    tile = pl.BlockSpec((tm, d), lambda i: (i, 0))
    lead_args, lead_specs = [], []
    if mlstm is not None:
        hf, hb, qkvo, gain = mlstm
        nv = hf.shape[1]
        scan = pl.BlockSpec((tm, nv), lambda i: (i, 0))
        lead_args = [hf, hb, qkvo, gain.reshape(1, nv)]
        lead_specs = [scan, scan, pl.BlockSpec((tm, nv), lambda i: (i, QKVO_COLS // nv - 1)),
                      pl.BlockSpec((1, nv), lambda i: (0, 0))]
        k_sizes = (nv,) + k_sizes
    assert w_bf.shape == (sum(k_sizes), d)
    in_specs = lead_specs + [pl.BlockSpec((tm, a.shape[1]), lambda i: (i, 0)) for a in a_list]
    in_specs += [pl.BlockSpec(w_bf.shape, lambda i: (0, 0), pipeline_mode=pl.Buffered(1)),
                 tile, vec, vec, vec, pl.BlockSpec((None, d, ne), lambda i: (layer, 0, 0))]
    return pl.pallas_call(
        functools.partial(_out_proj_router_kernel, k_sizes=k_sizes, with_mlstm=mlstm is not None),
        grid=(m // tm,), in_specs=in_specs,
        out_specs=[tile, tile, pl.BlockSpec((tm, ne), lambda i: (i, 0)), pl.BlockSpec((ne, tm), lambda i: (0, i))],
        out_shape=[jax.ShapeDtypeStruct((m, d), F32), jax.ShapeDtypeStruct((m, d), BF),
                   jax.ShapeDtypeStruct((m, ne), F32), jax.ShapeDtypeStruct((ne, m), F32)],
        compiler_params=_params(1), name="out_proj_router",
    )(*lead_args, *a_list, w_bf, x, gate, shift, scale, w_router)


def _mm_resident_kernel(a_ref, w_ref, o_ref, *, w_rows, chunk):
    w = w_ref[...].astype(BF)
    for r in range(0, a_ref.shape[0], chunk):
        rows = slice(r, r + chunk)
        a = a_ref[rows, :].astype(BF)
        if w_rows:
            out = lax.dot_general(a, w, (((1,), (1,)), ((), ())), preferred_element_type=F32)
        else:
            out = jnp.dot(a, w, preferred_element_type=F32)
        o_ref[rows, :] = out.astype(o_ref.dtype)


def _mm_resident(a, w3, layer, n_cols, tn, w_rows=False, name="mm_resident", out_dtype=F32):
    m, k = a.shape
    assert w3.shape[2 if w_rows else 1] == k and n_cols % tn == 0
    if w_rows:
        w_spec = pl.BlockSpec((None, tn, k), lambda j: (layer, j, 0))
    else:
        w_spec = pl.BlockSpec((None, k, tn), lambda j: (layer, 0, j))
    return pl.pallas_call(
        functools.partial(_mm_resident_kernel, w_rows=w_rows, chunk=min(IN_PROJ_ROW_CHUNK, m)),
        grid=(n_cols // tn,),
        in_specs=[pl.BlockSpec((m, k), lambda j: (0, 0), pipeline_mode=pl.Buffered(1)), w_spec],
        out_specs=pl.BlockSpec((m, tn), lambda j: (0, j)),
        out_shape=jax.ShapeDtypeStruct((m, n_cols), out_dtype),
        compiler_params=_params(1), name=name,
    )(a, w3)


def _norm_mm_kernel(x_ref, g_ref, w_ref, *out_refs, with_normed):
    wbf_ref = out_refs[-1]

    @pl.when(pl.program_id(0) == 0)
    def _():
        wbf_ref[...] = w_ref[...].astype(BF)

    xn = _rms(x_ref[...]) * g_ref[...]
    out_refs[0][...] = jnp.dot(xn.astype(BF), wbf_ref[...], preferred_element_type=F32).astype(out_refs[0].dtype)
    if with_normed:
        out_refs[1][...] = xn


def _norm_mm(x, col_blk, gain, w, with_normed, name, out_dtype=F32):
    t = x.shape[0]
    k, n = w.shape
    tm = ROW_TILE
    out_specs = [pl.BlockSpec((tm, n), lambda i: (i, 0))]
    out_shape = [jax.ShapeDtypeStruct((t, n), out_dtype)]
    if with_normed:
        out_specs.append(pl.BlockSpec((tm, k), lambda i: (i, 0)))
        out_shape.append(jax.ShapeDtypeStruct((t, k), F32))
    outs = pl.pallas_call(
        functools.partial(_norm_mm_kernel, with_normed=with_normed),
        grid=(t // tm,),
        in_specs=[pl.BlockSpec((tm, k), lambda i: (i, col_blk)),
                  pl.BlockSpec((1, k), lambda i: (0, 0)),
                  pl.BlockSpec((k, n), lambda i: (0, 0))],
        out_specs=out_specs, out_shape=out_shape,
        scratch_shapes=[pltpu.VMEM((k, n), BF)],
        compiler_params=_params(1), name=name,
    )(x, gain.reshape(1, k), w)
    return outs if with_normed else outs[0]


def _split3(x):
    hi = x.astype(BF)
    rest = x - hi.astype(F32)
    mid = rest.astype(BF)
    return hi, mid, (rest - mid.astype(F32)).astype(BF)


def _scan_max(x, reverse):
    n = x.shape[0]
    row = lax.broadcasted_iota(jnp.int32, x.shape, 0)
    k = 1
    while k < n:
        if reverse:
            shifted = jnp.where(row < n - k, pltpu.roll(x, n - k, axis=0), -jnp.inf)
        else:
            shifted = jnp.where(row >= k, pltpu.roll(x, k, axis=0), -jnp.inf)
        x = jnp.maximum(x, shifted)
        k *= 2
    return x


def _mlstm_kernel(qf_ref, kf_ref, vf_ref, qb_ref, kb_ref, vb_ref, gcf_ref, gcb_ref, grf_ref, grb_ref,
                  bc_ref, br_ref, *refs):
    hf_ref, hb_ref, c_ref, n_out_ref, m_ref, n_ref = refs[-6:]

    @pl.when(pl.program_id(1) == 0)
    def _():
        for state_ref, init_ref in zip((c_ref, n_ref, m_ref), refs[:-6] or (None,) * 3):
            state_ref[...] = jnp.zeros_like(state_ref) if init_ref is None else init_ref[...]

    nh, lc = ML_HEADS, ML_CHUNK
    row = lax.broadcasted_iota(jnp.int32, (lc, lc), 0)
    col = lax.broadcasted_iota(jnp.int32, (lc, lc), 1)
    lower = col <= row
    upper = col >= row
    ones_v = jnp.ones((lc, ML_V), BF)
    directions = ((qf_ref, kf_ref, vf_ref, gcf_ref, grf_ref, hf_ref, lower, upper),
                  (qb_ref, kb_ref, vb_ref, gcb_ref, grb_ref, hb_ref, upper, lower))
    seqs = range(c_ref.shape[0])
    gate_terms = {}
    for r, (d, (_, _, _, gc_ref, gr_ref, _, allowed, allowed_t)) in itertools.product(seqs, enumerate(directions)):
        g_col = gc_ref[r, :, GATE_LANE0:GATE_LANE0 + 4 * nh] + bc_ref[...]
        g_row = gr_ref[r] + br_ref[...]
        i_col = g_col[:, nh * d:nh * (d + 1)]
        f_col = _log_sigmoid(g_col[:, 2 * nh + nh * d:2 * nh + nh * (d + 1)])
        i_row = g_row[nh * d:nh * (d + 1), :]
        f_row = _log_sigmoid(g_row[2 * nh + nh * d:2 * nh + nh * (d + 1), :])
        tri = jnp.where(allowed, 1.0, 0.0).astype(BF)
        tri_t = jnp.where(allowed_t, 1.0, 0.0).astype(BF)
        b_col = sum(jnp.dot(tri, part, preferred_element_type=F32) for part in _split3(f_col))
        b_row = sum(jnp.dot(part, tri_t, preferred_element_type=F32) for part in _split3(f_row))
        b_end = jnp.sum(f_col, axis=0, keepdims=True)
        m_prev = m_ref[r, :, nh * d:nh * (d + 1)]
        m_t = b_col + jnp.maximum(m_prev, _scan_max(i_col - b_col, reverse=d == 1))
        g_col_end = b_end - b_col + i_col
        m_new = jnp.maximum(b_end + m_prev, jnp.max(g_col_end, axis=0, keepdims=True))
        gate_terms[r, d] = dict(
            u=b_col - m_t, r_row=i_row - b_row, w_inter=jnp.exp(b_col + m_prev - m_t), floor=jnp.exp(-m_t),
            k_scale=jnp.exp(g_col_end - m_new), decay=jnp.exp(b_end + m_prev - m_new), m_new=m_new)
    work = []
    for r, (d, (q_ref, k_ref, v_ref, _, _, h_ref, allowed, _)) in itertools.product(seqs, enumerate(directions)):
        for h in range(nh):
            w = dict(r=r, d=d, h=h, h_ref=h_ref, allowed=allowed, g=gate_terms[r, d])
            w["q"] = (q_ref[r, :, h * ML_QK:(h + 1) * ML_QK] * (ML_QK ** -0.5)).astype(BF)
            w["k"] = k_ref[r, :, h * ML_QK:(h + 1) * ML_QK]
            w["v1"] = jnp.concatenate([v_ref[r, :, h * ML_V:(h + 1) * ML_V].astype(BF), ones_v], axis=1)
            work.append(w)
    for w in work:
        w["qk"] = _bdot_nt(w["q"], w["k"])
    for w in work:
        r, d, h = w["r"], w["d"], w["h"]
        w["c_prev"], w["n_prev"] = c_ref[r, d, h], n_ref[r, d, h]
        state = jnp.concatenate([w["c_prev"], w["n_prev"]], axis=1).astype(BF)
        w["q_state"] = jnp.dot(w["q"], state, preferred_element_type=F32)
    for w in work:
        h, g = w["h"], w["g"]
        kw = w["k"] * g["k_scale"][:, h:h + 1]
        w["kv"] = _bdot_tn(kw, w["v1"])
    for w in work:
        h, g = w["h"], w["g"]
        dmat = jnp.where(w["allowed"], g["u"][:, h:h + 1] + g["r_row"][h:h + 1, :], -jnp.inf)
        sw = w["qk"] * jnp.exp(dmat)
        sw_hi = sw.astype(BF)
        sw_lo = (sw - sw_hi.astype(F32)).astype(BF)
        w["pv"] = jnp.dot(sw_hi, w["v1"], preferred_element_type=F32)
        w["den_lo"] = jnp.dot(sw_lo, ones_v, preferred_element_type=F32)
    for w in work:
        r, d, h, g = w["r"], w["d"], w["h"], w["g"]
        w_inter = g["w_inter"][:, h:h + 1]
        num = w_inter * w["q_state"][:, :ML_V] + w["pv"][:, :ML_V]
        den = w_inter * w["q_state"][:, ML_V:] + (w["pv"][:, ML_V:] + w["den_lo"])
        w["h_ref"][r, :, h * ML_V:(h + 1) * ML_V] = num / jnp.maximum(jnp.abs(den), g["floor"][:, h:h + 1])
        decay = g["decay"][:, h:h + 1]
        c_ref[r, d, h] = decay * w["c_prev"] + w["kv"][:, :ML_V]
        n_ref[r, d, h] = decay * w["n_prev"] + w["kv"][:, ML_V:]
    for r in seqs:
        m_ref[r, :, :nh] = gate_terms[r, 0]["m_new"]
        m_ref[r, :, nh:] = gate_terms[r, 1]["m_new"]

    @pl.when(pl.program_id(1) == pl.num_programs(1) - 1)
    def _():
        for r, d, h in itertools.product(seqs, range(2), range(nh)):
            n_out_ref[r, d, h:h + 1, :] = n_ref[r, d, h].T[:1, :]


def _mlstm(qkvo, side, g_row, b_col, b_row, init, bsz, seq):
    t = qkvo.shape[0]
    nc = seq // ML_CHUNK
    lc = ML_CHUNK
    nq = ML_HEADS * ML_QK
    nv = ML_HEADS * ML_V
    side_blk = (SIDE_COLS - 128) // 128
    rb = min(MLSTM_SEQS, bsz)
    assert bsz % rb == 0
    fwd = lambda c: c
    bwd = lambda c: nc - 1 - c

    def specs(pos):
        return [pl.BlockSpec((rb, lc, nq), lambda b, c: (b, pos(c), 0)),
                pl.BlockSpec((rb, lc, nq), lambda b, c: (b, pos(c), 1)),
                pl.BlockSpec((rb, lc, nv), lambda b, c: (b, pos(c), 1))]

    state = lambda *shape: pl.BlockSpec((rb,) + shape, lambda b, c: (b,) + (0,) * len(shape))
    state_specs = [state(2, ML_HEADS, ML_QK, ML_V), state(2, ML_HEADS, ML_QK, ML_V), state(1, 2 * ML_HEADS)]
    in_specs = specs(fwd) + specs(bwd) + [
        pl.BlockSpec((rb, lc, 128), lambda b, c: (b, fwd(c), side_blk)),
        pl.BlockSpec((rb, lc, 128), lambda b, c: (b, bwd(c), side_blk)),
        pl.BlockSpec((rb, None, 4 * ML_HEADS, lc), lambda b, c: (b, fwd(c), 0, 0)),
        pl.BlockSpec((rb, None, 4 * ML_HEADS, lc), lambda b, c: (b, bwd(c), 0, 0)),
        pl.BlockSpec((1, 4 * ML_HEADS), lambda b, c: (0, 0)),
        pl.BlockSpec((4 * ML_HEADS, 1), lambda b, c: (0, 0))]
    init = () if init is None else tuple(init)
    in_specs += state_specs[:len(init)]
    out_specs = [pl.BlockSpec((rb, lc, nv), lambda b, c: (b, fwd(c), 0)),
                 pl.BlockSpec((rb, lc, nv), lambda b, c: (b, bwd(c), 0)),
                 state_specs[0], state(2, ML_HEADS, ML_QK), state_specs[2]]
    out_shape = [jax.ShapeDtypeStruct((bsz, seq, nv), F32), jax.ShapeDtypeStruct((bsz, seq, nv), F32),
                 jax.ShapeDtypeStruct((bsz, 2, ML_HEADS, ML_QK, ML_V), F32),
                 jax.ShapeDtypeStruct((bsz, 2, ML_HEADS, ML_QK), F32),
                 jax.ShapeDtypeStruct((bsz, 1, 2 * ML_HEADS), F32)]
    qkvo3, side3 = qkvo.reshape(bsz, seq, -1), side.reshape(bsz, seq, -1)
    g_row4 = g_row.reshape(bsz, nc, 4 * ML_HEADS, lc)
    hf, hb, c_fin, n_fin, m_fin = pl.pallas_call(
        _mlstm_kernel, grid=(bsz // rb, nc), in_specs=in_specs, out_specs=out_specs, out_shape=out_shape,
        scratch_shapes=[pltpu.VMEM((rb, 2, ML_HEADS, ML_QK, ML_V), F32)],
        compiler_params=_params(2), name="mlstm",
    )(qkvo3, qkvo3, qkvo3, qkvo3, qkvo3, qkvo3, side3, side3, g_row4, g_row4, b_col, b_row, *init)
    return hf.reshape(t, nv), hb.reshape(t, nv), c_fin, n_fin, m_fin


LOG2_E = 1.4426950408889634


def _softmax_terms(scores, scale):
    c = scale * LOG2_E
    scaled = [s * c for s in scores]
    m = functools.reduce(jnp.maximum, [jnp.max(s, axis=-1, keepdims=True) for s in scaled])
    e = [jnp.exp2(s - m) for s in scaled]
    den = functools.reduce(lambda a, b: a + b, [jnp.sum(x, axis=-1, keepdims=True) for x in e])
    return [x.astype(BF) for x in e], 1.0 / den


def _walk_heads(n_heads, scores_fn, softmax_fn, values_fn, lag):
    scores, weights = {}, {}
    for t in range(n_heads + 2 * lag):
        if t < n_heads:
            scores[t] = scores_fn(t)
        if 0 <= t - lag < n_heads:
            weights[t - lag] = softmax_fn(t - lag, scores.pop(t - lag))
        if 0 <= t - 2 * lag < n_heads:
            values_fn(t - 2 * lag, weights.pop(t - 2 * lag))


def _mla_kernel(*refs, with_ctx):
    if with_ctx:
        qa_ref, kv_ref, side_ref, cq_ref, sq_ref, ck_ref, sk_ref, kvc_ref, krc_ref, o_ref = refs
    else:
        qa_ref, kv_ref, side_ref, o_ref = refs
    scale = (MLA_NOPE + MLA_ROPE) ** -0.5
    nope_cols = MLA_HEADS * MLA_NOPE
    q_rope = qa_ref[:, nope_cols:]
    k_rope = side_ref[...]
    if with_ctx:
        q_rope_rot = _rope(q_rope, cq_ref[...], sq_ref[...], MLA_ROPE // 4).astype(BF)
        k_rope_rot = _rope(k_rope, ck_ref[...], sk_ref[...], MLA_ROPE // 4)[:, :MLA_ROPE].astype(BF)
        k_rope_ctx = krc_ref[...].astype(BF)
        q_rope = q_rope.astype(BF)
    else:
        q_rope_rot = q_rope.astype(BF)
        k_rope_rot = k_rope[:, :MLA_ROPE].astype(BF)
    heads = range(MLA_HEADS)
    rope_cols = [slice(h * MLA_ROPE, (h + 1) * MLA_ROPE) for h in heads]
    kv0 = [h * (MLA_NOPE + MLA_V) for h in heads]
    q_n = [qa_ref[:, h * MLA_NOPE:(h + 1) * MLA_NOPE].astype(BF) for h in heads]

    def scores(h):
        blocks = [_bdot_nt(q_n[h], kv_ref[:, kv0[h]:kv0[h] + MLA_NOPE])
                  + _bdot_nt(q_rope_rot[:, rope_cols[h]], k_rope_rot)]
        if with_ctx:
            blocks.append(_bdot_nt(q_n[h], kvc_ref[:, kv0[h]:kv0[h] + MLA_NOPE])
                          + _bdot_nt(q_rope[:, rope_cols[h]], k_rope_ctx))
        return blocks

    def weighted_values(h, weights):
        (e, inv_den), v0 = weights, kv0[h] + MLA_NOPE
        out = jnp.dot(e[0], kv_ref[:, v0:v0 + MLA_V].astype(BF), preferred_element_type=F32)
        if with_ctx:
            out = out + jnp.dot(e[1], kvc_ref[:, v0:v0 + MLA_V].astype(BF), preferred_element_type=F32)
        o_ref[:, h * MLA_V:(h + 1) * MLA_V] = (out * inv_den).astype(BF)

    _walk_heads(MLA_HEADS, scores, lambda h, s: _softmax_terms(s, scale), weighted_values,
                lag=2 if with_ctx else MLA_HEADS)


def _mla_attention(qa, kv, side, bsz, seq, ctx=None):
    t = qa.shape[0]
    tq = ATTN_Q_ROWS
    nq = seq // tq
    side_blk = (SIDE_COLS - 128) // 128
    nkv = MLA_HEADS * (MLA_NOPE + MLA_V)
    in_specs = [pl.BlockSpec((tq, qa.shape[1]), lambda b, i: (b * nq + i, 0)),
                pl.BlockSpec((seq, nkv), lambda b, i: (b, 0)),
                pl.BlockSpec((seq, 128), lambda b, i: (b, side_blk))]
    args = [qa, kv, side]
    if ctx is not None:
        cos_q, sin_q, cos_k, sin_k, kvc, krc = ctx
        past = krc.shape[1]
        in_specs += [pl.BlockSpec((tq, cos_q.shape[1]), lambda b, i: (i, 0)),
                     pl.BlockSpec((tq, cos_q.shape[1]), lambda b, i: (i, 0)),
                     pl.BlockSpec((seq, 128), lambda b, i: (0, 0)),
                     pl.BlockSpec((seq, 128), lambda b, i: (0, 0)),
                     pl.BlockSpec((past, nkv), lambda b, i: (b, 0)),
                     pl.BlockSpec((None, past, MLA_ROPE), lambda b, i: (b, 0, 0))]
        args += [cos_q, sin_q, cos_k, sin_k, kvc, krc]
    nout = MLA_HEADS * MLA_V
    return pl.pallas_call(
        functools.partial(_mla_kernel, with_ctx=ctx is not None),
        grid=(bsz, nq), in_specs=in_specs,
        out_specs=pl.BlockSpec((tq, nout), lambda b, i: (b * nq + i, 0)),
        out_shape=jax.ShapeDtypeStruct((t, nout), BF),
        compiler_params=_params(2), name="mla_attention",
    )(*args)


def _gqa_kernel(*refs, with_ctx):
    if with_ctx:
        q_ref, k_ref, v_ref, gq_ref, gk_ref, cq_ref, sq_ref, ck_ref, sk_ref, kc_ref, vc_ref, o_ref, ksrc_ref = refs
    else:
        q_ref, k_ref, v_ref, gq_ref, gk_ref, o_ref, kn_ref, vo_ref, ksrc_ref = refs
    hd = GQA_HEAD_DIM
    scale = hd ** -0.5
    rep = GQA_HEADS // GQA_KV_HEADS
    n_kv = k_ref.shape[1] // hd
    kv_cols = [slice(g * hd, (g + 1) * hd) for g in range(n_kv)]

    @pl.when(pl.program_id(2) == 0)
    def _():
        for g in range(n_kv):
            k_n = _rms(k_ref[:, kv_cols[g]]) * gk_ref[...]
            if with_ctx:
                ksrc_ref[:, kv_cols[g]] = _rope(k_n, ck_ref[...], sk_ref[...], hd // 4).astype(BF)
            else:
                ksrc_ref[:, kv_cols[g]] = k_n.astype(BF)
                kn_ref[g] = k_n
                vo_ref[g] = v_ref[:, kv_cols[g]]

    v = [v_ref[:, kv_cols[g]].astype(BF) for g in range(n_kv)]
    if with_ctx:
        k_ctx, v_ctx = kc_ref[...].astype(BF), vc_ref[...].astype(BF)
    heads = range(n_kv * rep)
    cols = [slice(r * hd, (r + 1) * hd) for r in heads]
    q_n = [_rms(q_ref[:, cols[r]]) * gq_ref[...] for r in heads]
    k_src = [ksrc_ref[:, kv_cols[r // rep]] for r in heads]
    def scores(r):
        if with_ctx:
            return [_bdot_nt(_rope(q_n[r], cq_ref[...], sq_ref[...], hd // 4), k_src[r]), _bdot_nt(q_n[r], k_ctx)]
        return [_bdot_nt(q_n[r], k_src[r])]

    def weighted_values(r, weights):
        e, inv_den = weights
        out = jnp.dot(e[0], v[r // rep], preferred_element_type=F32)
        if with_ctx:
            out = out + jnp.dot(e[1], v_ctx, preferred_element_type=F32)
        o_ref[:, cols[r]] = (out * inv_den).astype(BF)

    _walk_heads(len(heads), scores, lambda r, s: _softmax_terms(s, scale), weighted_values, lag=1)


def _gqa_attention(qkv, g_q, g_k, bsz, seq, ctx=None):
    t = qkv.shape[0]
    hd = GQA_HEAD_DIM
    rep = GQA_HEADS // GQA_KV_HEADS
    tq = ATTN_Q_ROWS
    nq = seq // tq
    n_kv = 1 if ctx is not None else GQA_KV_HEADS
    k_blk0 = GQA_HEADS // n_kv
    in_specs = [pl.BlockSpec((tq, n_kv * rep * hd), lambda b, g, i: (b * nq + i, g)),
                pl.BlockSpec((seq, n_kv * hd), lambda b, g, i: (b, k_blk0 + g)),
                pl.BlockSpec((seq, n_kv * hd), lambda b, g, i: (b, k_blk0 + GQA_KV_HEADS // n_kv + g)),
                pl.BlockSpec((1, hd), lambda b, g, i: (0, 0)),
                pl.BlockSpec((1, hd), lambda b, g, i: (0, 0))]
    args = [qkv, qkv, qkv, g_q.reshape(1, hd), g_k.reshape(1, hd)]
    o_spec = pl.BlockSpec((tq, n_kv * rep * hd), lambda b, g, i: (b * nq + i, g))
    o_shape = jax.ShapeDtypeStruct((t, GQA_HEADS * hd), BF)
    if ctx is not None:
        cos_t, sin_t, kc, vc, j = ctx
        past = kc.shape[3]
        cache = pl.BlockSpec((None, None, None, past, hd), lambda b, g, i: (b, j, g, 0, 0))
        in_specs += [pl.BlockSpec((tq, hd), lambda b, g, i: (i, 0)),
                     pl.BlockSpec((tq, hd), lambda b, g, i: (i, 0)),
                     pl.BlockSpec((seq, hd), lambda b, g, i: (0, 0)),
                     pl.BlockSpec((seq, hd), lambda b, g, i: (0, 0)),
                     cache, cache]
        args += [cos_t, sin_t, cos_t, sin_t, kc, vc]
        out_specs, out_shape = o_spec, o_shape
    else:
        head_major = pl.BlockSpec((None, n_kv, seq, hd), lambda b, g, i: (b, g, 0, 0))
        out_specs = [o_spec, head_major, head_major]
        kv_shape = jax.ShapeDtypeStruct((bsz, GQA_KV_HEADS, seq, hd), F32)
        out_shape = [o_shape, kv_shape, kv_shape]
    return pl.pallas_call(
        functools.partial(_gqa_kernel, with_ctx=ctx is not None),
        grid=(bsz, GQA_KV_HEADS // n_kv, nq), in_specs=in_specs, out_specs=out_specs, out_shape=out_shape,
        scratch_shapes=[pltpu.VMEM((seq, n_kv * hd), BF)],
        compiler_params=_params(3), name="gqa_attention",
    )(*args)


def _dispatch_kernel(ar_ref, ac_ref, h_ref, xe_ref, gate_ref, rankc_ref, rankr_ref, below_ref, *, cap):
    n_groups, group, s = ar_ref.shape
    g = pl.program_id(1)

    @pl.when(g == 0)
    def _():
        blk = min(RANK_BLOCK, s)
        nb = s // blk
        below_ref[...] = jnp.zeros_like(below_ref)
        rankc_ref[...] = jnp.zeros_like(rankc_ref)
        ones_sub = jnp.ones((8, blk), BF)
        ones_lane = jnp.ones((blk, 128), BF)
        i0 = lax.broadcasted_iota(jnp.int32, (blk, blk), 0)
        i1 = lax.broadcasted_iota(jnp.int32, (blk, blk), 1)
        sub_first = jnp.where(i0 < i1, 1.0, 0.0)
        for e in range(n_groups * group):
            eg, ei = e // group, e % group
            for j in range(nb):
                rows = pl.ds(j * blk, blk)
                a_col = ac_ref[rows, e:e + 1]
                for k in range(j, nb):
                    cols = pl.ds(k * blk, blk)
                    a_row = ar_ref[eg, ei:ei + 1, cols]
                    if j == k:
                        ahead = jnp.where(a_col > a_row, 1.0, jnp.where(a_col >= a_row, sub_first, 0.0))
                    else:
                        ahead = jnp.where(a_col >= a_row, 1.0, 0.0)
                    ahead = ahead.astype(BF)
                    below_ref[e:e + 1, cols] += jnp.dot(ones_sub, ahead, preferred_element_type=F32)[:1, :]
                    if j < k:
                        rankc_ref[rows, e:e + 1] += blk - jnp.dot(ahead, ones_lane, preferred_element_type=F32)[:, :1]
        below, above = below_ref[...], rankc_ref[...]
        rank_rows = below + above.T
        for gi in range(n_groups):
            rankr_ref[gi] = rank_rows[gi * group:(gi + 1) * group, :]
        rankc_ref[...] = above + below.T

    slot = lax.broadcasted_iota(jnp.int32, (group, cap, s), 1).astype(F32)
    pick = rankr_ref[g][:, None, :] == slot
    onehot = jnp.where(pick, 1.0, 0.0).reshape(group * cap, s).astype(BF)
    rows_f32 = jnp.dot(onehot, h_ref[...], preferred_element_type=F32)
    xe_ref[...] = rows_f32.reshape(group, cap, -1).astype(BF)
    gate_ref[...] = jnp.sum(jnp.where(pick, ar_ref[g][:, None, :], 0.0), axis=2, keepdims=True)


def _dispatch(aff_row, aff_col, h, bsz, seq):
    t, d = h.shape
    ne = aff_row.shape[0]
    cap = EC_FACTOR * seq // ne
    group = GATHER_ROWS // cap
    n_groups = ne // group
    return pl.pallas_call(
        functools.partial(_dispatch_kernel, cap=cap),
        grid=(bsz, n_groups),
        in_specs=[pl.BlockSpec((n_groups, group, seq), lambda b, g: (0, 0, b)),
                  pl.BlockSpec((seq, ne), lambda b, g: (b, 0)),
                  pl.BlockSpec((seq, d), lambda b, g: (b, 0))],
        out_specs=[pl.BlockSpec((group, cap, d), lambda b, g: (g, b, 0)),
                   pl.BlockSpec((group, cap, 1), lambda b, g: (g, b, 0)),
                   pl.BlockSpec((seq, ne), lambda b, g: (b, 0))],
        out_shape=[jax.ShapeDtypeStruct((ne, bsz * cap, d), BF),
                   jax.ShapeDtypeStruct((ne, bsz * cap, 1), F32),
                   jax.ShapeDtypeStruct((t, ne), F32)],
        scratch_shapes=[pltpu.VMEM((n_groups, group, seq), F32), pltpu.VMEM((ne, seq), F32)],
        compiler_params=_params(2), name="ec_dispatch",
    )(aff_row.reshape(n_groups, group, t), aff_col, h)


def _experts_kernel(xp_ref, xs_ref, gp_ref, gs_ref, wg_ref, wu_ref, wd_ref, yp_ref, ys_ref, accp_ref, accs_ref):
    f = pl.program_id(1)
    tf = wg_ref.shape[1]
    w_gate_up = jnp.concatenate([wg_ref[...].astype(BF), wu_ref[...].astype(BF)], axis=1)
    wd = wd_ref[...].astype(BF)
    groups = ((xp_ref, gp_ref, yp_ref, accp_ref), (xs_ref, gs_ref, ys_ref, accs_ref))

    @pl.when(f == 0)
    def _():
        for _, _, _, acc_ref in groups:
            acc_ref[...] = jnp.zeros_like(acc_ref)

    for x_ref, _, _, acc_ref in groups:
        for r in range(0, x_ref.shape[0], EXPERT_ROWS):
            rows = slice(r, r + EXPERT_ROWS)
            au = jnp.dot(x_ref[rows, :], w_gate_up, preferred_element_type=F32)
            a, u = au[:, :tf], au[:, tf:]
            acc_ref[rows, :] += jnp.dot((a * _sigmoid(a) * u).astype(BF), wd, preferred_element_type=F32)

    @pl.when(f == pl.num_programs(1) - 1)
    def _():
        for _, g_ref, y_ref, acc_ref in groups:
            y_ref[...] = (acc_ref[...] * g_ref[...]).astype(BF)


def _experts(xe_p, xe_s, gate_p, gate_s, w_gate, w_up, w_down, layer):
    ne, mp, d = xe_p.shape
    ms = xe_s.shape[1]
    fdim = w_gate.shape[-1]
    tf = EXPERT_HIDDEN_COLS
    rows = lambda m, last: pl.BlockSpec((None, m, last), lambda e, f: (e, 0, 0))
    return pl.pallas_call(
        _experts_kernel, grid=(ne, fdim // tf),
        in_specs=[rows(mp, d), rows(ms, d), rows(mp, 1), rows(ms, 1),
                  pl.BlockSpec((None, None, d, tf), lambda e, f: (layer, e, 0, f)),
                  pl.BlockSpec((None, None, d, tf), lambda e, f: (layer, e, 0, f)),
                  pl.BlockSpec((None, None, tf, d), lambda e, f: (layer, e, f, 0))],
        out_specs=[rows(mp, d), rows(ms, d)],
        out_shape=[jax.ShapeDtypeStruct((ne, mp, d), BF), jax.ShapeDtypeStruct((ne, ms, d), BF)],
        scratch_shapes=[pltpu.VMEM((mp, d), F32), pltpu.VMEM((ms, d), F32)],
        compiler_params=_params(2), name="ec_experts",
    )(xe_p, xe_s, gate_p, gate_s, w_gate, w_up, w_down)


RANK_RADIX = 32


def _combine_kernel(y_ref, rank_ref, x_ref, g_ref, *refs, cap, final):
    post_refs, place_ref = refs[:-1], refs[-1]
    ne = y_ref.shape[0]
    n_slots = ne * cap

    @pl.when(pl.program_id(1) == 0)
    def _():
        rank = rank_ref[...]
        hi = jnp.floor(rank * (1.0 / RANK_RADIX))
        lo = rank - RANK_RADIX * hi
        lane = lax.broadcasted_iota(jnp.int32, (ne, n_slots), 1)
        expert = lax.broadcasted_iota(jnp.int32, (ne, n_slots), 0)
        own = (lane >= expert * cap) & (lane < (expert + 1) * cap)
        spread_hi = jnp.where(own, float(RANK_RADIX), 0.0).astype(BF)
        spread_lo = jnp.where(own, 1.0, 0.0).astype(BF)
        spread = (jnp.dot(hi.astype(BF), spread_hi, preferred_element_type=F32)
                  + jnp.dot(lo.astype(BF), spread_lo, preferred_element_type=F32))
        slot = (lax.broadcasted_iota(jnp.int32, (1, n_slots), 1) & (cap - 1)).astype(F32)
        place_ref[...] = jnp.where(spread == slot, 1.0, 0.0).astype(BF)

    ts = x_ref.shape[0]
    rows = pl.ds(pl.multiple_of(pl.program_id(1) * ts, ts), ts)
    y = y_ref[...].reshape(n_slots, y_ref.shape[2])
    x_new = x_ref[...] + g_ref[...] * jnp.dot(place_ref[rows, :], y, preferred_element_type=F32)
    if final:
        (gain_ref,), (out_ref,) = post_refs[:1], post_refs[1:]
        out_ref[...] = _rms(x_new) * gain_ref[...]
    else:
        (sh_ref, sc_ref), (xo_ref, h_ref) = post_refs[:2], post_refs[2:]
        xo_ref[...] = x_new
        h_ref[...] = (_rms(x_new) * (1.0 + sc_ref[...]) + sh_ref[...]).astype(BF)


def _combine(y, rank, x, gate, bsz, seq, next_shift_scale=None, final_gain=None):
    t, d = x.shape
    ne = y.shape[0]
    cap = EC_FACTOR * seq // ne
    assert seq <= RANK_RADIX * RANK_RADIX and cap & (cap - 1) == 0
    ts = min(COMBINE_ROWS, seq)
    nt = seq // ts
    final = final_gain is not None
    vec = lambda v: pl.BlockSpec((None, 1, d), lambda b, i: (b if v.shape[0] > 1 else 0, 0, 0))
    tile = pl.BlockSpec((ts, d), lambda b, i: (b * nt + i, 0))
    in_specs = [pl.BlockSpec((ne, cap, d), lambda b, i: (0, b, 0)),
                pl.BlockSpec((seq, ne), lambda b, i: (b, 0)), tile, vec(gate)]
    if final:
        args = [final_gain.reshape(1, d)]
        in_specs.append(pl.BlockSpec((1, d), lambda b, i: (0, 0)))
        out_specs, out_shape = tile, jax.ShapeDtypeStruct((t, d), F32)
    else:
        args = list(next_shift_scale)
        in_specs += [vec(v) for v in args]
        out_specs = [tile, tile]
        out_shape = [jax.ShapeDtypeStruct((t, d), F32), jax.ShapeDtypeStruct((t, d), BF)]
    return pl.pallas_call(
        functools.partial(_combine_kernel, cap=cap, final=final),
        grid=(bsz, nt), in_specs=in_specs, out_specs=out_specs, out_shape=out_shape,
        scratch_shapes=[pltpu.VMEM((seq, ne * cap), BF)],
        compiler_params=_params(2), name="ec_combine",
    )(y, rank, x, gate, *args)


def _rope_tables(n_tokens, rot_dim):
    rows = n_tokens // GRID_W
    row = jnp.repeat(jnp.arange(rows), GRID_W).astype(F32)
    col = jnp.tile(jnp.arange(GRID_W), rows).astype(F32)
    quarter = rot_dim // 4
    inv = ROPE_THETA ** (-jnp.arange(quarter, dtype=F32) / quarter)
    a_row, a_col = row[:, None] * inv, col[:, None] * inv
    cos_t = jnp.concatenate([jnp.cos(a_row), jnp.cos(a_row), jnp.cos(a_col), jnp.cos(a_col)], axis=-1)
    sin_t = jnp.concatenate([-jnp.sin(a_row), jnp.sin(a_row), -jnp.sin(a_col), jnp.sin(a_col)], axis=-1)
    return cos_t, sin_t


def kernel(x_prompt, x_sample, state_mlstm_c, state_mlstm_n, state_mlstm_m, cache_mla_ckv, cache_mla_krope,
           cache_gqa_k, cache_gqa_v, c, c_ctx, w_mod, b_mod, w_in_even, b_igate, b_fgate, g_mlstm, g_cq, w_uq,
           g_ckv, w_ukv, w_out_even, w_in_odd, g_qnorm, g_knorm, w_out_odd, w_router, w_expert_gate,
           w_expert_up, w_expert_down, g_final):
    d = D_MODEL
    bp, sp, _ = x_prompt.shape
    bs, ss, _ = x_sample.shape
    depth = w_mod.shape[0]
    nh = ML_HEADS
    streams = {"p": (bp, sp), "s": (bs, ss)}
    x = {"p": x_prompt.reshape(bp * sp, d), "s": x_sample.reshape(bs * ss, d)}

    c8 = jnp.concatenate([c_ctx[None], c, jnp.zeros((8 - 1 - bs, d), F32)], axis=0)
    mod_all = _mod_vectors(c8, w_mod, b_mod).reshape(depth, 8, 6, 1, d)

    def mod(layer, key, idx):
        rows = mod_all[layer, 0:1, idx] if key == "p" else mod_all[layer, 1:1 + bs, idx]
        return rows

    new_even, new_odd = [], []
    h_in = {key: _norm_mod(x[key], mod(0, key, 0), mod(0, key, 1), streams[key][1]) for key in streams}
    routed_in = {}
    for layer in range(depth):
        j = layer // 2
        if layer % 2 == 0:
            w_out = _to_bf16(w_out_even, j)
            w_in_rows = jnp.swapaxes(w_in_even, 1, 2)
            w_side = jnp.concatenate(
                [w_in_rows[j, QKVO_COLS + 4 * nh:],
                 w_in_rows[j, QKVO_COLS:QKVO_COLS + 4 * nh],
                 jnp.zeros((128 - MLA_ROPE - 4 * nh, d), F32)], axis=0)[None]
            w_q = w_uq[j].reshape(MLA_Q_RANK, MLA_HEADS, MLA_NOPE + MLA_ROPE)
            w_q = jnp.concatenate([w_q[:, :, :MLA_NOPE].reshape(MLA_Q_RANK, -1),
                                   w_q[:, :, MLA_NOPE:].reshape(MLA_Q_RANK, -1)], axis=1)
            bias_col = jnp.concatenate([b_igate[j].reshape(1, -1), b_fgate[j].reshape(1, -1)], axis=1)
            bias_row = bias_col.reshape(-1, 1)
            cos64, sin64 = _rope_tables(ss, MLA_ROPE)
            cos_q, sin_q = jnp.tile(cos64, (1, MLA_HEADS)), jnp.tile(sin64, (1, MLA_HEADS))
            pad = jnp.zeros((ss, 128 - MLA_ROPE), F32)
            cos_k, sin_k = jnp.concatenate([cos64, pad], axis=1), jnp.concatenate([sin64, pad], axis=1)
            kvc = _mm_resident(cache_mla_ckv[:, j].reshape(-1, MLA_KV_RANK), w_ukv, j, w_ukv.shape[-1], IN_PROJ_COLS,
                               name="mla_ctx_expand", out_dtype=BF)
            for key, (bsz, seq) in streams.items():
                h = h_in[key]
                qkvo = _mm_resident(h, w_in_rows, j, QKVO_COLS, IN_PROJ_COLS, w_rows=True, name="even_in_main")
                side = _mm_resident(h, w_side, 0, SIDE_COLS, SIDE_COLS // 3, w_rows=True, name="even_in_side")
                gates = side[:, SIDE_COLS - 128 + GATE_LANE0:SIDE_COLS - 128 + GATE_LANE0 + 4 * nh]
                g_row = gates.reshape(-1, ML_CHUNK, 4 * nh).transpose(0, 2, 1)
                if key == "p":
                    init = None
                else:
                    c0 = state_mlstm_c[:, j]
                    init = (c0, jnp.broadcast_to(state_mlstm_n[:, j][..., None], c0.shape),
                            state_mlstm_m[:, j].reshape(bsz, 1, 2 * nh))
                hf, hb, c_fin, n_fin, m_fin = _mlstm(qkvo, side, g_row, bias_col, bias_row, init, bsz, seq)
                qa = _norm_mm(side, 0, g_cq[j], w_q, False, "mla_q_up", out_dtype=BF if key == "p" else F32)
                kv, ckv_n = _norm_mm(side, 1, g_ckv[j], w_ukv[j], True, "mla_kv_up", out_dtype=BF)
                if key == "p":
                    y_a = _mla_attention(qa, kv, side, bsz, seq)
                    k_rope = side[:, SIDE_COLS - 128:SIDE_COLS - 128 + MLA_ROPE]
                    new_even.append((c_fin, n_fin, m_fin.reshape(bsz, 2, nh),
                                     ckv_n.reshape(bsz, seq, -1), k_rope.reshape(bsz, seq, -1)))
                else:
                    y_a = _mla_attention(qa, kv, side, bsz, seq,
                                         ctx=(cos_q, sin_q, cos_k, sin_k, kvc, cache_mla_krope[:, j]))
                routed_in[key] = _out_proj_router([y_a], w_out, x[key], mod(layer, key, 2), mod(layer, key, 3),
                                                  mod(layer, key, 4), w_router, layer, seq,
                                                  mlstm=(hf, hb, qkvo, g_mlstm[j]))
        else:
            w_out = _to_bf16(w_out_odd, j)
            cos_t, sin_t = _rope_tables(ss, GQA_HEAD_DIM)
            for key, (bsz, seq) in streams.items():
                qkv = _mm_resident(h_in[key], w_in_odd, j, w_in_odd.shape[-1], IN_PROJ_COLS, name="odd_in")
                if key == "p":
                    o, k_n, v = _gqa_attention(qkv, g_qnorm[j], g_knorm[j], bsz, seq)
                    new_odd.append((k_n, v))
                else:
                    o = _gqa_attention(qkv, g_qnorm[j], g_knorm[j], bsz, seq,
                                       ctx=(cos_t, sin_t, cache_gqa_k, cache_gqa_v, j))
                routed_in[key] = _out_proj_router([o], w_out, x[key], mod(layer, key, 2), mod(layer, key, 3),
                                                  mod(layer, key, 4), w_router, layer, seq)
        routed = {}
        for key, (bsz, seq) in streams.items():
            x[key], h, aff, aff_t = routed_in[key]
            routed[key] = _dispatch(aff_t, aff, h, bsz, seq)
        y_p, y_s = _experts(routed["p"][0], routed["s"][0], routed["p"][1], routed["s"][1],
                            w_expert_gate, w_expert_up, w_expert_down, layer)
        for key, y in (("p", y_p), ("s", y_s)):
            bsz, seq = streams[key]
            if layer + 1 < depth:
                x[key], h_in[key] = _combine(
                    y, routed[key][2], x[key], mod(layer, key, 5), bsz, seq,
                    next_shift_scale=(mod(layer + 1, key, 0), mod(layer + 1, key, 1)))
            else:
                x[key] = _combine(y, routed[key][2], x[key], mod(layer, key, 5), bsz, seq, final_gain=g_final)

    y_prompt = x["p"].reshape(bp, sp, d)
    y_sample = x["s"].reshape(bs, ss, d)
    new_c = jnp.stack([e[0] for e in new_even], axis=1)
    new_n = jnp.stack([e[1] for e in new_even], axis=1)
    new_m = jnp.stack([e[2] for e in new_even], axis=1)
    new_ckv = jnp.stack([e[3] for e in new_even], axis=1)
    new_krope = jnp.stack([e[4] for e in new_even], axis=1)
    new_k = jnp.stack([e[0] for e in new_odd], axis=1)
    new_v = jnp.stack([e[1] for e in new_odd], axis=1)
    return (y_prompt, y_sample, new_c, new_n, new_m, new_ckv, new_krope, new_k, new_v)
```

```python
import functools
import itertools

import jax
import jax.numpy as jnp
from jax import lax
from jax.experimental import pallas as pl
from jax.experimental.pallas import tpu as pltpu

BF = jnp.bfloat16
F32 = jnp.float32

D_MODEL = 2048
GRID_W = 64
ROPE_THETA = 10000.0
NORM_EPS = 1e-6
ML_HEADS = 8
ML_QK = 64
ML_V = 128
ML_CHUNK = 64
MLA_HEADS = 8
MLA_Q_RANK = 512
MLA_KV_RANK = 512
MLA_NOPE = 128
MLA_ROPE = 64
MLA_V = 128
GQA_HEADS = 16
GQA_KV_HEADS = 4
GQA_HEAD_DIM = 128
EC_FACTOR = 2

QKVO_COLS = 2 * ML_HEADS * ML_QK + 2 * ML_HEADS * ML_V
SIDE_COLS = MLA_Q_RANK + MLA_KV_RANK + 128
GATE_LANE0 = MLA_ROPE

VMEM_LIMIT_BYTES = 56 * 1024 * 1024
OUT_PROJ_ROWS = 512
MLSTM_SEQS = 4
MOD_COLS = 2048
IN_PROJ_COLS = 512
IN_PROJ_ROW_CHUNK = 512
CAST_ROWS = 1024
ROW_TILE = 1024
ATTN_Q_ROWS = 256
EXPERT_HIDDEN_COLS = 512
EXPERT_ROWS = 256
RANK_BLOCK = 256
GATHER_ROWS = 512
COMBINE_ROWS = 256


def _params(n_axes):
    return pltpu.CompilerParams(dimension_semantics=("arbitrary",) * n_axes,
                                vmem_limit_bytes=VMEM_LIMIT_BYTES)


def _bdot(a, b):
    return jnp.dot(a.astype(BF), b.astype(BF), preferred_element_type=F32)


def _bdot_nt(a, b):
    return lax.dot_general(a.astype(BF), b.astype(BF), (((1,), (1,)), ((), ())),
                           preferred_element_type=F32)


def _bdot_tn(a, b):
    return lax.dot_general(a.astype(BF), b.astype(BF), (((0,), (0,)), ((), ())),
                           preferred_element_type=F32)


def _sigmoid(x):
    return 1.0 / (1.0 + jnp.exp(-x))


def _log_sigmoid(x):
    return jnp.minimum(x, 0.0) - jnp.log1p(jnp.exp(-jnp.abs(x)))


def _rms(x):
    return x * lax.rsqrt(jnp.mean(x * x, axis=-1, keepdims=True) + NORM_EPS)


def _rope(x, cos_t, sin_t, quarter):
    width = x.shape[-1]
    axis = x.ndim - 1
    lane = lax.broadcasted_iota(jnp.int32, x.shape, axis)
    partner = jnp.where((lane & quarter) == 0,
                        pltpu.roll(x, width - quarter, axis=axis),
                        pltpu.roll(x, quarter, axis=axis))
    return x * cos_t + partner * sin_t


def _mod_kernel(c_ref, w_ref, b_ref, o_ref):
    c = c_ref[...]
    o_ref[...] = _bdot(c * _sigmoid(c), w_ref[...]) + b_ref[...]


def _mod_vectors(c8, w_mod, b_mod):
    n_layers, k, n = w_mod.shape
    tn = MOD_COLS
    return pl.pallas_call(
        _mod_kernel,
        grid=(n_layers, n // tn),
        in_specs=[pl.BlockSpec((8, k), lambda l, j: (0, 0)),
                  pl.BlockSpec((None, k, tn), lambda l, j: (l, 0, j)),
                  pl.BlockSpec((None, 1, tn), lambda l, j: (l, 0, j))],
        out_specs=pl.BlockSpec((None, 8, tn), lambda l, j: (l, 0, j)),
        out_shape=jax.ShapeDtypeStruct((n_layers, 8, n), F32),
        compiler_params=_params(2),
        name="mod_vectors",
    )(c8, w_mod, b_mod.reshape(n_layers, 1, n))


def _norm_mod_kernel(x_ref, sh_ref, sc_ref, h_ref):
    h_ref[...] = (_rms(x_ref[...]) * (1.0 + sc_ref[...]) + sh_ref[...]).astype(BF)


def _batch_of_tile(n_vectors, tm, seq):
    if n_vectors == 1:
        return lambda i: 0
    assert seq % tm == 0
    return lambda i: (i * tm) // seq


def _norm_mod(x, shift, scale, seq):
    t, d = x.shape
    tm = OUT_PROJ_ROWS
    which = _batch_of_tile(shift.shape[0], tm, seq)
    vec = pl.BlockSpec((None, 1, d), lambda i: (which(i), 0, 0))
    x_spec = pl.BlockSpec((tm, d), lambda i: (i, 0))
    return pl.pallas_call(
        _norm_mod_kernel, grid=(t // tm,),
        in_specs=[x_spec, vec, vec], out_specs=x_spec,
        out_shape=jax.ShapeDtypeStruct((t, d), BF),
        compiler_params=_params(1), name="norm_mod",
    )(x, shift, scale)


def _to_bf16_kernel(w_ref, o_ref):
    o_ref[...] = w_ref[...].astype(BF)


def _to_bf16(w3, layer):
    _, k, n = w3.shape
    tk = CAST_ROWS
    return pl.pallas_call(
        _to_bf16_kernel, grid=(k // tk,),
        in_specs=[pl.BlockSpec((None, tk, n), lambda i: (layer, i, 0))],
        out_specs=pl.BlockSpec((tk, n), lambda i: (i, 0)),
        out_shape=jax.ShapeDtypeStruct((k, n), BF),
        compiler_params=_params(1), name="weight_to_bf16",
    )(w3)


def _mlstm_gated_output(hf_ref, hb_ref, o_ref, g_ref, rows):
    parts = []
    for h in range(ML_HEADS):
        sl = slice(h * ML_V, (h + 1) * ML_V)
        hn = _rms(hf_ref[rows, sl] + hb_ref[rows, sl]) * g_ref[:, sl]
        parts.append((hn * _sigmoid(o_ref[rows, sl])).astype(BF))
    return jnp.concatenate(parts, axis=1)


def _out_proj_router_kernel(*refs, k_sizes, with_mlstm):
    n_lead = 4 if with_mlstm else 0
    n_a = len(k_sizes) - (1 if with_mlstm else 0)
    a_refs = refs[n_lead:n_lead + n_a]
    w_ref, x_ref, g_ref, sh_ref, sc_ref, wr_ref, xo_ref, h_ref, aff_ref, afft_ref = refs[n_lead + n_a:]
    w_router = wr_ref[...].astype(BF)
    chunk = OUT_PROJ_ROWS // 2
    for r in range(0, x_ref.shape[0], chunk):
        rows = slice(r, r + chunk)
        operands = [a_ref[rows, :] for a_ref in a_refs]
        if with_mlstm:
            operands.insert(0, _mlstm_gated_output(*refs[:4], rows))
        acc, off = None, 0
        for a, ks in zip(operands, k_sizes):
            term = jnp.dot(a, w_ref[off:off + ks, :], preferred_element_type=F32)
            acc = term if acc is None else acc + term
            off += ks
        x_new = x_ref[rows, :] + g_ref[...] * acc
        xo_ref[rows, :] = x_new
        h = (_rms(x_new) * (1.0 + sc_ref[...]) + sh_ref[...]).astype(BF)
        h_ref[rows, :] = h
        logits = jnp.dot(h, w_router, preferred_element_type=F32)
        e = jnp.exp(logits - jnp.max(logits, axis=-1, keepdims=True))
        aff = e / jnp.sum(e, axis=-1, keepdims=True)
        aff_ref[rows, :] = aff
        afft_ref[:, rows] = aff.T


def _out_proj_router(a_list, w_bf, x, gate, shift, scale, w_router, layer, seq, mlstm=None):
    m, d = x.shape
    k_sizes = tuple(a.shape[1] for a in a_list)
    assert all(a.dtype == BF for a in a_list)
    tm = OUT_PROJ_ROWS
    ne = w_router.shape[-1]
    which = _batch_of_tile(gate.shape[0], tm, seq)
    vec = pl.BlockSpec((None, 1, d), lambda i: (which(i), 0, 0))
    tile = pl.BlockSpec((tm, d), lambda i: (i, 0))
    lead_args, lead_specs = [], []
    if mlstm is not None:
        hf, hb, qkvo, gain = mlstm
        nv = hf.shape[1]
        scan = pl.BlockSpec((tm, nv), lambda i: (i, 0))
        lead_args = [hf, hb, qkvo, gain.reshape(1, nv)]
        lead_specs = [scan, scan, pl.BlockSpec((tm, nv), lambda i: (i, QKVO_COLS // nv - 1)),
                      pl.BlockSpec((1, nv), lambda i: (0, 0))]
        k_sizes = (nv,) + k_sizes
    assert w_bf.shape == (sum(k_sizes), d)
    in_specs = lead_specs + [pl.BlockSpec((tm, a.shape[1]), lambda i: (i, 0)) for a in a_list]
    in_specs += [pl.BlockSpec(w_bf.shape, lambda i: (0, 0), pipeline_mode=pl.Buffered(1)),
                 tile, vec, vec, vec, pl.BlockSpec((None, d, ne), lambda i: (layer, 0, 0))]
    return pl.pallas_call(
        functools.partial(_out_proj_router_kernel, k_sizes=k_sizes, with_mlstm=mlstm is not None),
        grid=(m // tm,), in_specs=in_specs,
        out_specs=[tile, tile, pl.BlockSpec((tm, ne), lambda i: (i, 0)), pl.BlockSpec((ne, tm), lambda i: (0, i))],
        out_shape=[jax.ShapeDtypeStruct((m, d), F32), jax.ShapeDtypeStruct((m, d), BF),
                   jax.ShapeDtypeStruct((m, ne), F32), jax.ShapeDtypeStruct((ne, m), F32)],
        compiler_params=_params(1), name="out_proj_router",
    )(*lead_args, *a_list, w_bf, x, gate, shift, scale, w_router)


def _mm_resident_kernel(a_ref, w_ref, o_ref, *, w_rows, chunk):
    w = w_ref[...].astype(BF)
    for r in range(0, a_ref.shape[0], chunk):
        rows = slice(r, r + chunk)
        a = a_ref[rows, :].astype(BF)
        if w_rows:
            out = lax.dot_general(a, w, (((1,), (1,)), ((), ())), preferred_element_type=F32)
        else:
            out = jnp.dot(a, w, preferred_element_type=F32)
        o_ref[rows, :] = out.astype(o_ref.dtype)


def _mm_resident(a, w3, layer, n_cols, tn, w_rows=False, name="mm_resident", out_dtype=F32):
    m, k = a.shape
    assert w3.shape[2 if w_rows else 1] == k and n_cols % tn == 0
    if w_rows:
        w_spec = pl.BlockSpec((None, tn, k), lambda j: (layer, j, 0))
    else:
        w_spec = pl.BlockSpec((None, k, tn), lambda j: (layer, 0, j))
    return pl.pallas_call(
        functools.partial(_mm_resident_kernel, w_rows=w_rows, chunk=min(IN_PROJ_ROW_CHUNK, m)),
        grid=(n_cols // tn,),
        in_specs=[pl.BlockSpec((m, k), lambda j: (0, 0), pipeline_mode=pl.Buffered(1)), w_spec],
        out_specs=pl.BlockSpec((m, tn), lambda j: (0, j)),
        out_shape=jax.ShapeDtypeStruct((m, n_cols), out_dtype),
        compiler_params=_params(1), name=name,
    )(a, w3)


def _split3(x):
    hi = x.astype(BF)
    rest = x - hi.astype(F32)
    mid = rest.astype(BF)
    return hi, mid, (rest - mid.astype(F32)).astype(BF)


def _scan_max(x, reverse):
    n = x.shape[0]
    row = lax.broadcasted_iota(jnp.int32, x.shape, 0)
    k = 1
    while k < n:
        if reverse:
            shifted = jnp.where(row < n - k, pltpu.roll(x, n - k, axis=0), -jnp.inf)
        else:
            shifted = jnp.where(row >= k, pltpu.roll(x, k, axis=0), -jnp.inf)
        x = jnp.maximum(x, shifted)
        k *= 2
    return x


def _mlstm_kernel(qf_ref, kf_ref, vf_ref, qb_ref, kb_ref, vb_ref, gcf_ref, gcb_ref, grf_ref, grb_ref,
                  bc_ref, br_ref, *refs):
    hf_ref, hb_ref, c_ref, n_out_ref, m_ref, n_ref = refs[-6:]

    @pl.when(pl.program_id(1) == 0)
    def _():
        for state_ref, init_ref in zip((c_ref, n_ref, m_ref), refs[:-6] or (None,) * 3):
            state_ref[...] = jnp.zeros_like(state_ref) if init_ref is None else init_ref[...]

    nh, lc = ML_HEADS, ML_CHUNK
    row = lax.broadcasted_iota(jnp.int32, (lc, lc), 0)
    col = lax.broadcasted_iota(jnp.int32, (lc, lc), 1)
    lower = col <= row
    upper = col >= row
    ones_v = jnp.ones((lc, ML_V), BF)
    directions = ((qf_ref, kf_ref, vf_ref, gcf_ref, grf_ref, hf_ref, lower, upper),
                  (qb_ref, kb_ref, vb_ref, gcb_ref, grb_ref, hb_ref, upper, lower))
    seqs = range(c_ref.shape[0])
    gate_terms = {}
    for r, (d, (_, _, _, gc_ref, gr_ref, _, allowed, allowed_t)) in itertools.product(seqs, enumerate(directions)):
        g_col = gc_ref[r, :, GATE_LANE0:GATE_LANE0 + 4 * nh] + bc_ref[...]
        g_row = gr_ref[r] + br_ref[...]
        i_col = g_col[:, nh * d:nh * (d + 1)]
        f_col = _log_sigmoid(g_col[:, 2 * nh + nh * d:2 * nh + nh * (d + 1)])
        i_row = g_row[nh * d:nh * (d + 1), :]
        f_row = _log_sigmoid(g_row[2 * nh + nh * d:2 * nh + nh * (d + 1), :])
        tri = jnp.where(allowed, 1.0, 0.0).astype(BF)
        tri_t = jnp.where(allowed_t, 1.0, 0.0).astype(BF)
        b_col = sum(jnp.dot(tri, part, preferred_element_type=F32) for part in _split3(f_col))
        b_row = sum(jnp.dot(part, tri_t, preferred_element_type=F32) for part in _split3(f_row))
        b_end = jnp.sum(f_col, axis=0, keepdims=True)
        m_prev = m_ref[r, :, nh * d:nh * (d + 1)]
        m_t = b_col + jnp.maximum(m_prev, _scan_max(i_col - b_col, reverse=d == 1))
        g_col_end = b_end - b_col + i_col
        m_new = jnp.maximum(b_end + m_prev, jnp.max(g_col_end, axis=0, keepdims=True))
        gate_terms[r, d] = dict(
            u=b_col - m_t, r_row=i_row - b_row, w_inter=jnp.exp(b_col + m_prev - m_t), floor=jnp.exp(-m_t),
            k_scale=jnp.exp(g_col_end - m_new), decay=jnp.exp(b_end + m_prev - m_new), m_new=m_new)
    work = []
    for r, (d, (q_ref, k_ref, v_ref, _, _, h_ref, allowed, _)) in itertools.product(seqs, enumerate(directions)):
        for h in range(nh):
            w = dict(r=r, d=d, h=h, h_ref=h_ref, allowed=allowed, g=gate_terms[r, d])
            w["q"] = (q_ref[r, :, h * ML_QK:(h + 1) * ML_QK] * (ML_QK ** -0.5)).astype(BF)
            w["k"] = k_ref[r, :, h * ML_QK:(h + 1) * ML_QK]
            w["v1"] = jnp.concatenate([v_ref[r, :, h * ML_V:(h + 1) * ML_V].astype(BF), ones_v], axis=1)
            work.append(w)
    for w in work:
        w["qk"] = _bdot_nt(w["q"], w["k"])
    for w in work:
        r, d, h = w["r"], w["d"], w["h"]
        w["c_prev"], w["n_prev"] = c_ref[r, d, h], n_ref[r, d, h]
        state = jnp.concatenate([w["c_prev"], w["n_prev"]], axis=1).astype(BF)
        w["q_state"] = jnp.dot(w["q"], state, preferred_element_type=F32)
    for w in work:
        h, g = w["h"], w["g"]
        kw = w["k"] * g["k_scale"][:, h:h + 1]
        w["kv"] = _bdot_tn(kw, w["v1"])
    for w in work:
        h, g = w["h"], w["g"]
        dmat = jnp.where(w["allowed"], g["u"][:, h:h + 1] + g["r_row"][h:h + 1, :], -jnp.inf)
        sw = w["qk"] * jnp.exp(dmat)
        sw_hi = sw.astype(BF)
        sw_lo = (sw - sw_hi.astype(F32)).astype(BF)
        w["pv"] = jnp.dot(sw_hi, w["v1"], preferred_element_type=F32)
        w["den_lo"] = jnp.dot(sw_lo, ones_v, preferred_element_type=F32)
    for w in work:
        r, d, h, g = w["r"], w["d"], w["h"], w["g"]
        w_inter = g["w_inter"][:, h:h + 1]
        num = w_inter * w["q_state"][:, :ML_V] + w["pv"][:, :ML_V]
        den = w_inter * w["q_state"][:, ML_V:] + (w["pv"][:, ML_V:] + w["den_lo"])
        w["h_ref"][r, :, h * ML_V:(h + 1) * ML_V] = num / jnp.maximum(jnp.abs(den), g["floor"][:, h:h + 1])
        decay = g["decay"][:, h:h + 1]
        c_ref[r, d, h] = decay * w["c_prev"] + w["kv"][:, :ML_V]
        n_ref[r, d, h] = decay * w["n_prev"] + w["kv"][:, ML_V:]
    for r in seqs:
        m_ref[r, :, :nh] = gate_terms[r, 0]["m_new"]
        m_ref[r, :, nh:] = gate_terms[r, 1]["m_new"]

    @pl.when(pl.program_id(1) == pl.num_programs(1) - 1)
    def _():
        for r, d, h in itertools.product(seqs, range(2), range(nh)):
            n_out_ref[r, d, h:h + 1, :] = n_ref[r, d, h].T[:1, :]


def _mlstm(qkvo, side, g_row, b_col, b_row, init, bsz, seq):
    t = qkvo.shape[0]
    nc = seq // ML_CHUNK
    lc = ML_CHUNK
    nq = ML_HEADS * ML_QK
    nv = ML_HEADS * ML_V
    side_blk = (SIDE_COLS - 128) // 128
    rb = min(MLSTM_SEQS, bsz)
    assert bsz % rb == 0
    fwd = lambda c: c
    bwd = lambda c: nc - 1 - c

    def specs(pos):
        return [pl.BlockSpec((rb, lc, nq), lambda b, c: (b, pos(c), 0)),
                pl.BlockSpec((rb, lc, nq), lambda b, c: (b, pos(c), 1)),
                pl.BlockSpec((rb, lc, nv), lambda b, c: (b, pos(c), 1))]

    state = lambda *shape: pl.BlockSpec((rb,) + shape, lambda b, c: (b,) + (0,) * len(shape))
    state_specs = [state(2, ML_HEADS, ML_QK, ML_V), state(2, ML_HEADS, ML_QK, ML_V), state(1, 2 * ML_HEADS)]
    in_specs = specs(fwd) + specs(bwd) + [
        pl.BlockSpec((rb, lc, 128), lambda b, c: (b, fwd(c), side_blk)),
        pl.BlockSpec((rb, lc, 128), lambda b, c: (b, bwd(c), side_blk)),
        pl.BlockSpec((rb, None, 4 * ML_HEADS, lc), lambda b, c: (b, fwd(c), 0, 0)),
        pl.BlockSpec((rb, None, 4 * ML_HEADS, lc), lambda b, c: (b, bwd(c), 0, 0)),
        pl.BlockSpec((1, 4 * ML_HEADS), lambda b, c: (0, 0)),
        pl.BlockSpec((4 * ML_HEADS, 1), lambda b, c: (0, 0))]
    init = () if init is None else tuple(init)
    in_specs += state_specs[:len(init)]
    out_specs = [pl.BlockSpec((rb, lc, nv), lambda b, c: (b, fwd(c), 0)),
                 pl.BlockSpec((rb, lc, nv), lambda b, c: (b, bwd(c), 0)),
                 state_specs[0], state(2, ML_HEADS, ML_QK), state_specs[2]]
    out_shape = [jax.ShapeDtypeStruct((bsz, seq, nv), F32), jax.ShapeDtypeStruct((bsz, seq, nv), F32),
                 jax.ShapeDtypeStruct((bsz, 2, ML_HEADS, ML_QK, ML_V), F32),
                 jax.ShapeDtypeStruct((bsz, 2, ML_HEADS, ML_QK), F32),
                 jax.ShapeDtypeStruct((bsz, 1, 2 * ML_HEADS), F32)]
    qkvo3, side3 = qkvo.reshape(bsz, seq, -1), side.reshape(bsz, seq, -1)
    g_row4 = g_row.reshape(bsz, nc, 4 * ML_HEADS, lc)
    hf, hb, c_fin, n_fin, m_fin = pl.pallas_call(
        _mlstm_kernel, grid=(bsz // rb, nc), in_specs=in_specs, out_specs=out_specs, out_shape=out_shape,
        scratch_shapes=[pltpu.VMEM((rb, 2, ML_HEADS, ML_QK, ML_V), F32)],
        compiler_params=_params(2), name="mlstm",
    )(qkvo3, qkvo3, qkvo3, qkvo3, qkvo3, qkvo3, side3, side3, g_row4, g_row4, b_col, b_row, *init)
    return hf.reshape(t, nv), hb.reshape(t, nv), c_fin, n_fin, m_fin


LOG2_E = 1.4426950408889634


def _softmax_terms(scores, scale):
    c = scale * LOG2_E
    scaled = [s * c for s in scores]
    m = functools.reduce(jnp.maximum, [jnp.max(s, axis=-1, keepdims=True) for s in scaled])
    e = [jnp.exp2(s - m) for s in scaled]
    den = functools.reduce(lambda a, b: a + b, [jnp.sum(x, axis=-1, keepdims=True) for x in e])
    return [x.astype(BF) for x in e], 1.0 / den


def _walk_heads(n_heads, scores_fn, softmax_fn, values_fn, lag):
    scores, weights = {}, {}
    for t in range(n_heads + 2 * lag):
        if t < n_heads:
            scores[t] = scores_fn(t)
        if 0 <= t - lag < n_heads:
            weights[t - lag] = softmax_fn(t - lag, scores.pop(t - lag))
        if 0 <= t - 2 * lag < n_heads:
            values_fn(t - 2 * lag, weights.pop(t - 2 * lag))


def _mla_kernel(*refs, with_ctx):
    lat_q_ref, lat_kv_ref, side_ref, gq_ref, gkv_ref, wq_ref, wkv_ref = refs[:7]
    wq_bf_ref, wkv_bf_ref, kv_ref = refs[-3:]
    if with_ctx:
        cq_ref, sq_ref, ck_ref, sk_ref, kvc_ref, krc_ref, o_ref = refs[7:-3]
    else:
        o_ref, ckvn_ref = refs[7:-3]

    @pl.when((pl.program_id(0) == 0) & (pl.program_id(1) == 0))
    def _():
        wq_bf_ref[...] = wq_ref[...].astype(BF)
        wkv_bf_ref[...] = wkv_ref[...].astype(BF)

    @pl.when(pl.program_id(1) == 0)
    def _():
        ckv_n = _rms(lat_kv_ref[...]) * gkv_ref[...]
        kv_ref[...] = jnp.dot(ckv_n.astype(BF), wkv_bf_ref[...], preferred_element_type=F32).astype(BF)
        if not with_ctx:
            ckvn_ref[...] = ckv_n

    qa_ref = jnp.dot((_rms(lat_q_ref[...]) * gq_ref[...]).astype(BF), wq_bf_ref[...], preferred_element_type=F32)
    scale = (MLA_NOPE + MLA_ROPE) ** -0.5
    nope_cols = MLA_HEADS * MLA_NOPE
    q_rope = qa_ref[:, nope_cols:]
    k_rope = side_ref[...]
    if with_ctx:
        q_rope_rot = _rope(q_rope, cq_ref[...], sq_ref[...], MLA_ROPE // 4).astype(BF)
        k_rope_rot = _rope(k_rope, ck_ref[...], sk_ref[...], MLA_ROPE // 4)[:, :MLA_ROPE].astype(BF)
        k_rope_ctx = krc_ref[...].astype(BF)
        q_rope = q_rope.astype(BF)
    else:
        q_rope_rot = q_rope.astype(BF)
        k_rope_rot = k_rope[:, :MLA_ROPE].astype(BF)
    heads = range(MLA_HEADS)
    rope_cols = [slice(h * MLA_ROPE, (h + 1) * MLA_ROPE) for h in heads]
    kv0 = [h * (MLA_NOPE + MLA_V) for h in heads]
    q_n = [qa_ref[:, h * MLA_NOPE:(h + 1) * MLA_NOPE].astype(BF) for h in heads]

    def scores(h):
        blocks = [_bdot_nt(q_n[h], kv_ref[:, kv0[h]:kv0[h] + MLA_NOPE])
                  + _bdot_nt(q_rope_rot[:, rope_cols[h]], k_rope_rot)]
        if with_ctx:
            blocks.append(_bdot_nt(q_n[h], kvc_ref[:, kv0[h]:kv0[h] + MLA_NOPE])
                          + _bdot_nt(q_rope[:, rope_cols[h]], k_rope_ctx))
        return blocks

    def weighted_values(h, weights):
        (e, inv_den), v0 = weights, kv0[h] + MLA_NOPE
        out = jnp.dot(e[0], kv_ref[:, v0:v0 + MLA_V].astype(BF), preferred_element_type=F32)
        if with_ctx:
            out = out + jnp.dot(e[1], kvc_ref[:, v0:v0 + MLA_V].astype(BF), preferred_element_type=F32)
        o_ref[:, h * MLA_V:(h + 1) * MLA_V] = (out * inv_den).astype(BF)

    _walk_heads(MLA_HEADS, scores, lambda h, s: _softmax_terms(s, scale), weighted_values,
                lag=2 if with_ctx else MLA_HEADS)


def _mla_attention(side, g_cq, g_ckv, w_q, w_kv, bsz, seq, ctx=None):
    t = side.shape[0]
    tq = ATTN_Q_ROWS
    nq = seq // tq
    side_blk = (SIDE_COLS - 128) // 128
    nkv = MLA_HEADS * (MLA_NOPE + MLA_V)
    rank = MLA_Q_RANK
    whole = lambda a: pl.BlockSpec(a.shape, lambda b, i: (0,) * a.ndim)
    args = [side, side, side, g_cq.reshape(1, rank), g_ckv.reshape(1, rank), w_q, w_kv]
    in_specs = [pl.BlockSpec((tq, rank), lambda b, i: (b * nq + i, 0)),
                pl.BlockSpec((seq, rank), lambda b, i: (b, 1)),
                pl.BlockSpec((seq, 128), lambda b, i: (b, side_blk))] + [whole(a) for a in args[3:]]
    if ctx is not None:
        cos_q, sin_q, cos_k, sin_k, kvc, krc = ctx
        past = krc.shape[1]
        in_specs += [pl.BlockSpec((tq, cos_q.shape[1]), lambda b, i: (i, 0)),
                     pl.BlockSpec((tq, cos_q.shape[1]), lambda b, i: (i, 0)),
                     pl.BlockSpec((seq, 128), lambda b, i: (0, 0)),
                     pl.BlockSpec((seq, 128), lambda b, i: (0, 0)),
                     pl.BlockSpec((past, nkv), lambda b, i: (b, 0)),
                     pl.BlockSpec((None, past, MLA_ROPE), lambda b, i: (b, 0, 0))]
        args += [cos_q, sin_q, cos_k, sin_k, kvc, krc]
    nout = MLA_HEADS * MLA_V
    out_specs = [pl.BlockSpec((tq, nout), lambda b, i: (b * nq + i, 0))]
    out_shape = [jax.ShapeDtypeStruct((t, nout), BF)]
    if ctx is None:
        assert nq == 1
        out_specs.append(pl.BlockSpec((seq, rank), lambda b, i: (b, 0)))
        out_shape.append(jax.ShapeDtypeStruct((t, rank), F32))
    outs = pl.pallas_call(
        functools.partial(_mla_kernel, with_ctx=ctx is not None),
        grid=(bsz, nq), in_specs=in_specs, out_specs=out_specs, out_shape=out_shape,
        scratch_shapes=[pltpu.VMEM(w_q.shape, BF), pltpu.VMEM(w_kv.shape, BF), pltpu.VMEM((seq, nkv), BF)],
        compiler_params=_params(2), name="mla_attention",
    )(*args)
    return outs if ctx is None else outs[0]


def _gqa_kernel(*refs, with_ctx):
    if with_ctx:
        q_ref, k_ref, v_ref, gq_ref, gk_ref, cq_ref, sq_ref, ck_ref, sk_ref, kc_ref, vc_ref, o_ref, ksrc_ref = refs
    else:
        q_ref, k_ref, v_ref, gq_ref, gk_ref, o_ref, kn_ref, vo_ref, ksrc_ref = refs
    hd = GQA_HEAD_DIM
    scale = hd ** -0.5
    rep = GQA_HEADS // GQA_KV_HEADS
    n_kv = k_ref.shape[1] // hd
    kv_cols = [slice(g * hd, (g + 1) * hd) for g in range(n_kv)]

    @pl.when(pl.program_id(2) == 0)
    def _():
        for g in range(n_kv):
            k_n = _rms(k_ref[:, kv_cols[g]]) * gk_ref[...]
            if with_ctx:
                ksrc_ref[:, kv_cols[g]] = _rope(k_n, ck_ref[...], sk_ref[...], hd // 4).astype(BF)
            else:
                ksrc_ref[:, kv_cols[g]] = k_n.astype(BF)
                kn_ref[g] = k_n
                vo_ref[g] = v_ref[:, kv_cols[g]]

    v = [v_ref[:, kv_cols[g]].astype(BF) for g in range(n_kv)]
    if with_ctx:
        k_ctx, v_ctx = kc_ref[...].astype(BF), vc_ref[...].astype(BF)
    heads = range(n_kv * rep)
    cols = [slice(r * hd, (r + 1) * hd) for r in heads]
    q_n = [_rms(q_ref[:, cols[r]]) * gq_ref[...] for r in heads]
    k_src = [ksrc_ref[:, kv_cols[r // rep]] for r in heads]
    def scores(r):
        if with_ctx:
            return [_bdot_nt(_rope(q_n[r], cq_ref[...], sq_ref[...], hd // 4), k_src[r]), _bdot_nt(q_n[r], k_ctx)]
        return [_bdot_nt(q_n[r], k_src[r])]

    def weighted_values(r, weights):
        e, inv_den = weights
        out = jnp.dot(e[0], v[r // rep], preferred_element_type=F32)
        if with_ctx:
            out = out + jnp.dot(e[1], v_ctx, preferred_element_type=F32)
        o_ref[:, cols[r]] = (out * inv_den).astype(BF)

    _walk_heads(len(heads), scores, lambda r, s: _softmax_terms(s, scale), weighted_values, lag=1)


def _gqa_attention(qkv, g_q, g_k, bsz, seq, ctx=None):
    t = qkv.shape[0]
    hd = GQA_HEAD_DIM
    rep = GQA_HEADS // GQA_KV_HEADS
    tq = ATTN_Q_ROWS
    nq = seq // tq
    n_kv = 1 if ctx is not None else GQA_KV_HEADS
    k_blk0 = GQA_HEADS // n_kv
    in_specs = [pl.BlockSpec((tq, n_kv * rep * hd), lambda b, g, i: (b * nq + i, g)),
                pl.BlockSpec((seq, n_kv * hd), lambda b, g, i: (b, k_blk0 + g)),
                pl.BlockSpec((seq, n_kv * hd), lambda b, g, i: (b, k_blk0 + GQA_KV_HEADS // n_kv + g)),
                pl.BlockSpec((1, hd), lambda b, g, i: (0, 0)),
                pl.BlockSpec((1, hd), lambda b, g, i: (0, 0))]
    args = [qkv, qkv, qkv, g_q.reshape(1, hd), g_k.reshape(1, hd)]
    o_spec = pl.BlockSpec((tq, n_kv * rep * hd), lambda b, g, i: (b * nq + i, g))
    o_shape = jax.ShapeDtypeStruct((t, GQA_HEADS * hd), BF)
    if ctx is not None:
        cos_t, sin_t, kc, vc, j = ctx
        past = kc.shape[3]
        cache = pl.BlockSpec((None, None, None, past, hd), lambda b, g, i: (b, j, g, 0, 0))
        in_specs += [pl.BlockSpec((tq, hd), lambda b, g, i: (i, 0)),
                     pl.BlockSpec((tq, hd), lambda b, g, i: (i, 0)),
                     pl.BlockSpec((seq, hd), lambda b, g, i: (0, 0)),
                     pl.BlockSpec((seq, hd), lambda b, g, i: (0, 0)),
                     cache, cache]
        args += [cos_t, sin_t, cos_t, sin_t, kc, vc]
        out_specs, out_shape = o_spec, o_shape
    else:
        head_major = pl.BlockSpec((None, n_kv, seq, hd), lambda b, g, i: (b, g, 0, 0))
        out_specs = [o_spec, head_major, head_major]
        kv_shape = jax.ShapeDtypeStruct((bsz, GQA_KV_HEADS, seq, hd), F32)
        out_shape = [o_shape, kv_shape, kv_shape]
    return pl.pallas_call(
        functools.partial(_gqa_kernel, with_ctx=ctx is not None),
        grid=(bsz, GQA_KV_HEADS // n_kv, nq), in_specs=in_specs, out_specs=out_specs, out_shape=out_shape,
        scratch_shapes=[pltpu.VMEM((seq, n_kv * hd), BF)],
        compiler_params=_params(3), name="gqa_attention",
    )(*args)


def _dispatch_kernel(ar_ref, ac_ref, h_ref, xe_ref, gate_ref, rankc_ref, rankr_ref, below_ref, *, cap):
    n_groups, group, s = ar_ref.shape
    g = pl.program_id(1)

    @pl.when(g == 0)
    def _():
        blk = min(RANK_BLOCK, s)
        nb = s // blk
        below_ref[...] = jnp.zeros_like(below_ref)
        rankc_ref[...] = jnp.zeros_like(rankc_ref)
        ones_sub = jnp.ones((8, blk), BF)
        ones_lane = jnp.ones((blk, 128), BF)
        i0 = lax.broadcasted_iota(jnp.int32, (blk, blk), 0)
        i1 = lax.broadcasted_iota(jnp.int32, (blk, blk), 1)
        sub_first = jnp.where(i0 < i1, 1.0, 0.0)
        for e in range(n_groups * group):
            eg, ei = e // group, e % group
            for j in range(nb):
                rows = pl.ds(j * blk, blk)
                a_col = ac_ref[rows, e:e + 1]
                for k in range(j, nb):
                    cols = pl.ds(k * blk, blk)
                    a_row = ar_ref[eg, ei:ei + 1, cols]
                    if j == k:
                        ahead = jnp.where(a_col > a_row, 1.0, jnp.where(a_col >= a_row, sub_first, 0.0))
                    else:
                        ahead = jnp.where(a_col >= a_row, 1.0, 0.0)
                    ahead = ahead.astype(BF)
                    below_ref[e:e + 1, cols] += jnp.dot(ones_sub, ahead, preferred_element_type=F32)[:1, :]
                    if j < k:
                        rankc_ref[rows, e:e + 1] += blk - jnp.dot(ahead, ones_lane, preferred_element_type=F32)[:, :1]
        below, above = below_ref[...], rankc_ref[...]
        rank_rows = below + above.T
        for gi in range(n_groups):
            rankr_ref[gi] = rank_rows[gi * group:(gi + 1) * group, :]
        rankc_ref[...] = above + below.T

    slot = lax.broadcasted_iota(jnp.int32, (group, cap, s), 1).astype(F32)
    pick = rankr_ref[g][:, None, :] == slot
    onehot = jnp.where(pick, 1.0, 0.0).reshape(group * cap, s).astype(BF)
    rows_f32 = jnp.dot(onehot, h_ref[...], preferred_element_type=F32)
    xe_ref[...] = rows_f32.reshape(group, cap, -1).astype(BF)
    gate_ref[...] = jnp.sum(jnp.where(pick, ar_ref[g][:, None, :], 0.0), axis=2, keepdims=True)


def _dispatch(aff_row, aff_col, h, bsz, seq):
    t, d = h.shape
    ne = aff_row.shape[0]
    cap = EC_FACTOR * seq // ne
    group = GATHER_ROWS // cap
    n_groups = ne // group
    return pl.pallas_call(
        functools.partial(_dispatch_kernel, cap=cap),
        grid=(bsz, n_groups),
        in_specs=[pl.BlockSpec((n_groups, group, seq), lambda b, g: (0, 0, b)),
                  pl.BlockSpec((seq, ne), lambda b, g: (b, 0)),
                  pl.BlockSpec((seq, d), lambda b, g: (b, 0))],
        out_specs=[pl.BlockSpec((group, cap, d), lambda b, g: (g, b, 0)),
                   pl.BlockSpec((group, cap, 1), lambda b, g: (g, b, 0)),
                   pl.BlockSpec((seq, ne), lambda b, g: (b, 0))],
        out_shape=[jax.ShapeDtypeStruct((ne, bsz * cap, d), BF),
                   jax.ShapeDtypeStruct((ne, bsz * cap, 1), F32),
                   jax.ShapeDtypeStruct((t, ne), F32)],
        scratch_shapes=[pltpu.VMEM((n_groups, group, seq), F32), pltpu.VMEM((ne, seq), F32)],
        compiler_params=_params(2), name="ec_dispatch",
    )(aff_row.reshape(n_groups, group, t), aff_col, h)


def _experts_kernel(xp_ref, xs_ref, gp_ref, gs_ref, wg_ref, wu_ref, wd_ref, yp_ref, ys_ref, accp_ref, accs_ref):
    f = pl.program_id(1)
    tf = wg_ref.shape[1]
    w_gate_up = jnp.concatenate([wg_ref[...].astype(BF), wu_ref[...].astype(BF)], axis=1)
    wd = wd_ref[...].astype(BF)
    groups = ((xp_ref, gp_ref, yp_ref, accp_ref), (xs_ref, gs_ref, ys_ref, accs_ref))

    @pl.when(f == 0)
    def _():
        for _, _, _, acc_ref in groups:
            acc_ref[...] = jnp.zeros_like(acc_ref)

    for x_ref, _, _, acc_ref in groups:
        for r in range(0, x_ref.shape[0], EXPERT_ROWS):
            rows = slice(r, r + EXPERT_ROWS)
            au = jnp.dot(x_ref[rows, :], w_gate_up, preferred_element_type=F32)
            a, u = au[:, :tf], au[:, tf:]
            acc_ref[rows, :] += jnp.dot((a * _sigmoid(a) * u).astype(BF), wd, preferred_element_type=F32)

    @pl.when(f == pl.num_programs(1) - 1)
    def _():
        for _, g_ref, y_ref, acc_ref in groups:
            y_ref[...] = (acc_ref[...] * g_ref[...]).astype(BF)


def _experts(xe_p, xe_s, gate_p, gate_s, w_gate, w_up, w_down, layer):
    ne, mp, d = xe_p.shape
    ms = xe_s.shape[1]
    fdim = w_gate.shape[-1]
    tf = EXPERT_HIDDEN_COLS
    rows = lambda m, last: pl.BlockSpec((None, m, last), lambda e, f: (e, 0, 0))
    return pl.pallas_call(
        _experts_kernel, grid=(ne, fdim // tf),
        in_specs=[rows(mp, d), rows(ms, d), rows(mp, 1), rows(ms, 1),
                  pl.BlockSpec((None, None, d, tf), lambda e, f: (layer, e, 0, f)),
                  pl.BlockSpec((None, None, d, tf), lambda e, f: (layer, e, 0, f)),
                  pl.BlockSpec((None, None, tf, d), lambda e, f: (layer, e, f, 0))],
        out_specs=[rows(mp, d), rows(ms, d)],
        out_shape=[jax.ShapeDtypeStruct((ne, mp, d), BF), jax.ShapeDtypeStruct((ne, ms, d), BF)],
        scratch_shapes=[pltpu.VMEM((mp, d), F32), pltpu.VMEM((ms, d), F32)],
        compiler_params=_params(2), name="ec_experts",
    )(xe_p, xe_s, gate_p, gate_s, w_gate, w_up, w_down)


RANK_RADIX = 32


def _combine_kernel(y_ref, rank_ref, x_ref, g_ref, *refs, cap, final):
    post_refs, place_ref = refs[:-1], refs[-1]
    ne = y_ref.shape[0]
    n_slots = ne * cap

    @pl.when(pl.program_id(1) == 0)
    def _():
        rank = rank_ref[...]
        hi = jnp.floor(rank * (1.0 / RANK_RADIX))
        lo = rank - RANK_RADIX * hi
        lane = lax.broadcasted_iota(jnp.int32, (ne, n_slots), 1)
        expert = lax.broadcasted_iota(jnp.int32, (ne, n_slots), 0)
        own = (lane >= expert * cap) & (lane < (expert + 1) * cap)
        spread_hi = jnp.where(own, float(RANK_RADIX), 0.0).astype(BF)
        spread_lo = jnp.where(own, 1.0, 0.0).astype(BF)
        spread = (jnp.dot(hi.astype(BF), spread_hi, preferred_element_type=F32)
                  + jnp.dot(lo.astype(BF), spread_lo, preferred_element_type=F32))
        slot = (lax.broadcasted_iota(jnp.int32, (1, n_slots), 1) & (cap - 1)).astype(F32)
        place_ref[...] = jnp.where(spread == slot, 1.0, 0.0).astype(BF)

    ts = x_ref.shape[0]
    rows = pl.ds(pl.multiple_of(pl.program_id(1) * ts, ts), ts)
    y = y_ref[...].reshape(n_slots, y_ref.shape[2])
    x_new = x_ref[...] + g_ref[...] * jnp.dot(place_ref[rows, :], y, preferred_element_type=F32)
    if final:
        (gain_ref,), (out_ref,) = post_refs[:1], post_refs[1:]
        out_ref[...] = _rms(x_new) * gain_ref[...]
    else:
        (sh_ref, sc_ref), (xo_ref, h_ref) = post_refs[:2], post_refs[2:]
        xo_ref[...] = x_new
        h_ref[...] = (_rms(x_new) * (1.0 + sc_ref[...]) + sh_ref[...]).astype(BF)


def _combine(y, rank, x, gate, bsz, seq, next_shift_scale=None, final_gain=None):
    t, d = x.shape
    ne = y.shape[0]
    cap = EC_FACTOR * seq // ne
    assert seq <= RANK_RADIX * RANK_RADIX and cap & (cap - 1) == 0
    ts = min(COMBINE_ROWS, seq)
    nt = seq // ts
    final = final_gain is not None
    vec = lambda v: pl.BlockSpec((None, 1, d), lambda b, i: (b if v.shape[0] > 1 else 0, 0, 0))
    tile = pl.BlockSpec((ts, d), lambda b, i: (b * nt + i, 0))
    in_specs = [pl.BlockSpec((ne, cap, d), lambda b, i: (0, b, 0)),
                pl.BlockSpec((seq, ne), lambda b, i: (b, 0)), tile, vec(gate)]
    if final:
        args = [final_gain.reshape(1, d)]
        in_specs.append(pl.BlockSpec((1, d), lambda b, i: (0, 0)))
        out_specs, out_shape = tile, jax.ShapeDtypeStruct((t, d), F32)
    else:
        args = list(next_shift_scale)
        in_specs += [vec(v) for v in args]
        out_specs = [tile, tile]
        out_shape = [jax.ShapeDtypeStruct((t, d), F32), jax.ShapeDtypeStruct((t, d), BF)]
    return pl.pallas_call(
        functools.partial(_combine_kernel, cap=cap, final=final),
        grid=(bsz, nt), in_specs=in_specs, out_specs=out_specs, out_shape=out_shape,
        scratch_shapes=[pltpu.VMEM((seq, ne * cap), BF)],
        compiler_params=_params(2), name="ec_combine",
    )(y, rank, x, gate, *args)


def _rope_tables(n_tokens, rot_dim):
    rows = n_tokens // GRID_W
    row = jnp.repeat(jnp.arange(rows), GRID_W).astype(F32)
    col = jnp.tile(jnp.arange(GRID_W), rows).astype(F32)
    quarter = rot_dim // 4
    inv = ROPE_THETA ** (-jnp.arange(quarter, dtype=F32) / quarter)
    a_row, a_col = row[:, None] * inv, col[:, None] * inv
    cos_t = jnp.concatenate([jnp.cos(a_row), jnp.cos(a_row), jnp.cos(a_col), jnp.cos(a_col)], axis=-1)
    sin_t = jnp.concatenate([-jnp.sin(a_row), jnp.sin(a_row), -jnp.sin(a_col), jnp.sin(a_col)], axis=-1)
    return cos_t, sin_t


def kernel(x_prompt, x_sample, state_mlstm_c, state_mlstm_n, state_mlstm_m, cache_mla_ckv, cache_mla_krope,
           cache_gqa_k, cache_gqa_v, c, c_ctx, w_mod, b_mod, w_in_even, b_igate, b_fgate, g_mlstm, g_cq, w_uq,
           g_ckv, w_ukv, w_out_even, w_in_odd, g_qnorm, g_knorm, w_out_odd, w_router, w_expert_gate,
           w_expert_up, w_expert_down, g_final):
    d = D_MODEL
    bp, sp, _ = x_prompt.shape
    bs, ss, _ = x_sample.shape
    depth = w_mod.shape[0]
    nh = ML_HEADS
    streams = {"p": (bp, sp), "s": (bs, ss)}
    x = {"p": x_prompt.reshape(bp * sp, d), "s": x_sample.reshape(bs * ss, d)}

    c8 = jnp.concatenate([c_ctx[None], c, jnp.zeros((8 - 1 - bs, d), F32)], axis=0)
    mod_all = _mod_vectors(c8, w_mod, b_mod).reshape(depth, 8, 6, 1, d)

    def mod(layer, key, idx):
        rows = mod_all[layer, 0:1, idx] if key == "p" else mod_all[layer, 1:1 + bs, idx]
        return rows

    new_even, new_odd = [], []
    h_in = {key: _norm_mod(x[key], mod(0, key, 0), mod(0, key, 1), streams[key][1]) for key in streams}
    routed_in = {}
    for layer in range(depth):
        j = layer // 2
        if layer % 2 == 0:
            w_out = _to_bf16(w_out_even, j)
            w_in_rows = jnp.swapaxes(w_in_even, 1, 2)
            w_side = jnp.concatenate(
                [w_in_rows[j, QKVO_COLS + 4 * nh:],
                 w_in_rows[j, QKVO_COLS:QKVO_COLS + 4 * nh],
                 jnp.zeros((128 - MLA_ROPE - 4 * nh, d), F32)], axis=0)[None]
            w_q = w_uq[j].reshape(MLA_Q_RANK, MLA_HEADS, MLA_NOPE + MLA_ROPE)
            w_q = jnp.concatenate([w_q[:, :, :MLA_NOPE].reshape(MLA_Q_RANK, -1),
                                   w_q[:, :, MLA_NOPE:].reshape(MLA_Q_RANK, -1)], axis=1)
            bias_col = jnp.concatenate([b_igate[j].reshape(1, -1), b_fgate[j].reshape(1, -1)], axis=1)
            bias_row = bias_col.reshape(-1, 1)
            cos64, sin64 = _rope_tables(ss, MLA_ROPE)
            cos_q, sin_q = jnp.tile(cos64, (1, MLA_HEADS)), jnp.tile(sin64, (1, MLA_HEADS))
            pad = jnp.zeros((ss, 128 - MLA_ROPE), F32)
            cos_k, sin_k = jnp.concatenate([cos64, pad], axis=1), jnp.concatenate([sin64, pad], axis=1)
            kvc = _mm_resident(cache_mla_ckv[:, j].reshape(-1, MLA_KV_RANK), w_ukv, j, w_ukv.shape[-1], IN_PROJ_COLS,
                               name="mla_ctx_expand", out_dtype=BF)
            for key, (bsz, seq) in streams.items():
                h = h_in[key]
                qkvo = _mm_resident(h, w_in_rows, j, QKVO_COLS, IN_PROJ_COLS, w_rows=True, name="even_in_main")
                side = _mm_resident(h, w_side, 0, SIDE_COLS, SIDE_COLS // 3, w_rows=True, name="even_in_side")
                gates = side[:, SIDE_COLS - 128 + GATE_LANE0:SIDE_COLS - 128 + GATE_LANE0 + 4 * nh]
                g_row = gates.reshape(-1, ML_CHUNK, 4 * nh).transpose(0, 2, 1)
                if key == "p":
                    init = None
                else:
                    c0 = state_mlstm_c[:, j]
                    init = (c0, jnp.broadcast_to(state_mlstm_n[:, j][..., None], c0.shape),
                            state_mlstm_m[:, j].reshape(bsz, 1, 2 * nh))
                hf, hb, c_fin, n_fin, m_fin = _mlstm(qkvo, side, g_row, bias_col, bias_row, init, bsz, seq)
                if key == "p":
                    y_a, ckv_n = _mla_attention(side, g_cq[j], g_ckv[j], w_q, w_ukv[j], bsz, seq)
                    k_rope = side[:, SIDE_COLS - 128:SIDE_COLS - 128 + MLA_ROPE]
                    new_even.append((c_fin, n_fin, m_fin.reshape(bsz, 2, nh),
                                     ckv_n.reshape(bsz, seq, -1), k_rope.reshape(bsz, seq, -1)))
                else:
                    y_a = _mla_attention(side, g_cq[j], g_ckv[j], w_q, w_ukv[j], bsz, seq,
                                         ctx=(cos_q, sin_q, cos_k, sin_k, kvc, cache_mla_krope[:, j]))
                routed_in[key] = _out_proj_router([y_a], w_out, x[key], mod(layer, key, 2), mod(layer, key, 3),
                                                  mod(layer, key, 4), w_router, layer, seq,
                                                  mlstm=(hf, hb, qkvo, g_mlstm[j]))
        else:
            w_out = _to_bf16(w_out_odd, j)
            cos_t, sin_t = _rope_tables(ss, GQA_HEAD_DIM)
            for key, (bsz, seq) in streams.items():
                qkv = _mm_resident(h_in[key], w_in_odd, j, w_in_odd.shape[-1], IN_PROJ_COLS, name="odd_in")
                if key == "p":
                    o, k_n, v = _gqa_attention(qkv, g_qnorm[j], g_knorm[j], bsz, seq)
                    new_odd.append((k_n, v))
                else:
                    o = _gqa_attention(qkv, g_qnorm[j], g_knorm[j], bsz, seq,
                                       ctx=(cos_t, sin_t, cache_gqa_k, cache_gqa_v, j))
                routed_in[key] = _out_proj_router([o], w_out, x[key], mod(layer, key, 2), mod(layer, key, 3),
                                                  mod(layer, key, 4), w_router, layer, seq)
        routed = {}
        for key, (bsz, seq) in streams.items():
            x[key], h, aff, aff_t = routed_in[key]
            routed[key] = _dispatch(aff_t, aff, h, bsz, seq)
        y_p, y_s = _experts(routed["p"][0], routed["s"][0], routed["p"][1], routed["s"][1],
                            w_expert_gate, w_expert_up, w_expert_down, layer)
        for key, y in (("p", y_p), ("s", y_s)):
            bsz, seq = streams[key]
            if layer + 1 < depth:
                x[key], h_in[key] = _combine(
                    y, routed[key][2], x[key], mod(layer, key, 5), bsz, seq,
                    next_shift_scale=(mod(layer + 1, key, 0), mod(layer + 1, key, 1)))
            else:
                x[key] = _combine(y, routed[key][2], x[key], mod(layer, key, 5), bsz, seq, final_gain=g_final)

    y_prompt = x["p"].reshape(bp, sp, d)
    y_sample = x["s"].reshape(bs, ss, d)
    new_c = jnp.stack([e[0] for e in new_even], axis=1)
    new_n = jnp.stack([e[1] for e in new_even], axis=1)
    new_m = jnp.stack([e[2] for e in new_even], axis=1)
    new_ckv = jnp.stack([e[3] for e in new_even], axis=1)
    new_krope = jnp.stack([e[4] for e in new_even], axis=1)
    new_k = jnp.stack([e[0] for e in new_odd], axis=1)
    new_v = jnp.stack([e[1] for e in new_odd], axis=1)
    return (y_prompt, y_sample, new_c, new_n, new_m, new_ckv, new_krope, new_k, new_v)
```
